```python
import jax, jax.numpy as jnp
from jax import lax
import numpy as np

D_MODEL = 1024
BATCH = 8
SEQ = 4096
DEPTH = 1

EPS = 1e-6
D_MIX = D_MODEL
GLA_HEADS = 4
GLA_DK = 64
GLA_DV = 128
GLA_GATE_RANK = 16
GLA_GATE_NORM = 16.0
GLA_CHUNK = 64
MLA_HEADS = 8
MLA_NOPE = 64
MLA_ROPE = 32
MLA_DV = 64
MLA_Q_RANK = 384
MLA_KV_RANK = 256
ROPE_THETA = 10000.0
Q_BLOCK = 128
N_GROUPS = 4
EXPERTS_PER_GROUP = 8
N_EXPERTS = N_GROUPS * EXPERTS_PER_GROUP
TOP_K = 2
D_EXPERT = 256
EXPERT_BLOCK = 128

IN_WIDTHS = (GLA_HEADS * GLA_DK, GLA_HEADS * GLA_DK, GLA_HEADS * GLA_DV, GLA_GATE_RANK,
             GLA_HEADS * GLA_DV, MLA_Q_RANK, MLA_KV_RANK, MLA_ROPE)
D_IN = 2 * GLA_HEADS * GLA_DK + 2 * GLA_HEADS * GLA_DV + GLA_GATE_RANK + MLA_Q_RANK + MLA_KV_RANK + MLA_ROPE

kernel_name = "hybrid_gla_mla_hier_moe"


def _split_points():
    pts, acc = [], 0
    for w in IN_WIDTHS[:-1]:
        acc += w
        pts.append(acc)
    return pts


def rmsnorm(x, w):
    x32 = x.astype(jnp.float32)
    y = x32 * lax.rsqrt(jnp.mean(x32 * x32, axis=-1, keepdims=True) + EPS)
    return (y * w.astype(jnp.float32)).astype(x.dtype)


def rope(x, pos):
    half = x.shape[-1] // 2
    inv = ROPE_THETA ** (-jnp.arange(half, dtype=jnp.float32) / half)
    ang = pos.astype(jnp.float32)[..., None] * inv
    cos, sin = jnp.cos(ang), jnp.sin(ang)
    x1, x2 = x[..., :half].astype(jnp.float32), x[..., half:].astype(jnp.float32)
    return jnp.concatenate([x1 * cos - x2 * sin, x1 * sin + x2 * cos], axis=-1).astype(x.dtype)


def gla_mixer(q, k, v, gate_lr, out_gate, gate_up, gate_bias, norm_w):
    B, S, _ = q.shape
    H, C = GLA_HEADS, GLA_CHUNK
    nc = S // C
    f32 = jnp.float32

    def heads(t, d):
        return t.reshape(B, nc, C, H, d).transpose(0, 3, 1, 2, 4).astype(f32)

    q = heads(q, GLA_DK) * (GLA_DK ** -0.5)
    k = heads(k, GLA_DK)
    v = heads(v, GLA_DV)
    z = (gate_lr @ gate_up + gate_bias).astype(f32)
    log_a = heads(jax.nn.log_sigmoid(z) / GLA_GATE_NORM, GLA_DK)
    b = jnp.cumsum(log_a, axis=3)
    b_last = b[:, :, :, -1:, :]
    q_e = q * jnp.exp(b)
    k_e = k * jnp.exp(-b)
    causal = jnp.tril(jnp.ones((C, C), dtype=bool))
    att = jnp.where(causal, jnp.einsum('bhnid,bhnjd->bhnij', q_e, k_e), 0.0)
    o_intra = jnp.einsum('bhnij,bhnjv->bhniv', att, v)
    upd = jnp.einsum('bhnjd,bhnjv->bhndv', k * jnp.exp(b_last - b), v)
    decay = jnp.exp(b_last[:, :, :, 0, :])

    def step(state, xs):
        d, u = xs
        return d[..., None] * state + u, state

    s0 = jnp.zeros((B, H, GLA_DK, GLA_DV), f32)
    _, s_prev = lax.scan(step, s0, (jnp.moveaxis(decay, 2, 0), jnp.moveaxis(upd, 2, 0)))
    s_prev = jnp.moveaxis(s_prev, 0, 2)
    o = o_intra + jnp.einsum('bhnid,bhndv->bhniv', q_e, s_prev)
    o = o * lax.rsqrt(jnp.mean(o * o, axis=-1, keepdims=True) + EPS) * norm_w.astype(f32)
    o = o.transpose(0, 2, 3, 1, 4).reshape(B, S, H * GLA_DV)
    return (o * jax.nn.silu(out_gate.astype(f32))).astype(out_gate.dtype)


def mla_mixer(c_q, c_kv, k_rope, positions, q_norm_w, w_uq, kv_norm_w, w_ukv):
    B, S, _ = c_q.shape
    H = MLA_HEADS
    q = (rmsnorm(c_q, q_norm_w) @ w_uq).reshape(B, S, H, MLA_NOPE + MLA_ROPE)
    q_nope = q[..., :MLA_NOPE]
    q_rope = rope(q[..., MLA_NOPE:], positions[:, :, None])
    kv = (rmsnorm(c_kv, kv_norm_w) @ w_ukv).reshape(B, S, H, MLA_NOPE + MLA_DV)
    k_nope, v = kv[..., :MLA_NOPE], kv[..., MLA_NOPE:]
    k_r = rope(k_rope, positions)
    scale = (MLA_NOPE + MLA_ROPE) ** -0.5
    nb = S // Q_BLOCK
    key_pos = jnp.arange(S)

    def block(args):
        qn, qr, i = args
        s = jnp.einsum('bqhd,bkhd->bhqk', qn, k_nope) + jnp.einsum('bqhr,bkr->bhqk', qr, k_r)
        q_pos = i * Q_BLOCK + jnp.arange(Q_BLOCK)
        s = jnp.where(key_pos[None, :] <= q_pos[:, None], s.astype(jnp.float32) * scale, -jnp.inf)
        p = jax.nn.softmax(s, axis=-1).astype(v.dtype)
        return jnp.einsum('bhqk,bkhd->bqhd', p, v)

    qn_b = q_nope.reshape(B, nb, Q_BLOCK, H, MLA_NOPE).transpose(1, 0, 2, 3, 4)
    qr_b = q_rope.reshape(B, nb, Q_BLOCK, H, MLA_ROPE).transpose(1, 0, 2, 3, 4)
    out = lax.map(block, (qn_b, qr_b, jnp.arange(nb)))
    return out.transpose(1, 0, 2, 3, 4).reshape(B, S, H * MLA_DV)


def hier_moe(x, wg, bg, we, be, w_gate, w_up, w_down):
    B, S, D = x.shape
    N = B * S
    xf = x.reshape(N, D)
    f32 = jnp.float32
    g_logits = (xf @ wg).astype(f32) + bg.astype(f32)
    g_prob = jax.nn.softmax(g_logits, axis=-1)
    g_sel = jnp.argmax(g_logits, axis=-1).astype(jnp.int32)
    p_g = jnp.take_along_axis(g_prob, g_sel[:, None], axis=1)
    e_logits = ((xf @ we).astype(f32) + be.astype(f32)).reshape(N, N_GROUPS, EXPERTS_PER_GROUP)
    e_logits = jnp.take_along_axis(e_logits, g_sel[:, None, None], axis=1)[:, 0]
    e_prob = jax.nn.softmax(e_logits, axis=-1)
    top_p, top_i = lax.top_k(e_prob, TOP_K)
    gate = p_g * top_p / jnp.sum(top_p, axis=-1, keepdims=True)
    expert_id = (g_sel[:, None] * EXPERTS_PER_GROUP + top_i.astype(jnp.int32)).reshape(-1)
    NK = N * TOP_K
    tok = jnp.repeat(jnp.arange(N, dtype=jnp.int32), TOP_K)
    order = jnp.argsort(expert_id)
    se, stok, sgate = expert_id[order], tok[order], gate.reshape(-1)[order]
    counts = jnp.bincount(expert_id, length=N_EXPERTS).astype(jnp.int32)
    starts = jnp.cumsum(counts) - counts
    padded = (counts + EXPERT_BLOCK - 1) // EXPERT_BLOCK * EXPERT_BLOCK
    pend = jnp.cumsum(padded)
    pstarts = pend - padded
    dest = pstarts[se] + (jnp.arange(NK, dtype=jnp.int32) - starts[se])
    nblk = -(-NK // EXPERT_BLOCK) + N_EXPERTS
    x_buf = jnp.zeros((nblk * EXPERT_BLOCK, D), xf.dtype).at[dest].set(xf[stok])
    blk_start = jnp.arange(nblk, dtype=jnp.int32) * EXPERT_BLOCK
    blk_expert = jnp.clip(jnp.searchsorted(pend, blk_start, side='right'), 0, N_EXPERTS - 1)

    def expert_block(args):
        xb, e = args
        hdn = jax.nn.silu(xb @ w_gate[e]) * (xb @ w_up[e])
        return hdn @ w_down[e]

    y_buf = lax.map(expert_block, (x_buf.reshape(nblk, EXPERT_BLOCK, D), blk_expert))
    y = y_buf.reshape(nblk * EXPERT_BLOCK, D)[dest] * sgate[:, None].astype(xf.dtype)
    out = jax.ops.segment_sum(y, stok, num_segments=N)
    return out.reshape(B, S, D)


def setup_inputs(seed: int = 0) -> dict:
    key = jax.random.key(seed)
    ks = jax.random.split(key, 24)
    f32 = jnp.float32
    L, D = DEPTH, D_MODEL

    def nrm(k, shape, scale):
        return jax.random.normal(k, shape, f32) * scale

    def gain(k, shape):
        return 1.0 + 0.02 * jax.random.normal(k, shape, f32)

    x = jax.random.normal(ks[0], (BATCH, SEQ, D), f32)
    offsets = jax.random.randint(ks[1], (BATCH, 1), 0, 2048, dtype=jnp.int32)
    positions = (offsets + jnp.arange(SEQ, dtype=jnp.int32)[None, :]).astype(jnp.int32)
    return {
        'x': x,
        'positions': positions,
        'attn_norm_w': gain(ks[2], (L, D)),
        'w_in': nrm(ks[3], (L, D, D_IN), D ** -0.5),
        'gla_gate_up': nrm(ks[4], (L, GLA_GATE_RANK, GLA_HEADS * GLA_DK), GLA_GATE_RANK ** -0.5),
        'gla_gate_bias': nrm(ks[5], (L, GLA_HEADS * GLA_DK), 0.02),
        'gla_norm_w': gain(ks[6], (L, GLA_DV)),
        'mla_q_norm_w': gain(ks[7], (L, MLA_Q_RANK)),
        'mla_w_uq': nrm(ks[8], (L, MLA_Q_RANK, MLA_HEADS * (MLA_NOPE + MLA_ROPE)), MLA_Q_RANK ** -0.5),
        'mla_kv_norm_w': gain(ks[9], (L, MLA_KV_RANK)),
        'mla_w_ukv': nrm(ks[10], (L, MLA_KV_RANK, MLA_HEADS * (MLA_NOPE + MLA_DV)), MLA_KV_RANK ** -0.5),
        'w_out': nrm(ks[11], (L, D_MIX, D), D_MIX ** -0.5),
        'ffn_norm_w': gain(ks[12], (L, D)),
        'router_group_w': nrm(ks[13], (L, D, N_GROUPS), D ** -0.5),
        'router_group_b': nrm(ks[14], (L, N_GROUPS), 0.01),
        'router_expert_w': nrm(ks[15], (L, D, N_EXPERTS), D ** -0.5),
        'router_expert_b': nrm(ks[16], (L, N_EXPERTS), 0.01),
        'expert_w_gate': nrm(ks[17], (L, N_EXPERTS, D, D_EXPERT), D ** -0.5),
        'expert_w_up': nrm(ks[18], (L, N_EXPERTS, D, D_EXPERT), D ** -0.5),
        'expert_w_down': nrm(ks[19], (L, N_EXPERTS, D_EXPERT, D), D_EXPERT ** -0.5),
        'final_norm_w': gain(ks[20], (D,)),
    }


def reference(x, positions, attn_norm_w, w_in, gla_gate_up, gla_gate_bias, gla_norm_w,
              mla_q_norm_w, mla_w_uq, mla_kv_norm_w, mla_w_ukv, w_out, ffn_norm_w,
              router_group_w, router_group_b, router_expert_w, router_expert_b,
              expert_w_gate, expert_w_up, expert_w_down, final_norm_w):
    h = x
    for l in range(DEPTH):
        xn = rmsnorm(h, attn_norm_w[l])
        g_q, g_k, g_v, g_lr, g_og, m_cq, m_ckv, m_kr = jnp.split(xn @ w_in[l], _split_points(), axis=-1)
        y_gla = gla_mixer(g_q, g_k, g_v, g_lr, g_og, gla_gate_up[l], gla_gate_bias[l], gla_norm_w[l])
        y_mla = mla_mixer(m_cq, m_ckv, m_kr, positions, mla_q_norm_w[l], mla_w_uq[l],
                          mla_kv_norm_w[l], mla_w_ukv[l])
        h = h + jnp.concatenate([y_gla, y_mla], axis=-1) @ w_out[l]
        h = h + hier_moe(rmsnorm(h, ffn_norm_w[l]), router_group_w[l], router_group_b[l],
                         router_expert_w[l], router_expert_b[l], expert_w_gate[l],
                         expert_w_up[l], expert_w_down[l])
    return rmsnorm(h, final_norm_w)
```

```python
import functools

import jax
import jax.numpy as jnp
from jax import lax
from jax.experimental import pallas as pl
from jax.experimental.pallas import tpu as pltpu

EPS = 1e-6
GLA_HEADS = 4
GLA_DK = 64
GLA_DV = 128
GLA_GATE_RANK = 16
GLA_GATE_NORM = 16.0
GLA_CHUNK = 64
MLA_HEADS = 8
MLA_NOPE = 64
MLA_ROPE = 32
MLA_DV = 64
MLA_Q_RANK = 384
MLA_KV_RANK = 256
ROPE_THETA = 10000.0
N_GROUPS = 4
EXPERTS_PER_GROUP = 8
N_EXPERTS = N_GROUPS * EXPERTS_PER_GROUP
TOP_K = 2
D_EXPERT = 256

LANES = 128
HEAD_PAD = 128
ROPE_HALF = MLA_ROPE // 2
ROPE_LO = MLA_NOPE
ROPE_HI = MLA_NOPE + ROPE_HALF
ROPE_END = MLA_NOPE + MLA_ROPE
GATE_LO = ROPE_END

PROJ_TILE = 512
GLA_TILE = 256
ATT_TILE = 256
ROW_TILE = 256
EXPERT_BLOCK = 256
VMEM_LIMIT = 56 * 1024 * 1024

F32 = jnp.float32
BF16 = jnp.bfloat16
NEG_INF = float("-inf")


def _dot(a, b, precision=None):
    return jnp.dot(a, b, preferred_element_type=F32, precision=precision)


def _dot_nt(a, b):
    return lax.dot_general(a, b, (((1,), (1,)), ((), ())), preferred_element_type=F32)


def _dot_tn(a, b):
    return lax.dot_general(a, b, (((0,), (0,)), ((), ())), preferred_element_type=F32)


def _rms(x, w):
    return x * lax.rsqrt(jnp.mean(x * x, axis=-1, keepdims=True) + EPS) * w


def _pack_bf16_pair(a, b):
    ua = lax.bitcast_convert_type(a.astype(BF16).astype(F32), jnp.uint32)
    ub = lax.bitcast_convert_type(b.astype(BF16).astype(F32), jnp.uint32)
    return (ua >> 16) | (ub & jnp.uint32(0xFFFF0000))


def _unpack_bf16_pair(u):
    a = lax.bitcast_convert_type(u << 16, F32)
    b = lax.bitcast_convert_type(u & jnp.uint32(0xFFFF0000), F32)
    return a, b


def _proj_kernel(x_ref, pos_ref, nw_ref, w1_ref, gu_ref, gb_ref, qnw_ref, wuq_ref, kvnw_ref,
                 wukv_ref, inv_ref,
                 gq_ref, gk_ref, gv_ref, gla_ref, gog_ref, q_ref, k_ref, v_ref, *, q_scale):
    x = x_ref[...]
    xn = _rms(x, nw_ref[...]).astype(BF16)
    proj = _dot(xn, w1_ref[...])
    dq = GLA_HEADS * GLA_DK
    dv = GLA_HEADS * GLA_DV
    o = 0
    gq_ref[...] = (proj[:, o:o + dq] * (GLA_DK ** -0.5)).astype(BF16); o += dq
    gk_ref[...] = proj[:, o:o + dq].astype(BF16); o += dq
    gv_ref[...] = proj[:, o:o + dv].astype(BF16); o += dv
    gog_ref[...] = proj[:, o:o + dv].astype(BF16); o += dv
    cq = proj[:, o:o + MLA_Q_RANK]; o += MLA_Q_RANK
    ckv = proj[:, o:o + MLA_KV_RANK]; o += MLA_KV_RANK
    misc = proj[:, o:o + LANES]

    z = _dot(misc.astype(BF16), gu_ref[...]) + gb_ref[...]
    log_sig = jnp.minimum(z, 0.0) - jnp.log1p(jnp.exp(-jnp.abs(z)))
    gla_ref[...] = log_sig * (1.0 / GLA_GATE_NORM)

    lane = lax.broadcasted_iota(jnp.int32, (x.shape[0], LANES), 1)
    ang = pos_ref[...].astype(F32) * inv_ref[...]
    cosv = jnp.cos(ang)
    sinv = jnp.sin(ang)
    in_lo = (lane >= ROPE_LO) & (lane < ROPE_HI)
    in_hi = (lane >= ROPE_HI) & (lane < ROPE_END)
    c_rope = jnp.where(in_lo | in_hi, cosv, 0.0)
    s_up = jnp.where(in_hi, sinv, 0.0)
    s_dn = jnp.where(in_lo, -sinv, 0.0)
    c_q = jnp.where(lane < MLA_NOPE, 1.0, c_rope)

    def rope(t, c):
        return (t * c + pltpu.roll(t, ROPE_HALF, 1) * s_up
                + pltpu.roll(t, LANES - ROPE_HALF, 1) * s_dn)

    k_rope = rope(misc, c_rope)

    q = _dot(_rms(cq, qnw_ref[...]).astype(BF16), wuq_ref[...])
    kv = _dot(_rms(ckv, kvnw_ref[...]).astype(BF16), wukv_ref[...])
    for h in range(MLA_HEADS):
        sl = slice(h * HEAD_PAD, (h + 1) * HEAD_PAD)
        q_ref[:, sl] = (rope(q[:, sl], c_q) * q_scale).astype(BF16)
        k_ref[:, sl] = (kv[:, sl] + k_rope).astype(BF16)
    v_ref[...] = kv[:, MLA_HEADS * HEAD_PAD:].astype(BF16)


def _gla_kernel(q_ref, k_ref, v_ref, la_ref, og_ref, nw_ref, o_ref, st_ref):
    t = q_ref.shape[1]
    nchunk = t // GLA_CHUNK

    @pl.when(pl.program_id(1) == 0)
    def _():
        st_ref[...] = jnp.zeros_like(st_ref)

    row = lax.broadcasted_iota(jnp.int32, (t, t), 0)
    col = lax.broadcasted_iota(jnp.int32, (t, t), 1)
    chunk_bits = GLA_CHUNK.bit_length() - 1
    tri = ((row >> chunk_bits) == (col >> chunk_bits)) & (col <= row)
    la = la_ref[0]
    b = _dot(tri.astype(F32), la, precision=lax.Precision.HIGHEST)
    b_last = jnp.concatenate(
        [jnp.broadcast_to(b[(c + 1) * GLA_CHUNK - 1:(c + 1) * GLA_CHUNK], (GLA_CHUNK, b.shape[1]))
         for c in range(nchunk)], axis=0)
    q_e = (q_ref[0].astype(F32) * jnp.exp(b)).astype(BF16)
    kf = k_ref[0].astype(F32)
    k_e = (kf * jnp.exp(-b)).astype(BF16)
    k_d = (kf * jnp.exp(b_last - b)).astype(BF16)
    decay = jnp.exp(b_last)
    nw = nw_ref[...]

    for h in range(GLA_HEADS):
        ks = slice(h * GLA_DK, (h + 1) * GLA_DK)
        vs = slice(h * GLA_DV, (h + 1) * GLA_DV)
        qh, keh, kdh = q_e[:, ks], k_e[:, ks], k_d[:, ks]
        vh = v_ref[0, :, vs]
        att = jnp.where(tri, _dot_nt(qh, keh), 0.0)
        o = _dot(att.astype(BF16), vh)
        state = st_ref[h]
        inter = []
        for c in range(nchunk):
            rs = slice(c * GLA_CHUNK, (c + 1) * GLA_CHUNK)
            inter.append(_dot_nt(qh[rs], state.astype(BF16)))
            upd = _dot_tn(vh[rs], kdh[rs])
            state = state * decay[c * GLA_CHUNK:c * GLA_CHUNK + 1, ks] + upd
        st_ref[h] = state
        o = o + jnp.concatenate(inter, axis=0)
        o = _rms(o, nw)
        g = og_ref[0, :, vs].astype(F32)
        o_ref[0, :, vs] = (o * (g * jax.nn.sigmoid(g))).astype(o_ref.dtype)


def _mla_kernel(q_ref, k_ref, v_ref, o_ref):
    tq = q_ref.shape[1]
    qi = pl.program_id(2)
    row = lax.broadcasted_iota(jnp.int32, (tq, tq), 0)
    col = lax.broadcasted_iota(jnp.int32, (tq, tq), 1)
    causal = col <= row
    lane = lax.broadcasted_iota(jnp.int32, (tq, LANES), 1)

    def step(qh, hh, j, carry, mask):
        m, l, acc = carry
        kj = k_ref[0, pl.ds(pl.multiple_of(j * tq, tq), tq), hh * HEAD_PAD:(hh + 1) * HEAD_PAD]
        vj = v_ref[0, pl.ds(pl.multiple_of(j * tq, tq), tq), :]
        s = _dot_nt(qh, kj)
        if mask:
            s = jnp.where(causal, s, NEG_INF)
        m_new = jnp.maximum(m, jnp.max(s, axis=-1, keepdims=True))
        alpha = jnp.exp(m - m_new)
        p = jnp.exp(s - m_new)
        l = alpha * l + jnp.sum(p, axis=-1, keepdims=True)
        acc = alpha * acc + _dot(p.astype(BF16), vj)
        return m_new, l, acc

    outs = []
    for hh in range(2):
        qh = q_ref[0, :, hh * HEAD_PAD:(hh + 1) * HEAD_PAD]
        init = (jnp.full((tq, 1), NEG_INF, F32), jnp.zeros((tq, 1), F32),
                jnp.zeros((tq, LANES), F32))
        carry = lax.fori_loop(0, qi, lambda j, c: step(qh, hh, j, c, False), init)
        m, l, acc = step(qh, hh, qi, carry, True)
        outs.append(acc / l)
    o_ref[0] = jnp.where(lane < MLA_DV, outs[0], outs[1]).astype(o_ref.dtype)


def _router_kernel(x_ref, yg_ref, ym_ref, wo_ref, fnw_ref, wr_ref, br_ref,
                   h_ref, hn_ref, route_ref, cnt_ref, carry_ref):
    t = x_ref.shape[0]
    half = wo_ref.shape[0] // 2

    @pl.when(pl.program_id(0) == 0)
    def _():
        carry_ref[...] = jnp.zeros_like(carry_ref)

    h = x_ref[...] + _dot(yg_ref[...], wo_ref[:half]) + _dot(ym_ref[...], wo_ref[half:])
    h_ref[...] = h
    hn = _rms(h, fnw_ref[...])
    hp = hn.shape[1] // 2
    hn_ref[...] = _pack_bf16_pair(hn[:, :hp], hn[:, hp:])

    logits = _dot(hn, wr_ref[...], precision=lax.Precision.HIGHEST) + br_ref[...]
    lane = lax.broadcasted_iota(jnp.int32, (t, LANES), 1)
    lane_f = lane.astype(F32)

    def first_argmax(vals, vmax):
        idx = jnp.min(jnp.where(vals == vmax, lane_f, float(LANES)), axis=-1, keepdims=True)
        return idx.astype(jnp.int32)

    gl = jnp.where((lane >= N_EXPERTS) & (lane < N_EXPERTS + N_GROUPS), logits, NEG_INF)
    gmax = jnp.max(gl, axis=-1, keepdims=True)
    gsel = first_argmax(gl, gmax) - N_EXPERTS
    p_g = 1.0 / jnp.sum(jnp.exp(gl - gmax), axis=-1, keepdims=True)
    lo = gsel * EXPERTS_PER_GROUP
    el = jnp.where((lane >= lo) & (lane < lo + EXPERTS_PER_GROUP), logits, NEG_INF)
    m1 = jnp.max(el, axis=-1, keepdims=True)
    i1 = first_argmax(el, m1)
    el2 = jnp.where(lane == i1, NEG_INF, el)
    m2 = jnp.max(el2, axis=-1, keepdims=True)
    i2 = first_argmax(el2, m2)
    e2 = jnp.exp(m2 - m1)
    g1 = p_g / (1.0 + e2)
    g2 = p_g * e2 / (1.0 + e2)

    is1 = lane == i1
    is2 = lane == i2
    onehot = (is1 | is2).astype(BF16)
    rr = lax.broadcasted_iota(jnp.int32, (t, t), 0)
    cc = lax.broadcasted_iota(jnp.int32, (t, t), 1)
    before = _dot((cc < rr).astype(BF16), onehot) + carry_ref[...]
    r1 = jnp.sum(jnp.where(is1, before, 0.0), axis=-1, keepdims=True)
    r2 = jnp.sum(jnp.where(is2, before, 0.0), axis=-1, keepdims=True)
    carry_ref[...] = carry_ref[...] + jnp.sum(onehot.astype(F32), axis=0, keepdims=True)
    cnt_ref[...] = carry_ref[...]

    route = jnp.where(lane == 0, i1.astype(F32), 0.0)
    route = jnp.where(lane == 1, i2.astype(F32), route)
    route = jnp.where(lane == 2, g1, route)
    route = jnp.where(lane == 3, g2, route)
    route = jnp.where(lane == 4, r1, route)
    route = jnp.where(lane == 5, r2, route)
    route_ref[...] = route


def _dispatch_kernel(dest_ref, hn_ref, xbuf_in_ref, xbuf_ref, sem):
    del xbuf_in_ref
    t = hn_ref.shape[0]

    def row_copy(tok, dst):
        return pltpu.make_async_copy(hn_ref.at[pl.ds(tok, 1)], xbuf_ref.at[pl.ds(dst, 1)], sem)

    def issue(i, c):
        for kk in range(TOP_K):
            row_copy(i, dest_ref[TOP_K * i + kk]).start()
        return c

    lax.fori_loop(0, t, issue, 0)

    def drain(i, c):
        row_copy(0, 0).wait()
        return c

    lax.fori_loop(0, TOP_K * t, drain, 0)


def _expert_kernel(be_ref, nu_ref, x_ref, wg_ref, wu_ref, wd_ref, y_ref):
    del be_ref

    @pl.when(pl.program_id(0) < nu_ref[0])
    def _():
        a, b = _unpack_bf16_pair(x_ref[...])
        x = jnp.concatenate([a, b], axis=1).astype(BF16)
        h1 = _dot(x, wg_ref[0].astype(BF16))
        h2 = _dot(x, wu_ref[0].astype(BF16))
        hdn = (h1 * jax.nn.sigmoid(h1) * h2).astype(BF16)
        y = _dot(hdn, wd_ref[0].astype(BF16))
        hp = y.shape[1] // 2
        y_ref[...] = _pack_bf16_pair(y[:, :hp], y[:, hp:])

    @pl.when(pl.program_id(0) >= nu_ref[0])
    def _():
        y_ref[...] = jnp.zeros_like(y_ref)


def _combine_kernel(dest_ref, h_ref, route_ref, fw_ref, ybuf_ref, o_ref, y0_ref, y1_ref, sem):
    t = h_ref.shape[0]
    bufs = (y0_ref, y1_ref)

    def row_copy(src, tok, kk):
        return pltpu.make_async_copy(ybuf_ref.at[pl.ds(src, 1)], bufs[kk].at[pl.ds(tok, 1)], sem)

    def issue(i, c):
        for kk in range(TOP_K):
            row_copy(dest_ref[TOP_K * i + kk], i, kk).start()
        return c

    lax.fori_loop(0, t, issue, 0)

    def drain(i, c):
        row_copy(0, 0, 0).wait()
        return c

    lax.fori_loop(0, TOP_K * t, drain, 0)

    route = route_ref[...]
    g1 = route[:, 2:3]
    g2 = route[:, 3:4]
    a0, b0 = _unpack_bf16_pair(y0_ref[...])
    a1, b1 = _unpack_bf16_pair(y1_ref[...])
    moe = jnp.concatenate([a0 * g1 + a1 * g2, b0 * g1 + b1 * g2], axis=1)
    o_ref[...] = _rms(h_ref[...] + moe, fw_ref[...])


def _params(*sem):
    return pltpu.CompilerParams(dimension_semantics=sem, vmem_limit_bytes=VMEM_LIMIT)


def _full(shape):
    return pl.BlockSpec(shape, lambda *_: (0,) * len(shape))


def _rows(tile, width):
    return pl.BlockSpec((tile, width), lambda i: (i, 0))


def _layer(x2, pos2, attn_norm_w, w_in, gla_gate_up, gla_gate_bias, gla_norm_w, mla_q_norm_w,
           mla_w_uq, mla_kv_norm_w, mla_w_ukv, w_out, ffn_norm_w, router_group_w, router_group_b,
           router_expert_w, router_expert_b, expert_w_gate, expert_w_up, expert_w_down,
           out_norm_w, batch, seq):
    n, d = x2.shape
    dq = GLA_HEADS * GLA_DK
    dv = GLA_HEADS * GLA_DV
    dmla = MLA_HEADS * HEAD_PAD
    dmv = MLA_HEADS * MLA_DV

    c_gq, c_gk, c_gv, c_lr, c_og, c_cq, c_ckv, c_kr = jnp.split(
        w_in, [dq, 2 * dq, 2 * dq + dv, 2 * dq + dv + GLA_GATE_RANK,
               2 * dq + 2 * dv + GLA_GATE_RANK,
               2 * dq + 2 * dv + GLA_GATE_RANK + MLA_Q_RANK,
               2 * dq + 2 * dv + GLA_GATE_RANK + MLA_Q_RANK + MLA_KV_RANK], axis=1)
    zeros = lambda r, c: jnp.zeros((r, c), w_in.dtype)
    misc = jnp.concatenate([zeros(d, MLA_NOPE), c_kr, c_lr,
                            zeros(d, LANES - ROPE_END - GLA_GATE_RANK)], axis=1)
    w1 = jnp.concatenate([c_gq, c_gk, c_gv, c_og, c_cq, c_ckv, misc], axis=1).astype(BF16)
    gate_up = jnp.concatenate([zeros(GATE_LO, dq), gla_gate_up,
                               zeros(LANES - GATE_LO - GLA_GATE_RANK, dq)], axis=0).astype(BF16)
    wuq = mla_w_uq.reshape(MLA_Q_RANK, MLA_HEADS, MLA_NOPE + MLA_ROPE)
    wuq = jnp.pad(wuq, ((0, 0), (0, 0), (0, HEAD_PAD - ROPE_END))).reshape(MLA_Q_RANK, dmla)
    wukv = mla_w_ukv.reshape(MLA_KV_RANK, MLA_HEADS, MLA_NOPE + MLA_DV)
    wuk = jnp.pad(wukv[:, :, :MLA_NOPE], ((0, 0), (0, 0), (0, HEAD_PAD - MLA_NOPE)))
    wukv = jnp.concatenate([wuk.reshape(MLA_KV_RANK, dmla),
                            wukv[:, :, MLA_NOPE:].reshape(MLA_KV_RANK, dmv)], axis=1)
    inv = ROPE_THETA ** (-jnp.arange(ROPE_HALF, dtype=F32) / ROPE_HALF)
    inv_pat = jnp.concatenate([jnp.zeros((MLA_NOPE,), F32), inv, inv,
                               jnp.zeros((LANES - ROPE_END,), F32)]).reshape(1, LANES)
    w_router = jnp.concatenate(
        [router_expert_w, router_group_w, zeros(d, LANES - N_EXPERTS - N_GROUPS)], axis=1)
    b_router = jnp.concatenate(
        [router_expert_b, router_group_b, jnp.zeros((LANES - N_EXPERTS - N_GROUPS,), F32)]
    ).reshape(1, LANES)
    row1 = lambda v: v.reshape(1, -1)

    tp = PROJ_TILE
    outs = pl.pallas_call(
        functools.partial(_proj_kernel, q_scale=(MLA_NOPE + MLA_ROPE) ** -0.5),
        grid=(n // tp,),
        in_specs=[_rows(tp, d), _rows(tp, 1), _full((1, d)), _full(w1.shape), _full(gate_up.shape),
                  _full((1, dq)), _full((1, MLA_Q_RANK)), _full(wuq.shape),
                  _full((1, MLA_KV_RANK)), _full(wukv.shape), _full((1, LANES))],
        out_specs=[_rows(tp, dq), _rows(tp, dq), _rows(tp, dv), _rows(tp, dq), _rows(tp, dv),
                   _rows(tp, dmla), _rows(tp, dmla), _rows(tp, dmv)],
        out_shape=[jax.ShapeDtypeStruct((n, dq), BF16), jax.ShapeDtypeStruct((n, dq), BF16),
                   jax.ShapeDtypeStruct((n, dv), BF16), jax.ShapeDtypeStruct((n, dq), F32),
                   jax.ShapeDtypeStruct((n, dv), BF16), jax.ShapeDtypeStruct((n, dmla), BF16),
                   jax.ShapeDtypeStruct((n, dmla), BF16), jax.ShapeDtypeStruct((n, dmv), BF16)],
        compiler_params=_params("parallel"),
        name="in_proj",
    )(x2, pos2, row1(attn_norm_w), w1, gate_up, row1(gla_gate_bias), row1(mla_q_norm_w),
      wuq.astype(BF16), row1(mla_kv_norm_w), wukv.astype(BF16), inv_pat)
    gq, gk, gv, gla, gog, q, k, v = outs

    tg = GLA_TILE
    seq3 = lambda a: a.reshape(batch, seq, a.shape[-1])
    gspec = lambda w: pl.BlockSpec((1, tg, w), lambda b, i: (b, i, 0))
    y_gla = pl.pallas_call(
        _gla_kernel,
        grid=(batch, seq // tg),
        in_specs=[gspec(dq), gspec(dq), gspec(dv), gspec(dq), gspec(dv), _full((1, GLA_DV))],
        out_specs=gspec(dv),
        out_shape=jax.ShapeDtypeStruct((batch, seq, dv), BF16),
        scratch_shapes=[pltpu.VMEM((GLA_HEADS, GLA_DV, GLA_DK), F32)],
        compiler_params=_params("parallel", "arbitrary"),
        name="gla",
    )(seq3(gq), seq3(gk), seq3(gv), seq3(gla), seq3(gog), row1(gla_norm_w))

    ta = ATT_TILE
    y_mla = pl.pallas_call(
        _mla_kernel,
        grid=(batch, MLA_HEADS // 2, seq // ta),
        in_specs=[pl.BlockSpec((1, ta, 2 * HEAD_PAD), lambda b, hp, i: (b, i, hp)),
                  pl.BlockSpec((1, seq, 2 * HEAD_PAD), lambda b, hp, i: (b, 0, hp)),
                  pl.BlockSpec((1, seq, 2 * MLA_DV), lambda b, hp, i: (b, 0, hp))],
        out_specs=pl.BlockSpec((1, ta, 2 * MLA_DV), lambda b, hp, i: (b, i, hp)),
        out_shape=jax.ShapeDtypeStruct((batch, seq, dmv), BF16),
        compiler_params=_params("parallel", "parallel", "arbitrary"),
        name="mla",
    )(seq3(q), seq3(k), seq3(v))

    h, hn, route, counts = pl.pallas_call(
        _router_kernel,
        grid=(n // tp,),
        in_specs=[_rows(tp, d), _rows(tp, dv), _rows(tp, dmv), _full(w_out.shape), _full((1, d)),
                  _full(w_router.shape), _full((1, LANES))],
        out_specs=[_rows(tp, d), _rows(tp, d // 2), _rows(tp, LANES), _full((1, LANES))],
        out_shape=[jax.ShapeDtypeStruct((n, d), F32), jax.ShapeDtypeStruct((n, d // 2), jnp.uint32),
                   jax.ShapeDtypeStruct((n, LANES), F32), jax.ShapeDtypeStruct((1, LANES), F32)],
        scratch_shapes=[pltpu.VMEM((1, LANES), F32)],
        compiler_params=_params("arbitrary"),
        name="out_proj_router",
    )(x2, y_gla.reshape(n, dv), y_mla.reshape(n, dmv), w_out.astype(BF16), row1(ffn_norm_w),
      w_router, b_router)

    blk = EXPERT_BLOCK
    nblk = (n * TOP_K) // blk + N_EXPERTS
    cnt = counts[0, :N_EXPERTS].astype(jnp.int32)
    padded = (cnt + blk - 1) // blk * blk
    pend = jnp.cumsum(padded)
    pstart = pend - padded
    eid = route[:, 0:TOP_K].astype(jnp.int32)
    rank = route[:, 4:4 + TOP_K].astype(jnp.int32)
    pstart_of = jnp.sum(jnp.where(eid[..., None] == jnp.arange(N_EXPERTS, dtype=jnp.int32),
                                  pstart, 0), axis=-1)
    dest = (pstart_of + rank).reshape(n * TOP_K)
    blk_expert = jnp.clip(
        jnp.searchsorted(pend, jnp.arange(nblk, dtype=jnp.int32) * blk, side="right"),
        0, N_EXPERTS - 1).astype(jnp.int32)
    n_used = (pend[-1] // blk).astype(jnp.int32).reshape(1)

    tr = ROW_TILE
    smem_rows = pl.BlockSpec((TOP_K * tr,), lambda i: (i,), memory_space=pltpu.SMEM)
    any_spec = pl.BlockSpec(memory_space=pl.ANY)
    xbuf = pl.pallas_call(
        _dispatch_kernel,
        grid=(n // tr,),
        in_specs=[smem_rows, _rows(tr, d // 2), any_spec],
        out_specs=any_spec,
        out_shape=jax.ShapeDtypeStruct((nblk * blk, d // 2), jnp.uint32),
        scratch_shapes=[pltpu.SemaphoreType.DMA],
        input_output_aliases={2: 0},
        compiler_params=_params("arbitrary"),
        name="dispatch",
    )(dest, hn, jnp.zeros((nblk * blk, d // 2), jnp.uint32))

    def used(j, nu):
        return jnp.minimum(j, nu[0] - 1)

    ybuf = pl.pallas_call(
        _expert_kernel,
        grid_spec=pltpu.PrefetchScalarGridSpec(
            num_scalar_prefetch=2,
            grid=(nblk,),
            in_specs=[
                pl.BlockSpec((blk, d // 2), lambda j, be, nu: (used(j, nu), 0)),
                pl.BlockSpec((1, d, D_EXPERT), lambda j, be, nu: (be[used(j, nu)], 0, 0)),
                pl.BlockSpec((1, d, D_EXPERT), lambda j, be, nu: (be[used(j, nu)], 0, 0)),
                pl.BlockSpec((1, D_EXPERT, d), lambda j, be, nu: (be[used(j, nu)], 0, 0)),
            ],
            out_specs=pl.BlockSpec((blk, d // 2), lambda j, be, nu: (j, 0)),
        ),
        out_shape=jax.ShapeDtypeStruct((nblk * blk, d // 2), jnp.uint32),
        compiler_params=_params("arbitrary"),
        name="experts",
    )(blk_expert, n_used, xbuf, expert_w_gate, expert_w_up, expert_w_down)

    return pl.pallas_call(
        _combine_kernel,
        grid=(n // tr,),
        in_specs=[smem_rows, _rows(tr, d), _rows(tr, LANES), _full((1, d)), any_spec],
        out_specs=_rows(tr, d),
        out_shape=jax.ShapeDtypeStruct((n, d), F32),
        scratch_shapes=[pltpu.VMEM((tr, d // 2), jnp.uint32), pltpu.VMEM((tr, d // 2), jnp.uint32),
                        pltpu.SemaphoreType.DMA],
        compiler_params=_params("arbitrary"),
        name="combine",
    )(dest, h, route, row1(out_norm_w), ybuf)


def kernel(x, positions, attn_norm_w, w_in, gla_gate_up, gla_gate_bias, gla_norm_w, mla_q_norm_w,
           mla_w_uq, mla_kv_norm_w, mla_w_ukv, w_out, ffn_norm_w, router_group_w, router_group_b,
           router_expert_w, router_expert_b, expert_w_gate, expert_w_up, expert_w_down,
           final_norm_w):
    batch, seq, d = x.shape
    depth = w_in.shape[0]
    assert depth == 1, "the final norm is fused into the last layer's combine step"
    out = _layer(x.reshape(batch * seq, d), positions.reshape(batch * seq, 1),
                 attn_norm_w[0], w_in[0], gla_gate_up[0], gla_gate_bias[0], gla_norm_w[0],
                 mla_q_norm_w[0], mla_w_uq[0], mla_kv_norm_w[0], mla_w_ukv[0], w_out[0],
                 ffn_norm_w[0], router_group_w[0], router_group_b[0], router_expert_w[0],
                 router_expert_b[0], expert_w_gate[0], expert_w_up[0], expert_w_down[0],
                 final_norm_w, batch, seq)
    return out.reshape(batch, seq, d)
```

```python
import functools

import jax
import jax.numpy as jnp
from jax import lax
from jax.experimental import pallas as pl
from jax.experimental.pallas import tpu as pltpu

EPS = 1e-6
GLA_HEADS = 4
GLA_DK = 64
GLA_DV = 128
GLA_GATE_RANK = 16
GLA_GATE_NORM = 16.0
GLA_CHUNK = 64
MLA_HEADS = 8
MLA_NOPE = 64
MLA_ROPE = 32
MLA_DV = 64
MLA_Q_RANK = 384
MLA_KV_RANK = 256
ROPE_THETA = 10000.0
N_GROUPS = 4
EXPERTS_PER_GROUP = 8
N_EXPERTS = N_GROUPS * EXPERTS_PER_GROUP
TOP_K = 2
D_EXPERT = 256

LANES = 128
HEAD_PAD = 128
ROPE_HALF = MLA_ROPE // 2
ROPE_LO = MLA_NOPE
ROPE_HI = MLA_NOPE + ROPE_HALF
ROPE_END = MLA_NOPE + MLA_ROPE
GATE_LO = ROPE_END

PROJ_TILE = 512
GLA_TILE = 256
ATT_TILE = 512
ATT_KEY_SHIFT = 0
ATT_HEADS_PER_STEP = 4
ROW_TILE = 256
EXPERT_BLOCK = 256
VMEM_LIMIT = 56 * 1024 * 1024

F32 = jnp.float32
BF16 = jnp.bfloat16
NEG_INF = float("-inf")


def _dot(a, b, precision=None):
    return jnp.dot(a, b, preferred_element_type=F32, precision=precision)


def _dot_nt(a, b):
    return lax.dot_general(a, b, (((1,), (1,)), ((), ())), preferred_element_type=F32)


def _dot_tn(a, b):
    return lax.dot_general(a, b, (((0,), (0,)), ((), ())), preferred_element_type=F32)


def _rms(x, w):
    return x * lax.rsqrt(jnp.mean(x * x, axis=-1, keepdims=True) + EPS) * w


def _pack_bf16_pair(a, b):
    ua = lax.bitcast_convert_type(a.astype(BF16).astype(F32), jnp.uint32)
    ub = lax.bitcast_convert_type(b.astype(BF16).astype(F32), jnp.uint32)
    return (ua >> 16) | (ub & jnp.uint32(0xFFFF0000))


def _unpack_bf16_pair(u):
    a = lax.bitcast_convert_type(u << 16, F32)
    b = lax.bitcast_convert_type(u & jnp.uint32(0xFFFF0000), F32)
    return a, b


def _proj_kernel(x_ref, pos_ref, nw_ref, w1_ref, gu_ref, gb_ref, qnw_ref, wuq_ref, kvnw_ref,
                 wukv_ref, inv_ref,
                 gq_ref, gk_ref, gv_ref, gla_ref, gog_ref, q_ref, k_ref, v_ref, *, q_scale):
    x = x_ref[...]
    xn = _rms(x, nw_ref[...]).astype(BF16)
    proj = _dot(xn, w1_ref[...])
    dq = GLA_HEADS * GLA_DK
    dv = GLA_HEADS * GLA_DV
    o = 0
    gq_ref[...] = (proj[:, o:o + dq] * (GLA_DK ** -0.5)).astype(BF16); o += dq
    gk_ref[...] = proj[:, o:o + dq].astype(BF16); o += dq
    gv_ref[...] = proj[:, o:o + dv].astype(BF16); o += dv
    gog_ref[...] = proj[:, o:o + dv].astype(BF16); o += dv
    cq = proj[:, o:o + MLA_Q_RANK]; o += MLA_Q_RANK
    ckv = proj[:, o:o + MLA_KV_RANK]; o += MLA_KV_RANK
    misc = proj[:, o:o + LANES]

    z = _dot(misc.astype(BF16), gu_ref[...]) + gb_ref[...]
    log_sig = jnp.minimum(z, 0.0) - jnp.log1p(jnp.exp(-jnp.abs(z)))
    gla_ref[...] = log_sig * (1.0 / GLA_GATE_NORM)

    lane = lax.broadcasted_iota(jnp.int32, (x.shape[0], LANES), 1)
    ang = pos_ref[...].astype(F32) * inv_ref[...]
    cosv = jnp.cos(ang)
    sinv = jnp.sin(ang)
    in_lo = (lane >= ROPE_LO) & (lane < ROPE_HI)
    in_hi = (lane >= ROPE_HI) & (lane < ROPE_END)
    c_rope = jnp.where(in_lo | in_hi, cosv, 0.0)
    s_up = jnp.where(in_hi, sinv, 0.0)
    s_dn = jnp.where(in_lo, -sinv, 0.0)
    c_q = jnp.where(lane < MLA_NOPE, 1.0, c_rope)

    def rope(t, c):
        return (t * c + pltpu.roll(t, ROPE_HALF, 1) * s_up
                + pltpu.roll(t, LANES - ROPE_HALF, 1) * s_dn)

    k_rope = rope(misc, c_rope)

    q = _dot(_rms(cq, qnw_ref[...]).astype(BF16), wuq_ref[...])
    kv = _dot(_rms(ckv, kvnw_ref[...]).astype(BF16), wukv_ref[...])
    ones_lane = jnp.where(lane == MLA_DV, 1.0, 0.0)
    for h in range(MLA_HEADS):
        sl = slice(h * HEAD_PAD, (h + 1) * HEAD_PAD)
        vsl = slice((MLA_HEADS + h) * HEAD_PAD, (MLA_HEADS + h + 1) * HEAD_PAD)
        q_ref[:, sl] = (rope(q[:, sl], c_q) * q_scale).astype(BF16)
        k_ref[:, sl] = (kv[:, sl] + k_rope).astype(BF16)
        v_ref[:, sl] = (kv[:, vsl] + ones_lane).astype(BF16)


def _gla_kernel(q_ref, k_ref, v_ref, la_ref, og_ref, nw_ref, o_ref, st_ref):
    t = q_ref.shape[1]
    nchunk = t // GLA_CHUNK

    @pl.when(pl.program_id(1) == 0)
    def _():
        st_ref[...] = jnp.zeros_like(st_ref)

    row = lax.broadcasted_iota(jnp.int32, (t, t), 0)
    col = lax.broadcasted_iota(jnp.int32, (t, t), 1)
    chunk_bits = GLA_CHUNK.bit_length() - 1
    tri = ((row >> chunk_bits) == (col >> chunk_bits)) & (col <= row)
    la = la_ref[0]
    b = _dot(tri.astype(F32), la, precision=lax.Precision.HIGHEST)
    b_last = jnp.concatenate(
        [jnp.broadcast_to(b[(c + 1) * GLA_CHUNK - 1:(c + 1) * GLA_CHUNK], (GLA_CHUNK, b.shape[1]))
         for c in range(nchunk)], axis=0)
    q_e = (q_ref[0].astype(F32) * jnp.exp(b)).astype(BF16)
    kf = k_ref[0].astype(F32)
    k_e = (kf * jnp.exp(-b)).astype(BF16)
    k_d = (kf * jnp.exp(b_last - b)).astype(BF16)
    decay = jnp.exp(b_last)
    nw = nw_ref[...]

    for h in range(GLA_HEADS):
        ks = slice(h * GLA_DK, (h + 1) * GLA_DK)
        vs = slice(h * GLA_DV, (h + 1) * GLA_DV)
        qh, keh, kdh = q_e[:, ks], k_e[:, ks], k_d[:, ks]
        vh = v_ref[0, :, vs]
        att = jnp.where(tri, _dot_nt(qh, keh), 0.0)
        o = _dot(att.astype(BF16), vh)
        state = st_ref[h]
        inter = []
        for c in range(nchunk):
            rs = slice(c * GLA_CHUNK, (c + 1) * GLA_CHUNK)
            inter.append(_dot_nt(qh[rs], state.astype(BF16)))
            upd = _dot_tn(vh[rs], kdh[rs])
            state = state * decay[c * GLA_CHUNK:c * GLA_CHUNK + 1, ks] + upd
        st_ref[h] = state
        o = o + jnp.concatenate(inter, axis=0)
        o = _rms(o, nw)
        g = og_ref[0, :, vs].astype(F32)
        o_ref[0, :, vs] = (o * (g * jax.nn.sigmoid(g))).astype(o_ref.dtype)


def _mla_kernel(q_ref, k_ref, v_ref, o_ref, *, key_shift):
    tq = q_ref.shape[1]
    tk = tq << key_shift
    heads = q_ref.shape[2] // HEAD_PAD
    qi = pl.program_id(2)
    lane = lax.broadcasted_iota(jnp.int32, (tq, LANES), 1)

    def step(j, carry, mask):
        rows = pl.ds(pl.multiple_of(j * tk, tk), tk)
        new = []
        for hh in range(heads):
            m, acc = carry[hh]
            hs = slice(hh * HEAD_PAD, (hh + 1) * HEAD_PAD)
            s = _dot_nt(q_ref[0, :, hs], k_ref[0, rows, hs])
            if mask:
                r = lax.broadcasted_iota(jnp.int32, (tq, tk), 0) + qi * tq
                c = lax.broadcasted_iota(jnp.int32, (tq, tk), 1) + j * tk
                s = jnp.where(c <= r, s, NEG_INF)
            m_new = jnp.maximum(m, jnp.max(s, axis=-1, keepdims=True))
            p = jnp.exp(s - m_new).astype(BF16)
            acc = jnp.exp(m - m_new) * acc + _dot(p, v_ref[0, rows, hs])
            new.append((m_new, acc))
        return tuple(new)

    init = tuple((jnp.full((tq, 1), NEG_INF, F32), jnp.zeros((tq, LANES), F32))
                 for _ in range(heads))
    nfull = lax.shift_right_logical(qi, key_shift)
    carry = lax.fori_loop(0, nfull, lambda j, c: step(j, c, False), init)
    carry = step(nfull, carry, True)
    for hp in range(heads // 2):
        o0, o1 = (acc / acc[:, MLA_DV:MLA_DV + 1] for _, acc in carry[2 * hp:2 * hp + 2])
        o_ref[0, :, hp * LANES:(hp + 1) * LANES] = jnp.where(
            lane < MLA_DV, o0, pltpu.roll(o1, MLA_DV, 1)).astype(o_ref.dtype)


def _router_kernel(x_ref, yg_ref, ym_ref, wo_ref, fnw_ref, wr_ref, br_ref,
                   h_ref, hn_ref, route_ref, cnt_ref, carry_ref):
    t = x_ref.shape[0]
    half = wo_ref.shape[0] // 2

    @pl.when(pl.program_id(0) == 0)
    def _():
        carry_ref[...] = jnp.zeros_like(carry_ref)

    h = x_ref[...] + _dot(yg_ref[...], wo_ref[:half]) + _dot(ym_ref[...], wo_ref[half:])
    h_ref[...] = h
    hn = _rms(h, fnw_ref[...])
    hp = hn.shape[1] // 2
    hn_ref[...] = _pack_bf16_pair(hn[:, :hp], hn[:, hp:])

    logits = _dot(hn, wr_ref[...], precision=lax.Precision.HIGHEST) + br_ref[...]
    lane = lax.broadcasted_iota(jnp.int32, (t, LANES), 1)
    lane_f = lane.astype(F32)

    def first_argmax(vals, vmax):
        idx = jnp.min(jnp.where(vals == vmax, lane_f, float(LANES)), axis=-1, keepdims=True)
        return idx.astype(jnp.int32)

    gl = jnp.where((lane >= N_EXPERTS) & (lane < N_EXPERTS + N_GROUPS), logits, NEG_INF)
    gmax = jnp.max(gl, axis=-1, keepdims=True)
    gsel = first_argmax(gl, gmax) - N_EXPERTS
    p_g = 1.0 / jnp.sum(jnp.exp(gl - gmax), axis=-1, keepdims=True)
    lo = gsel * EXPERTS_PER_GROUP
    el = jnp.where((lane >= lo) & (lane < lo + EXPERTS_PER_GROUP), logits, NEG_INF)
    m1 = jnp.max(el, axis=-1, keepdims=True)
    i1 = first_argmax(el, m1)
    el2 = jnp.where(lane == i1, NEG_INF, el)
    m2 = jnp.max(el2, axis=-1, keepdims=True)
    i2 = first_argmax(el2, m2)
    e2 = jnp.exp(m2 - m1)
    g1 = p_g / (1.0 + e2)
    g2 = p_g * e2 / (1.0 + e2)

    is1 = lane == i1
    is2 = lane == i2
    onehot = (is1 | is2).astype(BF16)
    rr = lax.broadcasted_iota(jnp.int32, (t, t), 0)
    cc = lax.broadcasted_iota(jnp.int32, (t, t), 1)
    before = _dot((cc < rr).astype(BF16), onehot) + carry_ref[...]
    r1 = jnp.sum(jnp.where(is1, before, 0.0), axis=-1, keepdims=True)
    r2 = jnp.sum(jnp.where(is2, before, 0.0), axis=-1, keepdims=True)
    carry_ref[...] = carry_ref[...] + jnp.sum(onehot.astype(F32), axis=0, keepdims=True)
    cnt_ref[...] = carry_ref[...]

    route = jnp.where(lane == 0, i1.astype(F32), 0.0)
    route = jnp.where(lane == 1, i2.astype(F32), route)
    route = jnp.where(lane == 2, g1, route)
    route = jnp.where(lane == 3, g2, route)
    route = jnp.where(lane == 4, r1, route)
    route = jnp.where(lane == 5, r2, route)
    route_ref[...] = route


def _dispatch_kernel(dest_ref, hn_ref, xbuf_in_ref, xbuf_ref, sem):
    del xbuf_in_ref
    t = hn_ref.shape[0]

    def row_copy(tok, dst):
        return pltpu.make_async_copy(hn_ref.at[pl.ds(tok, 1)], xbuf_ref.at[pl.ds(dst, 1)], sem)

    def issue(i, c):
        for kk in range(TOP_K):
            row_copy(i, dest_ref[TOP_K * i + kk]).start()
        return c

    lax.fori_loop(0, t, issue, 0)

    def drain(i, c):
        row_copy(0, 0).wait()
        return c

    lax.fori_loop(0, TOP_K * t, drain, 0)


def _expert_kernel(be_ref, nu_ref, x_ref, wg_ref, wu_ref, wd_ref, y_ref):
    del be_ref

    @pl.when(pl.program_id(0) < nu_ref[0])
    def _():
        a, b = _unpack_bf16_pair(x_ref[...])
        x = jnp.concatenate([a, b], axis=1).astype(BF16)
        h1 = _dot(x, wg_ref[0].astype(BF16))
        h2 = _dot(x, wu_ref[0].astype(BF16))
        hdn = (h1 * jax.nn.sigmoid(h1) * h2).astype(BF16)
        y = _dot(hdn, wd_ref[0].astype(BF16))
        hp = y.shape[1] // 2
        y_ref[...] = _pack_bf16_pair(y[:, :hp], y[:, hp:])

    @pl.when(pl.program_id(0) >= nu_ref[0])
    def _():
        y_ref[...] = jnp.zeros_like(y_ref)


def _combine_kernel(dest_ref, h_ref, route_ref, fw_ref, ybuf_ref, o_ref, y0_ref, y1_ref, sem):
    t = h_ref.shape[0]
    bufs = (y0_ref, y1_ref)

    def row_copy(src, tok, kk):
        return pltpu.make_async_copy(ybuf_ref.at[pl.ds(src, 1)], bufs[kk].at[pl.ds(tok, 1)], sem)

    def issue(i, c):
        for kk in range(TOP_K):
            row_copy(dest_ref[TOP_K * i + kk], i, kk).start()
        return c

    lax.fori_loop(0, t, issue, 0)

    def drain(i, c):
        row_copy(0, 0, 0).wait()
        return c

    lax.fori_loop(0, TOP_K * t, drain, 0)

    route = route_ref[...]
    g1 = route[:, 2:3]
    g2 = route[:, 3:4]
    a0, b0 = _unpack_bf16_pair(y0_ref[...])
    a1, b1 = _unpack_bf16_pair(y1_ref[...])
    moe = jnp.concatenate([a0 * g1 + a1 * g2, b0 * g1 + b1 * g2], axis=1)
    o_ref[...] = _rms(h_ref[...] + moe, fw_ref[...])


def _params(*sem):
    return pltpu.CompilerParams(dimension_semantics=sem, vmem_limit_bytes=VMEM_LIMIT)


def _full(shape):
    return pl.BlockSpec(shape, lambda *_: (0,) * len(shape))


def _rows(tile, width):
    return pl.BlockSpec((tile, width), lambda i: (i, 0))


def _layer(x2, pos2, attn_norm_w, w_in, gla_gate_up, gla_gate_bias, gla_norm_w, mla_q_norm_w,
           mla_w_uq, mla_kv_norm_w, mla_w_ukv, w_out, ffn_norm_w, router_group_w, router_group_b,
           router_expert_w, router_expert_b, expert_w_gate, expert_w_up, expert_w_down,
           out_norm_w, batch, seq):
    n, d = x2.shape
    dq = GLA_HEADS * GLA_DK
    dv = GLA_HEADS * GLA_DV
    dmla = MLA_HEADS * HEAD_PAD
    dmv = MLA_HEADS * MLA_DV

    c_gq, c_gk, c_gv, c_lr, c_og, c_cq, c_ckv, c_kr = jnp.split(
        w_in, [dq, 2 * dq, 2 * dq + dv, 2 * dq + dv + GLA_GATE_RANK,
               2 * dq + 2 * dv + GLA_GATE_RANK,
               2 * dq + 2 * dv + GLA_GATE_RANK + MLA_Q_RANK,
               2 * dq + 2 * dv + GLA_GATE_RANK + MLA_Q_RANK + MLA_KV_RANK], axis=1)
    zeros = lambda r, c: jnp.zeros((r, c), w_in.dtype)
    misc = jnp.concatenate([zeros(d, MLA_NOPE), c_kr, c_lr,
                            zeros(d, LANES - ROPE_END - GLA_GATE_RANK)], axis=1)
    w1 = jnp.concatenate([c_gq, c_gk, c_gv, c_og, c_cq, c_ckv, misc], axis=1).astype(BF16)
    gate_up = jnp.concatenate([zeros(GATE_LO, dq), gla_gate_up,
                               zeros(LANES - GATE_LO - GLA_GATE_RANK, dq)], axis=0).astype(BF16)
    wuq = mla_w_uq.reshape(MLA_Q_RANK, MLA_HEADS, MLA_NOPE + MLA_ROPE)
    wuq = jnp.pad(wuq, ((0, 0), (0, 0), (0, HEAD_PAD - ROPE_END))).reshape(MLA_Q_RANK, dmla)
    wukv = mla_w_ukv.reshape(MLA_KV_RANK, MLA_HEADS, MLA_NOPE + MLA_DV)
    wuk = jnp.pad(wukv[:, :, :MLA_NOPE], ((0, 0), (0, 0), (0, HEAD_PAD - MLA_NOPE)))
    wuv = jnp.pad(wukv[:, :, MLA_NOPE:], ((0, 0), (0, 0), (0, HEAD_PAD - MLA_DV)))
    wukv = jnp.concatenate([wuk.reshape(MLA_KV_RANK, dmla), wuv.reshape(MLA_KV_RANK, dmla)],
                           axis=1)
    inv = ROPE_THETA ** (-jnp.arange(ROPE_HALF, dtype=F32) / ROPE_HALF)
    inv_pat = jnp.concatenate([jnp.zeros((MLA_NOPE,), F32), inv, inv,
                               jnp.zeros((LANES - ROPE_END,), F32)]).reshape(1, LANES)
    w_router = jnp.concatenate(
        [router_expert_w, router_group_w, zeros(d, LANES - N_EXPERTS - N_GROUPS)], axis=1)
    b_router = jnp.concatenate(
        [router_expert_b, router_group_b, jnp.zeros((LANES - N_EXPERTS - N_GROUPS,), F32)]
    ).reshape(1, LANES)
    row1 = lambda v: v.reshape(1, -1)

    tp = PROJ_TILE
    outs = pl.pallas_call(
        functools.partial(_proj_kernel, q_scale=(MLA_NOPE + MLA_ROPE) ** -0.5),
        grid=(n // tp,),
        in_specs=[_rows(tp, d), _rows(tp, 1), _full((1, d)), _full(w1.shape), _full(gate_up.shape),
                  _full((1, dq)), _full((1, MLA_Q_RANK)), _full(wuq.shape),
                  _full((1, MLA_KV_RANK)), _full(wukv.shape), _full((1, LANES))],
        out_specs=[_rows(tp, dq), _rows(tp, dq), _rows(tp, dv), _rows(tp, dq), _rows(tp, dv),
                   _rows(tp, dmla), _rows(tp, dmla), _rows(tp, dmla)],
        out_shape=[jax.ShapeDtypeStruct((n, dq), BF16), jax.ShapeDtypeStruct((n, dq), BF16),
                   jax.ShapeDtypeStruct((n, dv), BF16), jax.ShapeDtypeStruct((n, dq), F32),
                   jax.ShapeDtypeStruct((n, dv), BF16), jax.ShapeDtypeStruct((n, dmla), BF16),
                   jax.ShapeDtypeStruct((n, dmla), BF16), jax.ShapeDtypeStruct((n, dmla), BF16)],
        compiler_params=_params("parallel"),
        name="in_proj",
    )(x2, pos2, row1(attn_norm_w), w1, gate_up, row1(gla_gate_bias), row1(mla_q_norm_w),
      wuq.astype(BF16), row1(mla_kv_norm_w), wukv.astype(BF16), inv_pat)
    gq, gk, gv, gla, gog, q, k, v = outs

    tg = GLA_TILE
    seq3 = lambda a: a.reshape(batch, seq, a.shape[-1])
    gspec = lambda w: pl.BlockSpec((1, tg, w), lambda b, i: (b, i, 0))
    y_gla = pl.pallas_call(
        _gla_kernel,
        grid=(batch, seq // tg),
        in_specs=[gspec(dq), gspec(dq), gspec(dv), gspec(dq), gspec(dv), _full((1, GLA_DV))],
        out_specs=gspec(dv),
        out_shape=jax.ShapeDtypeStruct((batch, seq, dv), BF16),
        scratch_shapes=[pltpu.VMEM((GLA_HEADS, GLA_DV, GLA_DK), F32)],
        compiler_params=_params("parallel", "arbitrary"),
        name="gla",
    )(seq3(gq), seq3(gk), seq3(gv), seq3(gla), seq3(gog), row1(gla_norm_w))

    ta = ATT_TILE
    hps = ATT_HEADS_PER_STEP
    y_mla = pl.pallas_call(
        functools.partial(_mla_kernel, key_shift=ATT_KEY_SHIFT),
        grid=(batch, MLA_HEADS // hps, seq // ta),
        in_specs=[pl.BlockSpec((1, ta, hps * HEAD_PAD), lambda b, hp, i: (b, i, hp)),
                  pl.BlockSpec((1, seq, hps * HEAD_PAD), lambda b, hp, i: (b, 0, hp)),
                  pl.BlockSpec((1, seq, hps * HEAD_PAD), lambda b, hp, i: (b, 0, hp))],
        out_specs=pl.BlockSpec((1, ta, hps * MLA_DV), lambda b, hp, i: (b, i, hp)),
        out_shape=jax.ShapeDtypeStruct((batch, seq, dmv), BF16),
        compiler_params=_params("parallel", "parallel", "arbitrary"),
        name="mla",
    )(seq3(q), seq3(k), seq3(v))

    h, hn, route, counts = pl.pallas_call(
        _router_kernel,
        grid=(n // tp,),
        in_specs=[_rows(tp, d), _rows(tp, dv), _rows(tp, dmv), _full(w_out.shape), _full((1, d)),
                  _full(w_router.shape), _full((1, LANES))],
        out_specs=[_rows(tp, d), _rows(tp, d // 2), _rows(tp, LANES), _full((1, LANES))],
        out_shape=[jax.ShapeDtypeStruct((n, d), F32), jax.ShapeDtypeStruct((n, d // 2), jnp.uint32),
                   jax.ShapeDtypeStruct((n, LANES), F32), jax.ShapeDtypeStruct((1, LANES), F32)],
        scratch_shapes=[pltpu.VMEM((1, LANES), F32)],
        compiler_params=_params("arbitrary"),
        name="out_proj_router",
    )(x2, y_gla.reshape(n, dv), y_mla.reshape(n, dmv), w_out.astype(BF16), row1(ffn_norm_w),
      w_router, b_router)

    blk = EXPERT_BLOCK
    nblk = (n * TOP_K) // blk + N_EXPERTS
    cnt = counts[0, :N_EXPERTS].astype(jnp.int32)
    padded = (cnt + blk - 1) // blk * blk
    pend = jnp.cumsum(padded)
    pstart = pend - padded
    eid = route[:, 0:TOP_K].astype(jnp.int32)
    rank = route[:, 4:4 + TOP_K].astype(jnp.int32)
    pstart_of = jnp.sum(jnp.where(eid[..., None] == jnp.arange(N_EXPERTS, dtype=jnp.int32),
                                  pstart, 0), axis=-1)
    dest = (pstart_of + rank).reshape(n * TOP_K)
    blk_start = jnp.arange(nblk, dtype=jnp.int32) * blk
    blk_expert = jnp.minimum(
        jnp.sum((pend[None, :] <= blk_start[:, None]).astype(jnp.int32), axis=1), N_EXPERTS - 1)
    n_used = (pend[-1] // blk).astype(jnp.int32).reshape(1)

    tr = ROW_TILE
    smem_rows = pl.BlockSpec((TOP_K * tr,), lambda i: (i,), memory_space=pltpu.SMEM)
    any_spec = pl.BlockSpec(memory_space=pl.ANY)
    xbuf = pl.pallas_call(
        _dispatch_kernel,
        grid=(n // tr,),
        in_specs=[smem_rows, _rows(tr, d // 2), any_spec],
        out_specs=any_spec,
        out_shape=jax.ShapeDtypeStruct((nblk * blk, d // 2), jnp.uint32),
        scratch_shapes=[pltpu.SemaphoreType.DMA],
        input_output_aliases={2: 0},
        compiler_params=_params("arbitrary"),
        name="dispatch",
    )(dest, hn, jnp.zeros((nblk * blk, d // 2), jnp.uint32))

    def used(j, nu):
        return jnp.minimum(j, nu[0] - 1)

    ybuf = pl.pallas_call(
        _expert_kernel,
        grid_spec=pltpu.PrefetchScalarGridSpec(
            num_scalar_prefetch=2,
            grid=(nblk,),
            in_specs=[
                pl.BlockSpec((blk, d // 2), lambda j, be, nu: (used(j, nu), 0)),
                pl.BlockSpec((1, d, D_EXPERT), lambda j, be, nu: (be[used(j, nu)], 0, 0)),
                pl.BlockSpec((1, d, D_EXPERT), lambda j, be, nu: (be[used(j, nu)], 0, 0)),
                pl.BlockSpec((1, D_EXPERT, d), lambda j, be, nu: (be[used(j, nu)], 0, 0)),
            ],
            out_specs=pl.BlockSpec((blk, d // 2), lambda j, be, nu: (j, 0)),
        ),
        out_shape=jax.ShapeDtypeStruct((nblk * blk, d // 2), jnp.uint32),
        compiler_params=_params("arbitrary"),
        name="experts",
    )(blk_expert, n_used, xbuf, expert_w_gate, expert_w_up, expert_w_down)

    return pl.pallas_call(
        _combine_kernel,
        grid=(n // tr,),
        in_specs=[smem_rows, _rows(tr, d), _rows(tr, LANES), _full((1, d)), any_spec],
        out_specs=_rows(tr, d),
        out_shape=jax.ShapeDtypeStruct((n, d), F32),
        scratch_shapes=[pltpu.VMEM((tr, d // 2), jnp.uint32), pltpu.VMEM((tr, d // 2), jnp.uint32),
                        pltpu.SemaphoreType.DMA],
        compiler_params=_params("arbitrary"),
        name="combine",
    )(dest, h, route, row1(out_norm_w), ybuf)


def kernel(x, positions, attn_norm_w, w_in, gla_gate_up, gla_gate_bias, gla_norm_w, mla_q_norm_w,
           mla_w_uq, mla_kv_norm_w, mla_w_ukv, w_out, ffn_norm_w, router_group_w, router_group_b,
           router_expert_w, router_expert_b, expert_w_gate, expert_w_up, expert_w_down,
           final_norm_w):
    batch, seq, d = x.shape
    depth = w_in.shape[0]
    assert depth == 1, "the final norm is fused into the last layer's combine step"
    out = _layer(x.reshape(batch * seq, d), positions.reshape(batch * seq, 1),
                 attn_norm_w[0], w_in[0], gla_gate_up[0], gla_gate_bias[0], gla_norm_w[0],
                 mla_q_norm_w[0], mla_w_uq[0], mla_kv_norm_w[0], mla_w_ukv[0], w_out[0],
                 ffn_norm_w[0], router_group_w[0], router_group_b[0], router_expert_w[0],
                 router_expert_b[0], expert_w_gate[0], expert_w_up[0], expert_w_down[0],
                 final_norm_w, batch, seq)
    return out.reshape(batch, seq, d)
```

```python
import functools

import jax
import jax.numpy as jnp
from jax import lax
from jax.experimental import pallas as pl
from jax.experimental.pallas import tpu as pltpu

EPS = 1e-6
GLA_HEADS = 4
GLA_DK = 64
GLA_DV = 128
GLA_GATE_RANK = 16
GLA_GATE_NORM = 16.0
GLA_CHUNK = 64
MLA_HEADS = 8
MLA_NOPE = 64
MLA_ROPE = 32
MLA_DV = 64
MLA_Q_RANK = 384
MLA_KV_RANK = 256
ROPE_THETA = 10000.0
N_GROUPS = 4
EXPERTS_PER_GROUP = 8
N_EXPERTS = N_GROUPS * EXPERTS_PER_GROUP
TOP_K = 2
D_EXPERT = 256

LANES = 128
HEAD_PAD = 128
ROPE_HALF = MLA_ROPE // 2
ROPE_LO = MLA_NOPE
ROPE_HI = MLA_NOPE + ROPE_HALF
ROPE_END = MLA_NOPE + MLA_ROPE
GATE_LO = ROPE_END

PROJ_TILE = 512
GLA_TILE = 256
ATT_TILE = 512
ATT_KEY_SHIFT = 0
ATT_HEADS_PER_STEP = 4
ROW_TILE = 256
ROW_UNROLL = 8
EXPERT_BLOCK = 256
VMEM_LIMIT = 56 * 1024 * 1024

F32 = jnp.float32
BF16 = jnp.bfloat16
NEG_INF = float("-inf")


def _dot(a, b, precision=None):
    return jnp.dot(a, b, preferred_element_type=F32, precision=precision)


def _dot_nt(a, b):
    return lax.dot_general(a, b, (((1,), (1,)), ((), ())), preferred_element_type=F32)


def _dot_tn(a, b):
    return lax.dot_general(a, b, (((0,), (0,)), ((), ())), preferred_element_type=F32)


def _rms(x, w):
    return x * lax.rsqrt(jnp.mean(x * x, axis=-1, keepdims=True) + EPS) * w


def _pack_bf16_pair(a, b):
    ua = lax.bitcast_convert_type(a.astype(BF16).astype(F32), jnp.uint32)
    ub = lax.bitcast_convert_type(b.astype(BF16).astype(F32), jnp.uint32)
    return (ua >> 16) | (ub & jnp.uint32(0xFFFF0000))


def _unpack_bf16_pair(u):
    a = lax.bitcast_convert_type(u << 16, F32)
    b = lax.bitcast_convert_type(u & jnp.uint32(0xFFFF0000), F32)
    return a, b


def _proj_kernel(x_ref, pos_ref, nw_ref, w1_ref, gu_ref, gb_ref, qnw_ref, wuq_ref, kvnw_ref,
                 wukv_ref, inv_ref,
                 gq_ref, gk_ref, gv_ref, gla_ref, gog_ref, q_ref, k_ref, v_ref, *, q_scale):
    x = x_ref[...]
    xn = _rms(x, nw_ref[...]).astype(BF16)
    proj = _dot(xn, w1_ref[...])
    dq = GLA_HEADS * GLA_DK
    dv = GLA_HEADS * GLA_DV
    o = 0
    gq_ref[...] = (proj[:, o:o + dq] * (GLA_DK ** -0.5)).astype(BF16); o += dq
    gk_ref[...] = proj[:, o:o + dq].astype(BF16); o += dq
    gv_ref[...] = proj[:, o:o + dv].astype(BF16); o += dv
    gog_ref[...] = proj[:, o:o + dv].astype(BF16); o += dv
    cq = proj[:, o:o + MLA_Q_RANK]; o += MLA_Q_RANK
    ckv = proj[:, o:o + MLA_KV_RANK]; o += MLA_KV_RANK
    misc = proj[:, o:o + LANES]

    z = _dot(misc.astype(BF16), gu_ref[...]) + gb_ref[...]
    log_sig = jnp.minimum(z, 0.0) - jnp.log1p(jnp.exp(-jnp.abs(z)))
    gla_ref[...] = log_sig * (1.0 / GLA_GATE_NORM)

    lane = lax.broadcasted_iota(jnp.int32, (x.shape[0], LANES), 1)
    ang = pos_ref[...].astype(F32) * inv_ref[...]
    cosv = jnp.cos(ang)
    sinv = jnp.sin(ang)
    in_lo = (lane >= ROPE_LO) & (lane < ROPE_HI)
    in_hi = (lane >= ROPE_HI) & (lane < ROPE_END)
    c_rope = jnp.where(in_lo | in_hi, cosv, 0.0)
    s_up = jnp.where(in_hi, sinv, 0.0)
    s_dn = jnp.where(in_lo, -sinv, 0.0)
    c_q = jnp.where(lane < MLA_NOPE, 1.0, c_rope)

    def rope(t, c):
        return (t * c + pltpu.roll(t, ROPE_HALF, 1) * s_up
                + pltpu.roll(t, LANES - ROPE_HALF, 1) * s_dn)

    k_rope = rope(misc, c_rope)

    q = _dot(_rms(cq, qnw_ref[...]).astype(BF16), wuq_ref[...])
    kv = _dot(_rms(ckv, kvnw_ref[...]).astype(BF16), wukv_ref[...])
    ones_lane = jnp.where(lane == MLA_DV, 1.0, 0.0)
    for h in range(MLA_HEADS):
        sl = slice(h * HEAD_PAD, (h + 1) * HEAD_PAD)
        vsl = slice((MLA_HEADS + h) * HEAD_PAD, (MLA_HEADS + h + 1) * HEAD_PAD)
        q_ref[:, sl] = (rope(q[:, sl], c_q) * q_scale).astype(BF16)
        k_ref[:, sl] = (kv[:, sl] + k_rope).astype(BF16)
        v_ref[:, sl] = (kv[:, vsl] + ones_lane).astype(BF16)


def _gla_kernel(q_ref, k_ref, v_ref, la_ref, og_ref, nw_ref, o_ref, st_ref):
    t = q_ref.shape[1]
    nchunk = t // GLA_CHUNK

    @pl.when(pl.program_id(1) == 0)
    def _():
        st_ref[...] = jnp.zeros_like(st_ref)

    row = lax.broadcasted_iota(jnp.int32, (t, t), 0)
    col = lax.broadcasted_iota(jnp.int32, (t, t), 1)
    chunk_bits = GLA_CHUNK.bit_length() - 1
    tri = ((row >> chunk_bits) == (col >> chunk_bits)) & (col <= row)
    la = la_ref[0]
    b = _dot(tri.astype(F32), la, precision=lax.Precision.HIGHEST)
    b_last = jnp.concatenate(
        [jnp.broadcast_to(b[(c + 1) * GLA_CHUNK - 1:(c + 1) * GLA_CHUNK], (GLA_CHUNK, b.shape[1]))
         for c in range(nchunk)], axis=0)
    q_e = (q_ref[0].astype(F32) * jnp.exp(b)).astype(BF16)
    kf = k_ref[0].astype(F32)
    k_e = (kf * jnp.exp(-b)).astype(BF16)
    k_d = (kf * jnp.exp(b_last - b)).astype(BF16)
    decay = jnp.exp(b_last)
    nw = nw_ref[...]

    for h in range(GLA_HEADS):
        ks = slice(h * GLA_DK, (h + 1) * GLA_DK)
        vs = slice(h * GLA_DV, (h + 1) * GLA_DV)
        qh, keh, kdh = q_e[:, ks], k_e[:, ks], k_d[:, ks]
        vh = v_ref[0, :, vs]
        att = jnp.where(tri, _dot_nt(qh, keh), 0.0)
        o = _dot(att.astype(BF16), vh)
        state = st_ref[h]
        inter = []
        for c in range(nchunk):
            rs = slice(c * GLA_CHUNK, (c + 1) * GLA_CHUNK)
            inter.append(_dot_nt(qh[rs], state.astype(BF16)))
            upd = _dot_tn(vh[rs], kdh[rs])
            state = state * decay[c * GLA_CHUNK:c * GLA_CHUNK + 1, ks] + upd
        st_ref[h] = state
        o = o + jnp.concatenate(inter, axis=0)
        o = _rms(o, nw)
        g = og_ref[0, :, vs].astype(F32)
        o_ref[0, :, vs] = (o * (g * jax.nn.sigmoid(g))).astype(o_ref.dtype)


def _mla_kernel(q_ref, k_ref, v_ref, o_ref, *, key_shift):
    tq = q_ref.shape[1]
    tk = tq << key_shift
    heads = q_ref.shape[2] // HEAD_PAD
    qi = pl.program_id(2)
    lane = lax.broadcasted_iota(jnp.int32, (tq, LANES), 1)

    def step(j, carry, mask):
        rows = pl.ds(pl.multiple_of(j * tk, tk), tk)
        new = []
        for hh in range(heads):
            m, acc = carry[hh]
            hs = slice(hh * HEAD_PAD, (hh + 1) * HEAD_PAD)
            s = _dot_nt(q_ref[0, :, hs], k_ref[0, rows, hs])
            if mask:
                r = lax.broadcasted_iota(jnp.int32, (tq, tk), 0) + qi * tq
                c = lax.broadcasted_iota(jnp.int32, (tq, tk), 1) + j * tk
                s = jnp.where(c <= r, s, NEG_INF)
            m_new = jnp.maximum(m, jnp.max(s, axis=-1, keepdims=True))
            p = jnp.exp(s - m_new).astype(BF16)
            acc = jnp.exp(m - m_new) * acc + _dot(p, v_ref[0, rows, hs])
            new.append((m_new, acc))
        return tuple(new)

    init = tuple((jnp.full((tq, 1), NEG_INF, F32), jnp.zeros((tq, LANES), F32))
                 for _ in range(heads))
    nfull = lax.shift_right_logical(qi, key_shift)
    carry = lax.fori_loop(0, nfull, lambda j, c: step(j, c, False), init)
    carry = step(nfull, carry, True)
    for hp in range(heads // 2):
        o0, o1 = (acc / acc[:, MLA_DV:MLA_DV + 1] for _, acc in carry[2 * hp:2 * hp + 2])
        o_ref[0, :, hp * LANES:(hp + 1) * LANES] = jnp.where(
            lane < MLA_DV, o0, pltpu.roll(o1, MLA_DV, 1)).astype(o_ref.dtype)


def _router_kernel(x_ref, yg_ref, ym_ref, wo_ref, fnw_ref, wr_ref, br_ref,
                   h_ref, hn_ref, route_ref, cnt_ref, carry_ref):
    t = x_ref.shape[0]
    half = wo_ref.shape[0] // 2

    @pl.when(pl.program_id(0) == 0)
    def _():
        carry_ref[...] = jnp.zeros_like(carry_ref)

    h = x_ref[...] + _dot(yg_ref[...], wo_ref[:half]) + _dot(ym_ref[...], wo_ref[half:])
    h_ref[...] = h
    hn = _rms(h, fnw_ref[...])
    hp = hn.shape[1] // 2
    hn_ref[...] = _pack_bf16_pair(hn[:, :hp], hn[:, hp:])

    logits = _dot(hn, wr_ref[...], precision=lax.Precision.HIGHEST) + br_ref[...]
    lane = lax.broadcasted_iota(jnp.int32, (t, LANES), 1)
    lane_f = lane.astype(F32)

    def first_argmax(vals, vmax):
        idx = jnp.min(jnp.where(vals == vmax, lane_f, float(LANES)), axis=-1, keepdims=True)
        return idx.astype(jnp.int32)

    gl = jnp.where((lane >= N_EXPERTS) & (lane < N_EXPERTS + N_GROUPS), logits, NEG_INF)
    gmax = jnp.max(gl, axis=-1, keepdims=True)
    gsel = first_argmax(gl, gmax) - N_EXPERTS
    p_g = 1.0 / jnp.sum(jnp.exp(gl - gmax), axis=-1, keepdims=True)
    lo = gsel * EXPERTS_PER_GROUP
    el = jnp.where((lane >= lo) & (lane < lo + EXPERTS_PER_GROUP), logits, NEG_INF)
    m1 = jnp.max(el, axis=-1, keepdims=True)
    i1 = first_argmax(el, m1)
    el2 = jnp.where(lane == i1, NEG_INF, el)
    m2 = jnp.max(el2, axis=-1, keepdims=True)
    i2 = first_argmax(el2, m2)
    e2 = jnp.exp(m2 - m1)
    g1 = p_g / (1.0 + e2)
    g2 = p_g * e2 / (1.0 + e2)

    is1 = lane == i1
    is2 = lane == i2
    onehot = (is1 | is2).astype(BF16)
    rr = lax.broadcasted_iota(jnp.int32, (t, t), 0)
    cc = lax.broadcasted_iota(jnp.int32, (t, t), 1)
    before = _dot((cc < rr).astype(BF16), onehot) + carry_ref[...]
    r1 = jnp.sum(jnp.where(is1, before, 0.0), axis=-1, keepdims=True)
    r2 = jnp.sum(jnp.where(is2, before, 0.0), axis=-1, keepdims=True)
    carry_ref[...] = carry_ref[...] + jnp.sum(onehot.astype(F32), axis=0, keepdims=True)
    cnt_ref[...] = carry_ref[...]

    route = jnp.where(lane == 0, i1.astype(F32), 0.0)
    route = jnp.where(lane == 1, i2.astype(F32), route)
    route = jnp.where(lane == 2, g1, route)
    route = jnp.where(lane == 3, g2, route)
    route = jnp.where(lane == 4, r1, route)
    route = jnp.where(lane == 5, r2, route)
    route_ref[...] = route


def _dispatch_kernel(dest_ref, hn_ref, xbuf_in_ref, xbuf_ref, sem):
    del xbuf_in_ref
    t = hn_ref.shape[0]

    def row_copy(tok, dst):
        return pltpu.make_async_copy(hn_ref.at[pl.ds(tok, 1)], xbuf_ref.at[pl.ds(dst, 1)], sem)

    def issue(i, c):
        for kk in range(TOP_K):
            row_copy(i, dest_ref[TOP_K * i + kk]).start(priority=kk)
        return c

    lax.fori_loop(0, t, issue, 0, unroll=ROW_UNROLL)
    for _ in range(TOP_K):
        pltpu.make_async_copy(hn_ref, xbuf_ref.at[pl.ds(0, t)], sem).wait()


def _expert_kernel(be_ref, nu_ref, x_ref, wg_ref, wu_ref, wd_ref, y_ref):
    del be_ref

    @pl.when(pl.program_id(0) < nu_ref[0])
    def _():
        a, b = _unpack_bf16_pair(x_ref[...])
        x = jnp.concatenate([a, b], axis=1).astype(BF16)
        h1 = _dot(x, wg_ref[0].astype(BF16))
        h2 = _dot(x, wu_ref[0].astype(BF16))
        hdn = (h1 * jax.nn.sigmoid(h1) * h2).astype(BF16)
        y = _dot(hdn, wd_ref[0].astype(BF16))
        hp = y.shape[1] // 2
        y_ref[...] = _pack_bf16_pair(y[:, :hp], y[:, hp:])

    @pl.when(pl.program_id(0) >= nu_ref[0])
    def _():
        y_ref[...] = jnp.zeros_like(y_ref)


def _combine_kernel(dest_ref, h_ref, route_ref, fw_ref, ybuf_ref, o_ref, y0_ref, y1_ref, sem):
    t = h_ref.shape[0]
    bufs = (y0_ref, y1_ref)

    def row_copy(src, tok, kk):
        return pltpu.make_async_copy(ybuf_ref.at[pl.ds(src, 1)], bufs[kk].at[pl.ds(tok, 1)], sem)

    def issue(i, c):
        for kk in range(TOP_K):
            row_copy(dest_ref[TOP_K * i + kk], i, kk).start(priority=kk)
        return c

    lax.fori_loop(0, t, issue, 0, unroll=ROW_UNROLL)
    for kk in range(TOP_K):
        pltpu.make_async_copy(ybuf_ref.at[pl.ds(0, t)], bufs[kk], sem).wait()

    route = route_ref[...]
    g1 = route[:, 2:3]
    g2 = route[:, 3:4]
    a0, b0 = _unpack_bf16_pair(y0_ref[...])
    a1, b1 = _unpack_bf16_pair(y1_ref[...])
    moe = jnp.concatenate([a0 * g1 + a1 * g2, b0 * g1 + b1 * g2], axis=1)
    o_ref[...] = _rms(h_ref[...] + moe, fw_ref[...])


def _params(*sem):
    return pltpu.CompilerParams(dimension_semantics=sem, vmem_limit_bytes=VMEM_LIMIT)


def _full(shape):
    return pl.BlockSpec(shape, lambda *_: (0,) * len(shape))


def _rows(tile, width):
    return pl.BlockSpec((tile, width), lambda i: (i, 0))


def _layer(x2, pos2, attn_norm_w, w_in, gla_gate_up, gla_gate_bias, gla_norm_w, mla_q_norm_w,
           mla_w_uq, mla_kv_norm_w, mla_w_ukv, w_out, ffn_norm_w, router_group_w, router_group_b,
           router_expert_w, router_expert_b, expert_w_gate, expert_w_up, expert_w_down,
           out_norm_w, batch, seq):
    n, d = x2.shape
    dq = GLA_HEADS * GLA_DK
    dv = GLA_HEADS * GLA_DV
    dmla = MLA_HEADS * HEAD_PAD
    dmv = MLA_HEADS * MLA_DV

    c_gq, c_gk, c_gv, c_lr, c_og, c_cq, c_ckv, c_kr = jnp.split(
        w_in, [dq, 2 * dq, 2 * dq + dv, 2 * dq + dv + GLA_GATE_RANK,
               2 * dq + 2 * dv + GLA_GATE_RANK,
               2 * dq + 2 * dv + GLA_GATE_RANK + MLA_Q_RANK,
               2 * dq + 2 * dv + GLA_GATE_RANK + MLA_Q_RANK + MLA_KV_RANK], axis=1)
    zeros = lambda r, c: jnp.zeros((r, c), w_in.dtype)
    misc = jnp.concatenate([zeros(d, MLA_NOPE), c_kr, c_lr,
                            zeros(d, LANES - ROPE_END - GLA_GATE_RANK)], axis=1)
    w1 = jnp.concatenate([c_gq, c_gk, c_gv, c_og, c_cq, c_ckv, misc], axis=1).astype(BF16)
    gate_up = jnp.concatenate([zeros(GATE_LO, dq), gla_gate_up,
                               zeros(LANES - GATE_LO - GLA_GATE_RANK, dq)], axis=0).astype(BF16)
    wuq = mla_w_uq.reshape(MLA_Q_RANK, MLA_HEADS, MLA_NOPE + MLA_ROPE)
    wuq = jnp.pad(wuq, ((0, 0), (0, 0), (0, HEAD_PAD - ROPE_END))).reshape(MLA_Q_RANK, dmla)
    wukv = mla_w_ukv.reshape(MLA_KV_RANK, MLA_HEADS, MLA_NOPE + MLA_DV)
    wuk = jnp.pad(wukv[:, :, :MLA_NOPE], ((0, 0), (0, 0), (0, HEAD_PAD - MLA_NOPE)))
    wuv = jnp.pad(wukv[:, :, MLA_NOPE:], ((0, 0), (0, 0), (0, HEAD_PAD - MLA_DV)))
    wukv = jnp.concatenate([wuk.reshape(MLA_KV_RANK, dmla), wuv.reshape(MLA_KV_RANK, dmla)],
                           axis=1)
    inv = ROPE_THETA ** (-jnp.arange(ROPE_HALF, dtype=F32) / ROPE_HALF)
    inv_pat = jnp.concatenate([jnp.zeros((MLA_NOPE,), F32), inv, inv,
                               jnp.zeros((LANES - ROPE_END,), F32)]).reshape(1, LANES)
    w_router = jnp.concatenate(
        [router_expert_w, router_group_w, zeros(d, LANES - N_EXPERTS - N_GROUPS)], axis=1)
    b_router = jnp.concatenate(
        [router_expert_b, router_group_b, jnp.zeros((LANES - N_EXPERTS - N_GROUPS,), F32)]
    ).reshape(1, LANES)
    row1 = lambda v: v.reshape(1, -1)

    tp = PROJ_TILE
    outs = pl.pallas_call(
        functools.partial(_proj_kernel, q_scale=(MLA_NOPE + MLA_ROPE) ** -0.5),
        grid=(n // tp,),
        in_specs=[_rows(tp, d), _rows(tp, 1), _full((1, d)), _full(w1.shape), _full(gate_up.shape),
                  _full((1, dq)), _full((1, MLA_Q_RANK)), _full(wuq.shape),
                  _full((1, MLA_KV_RANK)), _full(wukv.shape), _full((1, LANES))],
        out_specs=[_rows(tp, dq), _rows(tp, dq), _rows(tp, dv), _rows(tp, dq), _rows(tp, dv),
                   _rows(tp, dmla), _rows(tp, dmla), _rows(tp, dmla)],
        out_shape=[jax.ShapeDtypeStruct((n, dq), BF16), jax.ShapeDtypeStruct((n, dq), BF16),
                   jax.ShapeDtypeStruct((n, dv), BF16), jax.ShapeDtypeStruct((n, dq), F32),
                   jax.ShapeDtypeStruct((n, dv), BF16), jax.ShapeDtypeStruct((n, dmla), BF16),
                   jax.ShapeDtypeStruct((n, dmla), BF16), jax.ShapeDtypeStruct((n, dmla), BF16)],
        compiler_params=_params("parallel"),
        name="in_proj",
    )(x2, pos2, row1(attn_norm_w), w1, gate_up, row1(gla_gate_bias), row1(mla_q_norm_w),
      wuq.astype(BF16), row1(mla_kv_norm_w), wukv.astype(BF16), inv_pat)
    gq, gk, gv, gla, gog, q, k, v = outs

    tg = GLA_TILE
    seq3 = lambda a: a.reshape(batch, seq, a.shape[-1])
    gspec = lambda w: pl.BlockSpec((1, tg, w), lambda b, i: (b, i, 0))
    y_gla = pl.pallas_call(
        _gla_kernel,
        grid=(batch, seq // tg),
        in_specs=[gspec(dq), gspec(dq), gspec(dv), gspec(dq), gspec(dv), _full((1, GLA_DV))],
        out_specs=gspec(dv),
        out_shape=jax.ShapeDtypeStruct((batch, seq, dv), BF16),
        scratch_shapes=[pltpu.VMEM((GLA_HEADS, GLA_DV, GLA_DK), F32)],
        compiler_params=_params("parallel", "arbitrary"),
        name="gla",
    )(seq3(gq), seq3(gk), seq3(gv), seq3(gla), seq3(gog), row1(gla_norm_w))

    ta = ATT_TILE
    hps = ATT_HEADS_PER_STEP
    y_mla = pl.pallas_call(
        functools.partial(_mla_kernel, key_shift=ATT_KEY_SHIFT),
        grid=(batch, MLA_HEADS // hps, seq // ta),
        in_specs=[pl.BlockSpec((1, ta, hps * HEAD_PAD), lambda b, hp, i: (b, i, hp)),
                  pl.BlockSpec((1, seq, hps * HEAD_PAD), lambda b, hp, i: (b, 0, hp)),
                  pl.BlockSpec((1, seq, hps * HEAD_PAD), lambda b, hp, i: (b, 0, hp))],
        out_specs=pl.BlockSpec((1, ta, hps * MLA_DV), lambda b, hp, i: (b, i, hp)),
        out_shape=jax.ShapeDtypeStruct((batch, seq, dmv), BF16),
        compiler_params=_params("parallel", "parallel", "arbitrary"),
        name="mla",
    )(seq3(q), seq3(k), seq3(v))

    h, hn, route, counts = pl.pallas_call(
        _router_kernel,
        grid=(n // tp,),
        in_specs=[_rows(tp, d), _rows(tp, dv), _rows(tp, dmv), _full(w_out.shape), _full((1, d)),
                  _full(w_router.shape), _full((1, LANES))],
        out_specs=[_rows(tp, d), _rows(tp, d // 2), _rows(tp, LANES), _full((1, LANES))],
        out_shape=[jax.ShapeDtypeStruct((n, d), F32), jax.ShapeDtypeStruct((n, d // 2), jnp.uint32),
                   jax.ShapeDtypeStruct((n, LANES), F32), jax.ShapeDtypeStruct((1, LANES), F32)],
        scratch_shapes=[pltpu.VMEM((1, LANES), F32)],
        compiler_params=_params("arbitrary"),
        name="out_proj_router",
    )(x2, y_gla.reshape(n, dv), y_mla.reshape(n, dmv), w_out.astype(BF16), row1(ffn_norm_w),
      w_router, b_router)

    blk = EXPERT_BLOCK
    nblk = (n * TOP_K) // blk + N_EXPERTS
    cnt = counts[0, :N_EXPERTS].astype(jnp.int32)
    padded = (cnt + blk - 1) // blk * blk
    pend = jnp.cumsum(padded)
    pstart = pend - padded
    eid = route[:, 0:TOP_K].astype(jnp.int32)
    rank = route[:, 4:4 + TOP_K].astype(jnp.int32)
    pstart_of = jnp.sum(jnp.where(eid[..., None] == jnp.arange(N_EXPERTS, dtype=jnp.int32),
                                  pstart, 0), axis=-1)
    dest = (pstart_of + rank).reshape(n * TOP_K)
    blk_start = jnp.arange(nblk, dtype=jnp.int32) * blk
    blk_expert = jnp.minimum(
        jnp.sum((pend[None, :] <= blk_start[:, None]).astype(jnp.int32), axis=1), N_EXPERTS - 1)
    n_used = (pend[-1] // blk).astype(jnp.int32).reshape(1)

    tr = ROW_TILE
    smem_rows = pl.BlockSpec((TOP_K * tr,), lambda i: (i,), memory_space=pltpu.SMEM)
    any_spec = pl.BlockSpec(memory_space=pl.ANY)
    xbuf = pl.pallas_call(
        _dispatch_kernel,
        grid=(n // tr,),
        in_specs=[smem_rows, _rows(tr, d // 2), any_spec],
        out_specs=any_spec,
        out_shape=jax.ShapeDtypeStruct((nblk * blk, d // 2), jnp.uint32),
        scratch_shapes=[pltpu.SemaphoreType.DMA],
        input_output_aliases={2: 0},
        compiler_params=_params("arbitrary"),
        name="dispatch",
    )(dest, hn, jnp.zeros((nblk * blk, d // 2), jnp.uint32))

    def used(j, nu):
        return jnp.maximum(jnp.minimum(j, nu[0] - 1), 0)

    ybuf = pl.pallas_call(
        _expert_kernel,
        grid_spec=pltpu.PrefetchScalarGridSpec(
            num_scalar_prefetch=2,
            grid=(nblk,),
            in_specs=[
                pl.BlockSpec((blk, d // 2), lambda j, be, nu: (used(j, nu), 0)),
                pl.BlockSpec((1, d, D_EXPERT), lambda j, be, nu: (be[used(j, nu)], 0, 0)),
                pl.BlockSpec((1, d, D_EXPERT), lambda j, be, nu: (be[used(j, nu)], 0, 0)),
                pl.BlockSpec((1, D_EXPERT, d), lambda j, be, nu: (be[used(j, nu)], 0, 0)),
            ],
            out_specs=pl.BlockSpec((blk, d // 2), lambda j, be, nu: (j, 0)),
        ),
        out_shape=jax.ShapeDtypeStruct((nblk * blk, d // 2), jnp.uint32),
        compiler_params=_params("arbitrary"),
        name="experts",
    )(blk_expert, n_used, xbuf, expert_w_gate, expert_w_up, expert_w_down)

    return pl.pallas_call(
        _combine_kernel,
        grid=(n // tr,),
        in_specs=[smem_rows, _rows(tr, d), _rows(tr, LANES), _full((1, d)), any_spec],
        out_specs=_rows(tr, d),
        out_shape=jax.ShapeDtypeStruct((n, d), F32),
        scratch_shapes=[pltpu.VMEM((tr, d // 2), jnp.uint32), pltpu.VMEM((tr, d // 2), jnp.uint32),
                        pltpu.SemaphoreType.DMA],
        compiler_params=_params("arbitrary"),
        name="combine",
    )(dest, h, route, row1(out_norm_w), ybuf)


def kernel(x, positions, attn_norm_w, w_in, gla_gate_up, gla_gate_bias, gla_norm_w, mla_q_norm_w,
           mla_w_uq, mla_kv_norm_w, mla_w_ukv, w_out, ffn_norm_w, router_group_w, router_group_b,
           router_expert_w, router_expert_b, expert_w_gate, expert_w_up, expert_w_down,
           final_norm_w):
    batch, seq, d = x.shape
    depth = w_in.shape[0]
    assert depth == 1, "the final norm is fused into the last layer's combine step"
    out = _layer(x.reshape(batch * seq, d), positions.reshape(batch * seq, 1),
                 attn_norm_w[0], w_in[0], gla_gate_up[0], gla_gate_bias[0], gla_norm_w[0],
                 mla_q_norm_w[0], mla_w_uq[0], mla_kv_norm_w[0], mla_w_ukv[0], w_out[0],
                 ffn_norm_w[0], router_group_w[0], router_group_b[0], router_expert_w[0],
                 router_expert_b[0], expert_w_gate[0], expert_w_up[0], expert_w_down[0],
                 final_norm_w, batch, seq)
    return out.reshape(batch, seq, d)
```

```python
import functools

import jax
import jax.numpy as jnp
from jax import lax
from jax.experimental import pallas as pl
from jax.experimental.pallas import tpu as pltpu

EPS = 1e-6
GLA_HEADS = 4
GLA_DK = 64
GLA_DV = 128
GLA_GATE_RANK = 16
GLA_GATE_NORM = 16.0
GLA_CHUNK = 64
MLA_HEADS = 8
MLA_NOPE = 64
MLA_ROPE = 32
MLA_DV = 64
MLA_Q_RANK = 384
MLA_KV_RANK = 256
ROPE_THETA = 10000.0
N_GROUPS = 4
EXPERTS_PER_GROUP = 8
N_EXPERTS = N_GROUPS * EXPERTS_PER_GROUP
TOP_K = 2
D_EXPERT = 256

LANES = 128
HEAD_PAD = 128
ROPE_HALF = MLA_ROPE // 2
ROPE_LO = MLA_NOPE
ROPE_HI = MLA_NOPE + ROPE_HALF
ROPE_END = MLA_NOPE + MLA_ROPE
GATE_LO = ROPE_END

PROJ_TILE = 512
GLA_TILE = 256
ATT_TILE = 512
ATT_KEY_SHIFT = 0
ATT_HEADS_PER_STEP = 4
ROW_TILE = 256
ROW_UNROLL = 8
EXPERT_BLOCK = 256
VMEM_LIMIT = 56 * 1024 * 1024

F32 = jnp.float32
BF16 = jnp.bfloat16
NEG_INF = float("-inf")


def _dot(a, b, precision=None):
    return jnp.dot(a, b, preferred_element_type=F32, precision=precision)


def _dot_nt(a, b):
    return lax.dot_general(a, b, (((1,), (1,)), ((), ())), preferred_element_type=F32)


def _dot_tn(a, b):
    return lax.dot_general(a, b, (((0,), (0,)), ((), ())), preferred_element_type=F32)


def _rms(x, w):
    return x * lax.rsqrt(jnp.mean(x * x, axis=-1, keepdims=True) + EPS) * w


def _pack_bf16_pair(a, b):
    ua = lax.bitcast_convert_type(a.astype(BF16).astype(F32), jnp.uint32)
    ub = lax.bitcast_convert_type(b.astype(BF16).astype(F32), jnp.uint32)
    return (ua >> 16) | (ub & jnp.uint32(0xFFFF0000))


def _unpack_bf16_pair(u):
    a = lax.bitcast_convert_type(u << 16, F32)
    b = lax.bitcast_convert_type(u & jnp.uint32(0xFFFF0000), F32)
    return a, b


def _proj_kernel(x_ref, pos_ref, nw_ref, w1_ref, gu_ref, gb_ref, qnw_ref, wuq_ref, kvnw_ref,
                 wukv_ref, inv_ref,
                 gq_ref, gk_ref, gv_ref, gla_ref, gog_ref, q_ref, k_ref, v_ref, *, q_scale):
    x = x_ref[...]
    xn = _rms(x, nw_ref[...]).astype(BF16)
    proj = _dot(xn, w1_ref[...])
    dq = GLA_HEADS * GLA_DK
    dv = GLA_HEADS * GLA_DV
    o = 0
    gq_ref[...] = (proj[:, o:o + dq] * (GLA_DK ** -0.5)).astype(BF16); o += dq
    gk_ref[...] = proj[:, o:o + dq].astype(BF16); o += dq
    gv_ref[...] = proj[:, o:o + dv].astype(BF16); o += dv
    gog_ref[...] = proj[:, o:o + dv].astype(BF16); o += dv
    cq = proj[:, o:o + MLA_Q_RANK]; o += MLA_Q_RANK
    ckv = proj[:, o:o + MLA_KV_RANK]; o += MLA_KV_RANK
    misc = proj[:, o:o + LANES]

    z = _dot(misc.astype(BF16), gu_ref[...]) + gb_ref[...]
    log_sig = jnp.minimum(z, 0.0) - jnp.log1p(jnp.exp(-jnp.abs(z)))
    gla_ref[...] = log_sig * (1.0 / GLA_GATE_NORM)

    lane = lax.broadcasted_iota(jnp.int32, (x.shape[0], LANES), 1)
    ang = pos_ref[...].astype(F32) * inv_ref[...]
    cosv = jnp.cos(ang)
    sinv = jnp.sin(ang)
    in_lo = (lane >= ROPE_LO) & (lane < ROPE_HI)
    in_hi = (lane >= ROPE_HI) & (lane < ROPE_END)
    c_rope = jnp.where(in_lo | in_hi, cosv, 0.0)
    s_up = jnp.where(in_hi, sinv, 0.0)
    s_dn = jnp.where(in_lo, -sinv, 0.0)
    c_q = jnp.where(lane < MLA_NOPE, 1.0, c_rope)

    def rope(t, c):
        return (t * c + pltpu.roll(t, ROPE_HALF, 1) * s_up
                + pltpu.roll(t, LANES - ROPE_HALF, 1) * s_dn)

    k_rope = rope(misc, c_rope)

    q = _dot(_rms(cq, qnw_ref[...]).astype(BF16), wuq_ref[...])
    kv = _dot(_rms(ckv, kvnw_ref[...]).astype(BF16), wukv_ref[...])
    ones_lane = jnp.where(lane == MLA_DV, 1.0, 0.0)
    for h in range(MLA_HEADS):
        sl = slice(h * HEAD_PAD, (h + 1) * HEAD_PAD)
        vsl = slice((MLA_HEADS + h) * HEAD_PAD, (MLA_HEADS + h + 1) * HEAD_PAD)
        q_ref[:, sl] = (rope(q[:, sl], c_q) * q_scale).astype(BF16)
        k_ref[:, sl] = (kv[:, sl] + k_rope).astype(BF16)
        v_ref[:, sl] = (kv[:, vsl] + ones_lane).astype(BF16)


def _gla_kernel(q_ref, k_ref, v_ref, la_ref, og_ref, nw_ref, o_ref, st_ref):
    t = q_ref.shape[1]
    nchunk = t // GLA_CHUNK

    @pl.when(pl.program_id(1) == 0)
    def _():
        st_ref[...] = jnp.zeros_like(st_ref)

    row = lax.broadcasted_iota(jnp.int32, (t, t), 0)
    col = lax.broadcasted_iota(jnp.int32, (t, t), 1)
    chunk_bits = GLA_CHUNK.bit_length() - 1
    tri = ((row >> chunk_bits) == (col >> chunk_bits)) & (col <= row)
    la = la_ref[0]
    b = _dot(tri.astype(F32), la, precision=lax.Precision.HIGHEST)
    b_last = jnp.concatenate(
        [jnp.broadcast_to(b[(c + 1) * GLA_CHUNK - 1:(c + 1) * GLA_CHUNK], (GLA_CHUNK, b.shape[1]))
         for c in range(nchunk)], axis=0)
    q_e = (q_ref[0].astype(F32) * jnp.exp(b)).astype(BF16)
    kf = k_ref[0].astype(F32)
    k_e = (kf * jnp.exp(-b)).astype(BF16)
    k_d = (kf * jnp.exp(b_last - b)).astype(BF16)
    decay = jnp.exp(b_last)
    nw = nw_ref[...]

    for h in range(GLA_HEADS):
        ks = slice(h * GLA_DK, (h + 1) * GLA_DK)
        vs = slice(h * GLA_DV, (h + 1) * GLA_DV)
        qh, keh, kdh = q_e[:, ks], k_e[:, ks], k_d[:, ks]
        vh = v_ref[0, :, vs]
        att = jnp.where(tri, _dot_nt(qh, keh), 0.0)
        o = _dot(att.astype(BF16), vh)
        state = st_ref[h]
        inter = []
        for c in range(nchunk):
            rs = slice(c * GLA_CHUNK, (c + 1) * GLA_CHUNK)
            inter.append(_dot_nt(qh[rs], state.astype(BF16)))
            upd = _dot_tn(vh[rs], kdh[rs])
            state = state * decay[c * GLA_CHUNK:c * GLA_CHUNK + 1, ks] + upd
        st_ref[h] = state
        o = o + jnp.concatenate(inter, axis=0)
        o = _rms(o, nw)
        g = og_ref[0, :, vs].astype(F32)
        o_ref[0, :, vs] = (o * (g * jax.nn.sigmoid(g))).astype(o_ref.dtype)


def _mla_kernel(q_ref, k_ref, v_ref, o_ref, *, key_shift):
    tq = q_ref.shape[1]
    tk = tq << key_shift
    heads = q_ref.shape[2] // HEAD_PAD
    qi = pl.program_id(2)
    lane = lax.broadcasted_iota(jnp.int32, (tq, LANES), 1)

    def step(j, carry, mask):
        rows = pl.ds(pl.multiple_of(j * tk, tk), tk)
        new = []
        for hh in range(heads):
            m, acc = carry[hh]
            hs = slice(hh * HEAD_PAD, (hh + 1) * HEAD_PAD)
            s = _dot_nt(q_ref[0, :, hs], k_ref[0, rows, hs])
            if mask:
                r = lax.broadcasted_iota(jnp.int32, (tq, tk), 0) + qi * tq
                c = lax.broadcasted_iota(jnp.int32, (tq, tk), 1) + j * tk
                s = jnp.where(c <= r, s, NEG_INF)
            m_new = jnp.maximum(m, jnp.max(s, axis=-1, keepdims=True))
            p = jnp.exp(s - m_new).astype(BF16)
            acc = jnp.exp(m - m_new) * acc + _dot(p, v_ref[0, rows, hs])
            new.append((m_new, acc))
        return tuple(new)

    init = tuple((jnp.full((tq, 1), NEG_INF, F32), jnp.zeros((tq, LANES), F32))
                 for _ in range(heads))
    nfull = lax.shift_right_logical(qi, key_shift)
    carry = lax.fori_loop(0, nfull, lambda j, c: step(j, c, False), init)
    carry = step(nfull, carry, True)
    for hp in range(heads // 2):
        o0, o1 = (acc / acc[:, MLA_DV:MLA_DV + 1] for _, acc in carry[2 * hp:2 * hp + 2])
        o_ref[0, :, hp * LANES:(hp + 1) * LANES] = jnp.where(
            lane < MLA_DV, o0, pltpu.roll(o1, MLA_DV, 1)).astype(o_ref.dtype)


def _router_kernel(x_ref, yg_ref, ym_ref, wo_ref, fnw_ref, wr_ref, br_ref,
                   h_ref, hn_ref, route_ref, cnt_ref, carry_ref):
    t = x_ref.shape[0]
    half = wo_ref.shape[0] // 2

    @pl.when(pl.program_id(0) == 0)
    def _():
        carry_ref[...] = jnp.zeros_like(carry_ref)

    h = x_ref[...] + _dot(yg_ref[...], wo_ref[:half]) + _dot(ym_ref[...], wo_ref[half:])
    h_ref[...] = h
    hn = _rms(h, fnw_ref[...])
    hp = hn.shape[1] // 2
    hn_ref[...] = _pack_bf16_pair(hn[:, :hp], hn[:, hp:])

    hn_hi = hn.astype(BF16)
    hn_lo = (hn - hn_hi.astype(F32)).astype(BF16)
    parts = _dot(hn_hi, wr_ref[...]) + _dot(hn_lo, wr_ref[...])
    logits = parts[:, :LANES] + parts[:, LANES:] + br_ref[...]
    lane = lax.broadcasted_iota(jnp.int32, (t, LANES), 1)
    lane_f = lane.astype(F32)

    def first_argmax(vals, vmax):
        idx = jnp.min(jnp.where(vals == vmax, lane_f, float(LANES)), axis=-1, keepdims=True)
        return idx.astype(jnp.int32)

    gl = jnp.where((lane >= N_EXPERTS) & (lane < N_EXPERTS + N_GROUPS), logits, NEG_INF)
    gmax = jnp.max(gl, axis=-1, keepdims=True)
    gsel = first_argmax(gl, gmax) - N_EXPERTS
    p_g = 1.0 / jnp.sum(jnp.exp(gl - gmax), axis=-1, keepdims=True)
    lo = gsel * EXPERTS_PER_GROUP
    el = jnp.where((lane >= lo) & (lane < lo + EXPERTS_PER_GROUP), logits, NEG_INF)
    m1 = jnp.max(el, axis=-1, keepdims=True)
    i1 = first_argmax(el, m1)
    el2 = jnp.where(lane == i1, NEG_INF, el)
    m2 = jnp.max(el2, axis=-1, keepdims=True)
    i2 = first_argmax(el2, m2)
    e2 = jnp.exp(m2 - m1)
    g1 = p_g / (1.0 + e2)
    g2 = p_g * e2 / (1.0 + e2)

    is1 = lane == i1
    is2 = lane == i2
    onehot = (is1 | is2).astype(BF16)
    rr = lax.broadcasted_iota(jnp.int32, (t, t), 0)
    cc = lax.broadcasted_iota(jnp.int32, (t, t), 1)
    before = _dot((cc < rr).astype(BF16), onehot) + carry_ref[...]
    r1 = jnp.sum(jnp.where(is1, before, 0.0), axis=-1, keepdims=True)
    r2 = jnp.sum(jnp.where(is2, before, 0.0), axis=-1, keepdims=True)
    carry_ref[...] = carry_ref[...] + jnp.sum(onehot.astype(F32), axis=0, keepdims=True)
    cnt_ref[...] = carry_ref[...]

    route = jnp.where(lane == 0, i1.astype(F32), 0.0)
    route = jnp.where(lane == 1, i2.astype(F32), route)
    route = jnp.where(lane == 2, g1, route)
    route = jnp.where(lane == 3, g2, route)
    route = jnp.where(lane == 4, r1, route)
    route = jnp.where(lane == 5, r2, route)
    route_ref[...] = route


def _dispatch_kernel(dest_ref, hn_ref, xbuf_in_ref, xbuf_ref, sem):
    del xbuf_in_ref
    t = hn_ref.shape[0]

    def row_copy(tok, dst):
        return pltpu.make_async_copy(hn_ref.at[pl.ds(tok, 1)], xbuf_ref.at[pl.ds(dst, 1)], sem)

    def issue(i, c):
        for kk in range(TOP_K):
            row_copy(i, dest_ref[TOP_K * i + kk]).start(priority=kk)
        return c

    lax.fori_loop(0, t, issue, 0, unroll=ROW_UNROLL)
    for _ in range(TOP_K):
        pltpu.make_async_copy(hn_ref, xbuf_ref.at[pl.ds(0, t)], sem).wait()


def _expert_kernel(be_ref, nu_ref, x_ref, wg_ref, wu_ref, wd_ref, y_ref, wgu_s, wd_s):
    j = pl.program_id(0)
    de = wg_ref.shape[2]
    last = jnp.maximum(nu_ref[0] - 1, 0)
    cur = be_ref[jnp.minimum(j, last)]
    prev = be_ref[jnp.minimum(jnp.maximum(j - 1, 0), last)]

    @pl.when((j == 0) | (cur != prev))
    def _():
        wgu_s[:, :de] = wg_ref[0].astype(BF16)
        wgu_s[:, de:] = wu_ref[0].astype(BF16)
        wd_s[...] = wd_ref[0].astype(BF16)

    @pl.when(j < nu_ref[0])
    def _():
        a, b = _unpack_bf16_pair(x_ref[...])
        x = jnp.concatenate([a, b], axis=1).astype(BF16)
        h12 = _dot(x, wgu_s[...])
        h1, h2 = h12[:, :de], h12[:, de:]
        hdn = (h1 * jax.nn.sigmoid(h1) * h2).astype(BF16)
        y = _dot(hdn, wd_s[...])
        hp = y.shape[1] // 2
        y_ref[...] = _pack_bf16_pair(y[:, :hp], y[:, hp:])

    @pl.when(pl.program_id(0) >= nu_ref[0])
    def _():
        y_ref[...] = jnp.zeros_like(y_ref)


def _combine_kernel(dest_ref, h_ref, route_ref, fw_ref, ybuf_ref, o_ref, y0_ref, y1_ref, sem):
    t = h_ref.shape[0]
    bufs = (y0_ref, y1_ref)

    def row_copy(src, tok, kk):
        return pltpu.make_async_copy(ybuf_ref.at[pl.ds(src, 1)], bufs[kk].at[pl.ds(tok, 1)], sem)

    def issue(i, c):
        for kk in range(TOP_K):
            row_copy(dest_ref[TOP_K * i + kk], i, kk).start(priority=kk)
        return c

    lax.fori_loop(0, t, issue, 0, unroll=ROW_UNROLL)
    for kk in range(TOP_K):
        pltpu.make_async_copy(ybuf_ref.at[pl.ds(0, t)], bufs[kk], sem).wait()

    route = route_ref[...]
    g1 = route[:, 2:3]
    g2 = route[:, 3:4]
    a0, b0 = _unpack_bf16_pair(y0_ref[...])
    a1, b1 = _unpack_bf16_pair(y1_ref[...])
    moe = jnp.concatenate([a0 * g1 + a1 * g2, b0 * g1 + b1 * g2], axis=1)
    o_ref[...] = _rms(h_ref[...] + moe, fw_ref[...])


def _params(*sem):
    return pltpu.CompilerParams(dimension_semantics=sem, vmem_limit_bytes=VMEM_LIMIT)


def _full(shape):
    return pl.BlockSpec(shape, lambda *_: (0,) * len(shape))


def _rows(tile, width):
    return pl.BlockSpec((tile, width), lambda i: (i, 0))


def _layer(x2, pos2, attn_norm_w, w_in, gla_gate_up, gla_gate_bias, gla_norm_w, mla_q_norm_w,
           mla_w_uq, mla_kv_norm_w, mla_w_ukv, w_out, ffn_norm_w, router_group_w, router_group_b,
           router_expert_w, router_expert_b, expert_w_gate, expert_w_up, expert_w_down,
           out_norm_w, batch, seq):
    n, d = x2.shape
    dq = GLA_HEADS * GLA_DK
    dv = GLA_HEADS * GLA_DV
    dmla = MLA_HEADS * HEAD_PAD
    dmv = MLA_HEADS * MLA_DV

    c_gq, c_gk, c_gv, c_lr, c_og, c_cq, c_ckv, c_kr = jnp.split(
        w_in, [dq, 2 * dq, 2 * dq + dv, 2 * dq + dv + GLA_GATE_RANK,
               2 * dq + 2 * dv + GLA_GATE_RANK,
               2 * dq + 2 * dv + GLA_GATE_RANK + MLA_Q_RANK,
               2 * dq + 2 * dv + GLA_GATE_RANK + MLA_Q_RANK + MLA_KV_RANK], axis=1)
    zeros = lambda r, c: jnp.zeros((r, c), w_in.dtype)
    misc = jnp.concatenate([zeros(d, MLA_NOPE), c_kr, c_lr,
                            zeros(d, LANES - ROPE_END - GLA_GATE_RANK)], axis=1)
    w1 = jnp.concatenate([c_gq, c_gk, c_gv, c_og, c_cq, c_ckv, misc], axis=1).astype(BF16)
    gate_up = jnp.concatenate([zeros(GATE_LO, dq), gla_gate_up,
                               zeros(LANES - GATE_LO - GLA_GATE_RANK, dq)], axis=0).astype(BF16)
    wuq = mla_w_uq.reshape(MLA_Q_RANK, MLA_HEADS, MLA_NOPE + MLA_ROPE)
    wuq = jnp.pad(wuq, ((0, 0), (0, 0), (0, HEAD_PAD - ROPE_END))).reshape(MLA_Q_RANK, dmla)
    wukv = mla_w_ukv.reshape(MLA_KV_RANK, MLA_HEADS, MLA_NOPE + MLA_DV)
    wuk = jnp.pad(wukv[:, :, :MLA_NOPE], ((0, 0), (0, 0), (0, HEAD_PAD - MLA_NOPE)))
    wuv = jnp.pad(wukv[:, :, MLA_NOPE:], ((0, 0), (0, 0), (0, HEAD_PAD - MLA_DV)))
    wukv = jnp.concatenate([wuk.reshape(MLA_KV_RANK, dmla), wuv.reshape(MLA_KV_RANK, dmla)],
                           axis=1)
    inv = ROPE_THETA ** (-jnp.arange(ROPE_HALF, dtype=F32) / ROPE_HALF)
    inv_pat = jnp.concatenate([jnp.zeros((MLA_NOPE,), F32), inv, inv,
                               jnp.zeros((LANES - ROPE_END,), F32)]).reshape(1, LANES)
    w_router = jnp.concatenate(
        [router_expert_w, router_group_w, zeros(d, LANES - N_EXPERTS - N_GROUPS)], axis=1)
    w_router_hi = w_router.astype(BF16)
    w_router = jnp.concatenate(
        [w_router_hi, (w_router - w_router_hi.astype(F32)).astype(BF16)], axis=1)
    b_router = jnp.concatenate(
        [router_expert_b, router_group_b, jnp.zeros((LANES - N_EXPERTS - N_GROUPS,), F32)]
    ).reshape(1, LANES)
    row1 = lambda v: v.reshape(1, -1)

    tp = PROJ_TILE
    outs = pl.pallas_call(
        functools.partial(_proj_kernel, q_scale=(MLA_NOPE + MLA_ROPE) ** -0.5),
        grid=(n // tp,),
        in_specs=[_rows(tp, d), _rows(tp, 1), _full((1, d)), _full(w1.shape), _full(gate_up.shape),
                  _full((1, dq)), _full((1, MLA_Q_RANK)), _full(wuq.shape),
                  _full((1, MLA_KV_RANK)), _full(wukv.shape), _full((1, LANES))],
        out_specs=[_rows(tp, dq), _rows(tp, dq), _rows(tp, dv), _rows(tp, dq), _rows(tp, dv),
                   _rows(tp, dmla), _rows(tp, dmla), _rows(tp, dmla)],
        out_shape=[jax.ShapeDtypeStruct((n, dq), BF16), jax.ShapeDtypeStruct((n, dq), BF16),
                   jax.ShapeDtypeStruct((n, dv), BF16), jax.ShapeDtypeStruct((n, dq), F32),
                   jax.ShapeDtypeStruct((n, dv), BF16), jax.ShapeDtypeStruct((n, dmla), BF16),
                   jax.ShapeDtypeStruct((n, dmla), BF16), jax.ShapeDtypeStruct((n, dmla), BF16)],
        compiler_params=_params("parallel"),
        name="in_proj",
    )(x2, pos2, row1(attn_norm_w), w1, gate_up, row1(gla_gate_bias), row1(mla_q_norm_w),
      wuq.astype(BF16), row1(mla_kv_norm_w), wukv.astype(BF16), inv_pat)
    gq, gk, gv, gla, gog, q, k, v = outs

    tg = GLA_TILE
    seq3 = lambda a: a.reshape(batch, seq, a.shape[-1])
    gspec = lambda w: pl.BlockSpec((1, tg, w), lambda b, i: (b, i, 0))
    y_gla = pl.pallas_call(
        _gla_kernel,
        grid=(batch, seq // tg),
        in_specs=[gspec(dq), gspec(dq), gspec(dv), gspec(dq), gspec(dv), _full((1, GLA_DV))],
        out_specs=gspec(dv),
        out_shape=jax.ShapeDtypeStruct((batch, seq, dv), BF16),
        scratch_shapes=[pltpu.VMEM((GLA_HEADS, GLA_DV, GLA_DK), F32)],
        compiler_params=_params("parallel", "arbitrary"),
        name="gla",
    )(seq3(gq), seq3(gk), seq3(gv), seq3(gla), seq3(gog), row1(gla_norm_w))

    ta = ATT_TILE
    hps = ATT_HEADS_PER_STEP
    y_mla = pl.pallas_call(
        functools.partial(_mla_kernel, key_shift=ATT_KEY_SHIFT),
        grid=(batch, MLA_HEADS // hps, seq // ta),
        in_specs=[pl.BlockSpec((1, ta, hps * HEAD_PAD), lambda b, hp, i: (b, i, hp)),
                  pl.BlockSpec((1, seq, hps * HEAD_PAD), lambda b, hp, i: (b, 0, hp)),
                  pl.BlockSpec((1, seq, hps * HEAD_PAD), lambda b, hp, i: (b, 0, hp))],
        out_specs=pl.BlockSpec((1, ta, hps * MLA_DV), lambda b, hp, i: (b, i, hp)),
        out_shape=jax.ShapeDtypeStruct((batch, seq, dmv), BF16),
        compiler_params=_params("parallel", "parallel", "arbitrary"),
        name="mla",
    )(seq3(q), seq3(k), seq3(v))

    h, hn, route, counts = pl.pallas_call(
        _router_kernel,
        grid=(n // tp,),
        in_specs=[_rows(tp, d), _rows(tp, dv), _rows(tp, dmv), _full(w_out.shape), _full((1, d)),
                  _full(w_router.shape), _full((1, LANES))],
        out_specs=[_rows(tp, d), _rows(tp, d // 2), _rows(tp, LANES), _full((1, LANES))],
        out_shape=[jax.ShapeDtypeStruct((n, d), F32), jax.ShapeDtypeStruct((n, d // 2), jnp.uint32),
                   jax.ShapeDtypeStruct((n, LANES), F32), jax.ShapeDtypeStruct((1, LANES), F32)],
        scratch_shapes=[pltpu.VMEM((1, LANES), F32)],
        compiler_params=_params("arbitrary"),
        name="out_proj_router",
    )(x2, y_gla.reshape(n, dv), y_mla.reshape(n, dmv), w_out.astype(BF16), row1(ffn_norm_w),
      w_router, b_router)

    blk = EXPERT_BLOCK
    nblk = (n * TOP_K) // blk + N_EXPERTS
    cnt = counts[0, :N_EXPERTS].astype(jnp.int32)
    padded = (cnt + blk - 1) // blk * blk
    pend = jnp.cumsum(padded)
    pstart = pend - padded
    eid = route[:, 0:TOP_K].astype(jnp.int32)
    rank = route[:, 4:4 + TOP_K].astype(jnp.int32)
    pstart_of = jnp.sum(jnp.where(eid[..., None] == jnp.arange(N_EXPERTS, dtype=jnp.int32),
                                  pstart, 0), axis=-1)
    dest = (pstart_of + rank).reshape(n * TOP_K)
    blk_start = jnp.arange(nblk, dtype=jnp.int32) * blk
    blk_expert = jnp.minimum(
        jnp.sum((pend[None, :] <= blk_start[:, None]).astype(jnp.int32), axis=1), N_EXPERTS - 1)
    n_used = (pend[-1] // blk).astype(jnp.int32).reshape(1)

    tr = ROW_TILE
    smem_rows = pl.BlockSpec((TOP_K * tr,), lambda i: (i,), memory_space=pltpu.SMEM)
    any_spec = pl.BlockSpec(memory_space=pl.ANY)
    xbuf = pl.pallas_call(
        _dispatch_kernel,
        grid=(n // tr,),
        in_specs=[smem_rows, _rows(tr, d // 2), any_spec],
        out_specs=any_spec,
        out_shape=jax.ShapeDtypeStruct((nblk * blk, d // 2), jnp.uint32),
        scratch_shapes=[pltpu.SemaphoreType.DMA],
        input_output_aliases={2: 0},
        compiler_params=_params("arbitrary"),
        name="dispatch",
    )(dest, hn, jnp.zeros((nblk * blk, d // 2), jnp.uint32))

    def used(j, nu):
        return jnp.maximum(jnp.minimum(j, nu[0] - 1), 0)

    ybuf = pl.pallas_call(
        _expert_kernel,
        grid_spec=pltpu.PrefetchScalarGridSpec(
            num_scalar_prefetch=2,
            grid=(nblk,),
            in_specs=[
                pl.BlockSpec((blk, d // 2), lambda j, be, nu: (used(j, nu), 0)),
                pl.BlockSpec((1, d, D_EXPERT), lambda j, be, nu: (be[used(j, nu)], 0, 0)),
                pl.BlockSpec((1, d, D_EXPERT), lambda j, be, nu: (be[used(j, nu)], 0, 0)),
                pl.BlockSpec((1, D_EXPERT, d), lambda j, be, nu: (be[used(j, nu)], 0, 0)),
            ],
            out_specs=pl.BlockSpec((blk, d // 2), lambda j, be, nu: (j, 0)),
            scratch_shapes=[pltpu.VMEM((d, 2 * D_EXPERT), BF16), pltpu.VMEM((D_EXPERT, d), BF16)],
        ),
        out_shape=jax.ShapeDtypeStruct((nblk * blk, d // 2), jnp.uint32),
        compiler_params=_params("arbitrary"),
        name="experts",
    )(blk_expert, n_used, xbuf, expert_w_gate, expert_w_up, expert_w_down)

    return pl.pallas_call(
        _combine_kernel,
        grid=(n // tr,),
        in_specs=[smem_rows, _rows(tr, d), _rows(tr, LANES), _full((1, d)), any_spec],
        out_specs=_rows(tr, d),
        out_shape=jax.ShapeDtypeStruct((n, d), F32),
        scratch_shapes=[pltpu.VMEM((tr, d // 2), jnp.uint32), pltpu.VMEM((tr, d // 2), jnp.uint32),
                        pltpu.SemaphoreType.DMA],
        compiler_params=_params("arbitrary"),
        name="combine",
    )(dest, h, route, row1(out_norm_w), ybuf)


def kernel(x, positions, attn_norm_w, w_in, gla_gate_up, gla_gate_bias, gla_norm_w, mla_q_norm_w,
           mla_w_uq, mla_kv_norm_w, mla_w_ukv, w_out, ffn_norm_w, router_group_w, router_group_b,
           router_expert_w, router_expert_b, expert_w_gate, expert_w_up, expert_w_down,
           final_norm_w):
    batch, seq, d = x.shape
    depth = w_in.shape[0]
    assert depth == 1, "the final norm is fused into the last layer's combine step"
    out = _layer(x.reshape(batch * seq, d), positions.reshape(batch * seq, 1),
                 attn_norm_w[0], w_in[0], gla_gate_up[0], gla_gate_bias[0], gla_norm_w[0],
                 mla_q_norm_w[0], mla_w_uq[0], mla_kv_norm_w[0], mla_w_ukv[0], w_out[0],
                 ffn_norm_w[0], router_group_w[0], router_group_b[0], router_expert_w[0],
                 router_expert_b[0], expert_w_gate[0], expert_w_up[0], expert_w_down[0],
                 final_norm_w, batch, seq)
    return out.reshape(batch, seq, d)
```

```python
import functools

import jax
import jax.numpy as jnp
from jax import lax
from jax.experimental import pallas as pl
from jax.experimental.pallas import tpu as pltpu
from jax.experimental.pallas import tpu_sc as plsc

EPS = 1e-6
GLA_HEADS = 4
GLA_DK = 64
GLA_DV = 128
GLA_GATE_RANK = 16
GLA_GATE_NORM = 16.0
GLA_CHUNK = 64
MLA_HEADS = 8
MLA_NOPE = 64
MLA_ROPE = 32
MLA_DV = 64
MLA_Q_RANK = 384
MLA_KV_RANK = 256
ROPE_THETA = 10000.0
N_GROUPS = 4
EXPERTS_PER_GROUP = 8
N_EXPERTS = N_GROUPS * EXPERTS_PER_GROUP
TOP_K = 2
D_EXPERT = 256

LANES = 128
HEAD_PAD = 128
ROPE_HALF = MLA_ROPE // 2
ROPE_LO = MLA_NOPE
ROPE_HI = MLA_NOPE + ROPE_HALF
ROPE_END = MLA_NOPE + MLA_ROPE
GATE_LO = ROPE_END

PROJ_TILE = 512
GLA_TILE = 256
ATT_TILE = 512
ATT_KEY_SHIFT = 0
ATT_HEADS_PER_STEP = 4
ROW_TILE = 256
ROW_UNROLL = 8
EXPERT_BLOCK = 256
VMEM_LIMIT = 56 * 1024 * 1024
SC_CORES = 2
SC_SUBCORES = 16
SC_WORKERS = SC_CORES * SC_SUBCORES
SC_CHUNK = 64

F32 = jnp.float32
BF16 = jnp.bfloat16
NEG_INF = float("-inf")


def _dot(a, b, precision=None):
    return jnp.dot(a, b, preferred_element_type=F32, precision=precision)


def _dot_nt(a, b):
    return lax.dot_general(a, b, (((1,), (1,)), ((), ())), preferred_element_type=F32)


def _dot_tn(a, b):
    return lax.dot_general(a, b, (((0,), (0,)), ((), ())), preferred_element_type=F32)


def _rms(x, w):
    return x * lax.rsqrt(jnp.mean(x * x, axis=-1, keepdims=True) + EPS) * w


def _pack_bf16_pair(a, b):
    ua = lax.bitcast_convert_type(a.astype(BF16).astype(F32), jnp.uint32)
    ub = lax.bitcast_convert_type(b.astype(BF16).astype(F32), jnp.uint32)
    return (ua >> 16) | (ub & jnp.uint32(0xFFFF0000))


def _unpack_bf16_pair(u):
    a = lax.bitcast_convert_type(u << 16, F32)
    b = lax.bitcast_convert_type(u & jnp.uint32(0xFFFF0000), F32)
    return a, b


def _proj_kernel(x_ref, pos_ref, nw_ref, w1_ref, gu_ref, gb_ref, qnw_ref, wuq_ref, kvnw_ref,
                 wukv_ref, inv_ref,
                 gq_ref, gk_ref, gv_ref, gla_ref, gog_ref, q_ref, k_ref, v_ref, *, q_scale):
    x = x_ref[...]
    xn = _rms(x, nw_ref[...]).astype(BF16)
    proj = _dot(xn, w1_ref[...])
    dq = GLA_HEADS * GLA_DK
    dv = GLA_HEADS * GLA_DV
    o = 0
    gq_ref[...] = (proj[:, o:o + dq] * (GLA_DK ** -0.5)).astype(BF16); o += dq
    gk_ref[...] = proj[:, o:o + dq].astype(BF16); o += dq
    gv_ref[...] = proj[:, o:o + dv].astype(BF16); o += dv
    gog_ref[...] = proj[:, o:o + dv].astype(BF16); o += dv
    cq = proj[:, o:o + MLA_Q_RANK]; o += MLA_Q_RANK
    ckv = proj[:, o:o + MLA_KV_RANK]; o += MLA_KV_RANK
    misc = proj[:, o:o + LANES]

    z = _dot(misc.astype(BF16), gu_ref[...]) + gb_ref[...]
    log_sig = jnp.minimum(z, 0.0) - jnp.log1p(jnp.exp(-jnp.abs(z)))
    gla_ref[...] = log_sig * (1.0 / GLA_GATE_NORM)

    lane = lax.broadcasted_iota(jnp.int32, (x.shape[0], LANES), 1)
    ang = pos_ref[...].astype(F32) * inv_ref[...]
    cosv = jnp.cos(ang)
    sinv = jnp.sin(ang)
    in_lo = (lane >= ROPE_LO) & (lane < ROPE_HI)
    in_hi = (lane >= ROPE_HI) & (lane < ROPE_END)
    c_rope = jnp.where(in_lo | in_hi, cosv, 0.0)
    s_up = jnp.where(in_hi, sinv, 0.0)
    s_dn = jnp.where(in_lo, -sinv, 0.0)
    c_q = jnp.where(lane < MLA_NOPE, 1.0, c_rope)

    def rope(t, c):
        return (t * c + pltpu.roll(t, ROPE_HALF, 1) * s_up
                + pltpu.roll(t, LANES - ROPE_HALF, 1) * s_dn)

    k_rope = rope(misc, c_rope)

    q = _dot(_rms(cq, qnw_ref[...]).astype(BF16), wuq_ref[...])
    kv = _dot(_rms(ckv, kvnw_ref[...]).astype(BF16), wukv_ref[...])
    ones_lane = jnp.where(lane == MLA_DV, 1.0, 0.0)
    for h in range(MLA_HEADS):
        sl = slice(h * HEAD_PAD, (h + 1) * HEAD_PAD)
        vsl = slice((MLA_HEADS + h) * HEAD_PAD, (MLA_HEADS + h + 1) * HEAD_PAD)
        q_ref[:, sl] = (rope(q[:, sl], c_q) * q_scale).astype(BF16)
        k_ref[:, sl] = (kv[:, sl] + k_rope).astype(BF16)
        v_ref[:, sl] = (kv[:, vsl] + ones_lane).astype(BF16)


def _gla_kernel(q_ref, k_ref, v_ref, la_ref, og_ref, nw_ref, o_ref, st_ref):
    t = q_ref.shape[1]
    nchunk = t // GLA_CHUNK

    @pl.when(pl.program_id(1) == 0)
    def _():
        st_ref[...] = jnp.zeros_like(st_ref)

    row = lax.broadcasted_iota(jnp.int32, (t, t), 0)
    col = lax.broadcasted_iota(jnp.int32, (t, t), 1)
    chunk_bits = GLA_CHUNK.bit_length() - 1
    tri = ((row >> chunk_bits) == (col >> chunk_bits)) & (col <= row)
    la = la_ref[0]
    b = _dot(tri.astype(F32), la, precision=lax.Precision.HIGHEST)
    b_last = jnp.concatenate(
        [jnp.broadcast_to(b[(c + 1) * GLA_CHUNK - 1:(c + 1) * GLA_CHUNK], (GLA_CHUNK, b.shape[1]))
         for c in range(nchunk)], axis=0)
    q_e = (q_ref[0].astype(F32) * jnp.exp(b)).astype(BF16)
    kf = k_ref[0].astype(F32)
    k_e = (kf * jnp.exp(-b)).astype(BF16)
    k_d = (kf * jnp.exp(b_last - b)).astype(BF16)
    decay = jnp.exp(b_last)
    nw = nw_ref[...]

    for h in range(GLA_HEADS):
        ks = slice(h * GLA_DK, (h + 1) * GLA_DK)
        vs = slice(h * GLA_DV, (h + 1) * GLA_DV)
        qh, keh, kdh = q_e[:, ks], k_e[:, ks], k_d[:, ks]
        vh = v_ref[0, :, vs]
        att = jnp.where(tri, _dot_nt(qh, keh), 0.0)
        o = _dot(att.astype(BF16), vh)
        state = st_ref[h]
        inter = []
        for c in range(nchunk):
            rs = slice(c * GLA_CHUNK, (c + 1) * GLA_CHUNK)
            inter.append(_dot_nt(qh[rs], state.astype(BF16)))
            upd = _dot_tn(vh[rs], kdh[rs])
            state = state * decay[c * GLA_CHUNK:c * GLA_CHUNK + 1, ks] + upd
        st_ref[h] = state
        o = o + jnp.concatenate(inter, axis=0)
        o = _rms(o, nw)
        g = og_ref[0, :, vs].astype(F32)
        o_ref[0, :, vs] = (o * (g * jax.nn.sigmoid(g))).astype(o_ref.dtype)


def _mla_kernel(q_ref, k_ref, v_ref, o_ref, *, key_shift):
    tq = q_ref.shape[1]
    tk = tq << key_shift
    heads = q_ref.shape[2] // HEAD_PAD
    qi = pl.program_id(2)
    lane = lax.broadcasted_iota(jnp.int32, (tq, LANES), 1)

    def step(j, carry, mask):
        rows = pl.ds(pl.multiple_of(j * tk, tk), tk)
        new = []
        for hh in range(heads):
            m, acc = carry[hh]
            hs = slice(hh * HEAD_PAD, (hh + 1) * HEAD_PAD)
            s = _dot_nt(q_ref[0, :, hs], k_ref[0, rows, hs])
            if mask:
                r = lax.broadcasted_iota(jnp.int32, (tq, tk), 0) + qi * tq
                c = lax.broadcasted_iota(jnp.int32, (tq, tk), 1) + j * tk
                s = jnp.where(c <= r, s, NEG_INF)
            m_new = jnp.maximum(m, jnp.max(s, axis=-1, keepdims=True))
            p = jnp.exp(s - m_new).astype(BF16)
            acc = jnp.exp(m - m_new) * acc + _dot(p, v_ref[0, rows, hs])
            new.append((m_new, acc))
        return tuple(new)

    init = tuple((jnp.full((tq, 1), NEG_INF, F32), jnp.zeros((tq, LANES), F32))
                 for _ in range(heads))
    nfull = lax.shift_right_logical(qi, key_shift)
    carry = lax.fori_loop(0, nfull, lambda j, c: step(j, c, False), init)
    carry = step(nfull, carry, True)
    for hp in range(heads // 2):
        o0, o1 = (acc / acc[:, MLA_DV:MLA_DV + 1] for _, acc in carry[2 * hp:2 * hp + 2])
        o_ref[0, :, hp * LANES:(hp + 1) * LANES] = jnp.where(
            lane < MLA_DV, o0, pltpu.roll(o1, MLA_DV, 1)).astype(o_ref.dtype)


def _router_kernel(x_ref, yg_ref, ym_ref, wo_ref, fnw_ref, wr_ref, br_ref,
                   h_ref, hn_ref, route_ref, cnt_ref, carry_ref):
    t = x_ref.shape[0]
    half = wo_ref.shape[0] // 2

    @pl.when(pl.program_id(0) == 0)
    def _():
        carry_ref[...] = jnp.zeros_like(carry_ref)

    h = x_ref[...] + _dot(yg_ref[...], wo_ref[:half]) + _dot(ym_ref[...], wo_ref[half:])
    h_ref[...] = h
    hn = _rms(h, fnw_ref[...])
    hp = hn.shape[1] // 2
    hn_ref[...] = _pack_bf16_pair(hn[:, :hp], hn[:, hp:])

    hn_hi = hn.astype(BF16)
    hn_lo = (hn - hn_hi.astype(F32)).astype(BF16)
    parts = _dot(hn_hi, wr_ref[...]) + _dot(hn_lo, wr_ref[...])
    logits = parts[:, :LANES] + parts[:, LANES:] + br_ref[...]
    lane = lax.broadcasted_iota(jnp.int32, (t, LANES), 1)
    lane_f = lane.astype(F32)

    def first_argmax(vals, vmax):
        idx = jnp.min(jnp.where(vals == vmax, lane_f, float(LANES)), axis=-1, keepdims=True)
        return idx.astype(jnp.int32)

    gl = jnp.where((lane >= N_EXPERTS) & (lane < N_EXPERTS + N_GROUPS), logits, NEG_INF)
    gmax = jnp.max(gl, axis=-1, keepdims=True)
    gsel = first_argmax(gl, gmax) - N_EXPERTS
    p_g = 1.0 / jnp.sum(jnp.exp(gl - gmax), axis=-1, keepdims=True)
    lo = gsel * EXPERTS_PER_GROUP
    el = jnp.where((lane >= lo) & (lane < lo + EXPERTS_PER_GROUP), logits, NEG_INF)
    m1 = jnp.max(el, axis=-1, keepdims=True)
    i1 = first_argmax(el, m1)
    el2 = jnp.where(lane == i1, NEG_INF, el)
    m2 = jnp.max(el2, axis=-1, keepdims=True)
    i2 = first_argmax(el2, m2)
    e2 = jnp.exp(m2 - m1)
    g1 = p_g / (1.0 + e2)
    g2 = p_g * e2 / (1.0 + e2)

    is1 = lane == i1
    is2 = lane == i2
    onehot = (is1 | is2).astype(BF16)
    rr = lax.broadcasted_iota(jnp.int32, (t, t), 0)
    cc = lax.broadcasted_iota(jnp.int32, (t, t), 1)
    before = _dot((cc < rr).astype(BF16), onehot) + carry_ref[...]
    r1 = jnp.sum(jnp.where(is1, before, 0.0), axis=-1, keepdims=True)
    r2 = jnp.sum(jnp.where(is2, before, 0.0), axis=-1, keepdims=True)
    carry_ref[...] = carry_ref[...] + jnp.sum(onehot.astype(F32), axis=0, keepdims=True)
    cnt_ref[...] = carry_ref[...]

    route = jnp.where(lane == 0, i1.astype(F32), 0.0)
    route = jnp.where(lane == 1, i2.astype(F32), route)
    route = jnp.where(lane == 2, g1, route)
    route = jnp.where(lane == 3, g2, route)
    route = jnp.where(lane == 4, r1, route)
    route = jnp.where(lane == 5, r2, route)
    route_ref[...] = route


def _dispatch_kernel(dest_ref, hn_ref, xbuf_in_ref, xbuf_ref, sem):
    del xbuf_in_ref
    t = hn_ref.shape[0]

    def row_copy(tok, dst):
        return pltpu.make_async_copy(hn_ref.at[pl.ds(tok, 1)], xbuf_ref.at[pl.ds(dst, 1)], sem)

    def issue(i, c):
        for kk in range(TOP_K):
            row_copy(i, dest_ref[TOP_K * i + kk]).start(priority=kk)
        return c

    lax.fori_loop(0, t, issue, 0, unroll=ROW_UNROLL)
    for _ in range(TOP_K):
        pltpu.make_async_copy(hn_ref, xbuf_ref.at[pl.ds(0, t)], sem).wait()


def _expert_kernel(be_ref, nu_ref, x_ref, wg_ref, wu_ref, wd_ref, y_ref, wgu_s, wd_s):
    j = pl.program_id(0)
    de = wg_ref.shape[2]
    last = jnp.maximum(nu_ref[0] - 1, 0)
    cur = be_ref[jnp.minimum(j, last)]
    prev = be_ref[jnp.minimum(jnp.maximum(j - 1, 0), last)]

    @pl.when((j == 0) | (cur != prev))
    def _():
        wgu_s[:, :de] = wg_ref[0].astype(BF16)
        wgu_s[:, de:] = wu_ref[0].astype(BF16)
        wd_s[...] = wd_ref[0].astype(BF16)

    @pl.when(j < nu_ref[0])
    def _():
        a, b = _unpack_bf16_pair(x_ref[...])
        x = jnp.concatenate([a, b], axis=1).astype(BF16)
        h12 = _dot(x, wgu_s[...])
        h1, h2 = h12[:, :de], h12[:, de:]
        hdn = (h1 * jax.nn.sigmoid(h1) * h2).astype(BF16)
        y = _dot(hdn, wd_s[...])
        hp = y.shape[1] // 2
        y_ref[...] = _pack_bf16_pair(y[:, :hp], y[:, hp:])

    @pl.when(pl.program_id(0) >= nu_ref[0])
    def _():
        y_ref[...] = jnp.zeros_like(y_ref)


def _sc_gather_rows(table, idx):
    m = idx.shape[0]
    w = table.shape[1]
    per_worker = m // SC_WORKERS
    nchunk = per_worker // SC_CHUNK
    mesh = plsc.VectorSubcoreMesh(core_axis_name="c", subcore_axis_name="s",
                                  num_cores=SC_CORES, num_subcores=SC_SUBCORES)

    @functools.partial(
        pl.kernel, mesh=mesh, out_type=jax.ShapeDtypeStruct((m, w), table.dtype),
        scratch_types=[pltpu.VMEM((SC_CHUNK,), jnp.int32), pltpu.VMEM((SC_CHUNK, w), table.dtype),
                       pltpu.SemaphoreType.DMA],
        name="sc_gather_rows")
    def gather(table_hbm, idx_hbm, out_hbm, idx_v, rows_v, sem):
        worker = lax.axis_index("s") * SC_CORES + lax.axis_index("c")
        base = worker * per_worker

        @pl.loop(0, nchunk)
        def _(ci):
            off = pl.multiple_of(base + ci * SC_CHUNK, SC_CHUNK)
            pltpu.sync_copy(idx_hbm.at[pl.ds(off, SC_CHUNK)], idx_v)
            pltpu.async_copy(table_hbm.at[idx_v], rows_v, sem).wait()
            pltpu.sync_copy(rows_v, out_hbm.at[pl.ds(off, SC_CHUNK)])

    return gather(table, idx)


def _combine_kernel(h_ref, route_ref, fw_ref, g_ref, o_ref):
    route = route_ref[...]
    g1 = route[:, 2:3]
    g2 = route[:, 3:4]
    hp = g_ref.shape[1] // 2
    a0, b0 = _unpack_bf16_pair(g_ref[:, :hp])
    a1, b1 = _unpack_bf16_pair(g_ref[:, hp:])
    moe = jnp.concatenate([a0 * g1 + a1 * g2, b0 * g1 + b1 * g2], axis=1)
    o_ref[...] = _rms(h_ref[...] + moe, fw_ref[...])


def _params(*sem):
    return pltpu.CompilerParams(dimension_semantics=sem, vmem_limit_bytes=VMEM_LIMIT)


def _full(shape):
    return pl.BlockSpec(shape, lambda *_: (0,) * len(shape))


def _rows(tile, width):
    return pl.BlockSpec((tile, width), lambda i: (i, 0))


def _layer(x2, pos2, attn_norm_w, w_in, gla_gate_up, gla_gate_bias, gla_norm_w, mla_q_norm_w,
           mla_w_uq, mla_kv_norm_w, mla_w_ukv, w_out, ffn_norm_w, router_group_w, router_group_b,
           router_expert_w, router_expert_b, expert_w_gate, expert_w_up, expert_w_down,
           out_norm_w, batch, seq):
    n, d = x2.shape
    dq = GLA_HEADS * GLA_DK
    dv = GLA_HEADS * GLA_DV
    dmla = MLA_HEADS * HEAD_PAD
    dmv = MLA_HEADS * MLA_DV

    c_gq, c_gk, c_gv, c_lr, c_og, c_cq, c_ckv, c_kr = jnp.split(
        w_in, [dq, 2 * dq, 2 * dq + dv, 2 * dq + dv + GLA_GATE_RANK,
               2 * dq + 2 * dv + GLA_GATE_RANK,
               2 * dq + 2 * dv + GLA_GATE_RANK + MLA_Q_RANK,
               2 * dq + 2 * dv + GLA_GATE_RANK + MLA_Q_RANK + MLA_KV_RANK], axis=1)
    zeros = lambda r, c: jnp.zeros((r, c), w_in.dtype)
    misc = jnp.concatenate([zeros(d, MLA_NOPE), c_kr, c_lr,
                            zeros(d, LANES - ROPE_END - GLA_GATE_RANK)], axis=1)
    w1 = jnp.concatenate([c_gq, c_gk, c_gv, c_og, c_cq, c_ckv, misc], axis=1).astype(BF16)
    gate_up = jnp.concatenate([zeros(GATE_LO, dq), gla_gate_up,
                               zeros(LANES - GATE_LO - GLA_GATE_RANK, dq)], axis=0).astype(BF16)
    wuq = mla_w_uq.reshape(MLA_Q_RANK, MLA_HEADS, MLA_NOPE + MLA_ROPE)
    wuq = jnp.pad(wuq, ((0, 0), (0, 0), (0, HEAD_PAD - ROPE_END))).reshape(MLA_Q_RANK, dmla)
    wukv = mla_w_ukv.reshape(MLA_KV_RANK, MLA_HEADS, MLA_NOPE + MLA_DV)
    wuk = jnp.pad(wukv[:, :, :MLA_NOPE], ((0, 0), (0, 0), (0, HEAD_PAD - MLA_NOPE)))
    wuv = jnp.pad(wukv[:, :, MLA_NOPE:], ((0, 0), (0, 0), (0, HEAD_PAD - MLA_DV)))
    wukv = jnp.concatenate([wuk.reshape(MLA_KV_RANK, dmla), wuv.reshape(MLA_KV_RANK, dmla)],
                           axis=1)
    inv = ROPE_THETA ** (-jnp.arange(ROPE_HALF, dtype=F32) / ROPE_HALF)
    inv_pat = jnp.concatenate([jnp.zeros((MLA_NOPE,), F32), inv, inv,
                               jnp.zeros((LANES - ROPE_END,), F32)]).reshape(1, LANES)
    w_router = jnp.concatenate(
        [router_expert_w, router_group_w, zeros(d, LANES - N_EXPERTS - N_GROUPS)], axis=1)
    w_router_hi = w_router.astype(BF16)
    w_router = jnp.concatenate(
        [w_router_hi, (w_router - w_router_hi.astype(F32)).astype(BF16)], axis=1)
    b_router = jnp.concatenate(
        [router_expert_b, router_group_b, jnp.zeros((LANES - N_EXPERTS - N_GROUPS,), F32)]
    ).reshape(1, LANES)
    row1 = lambda v: v.reshape(1, -1)

    tp = PROJ_TILE
    outs = pl.pallas_call(
        functools.partial(_proj_kernel, q_scale=(MLA_NOPE + MLA_ROPE) ** -0.5),
        grid=(n // tp,),
        in_specs=[_rows(tp, d), _rows(tp, 1), _full((1, d)), _full(w1.shape), _full(gate_up.shape),
                  _full((1, dq)), _full((1, MLA_Q_RANK)), _full(wuq.shape),
                  _full((1, MLA_KV_RANK)), _full(wukv.shape), _full((1, LANES))],
        out_specs=[_rows(tp, dq), _rows(tp, dq), _rows(tp, dv), _rows(tp, dq), _rows(tp, dv),
                   _rows(tp, dmla), _rows(tp, dmla), _rows(tp, dmla)],
        out_shape=[jax.ShapeDtypeStruct((n, dq), BF16), jax.ShapeDtypeStruct((n, dq), BF16),
                   jax.ShapeDtypeStruct((n, dv), BF16), jax.ShapeDtypeStruct((n, dq), F32),
                   jax.ShapeDtypeStruct((n, dv), BF16), jax.ShapeDtypeStruct((n, dmla), BF16),
                   jax.ShapeDtypeStruct((n, dmla), BF16), jax.ShapeDtypeStruct((n, dmla), BF16)],
        compiler_params=_params("parallel"),
        name="in_proj",
    )(x2, pos2, row1(attn_norm_w), w1, gate_up, row1(gla_gate_bias), row1(mla_q_norm_w),
      wuq.astype(BF16), row1(mla_kv_norm_w), wukv.astype(BF16), inv_pat)
    gq, gk, gv, gla, gog, q, k, v = outs

    tg = GLA_TILE
    seq3 = lambda a: a.reshape(batch, seq, a.shape[-1])
    gspec = lambda w: pl.BlockSpec((1, tg, w), lambda b, i: (b, i, 0))
    y_gla = pl.pallas_call(
        _gla_kernel,
        grid=(batch, seq // tg),
        in_specs=[gspec(dq), gspec(dq), gspec(dv), gspec(dq), gspec(dv), _full((1, GLA_DV))],
        out_specs=gspec(dv),
        out_shape=jax.ShapeDtypeStruct((batch, seq, dv), BF16),
        scratch_shapes=[pltpu.VMEM((GLA_HEADS, GLA_DV, GLA_DK), F32)],
        compiler_params=_params("parallel", "arbitrary"),
        name="gla",
    )(seq3(gq), seq3(gk), seq3(gv), seq3(gla), seq3(gog), row1(gla_norm_w))

    ta = ATT_TILE
    hps = ATT_HEADS_PER_STEP
    y_mla = pl.pallas_call(
        functools.partial(_mla_kernel, key_shift=ATT_KEY_SHIFT),
        grid=(batch, MLA_HEADS // hps, seq // ta),
        in_specs=[pl.BlockSpec((1, ta, hps * HEAD_PAD), lambda b, hp, i: (b, i, hp)),
                  pl.BlockSpec((1, seq, hps * HEAD_PAD), lambda b, hp, i: (b, 0, hp)),
                  pl.BlockSpec((1, seq, hps * HEAD_PAD), lambda b, hp, i: (b, 0, hp))],
        out_specs=pl.BlockSpec((1, ta, hps * MLA_DV), lambda b, hp, i: (b, i, hp)),
        out_shape=jax.ShapeDtypeStruct((batch, seq, dmv), BF16),
        compiler_params=_params("parallel", "parallel", "arbitrary"),
        name="mla",
    )(seq3(q), seq3(k), seq3(v))

    h, hn, route, counts = pl.pallas_call(
        _router_kernel,
        grid=(n // tp,),
        in_specs=[_rows(tp, d), _rows(tp, dv), _rows(tp, dmv), _full(w_out.shape), _full((1, d)),
                  _full(w_router.shape), _full((1, LANES))],
        out_specs=[_rows(tp, d), _rows(tp, d // 2), _rows(tp, LANES), _full((1, LANES))],
        out_shape=[jax.ShapeDtypeStruct((n, d), F32), jax.ShapeDtypeStruct((n, d // 2), jnp.uint32),
                   jax.ShapeDtypeStruct((n, LANES), F32), jax.ShapeDtypeStruct((1, LANES), F32)],
        scratch_shapes=[pltpu.VMEM((1, LANES), F32)],
        compiler_params=_params("arbitrary"),
        name="out_proj_router",
    )(x2, y_gla.reshape(n, dv), y_mla.reshape(n, dmv), w_out.astype(BF16), row1(ffn_norm_w),
      w_router, b_router)

    blk = EXPERT_BLOCK
    nblk = (n * TOP_K) // blk + N_EXPERTS
    cnt = counts[0, :N_EXPERTS].astype(jnp.int32)
    padded = (cnt + blk - 1) // blk * blk
    pend = jnp.cumsum(padded)
    pstart = pend - padded
    eid = route[:, 0:TOP_K].astype(jnp.int32)
    rank = route[:, 4:4 + TOP_K].astype(jnp.int32)
    pstart_of = jnp.sum(jnp.where(eid[..., None] == jnp.arange(N_EXPERTS, dtype=jnp.int32),
                                  pstart, 0), axis=-1)
    dest = (pstart_of + rank).reshape(n * TOP_K)
    blk_start = jnp.arange(nblk, dtype=jnp.int32) * blk
    blk_expert = jnp.minimum(
        jnp.sum((pend[None, :] <= blk_start[:, None]).astype(jnp.int32), axis=1), N_EXPERTS - 1)
    n_used = (pend[-1] // blk).astype(jnp.int32).reshape(1)

    tr = ROW_TILE
    smem_rows = pl.BlockSpec((TOP_K * tr,), lambda i: (i,), memory_space=pltpu.SMEM)
    any_spec = pl.BlockSpec(memory_space=pl.ANY)
    xbuf = pl.pallas_call(
        _dispatch_kernel,
        grid=(n // tr,),
        in_specs=[smem_rows, _rows(tr, d // 2), any_spec],
        out_specs=any_spec,
        out_shape=jax.ShapeDtypeStruct((nblk * blk, d // 2), jnp.uint32),
        scratch_shapes=[pltpu.SemaphoreType.DMA],
        input_output_aliases={2: 0},
        compiler_params=_params("arbitrary"),
        name="dispatch",
    )(dest, hn, jnp.zeros((nblk * blk, d // 2), jnp.uint32))

    def used(j, nu):
        return jnp.maximum(jnp.minimum(j, nu[0] - 1), 0)

    ybuf = pl.pallas_call(
        _expert_kernel,
        grid_spec=pltpu.PrefetchScalarGridSpec(
            num_scalar_prefetch=2,
            grid=(nblk,),
            in_specs=[
                pl.BlockSpec((blk, d // 2), lambda j, be, nu: (used(j, nu), 0)),
                pl.BlockSpec((1, d, D_EXPERT), lambda j, be, nu: (be[used(j, nu)], 0, 0)),
                pl.BlockSpec((1, d, D_EXPERT), lambda j, be, nu: (be[used(j, nu)], 0, 0)),
                pl.BlockSpec((1, D_EXPERT, d), lambda j, be, nu: (be[used(j, nu)], 0, 0)),
            ],
            out_specs=pl.BlockSpec((blk, d // 2), lambda j, be, nu: (j, 0)),
            scratch_shapes=[pltpu.VMEM((d, 2 * D_EXPERT), BF16), pltpu.VMEM((D_EXPERT, d), BF16)],
        ),
        out_shape=jax.ShapeDtypeStruct((nblk * blk, d // 2), jnp.uint32),
        compiler_params=_params("arbitrary"),
        name="experts",
    )(blk_expert, n_used, xbuf, expert_w_gate, expert_w_up, expert_w_down)

    gathered = _sc_gather_rows(ybuf, dest).reshape(n, TOP_K * (d // 2))
    return pl.pallas_call(
        _combine_kernel,
        grid=(n // tp,),
        in_specs=[_rows(tp, d), _rows(tp, LANES), _full((1, d)), _rows(tp, d)],
        out_specs=_rows(tp, d),
        out_shape=jax.ShapeDtypeStruct((n, d), F32),
        compiler_params=_params("parallel"),
        name="combine",
    )(h, route, row1(out_norm_w), gathered)


def kernel(x, positions, attn_norm_w, w_in, gla_gate_up, gla_gate_bias, gla_norm_w, mla_q_norm_w,
           mla_w_uq, mla_kv_norm_w, mla_w_ukv, w_out, ffn_norm_w, router_group_w, router_group_b,
           router_expert_w, router_expert_b, expert_w_gate, expert_w_up, expert_w_down,
           final_norm_w):
    batch, seq, d = x.shape
    depth = w_in.shape[0]
    assert depth == 1, "the final norm is fused into the last layer's combine step"
    out = _layer(x.reshape(batch * seq, d), positions.reshape(batch * seq, 1),
                 attn_norm_w[0], w_in[0], gla_gate_up[0], gla_gate_bias[0], gla_norm_w[0],
                 mla_q_norm_w[0], mla_w_uq[0], mla_kv_norm_w[0], mla_w_ukv[0], w_out[0],
                 ffn_norm_w[0], router_group_w[0], router_group_b[0], router_expert_w[0],
                 router_expert_b[0], expert_w_gate[0], expert_w_up[0], expert_w_down[0],
                 final_norm_w, batch, seq)
    return out.reshape(batch, seq, d)
```

```python
import functools

import jax
import jax.numpy as jnp
from jax import lax
from jax.experimental import pallas as pl
from jax.experimental.pallas import tpu as pltpu
from jax.experimental.pallas import tpu_sc as plsc

EPS = 1e-6
GLA_HEADS = 4
GLA_DK = 64
GLA_DV = 128
GLA_GATE_RANK = 16
GLA_GATE_NORM = 16.0
GLA_CHUNK = 64
MLA_HEADS = 8
MLA_NOPE = 64
MLA_ROPE = 32
MLA_DV = 64
MLA_Q_RANK = 384
MLA_KV_RANK = 256
ROPE_THETA = 10000.0
N_GROUPS = 4
EXPERTS_PER_GROUP = 8
N_EXPERTS = N_GROUPS * EXPERTS_PER_GROUP
TOP_K = 2
D_EXPERT = 256

LANES = 128
HEAD_PAD = 128
ROPE_HALF = MLA_ROPE // 2
ROPE_LO = MLA_NOPE
ROPE_HI = MLA_NOPE + ROPE_HALF
ROPE_END = MLA_NOPE + MLA_ROPE
GATE_LO = ROPE_END

PROJ_TILE = 512
GLA_TILE = 256
ATT_TILE = 512
ATT_KEY_SHIFT = 0
ATT_HEADS_PER_STEP = 4
ROW_TILE = 256
ROW_UNROLL = 8
EXPERT_BLOCK = 256
VMEM_LIMIT = 56 * 1024 * 1024
SC_CORES = 2
SC_SUBCORES = 16
SC_WORKERS = SC_CORES * SC_SUBCORES
SC_CHUNK = 64

F32 = jnp.float32
BF16 = jnp.bfloat16
NEG_INF = float("-inf")


def _dot(a, b, precision=None):
    return jnp.dot(a, b, preferred_element_type=F32, precision=precision)


def _dot_nt(a, b):
    return lax.dot_general(a, b, (((1,), (1,)), ((), ())), preferred_element_type=F32)


def _dot_tn(a, b):
    return lax.dot_general(a, b, (((0,), (0,)), ((), ())), preferred_element_type=F32)


def _rms(x, w):
    return x * lax.rsqrt(jnp.mean(x * x, axis=-1, keepdims=True) + EPS) * w


def _pack_bf16_pair(a, b):
    ua = lax.bitcast_convert_type(a.astype(BF16).astype(F32), jnp.uint32)
    ub = lax.bitcast_convert_type(b.astype(BF16).astype(F32), jnp.uint32)
    return (ua >> 16) | (ub & jnp.uint32(0xFFFF0000))


def _unpack_bf16_pair(u):
    a = lax.bitcast_convert_type(u << 16, F32)
    b = lax.bitcast_convert_type(u & jnp.uint32(0xFFFF0000), F32)
    return a, b


def _proj_kernel(x_ref, pos_ref, nw_ref, w1_ref, gu_ref, gb_ref, qnw_ref, wuq_ref, kvnw_ref,
                 wukv_ref, inv_ref,
                 gq_ref, gk_ref, gv_ref, gla_ref, gog_ref, q_ref, k_ref, v_ref, *, q_scale):
    x = x_ref[...]
    xn = _rms(x, nw_ref[...]).astype(BF16)
    proj = _dot(xn, w1_ref[...])
    dq = GLA_HEADS * GLA_DK
    dv = GLA_HEADS * GLA_DV
    o = 0
    gq_ref[...] = (proj[:, o:o + dq] * (GLA_DK ** -0.5)).astype(BF16); o += dq
    gk_ref[...] = proj[:, o:o + dq].astype(BF16); o += dq
    gv_ref[...] = proj[:, o:o + dv].astype(BF16); o += dv
    gog_ref[...] = proj[:, o:o + dv].astype(BF16); o += dv
    cq = proj[:, o:o + MLA_Q_RANK]; o += MLA_Q_RANK
    ckv = proj[:, o:o + MLA_KV_RANK]; o += MLA_KV_RANK
    misc = proj[:, o:o + LANES]

    z = _dot(misc.astype(BF16), gu_ref[...]) + gb_ref[...]
    log_sig = jnp.minimum(z, 0.0) - jnp.log1p(jnp.exp(-jnp.abs(z)))
    gla_ref[...] = log_sig * (1.0 / GLA_GATE_NORM)

    lane = lax.broadcasted_iota(jnp.int32, (x.shape[0], LANES), 1)
    ang = pos_ref[...].astype(F32) * inv_ref[...]
    cosv = jnp.cos(ang)
    sinv = jnp.sin(ang)
    in_lo = (lane >= ROPE_LO) & (lane < ROPE_HI)
    in_hi = (lane >= ROPE_HI) & (lane < ROPE_END)
    c_rope = jnp.where(in_lo | in_hi, cosv, 0.0)
    s_up = jnp.where(in_hi, sinv, 0.0)
    s_dn = jnp.where(in_lo, -sinv, 0.0)
    c_q = jnp.where(lane < MLA_NOPE, 1.0, c_rope)

    def rope(t, c):
        return (t * c + pltpu.roll(t, ROPE_HALF, 1) * s_up
                + pltpu.roll(t, LANES - ROPE_HALF, 1) * s_dn)

    k_rope = rope(misc, c_rope)

    q = _dot(_rms(cq, qnw_ref[...]).astype(BF16), wuq_ref[...])
    kv = _dot(_rms(ckv, kvnw_ref[...]).astype(BF16), wukv_ref[...])
    ones_lane = jnp.where(lane == MLA_DV, 1.0, 0.0)
    for h in range(MLA_HEADS):
        sl = slice(h * HEAD_PAD, (h + 1) * HEAD_PAD)
        vsl = slice((MLA_HEADS + h) * HEAD_PAD, (MLA_HEADS + h + 1) * HEAD_PAD)
        q_ref[:, sl] = (rope(q[:, sl], c_q) * q_scale).astype(BF16)
        k_ref[:, sl] = (kv[:, sl] + k_rope).astype(BF16)
        v_ref[:, sl] = (kv[:, vsl] + ones_lane).astype(BF16)


def _gla_kernel(q_ref, k_ref, v_ref, la_ref, og_ref, nw_ref, o_ref, st_ref):
    t = q_ref.shape[1]
    nchunk = t // GLA_CHUNK

    @pl.when(pl.program_id(1) == 0)
    def _():
        st_ref[...] = jnp.zeros_like(st_ref)

    row = lax.broadcasted_iota(jnp.int32, (t, t), 0)
    col = lax.broadcasted_iota(jnp.int32, (t, t), 1)
    chunk_bits = GLA_CHUNK.bit_length() - 1
    tri = ((row >> chunk_bits) == (col >> chunk_bits)) & (col <= row)
    la = la_ref[0]
    b = _dot(tri.astype(F32), la, precision=lax.Precision.HIGHEST)
    b_last = jnp.concatenate(
        [jnp.broadcast_to(b[(c + 1) * GLA_CHUNK - 1:(c + 1) * GLA_CHUNK], (GLA_CHUNK, b.shape[1]))
         for c in range(nchunk)], axis=0)
    q_e = (q_ref[0].astype(F32) * jnp.exp(b)).astype(BF16)
    kf = k_ref[0].astype(F32)
    k_e = (kf * jnp.exp(-b)).astype(BF16)
    k_d = (kf * jnp.exp(b_last - b)).astype(BF16)
    decay = jnp.exp(b_last)
    nw = nw_ref[...]

    for h in range(GLA_HEADS):
        ks = slice(h * GLA_DK, (h + 1) * GLA_DK)
        vs = slice(h * GLA_DV, (h + 1) * GLA_DV)
        qh, keh, kdh = q_e[:, ks], k_e[:, ks], k_d[:, ks]
        vh = v_ref[0, :, vs]
        att = jnp.where(tri, _dot_nt(qh, keh), 0.0)
        o = _dot(att.astype(BF16), vh)
        state = st_ref[h]
        inter = []
        for c in range(nchunk):
            rs = slice(c * GLA_CHUNK, (c + 1) * GLA_CHUNK)
            inter.append(_dot_nt(qh[rs], state.astype(BF16)))
            upd = _dot_tn(vh[rs], kdh[rs])
            state = state * decay[c * GLA_CHUNK:c * GLA_CHUNK + 1, ks] + upd
        st_ref[h] = state
        o = o + jnp.concatenate(inter, axis=0)
        o = _rms(o, nw)
        g = og_ref[0, :, vs].astype(F32)
        o_ref[0, :, vs] = (o * (g * jax.nn.sigmoid(g))).astype(o_ref.dtype)


def _mla_kernel(q_ref, k_ref, v_ref, o_ref, *, key_shift):
    tq = q_ref.shape[1]
    tk = tq << key_shift
    heads = q_ref.shape[2] // HEAD_PAD
    qi = pl.program_id(2)
    lane = lax.broadcasted_iota(jnp.int32, (tq, LANES), 1)

    def step(j, carry, mask):
        rows = pl.ds(pl.multiple_of(j * tk, tk), tk)
        new = []
        for hh in range(heads):
            m, acc = carry[hh]
            hs = slice(hh * HEAD_PAD, (hh + 1) * HEAD_PAD)
            s = _dot_nt(q_ref[0, :, hs], k_ref[0, rows, hs])
            if mask:
                r = lax.broadcasted_iota(jnp.int32, (tq, tk), 0) + qi * tq
                c = lax.broadcasted_iota(jnp.int32, (tq, tk), 1) + j * tk
                s = jnp.where(c <= r, s, NEG_INF)
            m_new = jnp.maximum(m, jnp.max(s, axis=-1, keepdims=True))
            p = jnp.exp(s - m_new).astype(BF16)
            acc = jnp.exp(m - m_new) * acc + _dot(p, v_ref[0, rows, hs])
            new.append((m_new, acc))
        return tuple(new)

    init = tuple((jnp.full((tq, 1), NEG_INF, F32), jnp.zeros((tq, LANES), F32))
                 for _ in range(heads))
    nfull = lax.shift_right_logical(qi, key_shift)
    carry = lax.fori_loop(0, nfull, lambda j, c: step(j, c, False), init)
    carry = step(nfull, carry, True)
    for hp in range(heads // 2):
        o0, o1 = (acc / acc[:, MLA_DV:MLA_DV + 1] for _, acc in carry[2 * hp:2 * hp + 2])
        o_ref[0, :, hp * LANES:(hp + 1) * LANES] = jnp.where(
            lane < MLA_DV, o0, pltpu.roll(o1, MLA_DV, 1)).astype(o_ref.dtype)


def _router_kernel(x_ref, yg_ref, ym_ref, wo_ref, fnw_ref, wr_ref, br_ref,
                   h_ref, hn_ref, route_ref, cnt_ref, carry_ref):
    t = x_ref.shape[0]
    half = wo_ref.shape[0] // 2

    @pl.when(pl.program_id(0) == 0)
    def _():
        carry_ref[...] = jnp.zeros_like(carry_ref)

    h = x_ref[...] + _dot(yg_ref[...], wo_ref[:half]) + _dot(ym_ref[...], wo_ref[half:])
    h_ref[...] = h
    hn = _rms(h, fnw_ref[...])
    hp = hn.shape[1] // 2
    hn_ref[...] = _pack_bf16_pair(hn[:, :hp], hn[:, hp:])

    hn_hi = hn.astype(BF16)
    hn_lo = (hn - hn_hi.astype(F32)).astype(BF16)
    parts = _dot(hn_hi, wr_ref[...]) + _dot(hn_lo, wr_ref[...])
    logits = parts[:, :LANES] + parts[:, LANES:] + br_ref[...]
    lane = lax.broadcasted_iota(jnp.int32, (t, LANES), 1)
    lane_f = lane.astype(F32)

    def first_argmax(vals, vmax):
        idx = jnp.min(jnp.where(vals == vmax, lane_f, float(LANES)), axis=-1, keepdims=True)
        return idx.astype(jnp.int32)

    gl = jnp.where((lane >= N_EXPERTS) & (lane < N_EXPERTS + N_GROUPS), logits, NEG_INF)
    gmax = jnp.max(gl, axis=-1, keepdims=True)
    gsel = first_argmax(gl, gmax) - N_EXPERTS
    p_g = 1.0 / jnp.sum(jnp.exp(gl - gmax), axis=-1, keepdims=True)
    lo = gsel * EXPERTS_PER_GROUP
    el = jnp.where((lane >= lo) & (lane < lo + EXPERTS_PER_GROUP), logits, NEG_INF)
    m1 = jnp.max(el, axis=-1, keepdims=True)
    i1 = first_argmax(el, m1)
    el2 = jnp.where(lane == i1, NEG_INF, el)
    m2 = jnp.max(el2, axis=-1, keepdims=True)
    i2 = first_argmax(el2, m2)
    e2 = jnp.exp(m2 - m1)
    g1 = p_g / (1.0 + e2)
    g2 = p_g * e2 / (1.0 + e2)

    is1 = lane == i1
    is2 = lane == i2
    onehot = (is1 | is2).astype(BF16)
    rr = lax.broadcasted_iota(jnp.int32, (t, t), 0)
    cc = lax.broadcasted_iota(jnp.int32, (t, t), 1)
    before = _dot((cc < rr).astype(BF16), onehot) + carry_ref[...]
    r1 = jnp.sum(jnp.where(is1, before, 0.0), axis=-1, keepdims=True)
    r2 = jnp.sum(jnp.where(is2, before, 0.0), axis=-1, keepdims=True)
    carry_ref[...] = carry_ref[...] + jnp.sum(onehot.astype(F32), axis=0, keepdims=True)
    cnt_ref[...] = carry_ref[...]

    route = jnp.where(lane == 0, i1.astype(F32), 0.0)
    route = jnp.where(lane == 1, i2.astype(F32), route)
    route = jnp.where(lane == 2, g1, route)
    route = jnp.where(lane == 3, g2, route)
    route = jnp.where(lane == 4, r1, route)
    route = jnp.where(lane == 5, r2, route)
    route_ref[...] = route


def _sc_mesh():
    return plsc.VectorSubcoreMesh(core_axis_name="c", subcore_axis_name="s",
                                  num_cores=SC_CORES, num_subcores=SC_SUBCORES)


def _sc_worker_base(per_worker):
    return (lax.axis_index("s") * SC_CORES + lax.axis_index("c")) * per_worker


def _sc_scatter_rows(rows, idx_list, nrows):
    m, w = rows.shape
    nidx = len(idx_list)
    per_worker = m // SC_WORKERS
    nchunk = per_worker // SC_CHUNK

    @functools.partial(
        pl.kernel, mesh=_sc_mesh(), out_type=jax.ShapeDtypeStruct((nrows, w), rows.dtype),
        scratch_types=[pltpu.VMEM((SC_CHUNK,), jnp.int32)] * nidx
        + [pltpu.VMEM((SC_CHUNK, w), rows.dtype), pltpu.SemaphoreType.DMA],
        name="sc_scatter_rows")
    def scatter(rows_hbm, *refs):
        idx_hbm, out_hbm = refs[:nidx], refs[nidx]
        idx_v, rows_v, sem = refs[nidx + 1:2 * nidx + 1], refs[2 * nidx + 1], refs[2 * nidx + 2]
        base = _sc_worker_base(per_worker)

        @pl.loop(0, nchunk)
        def _(ci):
            off = pl.multiple_of(base + ci * SC_CHUNK, SC_CHUNK)
            pltpu.sync_copy(rows_hbm.at[pl.ds(off, SC_CHUNK)], rows_v)
            for kk in range(nidx):
                pltpu.sync_copy(idx_hbm[kk].at[pl.ds(off, SC_CHUNK)], idx_v[kk])
            copies = [pltpu.async_copy(rows_v, out_hbm.at[idx_v[kk]], sem) for kk in range(nidx)]
            for c in copies:
                c.wait()

    return scatter(rows, *idx_list)


def _expert_kernel(be_ref, nu_ref, bv_ref, x_ref, wg_ref, wu_ref, wd_ref, y_ref, wgu_s, wd_s):
    j = pl.program_id(0)
    de = wg_ref.shape[2]
    last = jnp.maximum(nu_ref[0] - 1, 0)
    cur = be_ref[jnp.minimum(j, last)]
    prev = be_ref[jnp.minimum(jnp.maximum(j - 1, 0), last)]

    @pl.when((j == 0) | (cur != prev))
    def _():
        wgu_s[:, :de] = wg_ref[0].astype(BF16)
        wgu_s[:, de:] = wu_ref[0].astype(BF16)
        wd_s[...] = wd_ref[0].astype(BF16)

    @pl.when(j < nu_ref[0])
    def _():
        row = lax.broadcasted_iota(jnp.int32, x_ref.shape, 0)
        xu = jnp.where(row < bv_ref[j], x_ref[...], jnp.uint32(0))
        a, b = _unpack_bf16_pair(xu)
        x = jnp.concatenate([a, b], axis=1).astype(BF16)
        h12 = _dot(x, wgu_s[...])
        h1, h2 = h12[:, :de], h12[:, de:]
        hdn = (h1 * jax.nn.sigmoid(h1) * h2).astype(BF16)
        y = _dot(hdn, wd_s[...])
        hp = y.shape[1] // 2
        y_ref[...] = _pack_bf16_pair(y[:, :hp], y[:, hp:])

    @pl.when(pl.program_id(0) >= nu_ref[0])
    def _():
        y_ref[...] = jnp.zeros_like(y_ref)


def _sc_gather_rows(table, idx):
    m = idx.shape[0]
    w = table.shape[1]
    per_worker = m // SC_WORKERS
    nchunk = per_worker // SC_CHUNK

    @functools.partial(
        pl.kernel, mesh=_sc_mesh(), out_type=jax.ShapeDtypeStruct((m, w), table.dtype),
        scratch_types=[pltpu.VMEM((SC_CHUNK,), jnp.int32), pltpu.VMEM((SC_CHUNK, w), table.dtype),
                       pltpu.SemaphoreType.DMA],
        name="sc_gather_rows")
    def gather(table_hbm, idx_hbm, out_hbm, idx_v, rows_v, sem):
        base = _sc_worker_base(per_worker)

        @pl.loop(0, nchunk)
        def _(ci):
            off = pl.multiple_of(base + ci * SC_CHUNK, SC_CHUNK)
            pltpu.sync_copy(idx_hbm.at[pl.ds(off, SC_CHUNK)], idx_v)
            pltpu.async_copy(table_hbm.at[idx_v], rows_v, sem).wait()
            pltpu.sync_copy(rows_v, out_hbm.at[pl.ds(off, SC_CHUNK)])

    return gather(table, idx)


def _combine_kernel(h_ref, route_ref, fw_ref, y0_ref, y1_ref, o_ref):
    route = route_ref[...]
    g1 = route[:, 2:3]
    g2 = route[:, 3:4]
    a0, b0 = _unpack_bf16_pair(y0_ref[...])
    a1, b1 = _unpack_bf16_pair(y1_ref[...])
    moe = jnp.concatenate([a0 * g1 + a1 * g2, b0 * g1 + b1 * g2], axis=1)
    o_ref[...] = _rms(h_ref[...] + moe, fw_ref[...])


def _params(*sem):
    return pltpu.CompilerParams(dimension_semantics=sem, vmem_limit_bytes=VMEM_LIMIT)


def _full(shape):
    return pl.BlockSpec(shape, lambda *_: (0,) * len(shape))


def _rows(tile, width):
    return pl.BlockSpec((tile, width), lambda i: (i, 0))


def _layer(x2, pos2, attn_norm_w, w_in, gla_gate_up, gla_gate_bias, gla_norm_w, mla_q_norm_w,
           mla_w_uq, mla_kv_norm_w, mla_w_ukv, w_out, ffn_norm_w, router_group_w, router_group_b,
           router_expert_w, router_expert_b, expert_w_gate, expert_w_up, expert_w_down,
           out_norm_w, batch, seq):
    n, d = x2.shape
    dq = GLA_HEADS * GLA_DK
    dv = GLA_HEADS * GLA_DV
    dmla = MLA_HEADS * HEAD_PAD
    dmv = MLA_HEADS * MLA_DV

    c_gq, c_gk, c_gv, c_lr, c_og, c_cq, c_ckv, c_kr = jnp.split(
        w_in, [dq, 2 * dq, 2 * dq + dv, 2 * dq + dv + GLA_GATE_RANK,
               2 * dq + 2 * dv + GLA_GATE_RANK,
               2 * dq + 2 * dv + GLA_GATE_RANK + MLA_Q_RANK,
               2 * dq + 2 * dv + GLA_GATE_RANK + MLA_Q_RANK + MLA_KV_RANK], axis=1)
    zeros = lambda r, c: jnp.zeros((r, c), w_in.dtype)
    misc = jnp.concatenate([zeros(d, MLA_NOPE), c_kr, c_lr,
                            zeros(d, LANES - ROPE_END - GLA_GATE_RANK)], axis=1)
    w1 = jnp.concatenate([c_gq, c_gk, c_gv, c_og, c_cq, c_ckv, misc], axis=1).astype(BF16)
    gate_up = jnp.concatenate([zeros(GATE_LO, dq), gla_gate_up,
                               zeros(LANES - GATE_LO - GLA_GATE_RANK, dq)], axis=0).astype(BF16)
    wuq = mla_w_uq.reshape(MLA_Q_RANK, MLA_HEADS, MLA_NOPE + MLA_ROPE)
    wuq = jnp.pad(wuq, ((0, 0), (0, 0), (0, HEAD_PAD - ROPE_END))).reshape(MLA_Q_RANK, dmla)
    wukv = mla_w_ukv.reshape(MLA_KV_RANK, MLA_HEADS, MLA_NOPE + MLA_DV)
    wuk = jnp.pad(wukv[:, :, :MLA_NOPE], ((0, 0), (0, 0), (0, HEAD_PAD - MLA_NOPE)))
    wuv = jnp.pad(wukv[:, :, MLA_NOPE:], ((0, 0), (0, 0), (0, HEAD_PAD - MLA_DV)))
    wukv = jnp.concatenate([wuk.reshape(MLA_KV_RANK, dmla), wuv.reshape(MLA_KV_RANK, dmla)],
                           axis=1)
    inv = ROPE_THETA ** (-jnp.arange(ROPE_HALF, dtype=F32) / ROPE_HALF)
    inv_pat = jnp.concatenate([jnp.zeros((MLA_NOPE,), F32), inv, inv,
                               jnp.zeros((LANES - ROPE_END,), F32)]).reshape(1, LANES)
    w_router = jnp.concatenate(
        [router_expert_w, router_group_w, zeros(d, LANES - N_EXPERTS - N_GROUPS)], axis=1)
    w_router_hi = w_router.astype(BF16)
    w_router = jnp.concatenate(
        [w_router_hi, (w_router - w_router_hi.astype(F32)).astype(BF16)], axis=1)
    b_router = jnp.concatenate(
        [router_expert_b, router_group_b, jnp.zeros((LANES - N_EXPERTS - N_GROUPS,), F32)]
    ).reshape(1, LANES)
    row1 = lambda v: v.reshape(1, -1)

    tp = PROJ_TILE
    outs = pl.pallas_call(
        functools.partial(_proj_kernel, q_scale=(MLA_NOPE + MLA_ROPE) ** -0.5),
        grid=(n // tp,),
        in_specs=[_rows(tp, d), _rows(tp, 1), _full((1, d)), _full(w1.shape), _full(gate_up.shape),
                  _full((1, dq)), _full((1, MLA_Q_RANK)), _full(wuq.shape),
                  _full((1, MLA_KV_RANK)), _full(wukv.shape), _full((1, LANES))],
        out_specs=[_rows(tp, dq), _rows(tp, dq), _rows(tp, dv), _rows(tp, dq), _rows(tp, dv),
                   _rows(tp, dmla), _rows(tp, dmla), _rows(tp, dmla)],
        out_shape=[jax.ShapeDtypeStruct((n, dq), BF16), jax.ShapeDtypeStruct((n, dq), BF16),
                   jax.ShapeDtypeStruct((n, dv), BF16), jax.ShapeDtypeStruct((n, dq), F32),
                   jax.ShapeDtypeStruct((n, dv), BF16), jax.ShapeDtypeStruct((n, dmla), BF16),
                   jax.ShapeDtypeStruct((n, dmla), BF16), jax.ShapeDtypeStruct((n, dmla), BF16)],
        compiler_params=_params("parallel"),
        name="in_proj",
    )(x2, pos2, row1(attn_norm_w), w1, gate_up, row1(gla_gate_bias), row1(mla_q_norm_w),
      wuq.astype(BF16), row1(mla_kv_norm_w), wukv.astype(BF16), inv_pat)
    gq, gk, gv, gla, gog, q, k, v = outs

    tg = GLA_TILE
    seq3 = lambda a: a.reshape(batch, seq, a.shape[-1])
    gspec = lambda w: pl.BlockSpec((1, tg, w), lambda b, i: (b, i, 0))
    y_gla = pl.pallas_call(
        _gla_kernel,
        grid=(batch, seq // tg),
        in_specs=[gspec(dq), gspec(dq), gspec(dv), gspec(dq), gspec(dv), _full((1, GLA_DV))],
        out_specs=gspec(dv),
        out_shape=jax.ShapeDtypeStruct((batch, seq, dv), BF16),
        scratch_shapes=[pltpu.VMEM((GLA_HEADS, GLA_DV, GLA_DK), F32)],
        compiler_params=_params("parallel", "arbitrary"),
        name="gla",
    )(seq3(gq), seq3(gk), seq3(gv), seq3(gla), seq3(gog), row1(gla_norm_w))

    ta = ATT_TILE
    hps = ATT_HEADS_PER_STEP
    y_mla = pl.pallas_call(
        functools.partial(_mla_kernel, key_shift=ATT_KEY_SHIFT),
        grid=(batch, MLA_HEADS // hps, seq // ta),
        in_specs=[pl.BlockSpec((1, ta, hps * HEAD_PAD), lambda b, hp, i: (b, i, hp)),
                  pl.BlockSpec((1, seq, hps * HEAD_PAD), lambda b, hp, i: (b, 0, hp)),
                  pl.BlockSpec((1, seq, hps * HEAD_PAD), lambda b, hp, i: (b, 0, hp))],
        out_specs=pl.BlockSpec((1, ta, hps * MLA_DV), lambda b, hp, i: (b, i, hp)),
        out_shape=jax.ShapeDtypeStruct((batch, seq, dmv), BF16),
        compiler_params=_params("parallel", "parallel", "arbitrary"),
        name="mla",
    )(seq3(q), seq3(k), seq3(v))

    h, hn, route, counts = pl.pallas_call(
        _router_kernel,
        grid=(n // tp,),
        in_specs=[_rows(tp, d), _rows(tp, dv), _rows(tp, dmv), _full(w_out.shape), _full((1, d)),
                  _full(w_router.shape), _full((1, LANES))],
        out_specs=[_rows(tp, d), _rows(tp, d // 2), _rows(tp, LANES), _full((1, LANES))],
        out_shape=[jax.ShapeDtypeStruct((n, d), F32), jax.ShapeDtypeStruct((n, d // 2), jnp.uint32),
                   jax.ShapeDtypeStruct((n, LANES), F32), jax.ShapeDtypeStruct((1, LANES), F32)],
        scratch_shapes=[pltpu.VMEM((1, LANES), F32)],
        compiler_params=_params("arbitrary"),
        name="out_proj_router",
    )(x2, y_gla.reshape(n, dv), y_mla.reshape(n, dmv), w_out.astype(BF16), row1(ffn_norm_w),
      w_router, b_router)

    blk = EXPERT_BLOCK
    nblk = (n * TOP_K) // blk + N_EXPERTS
    cnt = counts[0, :N_EXPERTS].astype(jnp.int32)
    padded = (cnt + blk - 1) // blk * blk
    pend = jnp.cumsum(padded)
    pstart = pend - padded
    eid = route[:, 0:TOP_K].astype(jnp.int32)
    rank = route[:, 4:4 + TOP_K].astype(jnp.int32)
    pstart_of = jnp.sum(jnp.where(eid[..., None] == jnp.arange(N_EXPERTS, dtype=jnp.int32),
                                  pstart, 0), axis=-1)
    dest = pstart_of + rank
    blk_start = jnp.arange(nblk, dtype=jnp.int32) * blk
    blk_expert = jnp.minimum(
        jnp.sum((pend[None, :] <= blk_start[:, None]).astype(jnp.int32), axis=1), N_EXPERTS - 1)
    n_used = (pend[-1] // blk).astype(jnp.int32).reshape(1)
    blk_valid = jnp.clip(cnt[blk_expert] - (blk_start - pstart[blk_expert]), 0, blk)
    blk_valid = jnp.where(jnp.arange(nblk) < n_used[0], blk_valid, 0).astype(jnp.int32)
    dest_slots = [dest[:, kk] for kk in range(TOP_K)]

    xbuf = _sc_scatter_rows(hn, dest_slots, nblk * blk)

    def used(j, nu):
        return jnp.maximum(jnp.minimum(j, nu[0] - 1), 0)

    ybuf = pl.pallas_call(
        _expert_kernel,
        grid_spec=pltpu.PrefetchScalarGridSpec(
            num_scalar_prefetch=3,
            grid=(nblk,),
            in_specs=[
                pl.BlockSpec((blk, d // 2), lambda j, be, nu, bv: (used(j, nu), 0)),
                pl.BlockSpec((1, d, D_EXPERT), lambda j, be, nu, bv: (be[used(j, nu)], 0, 0)),
                pl.BlockSpec((1, d, D_EXPERT), lambda j, be, nu, bv: (be[used(j, nu)], 0, 0)),
                pl.BlockSpec((1, D_EXPERT, d), lambda j, be, nu, bv: (be[used(j, nu)], 0, 0)),
            ],
            out_specs=pl.BlockSpec((blk, d // 2), lambda j, be, nu, bv: (j, 0)),
            scratch_shapes=[pltpu.VMEM((d, 2 * D_EXPERT), BF16), pltpu.VMEM((D_EXPERT, d), BF16)],
        ),
        out_shape=jax.ShapeDtypeStruct((nblk * blk, d // 2), jnp.uint32),
        compiler_params=_params("arbitrary"),
        name="experts",
    )(blk_expert, n_used, blk_valid, xbuf, expert_w_gate, expert_w_up, expert_w_down)

    gathered = _sc_gather_rows(ybuf, jnp.concatenate(dest_slots))
    nt = n // tp
    return pl.pallas_call(
        _combine_kernel,
        grid=(nt,),
        in_specs=[_rows(tp, d), _rows(tp, LANES), _full((1, d)), _rows(tp, d // 2),
                  pl.BlockSpec((tp, d // 2), lambda i: (i + nt, 0))],
        out_specs=_rows(tp, d),
        out_shape=jax.ShapeDtypeStruct((n, d), F32),
        compiler_params=_params("parallel"),
        name="combine",
    )(h, route, row1(out_norm_w), gathered, gathered)


def kernel(x, positions, attn_norm_w, w_in, gla_gate_up, gla_gate_bias, gla_norm_w, mla_q_norm_w,
           mla_w_uq, mla_kv_norm_w, mla_w_ukv, w_out, ffn_norm_w, router_group_w, router_group_b,
           router_expert_w, router_expert_b, expert_w_gate, expert_w_up, expert_w_down,
           final_norm_w):
    batch, seq, d = x.shape
    depth = w_in.shape[0]
    assert depth == 1, "the final norm is fused into the last layer's combine step"
    out = _layer(x.reshape(batch * seq, d), positions.reshape(batch * seq, 1),
                 attn_norm_w[0], w_in[0], gla_gate_up[0], gla_gate_bias[0], gla_norm_w[0],
                 mla_q_norm_w[0], mla_w_uq[0], mla_kv_norm_w[0], mla_w_ukv[0], w_out[0],
                 ffn_norm_w[0], router_group_w[0], router_group_b[0], router_expert_w[0],
                 router_expert_b[0], expert_w_gate[0], expert_w_up[0], expert_w_down[0],
                 final_norm_w, batch, seq)
    return out.reshape(batch, seq, d)
```

```python
import functools

import jax
import jax.numpy as jnp
from jax import lax
from jax.experimental import pallas as pl
from jax.experimental.pallas import tpu as pltpu
from jax.experimental.pallas import tpu_sc as plsc

EPS = 1e-6
GLA_HEADS = 4
GLA_DK = 64
GLA_DV = 128
GLA_GATE_RANK = 16
GLA_GATE_NORM = 16.0
GLA_CHUNK = 64
MLA_HEADS = 8
MLA_NOPE = 64
MLA_ROPE = 32
MLA_DV = 64
MLA_Q_RANK = 384
MLA_KV_RANK = 256
ROPE_THETA = 10000.0
N_GROUPS = 4
EXPERTS_PER_GROUP = 8
N_EXPERTS = N_GROUPS * EXPERTS_PER_GROUP
TOP_K = 2
D_EXPERT = 256

LANES = 128
HEAD_PAD = 128
ROPE_HALF = MLA_ROPE // 2
ROPE_LO = MLA_NOPE
ROPE_HI = MLA_NOPE + ROPE_HALF
ROPE_END = MLA_NOPE + MLA_ROPE
GATE_LO = ROPE_END

PROJ_TILE = 512
GLA_TILE = 256
ATT_TILE = 512
ATT_KEY_SHIFT = 0
ATT_HEADS_PER_STEP = 4
ROW_TILE = 256
ROW_UNROLL = 8
EXPERT_BLOCK = 256
VMEM_LIMIT = 56 * 1024 * 1024
SC_CORES = 2
SC_SUBCORES = 16
SC_WORKERS = SC_CORES * SC_SUBCORES
SC_CHUNK = 64

F32 = jnp.float32
BF16 = jnp.bfloat16
NEG_INF = float("-inf")


def _dot(a, b, precision=None):
    return jnp.dot(a, b, preferred_element_type=F32, precision=precision)


def _dot_nt(a, b):
    return lax.dot_general(a, b, (((1,), (1,)), ((), ())), preferred_element_type=F32)


def _dot_tn(a, b):
    return lax.dot_general(a, b, (((0,), (0,)), ((), ())), preferred_element_type=F32)


def _rms(x, w):
    return x * lax.rsqrt(jnp.mean(x * x, axis=-1, keepdims=True) + EPS) * w


def _pack_bf16_pair(a, b):
    ua = lax.bitcast_convert_type(a.astype(BF16).astype(F32), jnp.uint32)
    ub = lax.bitcast_convert_type(b.astype(BF16).astype(F32), jnp.uint32)
    return (ua >> 16) | (ub & jnp.uint32(0xFFFF0000))


def _unpack_bf16_pair(u):
    a = lax.bitcast_convert_type(u << 16, F32)
    b = lax.bitcast_convert_type(u & jnp.uint32(0xFFFF0000), F32)
    return a, b


def _proj_kernel(x_ref, pos_ref, nw_ref, w1_ref, gu_ref, gb_ref, qnw_ref, wuq_ref, kvnw_ref,
                 wukv_ref, inv_ref,
                 gq_ref, gk_ref, gv_ref, gla_ref, gog_ref, q_ref, k_ref, v_ref, *, q_scale):
    x = x_ref[...]
    xn = _rms(x, nw_ref[...]).astype(BF16)
    proj = _dot(xn, w1_ref[...])
    dq = GLA_HEADS * GLA_DK
    dv = GLA_HEADS * GLA_DV
    o = 0
    gq_ref[...] = (proj[:, o:o + dq] * (GLA_DK ** -0.5)).astype(BF16); o += dq
    gk_ref[...] = proj[:, o:o + dq].astype(BF16); o += dq
    gv_ref[...] = proj[:, o:o + dv].astype(BF16); o += dv
    gog_ref[...] = proj[:, o:o + dv].astype(BF16); o += dv
    cq = proj[:, o:o + MLA_Q_RANK]; o += MLA_Q_RANK
    ckv = proj[:, o:o + MLA_KV_RANK]; o += MLA_KV_RANK
    misc = proj[:, o:o + LANES]

    z = _dot(misc.astype(BF16), gu_ref[...]) + gb_ref[...]
    log_sig = jnp.minimum(z, 0.0) - jnp.log1p(jnp.exp(-jnp.abs(z)))
    gla_ref[...] = log_sig * (1.0 / GLA_GATE_NORM)

    lane = lax.broadcasted_iota(jnp.int32, (x.shape[0], LANES), 1)
    ang = pos_ref[...].astype(F32) * inv_ref[...]
    cosv = jnp.cos(ang)
    sinv = jnp.sin(ang)
    in_lo = (lane >= ROPE_LO) & (lane < ROPE_HI)
    in_hi = (lane >= ROPE_HI) & (lane < ROPE_END)
    c_rope = jnp.where(in_lo | in_hi, cosv, 0.0)
    s_up = jnp.where(in_hi, sinv, 0.0)
    s_dn = jnp.where(in_lo, -sinv, 0.0)
    c_q = jnp.where(lane < MLA_NOPE, 1.0, c_rope)

    def rope(t, c):
        return (t * c + pltpu.roll(t, ROPE_HALF, 1) * s_up
                + pltpu.roll(t, LANES - ROPE_HALF, 1) * s_dn)

    k_rope = rope(misc, c_rope)

    q = _dot(_rms(cq, qnw_ref[...]).astype(BF16), wuq_ref[...])
    kv = _dot(_rms(ckv, kvnw_ref[...]).astype(BF16), wukv_ref[...])
    ones_lane = jnp.where(lane == MLA_DV, 1.0, 0.0)
    for h in range(MLA_HEADS):
        sl = slice(h * HEAD_PAD, (h + 1) * HEAD_PAD)
        vsl = slice((MLA_HEADS + h) * HEAD_PAD, (MLA_HEADS + h + 1) * HEAD_PAD)
        q_ref[:, sl] = (rope(q[:, sl], c_q) * q_scale).astype(BF16)
        k_ref[:, sl] = (kv[:, sl] + k_rope).astype(BF16)
        v_ref[:, sl] = (kv[:, vsl] + ones_lane).astype(BF16)


def _gla_kernel(q_ref, k_ref, v_ref, la_ref, og_ref, nw_ref, o_ref, st_ref):
    t = q_ref.shape[1]
    nchunk = t // GLA_CHUNK

    @pl.when(pl.program_id(1) == 0)
    def _():
        st_ref[...] = jnp.zeros_like(st_ref)

    row = lax.broadcasted_iota(jnp.int32, (t, t), 0)
    col = lax.broadcasted_iota(jnp.int32, (t, t), 1)
    chunk_bits = GLA_CHUNK.bit_length() - 1
    tri = ((row >> chunk_bits) == (col >> chunk_bits)) & (col <= row)
    la = la_ref[0]
    la_hi = la.astype(BF16)
    la_lo = (la - la_hi.astype(F32)).astype(BF16)
    parts = _dot(tri.astype(BF16), jnp.concatenate([la_hi, la_lo], axis=1))
    b = parts[:, :la.shape[1]] + parts[:, la.shape[1]:]
    b_last = jnp.concatenate(
        [jnp.broadcast_to(b[(c + 1) * GLA_CHUNK - 1:(c + 1) * GLA_CHUNK], (GLA_CHUNK, b.shape[1]))
         for c in range(nchunk)], axis=0)
    q_e = (q_ref[0].astype(F32) * jnp.exp(b)).astype(BF16)
    kf = k_ref[0].astype(F32)
    k_e = (kf * jnp.exp(-b)).astype(BF16)
    k_d = (kf * jnp.exp(b_last - b)).astype(BF16)
    decay = jnp.exp(b_last)
    nw = nw_ref[...]

    for h in range(GLA_HEADS):
        ks = slice(h * GLA_DK, (h + 1) * GLA_DK)
        vs = slice(h * GLA_DV, (h + 1) * GLA_DV)
        qh, keh, kdh = q_e[:, ks], k_e[:, ks], k_d[:, ks]
        vh = v_ref[0, :, vs]
        att = jnp.where(tri, _dot_nt(qh, keh), 0.0)
        o = _dot(att.astype(BF16), vh)
        state = st_ref[h]
        inter = []
        for c in range(nchunk):
            rs = slice(c * GLA_CHUNK, (c + 1) * GLA_CHUNK)
            inter.append(_dot_nt(qh[rs], state.astype(BF16)))
            upd = _dot_tn(vh[rs], kdh[rs])
            state = state * decay[c * GLA_CHUNK:c * GLA_CHUNK + 1, ks] + upd
        st_ref[h] = state
        o = o + jnp.concatenate(inter, axis=0)
        o = _rms(o, nw)
        g = og_ref[0, :, vs].astype(F32)
        o_ref[0, :, vs] = (o * (g * jax.nn.sigmoid(g))).astype(o_ref.dtype)


def _mla_kernel(q_ref, k_ref, v_ref, o_ref, s0_ref, s1_ref, *, key_shift):
    tq = q_ref.shape[1]
    tk = tq << key_shift
    heads = q_ref.shape[2] // HEAD_PAD
    qi = pl.program_id(2)
    lane = lax.broadcasted_iota(jnp.int32, (tq, LANES), 1)
    slots = (s0_ref, s1_ref)

    def key_rows(j):
        return pl.ds(pl.multiple_of(j * tk, tk), tk)

    def produce(j, slot):
        for hh in range(heads):
            hs = slice(hh * HEAD_PAD, (hh + 1) * HEAD_PAD)
            slots[slot][hh] = _dot_nt(q_ref[0, :, hs], k_ref[0, key_rows(j), hs])

    def consume(j, slot, carry, mask):
        new = []
        for hh in range(heads):
            m, acc = carry[hh]
            hs = slice(hh * HEAD_PAD, (hh + 1) * HEAD_PAD)
            s = slots[slot][hh]
            if mask:
                r = lax.broadcasted_iota(jnp.int32, (tq, tk), 0) + qi * tq
                c = lax.broadcasted_iota(jnp.int32, (tq, tk), 1) + j * tk
                s = jnp.where(c <= r, s, NEG_INF)
            m_new = jnp.maximum(m, jnp.max(s, axis=-1, keepdims=True))
            p = jnp.exp(s - m_new).astype(BF16)
            acc = jnp.exp(m - m_new) * acc + _dot(p, v_ref[0, key_rows(j), hs])
            new.append((m_new, acc))
        return tuple(new)

    def finish(carry):
        for hp in range(heads // 2):
            o0, o1 = (acc / acc[:, MLA_DV:MLA_DV + 1] for _, acc in carry[2 * hp:2 * hp + 2])
            o_ref[0, :, hp * LANES:(hp + 1) * LANES] = jnp.where(
                lane < MLA_DV, o0, pltpu.roll(o1, MLA_DV, 1)).astype(o_ref.dtype)

    def pair(i, carry):
        j = 2 * i
        produce(j + 1, 1)
        carry = consume(j, 0, carry, False)
        produce(j + 2, 0)
        return consume(j + 1, 1, carry, False)

    init = tuple((jnp.full((tq, 1), NEG_INF, F32), jnp.zeros((tq, LANES), F32))
                 for _ in range(heads))
    nfull = lax.shift_right_logical(qi, key_shift)
    produce(0, 0)
    carry = lax.fori_loop(0, lax.shift_right_logical(nfull, 1), pair, init)
    odd = (nfull & 1) == 1

    @pl.when(odd)
    def _():
        produce(nfull, 1)
        finish(consume(nfull, 1, consume(nfull - 1, 0, carry, False), True))

    @pl.when(jnp.logical_not(odd))
    def _():
        finish(consume(nfull, 0, carry, True))


def _router_kernel(x_ref, yg_ref, ym_ref, wo_ref, fnw_ref, wr_ref, br_ref,
                   h_ref, hn_ref, route_ref, cnt_ref, carry_ref):
    t = x_ref.shape[0]
    half = wo_ref.shape[0] // 2

    @pl.when(pl.program_id(0) == 0)
    def _():
        carry_ref[...] = jnp.zeros_like(carry_ref)

    h = x_ref[...] + _dot(yg_ref[...], wo_ref[:half]) + _dot(ym_ref[...], wo_ref[half:])
    h_ref[...] = h
    hn = _rms(h, fnw_ref[...])
    hp = hn.shape[1] // 2
    hn_ref[...] = _pack_bf16_pair(hn[:, :hp], hn[:, hp:])

    hn_hi = hn.astype(BF16)
    hn_lo = (hn - hn_hi.astype(F32)).astype(BF16)
    parts = _dot(hn_hi, wr_ref[...]) + _dot(hn_lo, wr_ref[...])
    logits = parts[:, :LANES] + parts[:, LANES:] + br_ref[...]
    lane = lax.broadcasted_iota(jnp.int32, (t, LANES), 1)
    lane_f = lane.astype(F32)

    def first_argmax(vals, vmax):
        idx = jnp.min(jnp.where(vals == vmax, lane_f, float(LANES)), axis=-1, keepdims=True)
        return idx.astype(jnp.int32)

    gl = jnp.where((lane >= N_EXPERTS) & (lane < N_EXPERTS + N_GROUPS), logits, NEG_INF)
    gmax = jnp.max(gl, axis=-1, keepdims=True)
    gsel = first_argmax(gl, gmax) - N_EXPERTS
    p_g = 1.0 / jnp.sum(jnp.exp(gl - gmax), axis=-1, keepdims=True)
    lo = gsel * EXPERTS_PER_GROUP
    el = jnp.where((lane >= lo) & (lane < lo + EXPERTS_PER_GROUP), logits, NEG_INF)
    m1 = jnp.max(el, axis=-1, keepdims=True)
    i1 = first_argmax(el, m1)
    el2 = jnp.where(lane == i1, NEG_INF, el)
    m2 = jnp.max(el2, axis=-1, keepdims=True)
    i2 = first_argmax(el2, m2)
    e2 = jnp.exp(m2 - m1)
    g1 = p_g / (1.0 + e2)
    g2 = p_g * e2 / (1.0 + e2)

    is1 = lane == i1
    is2 = lane == i2
    onehot = (is1 | is2).astype(BF16)
    rr = lax.broadcasted_iota(jnp.int32, (t, t), 0)
    cc = lax.broadcasted_iota(jnp.int32, (t, t), 1)
    before = _dot((cc < rr).astype(BF16), onehot) + carry_ref[...]
    r1 = jnp.sum(jnp.where(is1, before, 0.0), axis=-1, keepdims=True)
    r2 = jnp.sum(jnp.where(is2, before, 0.0), axis=-1, keepdims=True)
    carry_ref[...] = carry_ref[...] + jnp.sum(onehot.astype(F32), axis=0, keepdims=True)
    cnt_ref[...] = carry_ref[...]

    route = jnp.where(lane == 0, i1.astype(F32), 0.0)
    route = jnp.where(lane == 1, i2.astype(F32), route)
    route = jnp.where(lane == 2, g1, route)
    route = jnp.where(lane == 3, g2, route)
    route = jnp.where(lane == 4, r1, route)
    route = jnp.where(lane == 5, r2, route)
    route_ref[...] = route


def _sc_mesh():
    return plsc.VectorSubcoreMesh(core_axis_name="c", subcore_axis_name="s",
                                  num_cores=SC_CORES, num_subcores=SC_SUBCORES)


def _sc_worker_base(per_worker):
    return (lax.axis_index("s") * SC_CORES + lax.axis_index("c")) * per_worker


def _sc_scatter_rows(rows, idx_list, nrows):
    m, w = rows.shape
    nidx = len(idx_list)
    per_worker = m // SC_WORKERS
    nchunk = per_worker // SC_CHUNK

    @functools.partial(
        pl.kernel, mesh=_sc_mesh(), out_type=jax.ShapeDtypeStruct((nrows, w), rows.dtype),
        scratch_types=[pltpu.VMEM((SC_CHUNK,), jnp.int32)] * nidx
        + [pltpu.VMEM((SC_CHUNK, w), rows.dtype), pltpu.SemaphoreType.DMA],
        name="sc_scatter_rows")
    def scatter(rows_hbm, *refs):
        idx_hbm, out_hbm = refs[:nidx], refs[nidx]
        idx_v, rows_v, sem = refs[nidx + 1:2 * nidx + 1], refs[2 * nidx + 1], refs[2 * nidx + 2]
        base = _sc_worker_base(per_worker)

        @pl.loop(0, nchunk)
        def _(ci):
            off = pl.multiple_of(base + ci * SC_CHUNK, SC_CHUNK)
            pltpu.sync_copy(rows_hbm.at[pl.ds(off, SC_CHUNK)], rows_v)
            for kk in range(nidx):
                pltpu.sync_copy(idx_hbm[kk].at[pl.ds(off, SC_CHUNK)], idx_v[kk])
            copies = [pltpu.async_copy(rows_v, out_hbm.at[idx_v[kk]], sem) for kk in range(nidx)]
            for c in copies:
                c.wait()

    return scatter(rows, *idx_list)


def _expert_kernel(be_ref, nu_ref, bv_ref, x_ref, wg_ref, wu_ref, wd_ref, y_ref, wgu_s, wd_s):
    j = pl.program_id(0)
    de = wg_ref.shape[2]
    last = jnp.maximum(nu_ref[0] - 1, 0)
    cur = be_ref[jnp.minimum(j, last)]
    prev = be_ref[jnp.minimum(jnp.maximum(j - 1, 0), last)]

    @pl.when((j == 0) | (cur != prev))
    def _():
        wgu_s[:, :de] = wg_ref[0].astype(BF16)
        wgu_s[:, de:] = wu_ref[0].astype(BF16)
        wd_s[...] = wd_ref[0].astype(BF16)

    @pl.when(j < nu_ref[0])
    def _():
        row = lax.broadcasted_iota(jnp.int32, x_ref.shape, 0)
        xu = jnp.where(row < bv_ref[j], x_ref[...], jnp.uint32(0))
        a, b = _unpack_bf16_pair(xu)
        x = jnp.concatenate([a, b], axis=1).astype(BF16)
        h12 = _dot(x, wgu_s[...])
        h1, h2 = h12[:, :de], h12[:, de:]
        hdn = (h1 * jax.nn.sigmoid(h1) * h2).astype(BF16)
        y = _dot(hdn, wd_s[...])
        hp = y.shape[1] // 2
        y_ref[...] = _pack_bf16_pair(y[:, :hp], y[:, hp:])

    @pl.when(pl.program_id(0) >= nu_ref[0])
    def _():
        y_ref[...] = jnp.zeros_like(y_ref)


def _sc_gather_rows(table, idx):
    m = idx.shape[0]
    w = table.shape[1]
    per_worker = m // SC_WORKERS
    nchunk = per_worker // SC_CHUNK

    @functools.partial(
        pl.kernel, mesh=_sc_mesh(), out_type=jax.ShapeDtypeStruct((m, w), table.dtype),
        scratch_types=[pltpu.VMEM((SC_CHUNK,), jnp.int32), pltpu.VMEM((SC_CHUNK, w), table.dtype),
                       pltpu.SemaphoreType.DMA],
        name="sc_gather_rows")
    def gather(table_hbm, idx_hbm, out_hbm, idx_v, rows_v, sem):
        base = _sc_worker_base(per_worker)

        @pl.loop(0, nchunk)
        def _(ci):
            off = pl.multiple_of(base + ci * SC_CHUNK, SC_CHUNK)
            pltpu.sync_copy(idx_hbm.at[pl.ds(off, SC_CHUNK)], idx_v)
            pltpu.async_copy(table_hbm.at[idx_v], rows_v, sem).wait()
            pltpu.sync_copy(rows_v, out_hbm.at[pl.ds(off, SC_CHUNK)])

    return gather(table, idx)


def _combine_kernel(h_ref, route_ref, fw_ref, y0_ref, y1_ref, o_ref):
    route = route_ref[...]
    g1 = route[:, 2:3]
    g2 = route[:, 3:4]
    a0, b0 = _unpack_bf16_pair(y0_ref[...])
    a1, b1 = _unpack_bf16_pair(y1_ref[...])
    moe = jnp.concatenate([a0 * g1 + a1 * g2, b0 * g1 + b1 * g2], axis=1)
    o_ref[...] = _rms(h_ref[...] + moe, fw_ref[...])


def _params(*sem):
    return pltpu.CompilerParams(dimension_semantics=sem, vmem_limit_bytes=VMEM_LIMIT)


def _full(shape):
    return pl.BlockSpec(shape, lambda *_: (0,) * len(shape))


def _rows(tile, width):
    return pl.BlockSpec((tile, width), lambda i: (i, 0))


def _layer(x2, pos2, attn_norm_w, w_in, gla_gate_up, gla_gate_bias, gla_norm_w, mla_q_norm_w,
           mla_w_uq, mla_kv_norm_w, mla_w_ukv, w_out, ffn_norm_w, router_group_w, router_group_b,
           router_expert_w, router_expert_b, expert_w_gate, expert_w_up, expert_w_down,
           out_norm_w, batch, seq):
    n, d = x2.shape
    dq = GLA_HEADS * GLA_DK
    dv = GLA_HEADS * GLA_DV
    dmla = MLA_HEADS * HEAD_PAD
    dmv = MLA_HEADS * MLA_DV

    c_gq, c_gk, c_gv, c_lr, c_og, c_cq, c_ckv, c_kr = jnp.split(
        w_in, [dq, 2 * dq, 2 * dq + dv, 2 * dq + dv + GLA_GATE_RANK,
               2 * dq + 2 * dv + GLA_GATE_RANK,
               2 * dq + 2 * dv + GLA_GATE_RANK + MLA_Q_RANK,
               2 * dq + 2 * dv + GLA_GATE_RANK + MLA_Q_RANK + MLA_KV_RANK], axis=1)
    zeros = lambda r, c: jnp.zeros((r, c), w_in.dtype)
    misc = jnp.concatenate([zeros(d, MLA_NOPE), c_kr, c_lr,
                            zeros(d, LANES - ROPE_END - GLA_GATE_RANK)], axis=1)
    w1 = jnp.concatenate([c_gq, c_gk, c_gv, c_og, c_cq, c_ckv, misc], axis=1).astype(BF16)
    gate_up = jnp.concatenate([zeros(GATE_LO, dq), gla_gate_up,
                               zeros(LANES - GATE_LO - GLA_GATE_RANK, dq)], axis=0).astype(BF16)
    wuq = mla_w_uq.reshape(MLA_Q_RANK, MLA_HEADS, MLA_NOPE + MLA_ROPE)
    wuq = jnp.pad(wuq, ((0, 0), (0, 0), (0, HEAD_PAD - ROPE_END))).reshape(MLA_Q_RANK, dmla)
    wukv = mla_w_ukv.reshape(MLA_KV_RANK, MLA_HEADS, MLA_NOPE + MLA_DV)
    wuk = jnp.pad(wukv[:, :, :MLA_NOPE], ((0, 0), (0, 0), (0, HEAD_PAD - MLA_NOPE)))
    wuv = jnp.pad(wukv[:, :, MLA_NOPE:], ((0, 0), (0, 0), (0, HEAD_PAD - MLA_DV)))
    wukv = jnp.concatenate([wuk.reshape(MLA_KV_RANK, dmla), wuv.reshape(MLA_KV_RANK, dmla)],
                           axis=1)
    inv = ROPE_THETA ** (-jnp.arange(ROPE_HALF, dtype=F32) / ROPE_HALF)
    inv_pat = jnp.concatenate([jnp.zeros((MLA_NOPE,), F32), inv, inv,
                               jnp.zeros((LANES - ROPE_END,), F32)]).reshape(1, LANES)
    w_router = jnp.concatenate(
        [router_expert_w, router_group_w, zeros(d, LANES - N_EXPERTS - N_GROUPS)], axis=1)
    w_router_hi = w_router.astype(BF16)
    w_router = jnp.concatenate(
        [w_router_hi, (w_router - w_router_hi.astype(F32)).astype(BF16)], axis=1)
    b_router = jnp.concatenate(
        [router_expert_b, router_group_b, jnp.zeros((LANES - N_EXPERTS - N_GROUPS,), F32)]
    ).reshape(1, LANES)
    row1 = lambda v: v.reshape(1, -1)

    tp = PROJ_TILE
    outs = pl.pallas_call(
        functools.partial(_proj_kernel, q_scale=(MLA_NOPE + MLA_ROPE) ** -0.5),
        grid=(n // tp,),
        in_specs=[_rows(tp, d), _rows(tp, 1), _full((1, d)), _full(w1.shape), _full(gate_up.shape),
                  _full((1, dq)), _full((1, MLA_Q_RANK)), _full(wuq.shape),
                  _full((1, MLA_KV_RANK)), _full(wukv.shape), _full((1, LANES))],
        out_specs=[_rows(tp, dq), _rows(tp, dq), _rows(tp, dv), _rows(tp, dq), _rows(tp, dv),
                   _rows(tp, dmla), _rows(tp, dmla), _rows(tp, dmla)],
        out_shape=[jax.ShapeDtypeStruct((n, dq), BF16), jax.ShapeDtypeStruct((n, dq), BF16),
                   jax.ShapeDtypeStruct((n, dv), BF16), jax.ShapeDtypeStruct((n, dq), F32),
                   jax.ShapeDtypeStruct((n, dv), BF16), jax.ShapeDtypeStruct((n, dmla), BF16),
                   jax.ShapeDtypeStruct((n, dmla), BF16), jax.ShapeDtypeStruct((n, dmla), BF16)],
        compiler_params=_params("parallel"),
        name="in_proj",
    )(x2, pos2, row1(attn_norm_w), w1, gate_up, row1(gla_gate_bias), row1(mla_q_norm_w),
      wuq.astype(BF16), row1(mla_kv_norm_w), wukv.astype(BF16), inv_pat)
    gq, gk, gv, gla, gog, q, k, v = outs

    tg = GLA_TILE
    seq3 = lambda a: a.reshape(batch, seq, a.shape[-1])
    gspec = lambda w: pl.BlockSpec((1, tg, w), lambda b, i: (b, i, 0))
    y_gla = pl.pallas_call(
        _gla_kernel,
        grid=(batch, seq // tg),
        in_specs=[gspec(dq), gspec(dq), gspec(dv), gspec(dq), gspec(dv), _full((1, GLA_DV))],
        out_specs=gspec(dv),
        out_shape=jax.ShapeDtypeStruct((batch, seq, dv), BF16),
        scratch_shapes=[pltpu.VMEM((GLA_HEADS, GLA_DV, GLA_DK), F32)],
        compiler_params=_params("parallel", "arbitrary"),
        name="gla",
    )(seq3(gq), seq3(gk), seq3(gv), seq3(gla), seq3(gog), row1(gla_norm_w))

    ta = ATT_TILE
    hps = ATT_HEADS_PER_STEP
    y_mla = pl.pallas_call(
        functools.partial(_mla_kernel, key_shift=ATT_KEY_SHIFT),
        grid=(batch, MLA_HEADS // hps, seq // ta),
        in_specs=[pl.BlockSpec((1, ta, hps * HEAD_PAD), lambda b, hp, i: (b, i, hp)),
                  pl.BlockSpec((1, seq, hps * HEAD_PAD), lambda b, hp, i: (b, 0, hp)),
                  pl.BlockSpec((1, seq, hps * HEAD_PAD), lambda b, hp, i: (b, 0, hp))],
        out_specs=pl.BlockSpec((1, ta, hps * MLA_DV), lambda b, hp, i: (b, i, hp)),
        out_shape=jax.ShapeDtypeStruct((batch, seq, dmv), BF16),
        scratch_shapes=[pltpu.VMEM((hps, ta, ta << ATT_KEY_SHIFT), F32)] * 2,
        compiler_params=_params("parallel", "parallel", "arbitrary"),
        name="mla",
    )(seq3(q), seq3(k), seq3(v))

    h, hn, route, counts = pl.pallas_call(
        _router_kernel,
        grid=(n // tp,),
        in_specs=[_rows(tp, d), _rows(tp, dv), _rows(tp, dmv), _full(w_out.shape), _full((1, d)),
                  _full(w_router.shape), _full((1, LANES))],
        out_specs=[_rows(tp, d), _rows(tp, d // 2), _rows(tp, LANES), _full((1, LANES))],
        out_shape=[jax.ShapeDtypeStruct((n, d), F32), jax.ShapeDtypeStruct((n, d // 2), jnp.uint32),
                   jax.ShapeDtypeStruct((n, LANES), F32), jax.ShapeDtypeStruct((1, LANES), F32)],
        scratch_shapes=[pltpu.VMEM((1, LANES), F32)],
        compiler_params=_params("arbitrary"),
        name="out_proj_router",
    )(x2, y_gla.reshape(n, dv), y_mla.reshape(n, dmv), w_out.astype(BF16), row1(ffn_norm_w),
      w_router, b_router)

    blk = EXPERT_BLOCK
    nblk = (n * TOP_K) // blk + N_EXPERTS
    cnt = counts[0, :N_EXPERTS].astype(jnp.int32)
    padded = (cnt + blk - 1) // blk * blk
    pend = jnp.cumsum(padded)
    pstart = pend - padded
    eid = route[:, 0:TOP_K].astype(jnp.int32)
    rank = route[:, 4:4 + TOP_K].astype(jnp.int32)
    pstart_of = jnp.sum(jnp.where(eid[..., None] == jnp.arange(N_EXPERTS, dtype=jnp.int32),
                                  pstart, 0), axis=-1)
    dest = pstart_of + rank
    blk_start = jnp.arange(nblk, dtype=jnp.int32) * blk
    blk_expert = jnp.minimum(
        jnp.sum((pend[None, :] <= blk_start[:, None]).astype(jnp.int32), axis=1), N_EXPERTS - 1)
    n_used = (pend[-1] // blk).astype(jnp.int32).reshape(1)
    blk_valid = jnp.clip(cnt[blk_expert] - (blk_start - pstart[blk_expert]), 0, blk)
    blk_valid = jnp.where(jnp.arange(nblk) < n_used[0], blk_valid, 0).astype(jnp.int32)
    dest_slots = [dest[:, kk] for kk in range(TOP_K)]

    xbuf = _sc_scatter_rows(hn, dest_slots, nblk * blk)

    def used(j, nu):
        return jnp.maximum(jnp.minimum(j, nu[0] - 1), 0)

    ybuf = pl.pallas_call(
        _expert_kernel,
        grid_spec=pltpu.PrefetchScalarGridSpec(
            num_scalar_prefetch=3,
            grid=(nblk,),
            in_specs=[
                pl.BlockSpec((blk, d // 2), lambda j, be, nu, bv: (used(j, nu), 0)),
                pl.BlockSpec((1, d, D_EXPERT), lambda j, be, nu, bv: (be[used(j, nu)], 0, 0)),
                pl.BlockSpec((1, d, D_EXPERT), lambda j, be, nu, bv: (be[used(j, nu)], 0, 0)),
                pl.BlockSpec((1, D_EXPERT, d), lambda j, be, nu, bv: (be[used(j, nu)], 0, 0)),
            ],
            out_specs=pl.BlockSpec((blk, d // 2), lambda j, be, nu, bv: (j, 0)),
            scratch_shapes=[pltpu.VMEM((d, 2 * D_EXPERT), BF16), pltpu.VMEM((D_EXPERT, d), BF16)],
        ),
        out_shape=jax.ShapeDtypeStruct((nblk * blk, d // 2), jnp.uint32),
        compiler_params=_params("arbitrary"),
        name="experts",
    )(blk_expert, n_used, blk_valid, xbuf, expert_w_gate, expert_w_up, expert_w_down)

    gathered = _sc_gather_rows(ybuf, jnp.concatenate(dest_slots))
    nt = n // tp
    return pl.pallas_call(
        _combine_kernel,
        grid=(nt,),
        in_specs=[_rows(tp, d), _rows(tp, LANES), _full((1, d)), _rows(tp, d // 2),
                  pl.BlockSpec((tp, d // 2), lambda i: (i + nt, 0))],
        out_specs=_rows(tp, d),
        out_shape=jax.ShapeDtypeStruct((n, d), F32),
        compiler_params=_params("parallel"),
        name="combine",
    )(h, route, row1(out_norm_w), gathered, gathered)


def kernel(x, positions, attn_norm_w, w_in, gla_gate_up, gla_gate_bias, gla_norm_w, mla_q_norm_w,
           mla_w_uq, mla_kv_norm_w, mla_w_ukv, w_out, ffn_norm_w, router_group_w, router_group_b,
           router_expert_w, router_expert_b, expert_w_gate, expert_w_up, expert_w_down,
           final_norm_w):
    batch, seq, d = x.shape
    depth = w_in.shape[0]
    assert depth == 1, "the final norm is fused into the last layer's combine step"
    out = _layer(x.reshape(batch * seq, d), positions.reshape(batch * seq, 1),
                 attn_norm_w[0], w_in[0], gla_gate_up[0], gla_gate_bias[0], gla_norm_w[0],
                 mla_q_norm_w[0], mla_w_uq[0], mla_kv_norm_w[0], mla_w_ukv[0], w_out[0],
                 ffn_norm_w[0], router_group_w[0], router_group_b[0], router_expert_w[0],
                 router_expert_b[0], expert_w_gate[0], expert_w_up[0], expert_w_down[0],
                 final_norm_w, batch, seq)
    return out.reshape(batch, seq, d)
```

```python
import functools

import jax
import jax.numpy as jnp
from jax import lax
from jax.experimental import pallas as pl
from jax.experimental.pallas import tpu as pltpu
from jax.experimental.pallas import tpu_sc as plsc

EPS = 1e-6
GLA_HEADS = 4
GLA_DK = 64
GLA_DV = 128
GLA_GATE_RANK = 16
GLA_GATE_NORM = 16.0
GLA_CHUNK = 64
MLA_HEADS = 8
MLA_NOPE = 64
MLA_ROPE = 32
MLA_DV = 64
MLA_Q_RANK = 384
MLA_KV_RANK = 256
ROPE_THETA = 10000.0
N_GROUPS = 4
EXPERTS_PER_GROUP = 8
N_EXPERTS = N_GROUPS * EXPERTS_PER_GROUP
TOP_K = 2
D_EXPERT = 256

LANES = 128
HEAD_PAD = 128
ROPE_HALF = MLA_ROPE // 2
ROPE_LO = MLA_NOPE
ROPE_HI = MLA_NOPE + ROPE_HALF
ROPE_END = MLA_NOPE + MLA_ROPE
GATE_LO = ROPE_END

PROJ_TILE = 512
GLA_TILE = 512
GLA_SUBTILE = 256
ATT_TILE = 512
ATT_KEY_SHIFT = 0
ATT_HEADS_PER_STEP = 4
EXPERT_BLOCK = 512
ROUTER_TILE = 1024
ROUTE_ROWS = 512
ROUTE_FIELDS = 8
VMEM_LIMIT = 56 * 1024 * 1024
SC_CORES = 2
SC_SUBCORES = 16
SC_WORKERS = SC_CORES * SC_SUBCORES
SC_CHUNK = 64

F32 = jnp.float32
BF16 = jnp.bfloat16
NEG_INF = float("-inf")


def _dot(a, b, precision=None):
    return jnp.dot(a, b, preferred_element_type=F32, precision=precision)


def _dot_nt(a, b):
    return lax.dot_general(a, b, (((1,), (1,)), ((), ())), preferred_element_type=F32)


def _dot_tn(a, b):
    return lax.dot_general(a, b, (((0,), (0,)), ((), ())), preferred_element_type=F32)


def _rms(x, w):
    return x * lax.rsqrt(jnp.mean(x * x, axis=-1, keepdims=True) + EPS) * w


def _pack_bf16_pair(a, b):
    ua = lax.bitcast_convert_type(a.astype(BF16).astype(F32), jnp.uint32)
    ub = lax.bitcast_convert_type(b.astype(BF16).astype(F32), jnp.uint32)
    return (ua >> 16) | (ub & jnp.uint32(0xFFFF0000))


def _unpack_bf16_pair(u):
    a = lax.bitcast_convert_type(u << 16, F32)
    b = lax.bitcast_convert_type(u & jnp.uint32(0xFFFF0000), F32)
    return a, b


def _proj_kernel(x_ref, pos_ref, nw_ref, w1_ref, gu_ref, gb_ref, qnw_ref, wuq_ref, kvnw_ref,
                 wukv_ref, inv_ref,
                 gq_ref, gk_ref, gv_ref, gla_ref, gog_ref, q_ref, k_ref, v_ref, *, q_scale):
    x = x_ref[...]
    xn = _rms(x, nw_ref[...]).astype(BF16)
    proj = _dot(xn, w1_ref[...])
    dq = GLA_HEADS * GLA_DK
    dv = GLA_HEADS * GLA_DV
    o = 0
    gq_ref[...] = (proj[:, o:o + dq] * (GLA_DK ** -0.5)).astype(BF16); o += dq
    gk_ref[...] = proj[:, o:o + dq].astype(BF16); o += dq
    gv_ref[...] = proj[:, o:o + dv].astype(BF16); o += dv
    gog_ref[...] = proj[:, o:o + dv].astype(BF16); o += dv
    cq = proj[:, o:o + MLA_Q_RANK]; o += MLA_Q_RANK
    ckv = proj[:, o:o + MLA_KV_RANK]; o += MLA_KV_RANK
    misc = proj[:, o:o + LANES]

    z = _dot(misc.astype(BF16), gu_ref[...]) + gb_ref[...]
    log_sig = jnp.minimum(z, 0.0) - jnp.log1p(jnp.exp(-jnp.abs(z)))
    gla_ref[...] = log_sig * (1.0 / GLA_GATE_NORM)

    lane = lax.broadcasted_iota(jnp.int32, (x.shape[0], LANES), 1)
    ang = pos_ref[...].astype(F32) * inv_ref[...]
    cosv = jnp.cos(ang)
    sinv = jnp.sin(ang)
    in_lo = (lane >= ROPE_LO) & (lane < ROPE_HI)
    in_hi = (lane >= ROPE_HI) & (lane < ROPE_END)
    c_rope = jnp.where(in_lo | in_hi, cosv, 0.0)
    s_up = jnp.where(in_hi, sinv, 0.0)
    s_dn = jnp.where(in_lo, -sinv, 0.0)
    c_q = jnp.where(lane < MLA_NOPE, 1.0, c_rope)

    def rope(t, c):
        return (t * c + pltpu.roll(t, ROPE_HALF, 1) * s_up
                + pltpu.roll(t, LANES - ROPE_HALF, 1) * s_dn)

    k_rope = rope(misc, c_rope)

    q = _dot(_rms(cq, qnw_ref[...]).astype(BF16), wuq_ref[...])
    kv = _dot(_rms(ckv, kvnw_ref[...]).astype(BF16), wukv_ref[...])
    ones_lane = jnp.where(lane == MLA_DV, 1.0, 0.0)
    for h in range(MLA_HEADS):
        sl = slice(h * HEAD_PAD, (h + 1) * HEAD_PAD)
        vsl = slice((MLA_HEADS + h) * HEAD_PAD, (MLA_HEADS + h + 1) * HEAD_PAD)
        q_ref[:, sl] = (rope(q[:, sl], c_q) * q_scale).astype(BF16)
        k_ref[:, sl] = (kv[:, sl] + k_rope).astype(BF16)
        v_ref[:, sl] = (kv[:, vsl] + ones_lane).astype(BF16)


def _gla_kernel(q_ref, k_ref, v_ref, la_ref, og_ref, nw_ref, o_ref, st_ref):
    t = GLA_SUBTILE
    nchunk = t // GLA_CHUNK

    @pl.when(pl.program_id(1) == 0)
    def _():
        st_ref[...] = jnp.zeros_like(st_ref)

    row = lax.broadcasted_iota(jnp.int32, (t, t), 0)
    col = lax.broadcasted_iota(jnp.int32, (t, t), 1)
    chunk_bits = GLA_CHUNK.bit_length() - 1
    tri = ((row >> chunk_bits) == (col >> chunk_bits)) & (col <= row)
    tri_b = tri.astype(BF16)
    nw = nw_ref[...]
    states = [st_ref[h] for h in range(GLA_HEADS)]

    for sub in range(q_ref.shape[1] // t):
        tr = slice(sub * t, (sub + 1) * t)
        la = la_ref[0, tr]
        la_hi = la.astype(BF16)
        la_lo = (la - la_hi.astype(F32)).astype(BF16)
        parts = _dot(tri_b, jnp.concatenate([la_hi, la_lo], axis=1))
        b = parts[:, :la.shape[1]] + parts[:, la.shape[1]:]
        b_last = jnp.concatenate(
            [jnp.broadcast_to(b[(c + 1) * GLA_CHUNK - 1:(c + 1) * GLA_CHUNK],
                              (GLA_CHUNK, b.shape[1])) for c in range(nchunk)], axis=0)
        q_e = (q_ref[0, tr].astype(F32) * jnp.exp(b)).astype(BF16)
        kf = k_ref[0, tr].astype(F32)
        k_e = (kf * jnp.exp(-b)).astype(BF16)
        k_d = (kf * jnp.exp(b_last - b)).astype(BF16)
        decay = jnp.exp(b_last)

        for h in range(GLA_HEADS):
            ks = slice(h * GLA_DK, (h + 1) * GLA_DK)
            vs = slice(h * GLA_DV, (h + 1) * GLA_DV)
            qh, keh, kdh = q_e[:, ks], k_e[:, ks], k_d[:, ks]
            vh = v_ref[0, tr, vs]
            att = jnp.where(tri, _dot_nt(qh, keh), 0.0)
            o = _dot(att.astype(BF16), vh)
            state = states[h]
            inter = []
            for c in range(nchunk):
                rs = slice(c * GLA_CHUNK, (c + 1) * GLA_CHUNK)
                inter.append(_dot_nt(qh[rs], state.astype(BF16)))
                upd = _dot_tn(vh[rs], kdh[rs])
                state = state * decay[c * GLA_CHUNK:c * GLA_CHUNK + 1, ks] + upd
            states[h] = state
            o = o + jnp.concatenate(inter, axis=0)
            o = _rms(o, nw)
            g = og_ref[0, tr, vs].astype(F32)
            o_ref[0, tr, vs] = (o * (g * jax.nn.sigmoid(g))).astype(o_ref.dtype)

    for h in range(GLA_HEADS):
        st_ref[h] = states[h]


def _mla_kernel(q_ref, k_ref, v_ref, o_ref, s0_ref, s1_ref, *, key_shift):
    tq = q_ref.shape[1]
    tk = tq << key_shift
    heads = q_ref.shape[2] // HEAD_PAD
    qi = pl.program_id(2)
    lane = lax.broadcasted_iota(jnp.int32, (tq, LANES), 1)
    slots = (s0_ref, s1_ref)

    def key_rows(j):
        return pl.ds(pl.multiple_of(j * tk, tk), tk)

    def produce(j, slot):
        for hh in range(heads):
            hs = slice(hh * HEAD_PAD, (hh + 1) * HEAD_PAD)
            slots[slot][hh] = _dot_nt(q_ref[0, :, hs], k_ref[0, key_rows(j), hs])

    def consume(j, slot, carry, mask):
        new = []
        for hh in range(heads):
            m, acc = carry[hh]
            hs = slice(hh * HEAD_PAD, (hh + 1) * HEAD_PAD)
            s = slots[slot][hh]
            if mask:
                r = lax.broadcasted_iota(jnp.int32, (tq, tk), 0) + qi * tq
                c = lax.broadcasted_iota(jnp.int32, (tq, tk), 1) + j * tk
                s = jnp.where(c <= r, s, NEG_INF)
            m_new = jnp.maximum(m, jnp.max(s, axis=-1, keepdims=True))
            p = jnp.exp(s - m_new).astype(BF16)
            acc = jnp.exp(m - m_new) * acc + _dot(p, v_ref[0, key_rows(j), hs])
            new.append((m_new, acc))
        return tuple(new)

    def finish(carry):
        for hp in range(heads // 2):
            o0, o1 = (acc / acc[:, MLA_DV:MLA_DV + 1] for _, acc in carry[2 * hp:2 * hp + 2])
            o_ref[0, :, hp * LANES:(hp + 1) * LANES] = jnp.where(
                lane < MLA_DV, o0, pltpu.roll(o1, MLA_DV, 1)).astype(o_ref.dtype)

    def pair(i, carry):
        j = 2 * i
        produce(j + 1, 1)
        carry = consume(j, 0, carry, False)
        produce(j + 2, 0)
        return consume(j + 1, 1, carry, False)

    init = tuple((jnp.full((tq, 1), NEG_INF, F32), jnp.zeros((tq, LANES), F32))
                 for _ in range(heads))
    nfull = lax.shift_right_logical(qi, key_shift)
    produce(0, 0)
    carry = lax.fori_loop(0, lax.shift_right_logical(nfull, 1), pair, init)
    odd = (nfull & 1) == 1

    @pl.when(odd)
    def _():
        produce(nfull, 1)
        finish(consume(nfull, 1, consume(nfull - 1, 0, carry, False), True))

    @pl.when(jnp.logical_not(odd))
    def _():
        finish(consume(nfull, 0, carry, True))


def _router_kernel(x_ref, yg_ref, ym_ref, wo_ref, fnw_ref, wr_ref, br_ref,
                   h_ref, hn_ref, route_ref, route_t_ref, cnt_ref, carry_ref):
    t = x_ref.shape[0]
    half = wo_ref.shape[0] // 2

    @pl.when(pl.program_id(0) == 0)
    def _():
        carry_ref[...] = jnp.zeros_like(carry_ref)

    h = x_ref[...] + _dot(yg_ref[...], wo_ref[:half]) + _dot(ym_ref[...], wo_ref[half:])
    h_ref[...] = h
    hn = _rms(h, fnw_ref[...])
    hp = hn.shape[1] // 2
    hn_ref[...] = _pack_bf16_pair(hn[:, :hp], hn[:, hp:])

    hn_hi = hn.astype(BF16)
    hn_lo = (hn - hn_hi.astype(F32)).astype(BF16)
    parts = _dot(hn_hi, wr_ref[...]) + _dot(hn_lo, wr_ref[...])
    logits_all = parts[:, :LANES] + parts[:, LANES:] + br_ref[...]

    tg = ROUTE_ROWS
    lane = lax.broadcasted_iota(jnp.int32, (tg, LANES), 1)
    lane_f = lane.astype(F32)
    rr = lax.broadcasted_iota(jnp.int32, (tg, tg), 0)
    cc = lax.broadcasted_iota(jnp.int32, (tg, tg), 1)
    earlier = (cc < rr).astype(BF16)

    def first_argmax(vals, vmax):
        idx = jnp.min(jnp.where(vals == vmax, lane_f, float(LANES)), axis=-1, keepdims=True)
        return idx.astype(jnp.int32)

    carry = carry_ref[...]
    for g in range(t // tg):
        rows = slice(g * tg, (g + 1) * tg)
        logits = logits_all[rows]
        gl = jnp.where((lane >= N_EXPERTS) & (lane < N_EXPERTS + N_GROUPS), logits, NEG_INF)
        gmax = jnp.max(gl, axis=-1, keepdims=True)
        gsel = first_argmax(gl, gmax) - N_EXPERTS
        p_g = 1.0 / jnp.sum(jnp.exp(gl - gmax), axis=-1, keepdims=True)
        lo = gsel * EXPERTS_PER_GROUP
        el = jnp.where((lane >= lo) & (lane < lo + EXPERTS_PER_GROUP), logits, NEG_INF)
        m1 = jnp.max(el, axis=-1, keepdims=True)
        i1 = first_argmax(el, m1)
        el2 = jnp.where(lane == i1, NEG_INF, el)
        m2 = jnp.max(el2, axis=-1, keepdims=True)
        i2 = first_argmax(el2, m2)
        e2 = jnp.exp(m2 - m1)
        g1 = p_g / (1.0 + e2)
        g2 = p_g * e2 / (1.0 + e2)

        is1 = lane == i1
        is2 = lane == i2
        onehot = (is1 | is2).astype(BF16)
        before = _dot(earlier, onehot) + carry
        r1 = jnp.sum(jnp.where(is1, before, 0.0), axis=-1, keepdims=True)
        r2 = jnp.sum(jnp.where(is2, before, 0.0), axis=-1, keepdims=True)
        carry = carry + jnp.sum(onehot.astype(F32), axis=0, keepdims=True)

        route = jnp.where(lane == 0, i1.astype(F32), 0.0)
        route = jnp.where(lane == 1, i2.astype(F32), route)
        route = jnp.where(lane == 2, g1, route)
        route = jnp.where(lane == 3, g2, route)
        route = jnp.where(lane == 4, r1, route)
        route = jnp.where(lane == 5, r2, route)
        route_ref[rows] = route
        route_t_ref[:, rows] = route.T[:ROUTE_FIELDS]
    carry_ref[...] = carry
    cnt_ref[...] = carry


def _sc_mesh():
    return plsc.VectorSubcoreMesh(core_axis_name="c", subcore_axis_name="s",
                                  num_cores=SC_CORES, num_subcores=SC_SUBCORES)


def _sc_worker_base(per_worker):
    return (lax.axis_index("s") * SC_CORES + lax.axis_index("c")) * per_worker


def _sc_scatter_rows(rows, idx_list, nrows):
    m, w = rows.shape
    nidx = len(idx_list)
    per_worker = m // SC_WORKERS
    nchunk = per_worker // SC_CHUNK

    @functools.partial(
        pl.kernel, mesh=_sc_mesh(), out_type=jax.ShapeDtypeStruct((nrows, w), rows.dtype),
        scratch_types=[pltpu.VMEM((SC_CHUNK,), jnp.int32)] * nidx
        + [pltpu.VMEM((SC_CHUNK, w), rows.dtype), pltpu.SemaphoreType.DMA],
        name="sc_scatter_rows")
    def scatter(rows_hbm, *refs):
        idx_hbm, out_hbm = refs[:nidx], refs[nidx]
        idx_v, rows_v, sem = refs[nidx + 1:2 * nidx + 1], refs[2 * nidx + 1], refs[2 * nidx + 2]
        base = _sc_worker_base(per_worker)

        @pl.loop(0, nchunk)
        def _(ci):
            off = pl.multiple_of(base + ci * SC_CHUNK, SC_CHUNK)
            pltpu.sync_copy(rows_hbm.at[pl.ds(off, SC_CHUNK)], rows_v)
            for kk in range(nidx):
                pltpu.sync_copy(idx_hbm[kk].at[pl.ds(off, SC_CHUNK)], idx_v[kk])
            copies = [pltpu.async_copy(rows_v, out_hbm.at[idx_v[kk]], sem) for kk in range(nidx)]
            for c in copies:
                c.wait()

    return scatter(rows, *idx_list)


def _expert_kernel(be_ref, nu_ref, bv_ref, x_ref, wg_ref, wu_ref, wd_ref, y_ref, wgu_s, wd_s):
    j = pl.program_id(0)
    de = wg_ref.shape[2]
    last = jnp.maximum(nu_ref[0] - 1, 0)
    cur = be_ref[jnp.minimum(j, last)]
    prev = be_ref[jnp.minimum(jnp.maximum(j - 1, 0), last)]

    @pl.when((j == 0) | (cur != prev))
    def _():
        wgu_s[:, :de] = wg_ref[0].astype(BF16)
        wgu_s[:, de:] = wu_ref[0].astype(BF16)
        wd_s[...] = wd_ref[0].astype(BF16)

    @pl.when(j < nu_ref[0])
    def _():
        row = lax.broadcasted_iota(jnp.int32, x_ref.shape, 0)
        xu = jnp.where(row < bv_ref[j], x_ref[...], jnp.uint32(0))
        a, b = _unpack_bf16_pair(xu)
        x = jnp.concatenate([a, b], axis=1).astype(BF16)
        h12 = _dot(x, wgu_s[...])
        h1, h2 = h12[:, :de], h12[:, de:]
        hdn = (h1 * jax.nn.sigmoid(h1) * h2).astype(BF16)
        y = _dot(hdn, wd_s[...])
        hp = y.shape[1] // 2
        y_ref[...] = _pack_bf16_pair(y[:, :hp], y[:, hp:])

    @pl.when(pl.program_id(0) >= nu_ref[0])
    def _():
        y_ref[...] = jnp.zeros_like(y_ref)


def _sc_gather_rows(table, idx):
    m = idx.shape[0]
    w = table.shape[1]
    per_worker = m // SC_WORKERS
    nchunk = per_worker // SC_CHUNK

    @functools.partial(
        pl.kernel, mesh=_sc_mesh(), out_type=jax.ShapeDtypeStruct((m, w), table.dtype),
        scratch_types=[pltpu.VMEM((SC_CHUNK,), jnp.int32), pltpu.VMEM((SC_CHUNK, w), table.dtype),
                       pltpu.SemaphoreType.DMA],
        name="sc_gather_rows")
    def gather(table_hbm, idx_hbm, out_hbm, idx_v, rows_v, sem):
        base = _sc_worker_base(per_worker)

        @pl.loop(0, nchunk)
        def _(ci):
            off = pl.multiple_of(base + ci * SC_CHUNK, SC_CHUNK)
            pltpu.sync_copy(idx_hbm.at[pl.ds(off, SC_CHUNK)], idx_v)
            pltpu.async_copy(table_hbm.at[idx_v], rows_v, sem).wait()
            pltpu.sync_copy(rows_v, out_hbm.at[pl.ds(off, SC_CHUNK)])

    return gather(table, idx)


def _combine_kernel(h_ref, route_ref, fw_ref, y0_ref, y1_ref, o_ref):
    route = route_ref[...]
    g1 = route[:, 2:3]
    g2 = route[:, 3:4]
    a0, b0 = _unpack_bf16_pair(y0_ref[...])
    a1, b1 = _unpack_bf16_pair(y1_ref[...])
    moe = jnp.concatenate([a0 * g1 + a1 * g2, b0 * g1 + b1 * g2], axis=1)
    o_ref[...] = _rms(h_ref[...] + moe, fw_ref[...])


def _params(*sem):
    return pltpu.CompilerParams(dimension_semantics=sem, vmem_limit_bytes=VMEM_LIMIT)


def _full(shape):
    return pl.BlockSpec(shape, lambda *_: (0,) * len(shape))


def _rows(tile, width):
    return pl.BlockSpec((tile, width), lambda i: (i, 0))


def _layer(x2, pos2, attn_norm_w, w_in, gla_gate_up, gla_gate_bias, gla_norm_w, mla_q_norm_w,
           mla_w_uq, mla_kv_norm_w, mla_w_ukv, w_out, ffn_norm_w, router_group_w, router_group_b,
           router_expert_w, router_expert_b, expert_w_gate, expert_w_up, expert_w_down,
           out_norm_w, batch, seq):
    n, d = x2.shape
    dq = GLA_HEADS * GLA_DK
    dv = GLA_HEADS * GLA_DV
    dmla = MLA_HEADS * HEAD_PAD
    dmv = MLA_HEADS * MLA_DV

    c_gq, c_gk, c_gv, c_lr, c_og, c_cq, c_ckv, c_kr = jnp.split(
        w_in, [dq, 2 * dq, 2 * dq + dv, 2 * dq + dv + GLA_GATE_RANK,
               2 * dq + 2 * dv + GLA_GATE_RANK,
               2 * dq + 2 * dv + GLA_GATE_RANK + MLA_Q_RANK,
               2 * dq + 2 * dv + GLA_GATE_RANK + MLA_Q_RANK + MLA_KV_RANK], axis=1)
    zeros = lambda r, c: jnp.zeros((r, c), w_in.dtype)
    misc = jnp.concatenate([zeros(d, MLA_NOPE), c_kr, c_lr,
                            zeros(d, LANES - ROPE_END - GLA_GATE_RANK)], axis=1)
    w1 = jnp.concatenate([c_gq, c_gk, c_gv, c_og, c_cq, c_ckv, misc], axis=1).astype(BF16)
    gate_up = jnp.concatenate([zeros(GATE_LO, dq), gla_gate_up,
                               zeros(LANES - GATE_LO - GLA_GATE_RANK, dq)], axis=0).astype(BF16)
    wuq = mla_w_uq.reshape(MLA_Q_RANK, MLA_HEADS, MLA_NOPE + MLA_ROPE)
    wuq = jnp.pad(wuq, ((0, 0), (0, 0), (0, HEAD_PAD - ROPE_END))).reshape(MLA_Q_RANK, dmla)
    wukv = mla_w_ukv.reshape(MLA_KV_RANK, MLA_HEADS, MLA_NOPE + MLA_DV)
    wuk = jnp.pad(wukv[:, :, :MLA_NOPE], ((0, 0), (0, 0), (0, HEAD_PAD - MLA_NOPE)))
    wuv = jnp.pad(wukv[:, :, MLA_NOPE:], ((0, 0), (0, 0), (0, HEAD_PAD - MLA_DV)))
    wukv = jnp.concatenate([wuk.reshape(MLA_KV_RANK, dmla), wuv.reshape(MLA_KV_RANK, dmla)],
                           axis=1)
    inv = ROPE_THETA ** (-jnp.arange(ROPE_HALF, dtype=F32) / ROPE_HALF)
    inv_pat = jnp.concatenate([jnp.zeros((MLA_NOPE,), F32), inv, inv,
                               jnp.zeros((LANES - ROPE_END,), F32)]).reshape(1, LANES)
    w_router = jnp.concatenate(
        [router_expert_w, router_group_w, zeros(d, LANES - N_EXPERTS - N_GROUPS)], axis=1)
    w_router_hi = w_router.astype(BF16)
    w_router = jnp.concatenate(
        [w_router_hi, (w_router - w_router_hi.astype(F32)).astype(BF16)], axis=1)
    b_router = jnp.concatenate(
        [router_expert_b, router_group_b, jnp.zeros((LANES - N_EXPERTS - N_GROUPS,), F32)]
    ).reshape(1, LANES)
    row1 = lambda v: v.reshape(1, -1)

    tp = PROJ_TILE
    outs = pl.pallas_call(
        functools.partial(_proj_kernel, q_scale=(MLA_NOPE + MLA_ROPE) ** -0.5),
        grid=(n // tp,),
        in_specs=[_rows(tp, d), _rows(tp, 1), _full((1, d)), _full(w1.shape), _full(gate_up.shape),
                  _full((1, dq)), _full((1, MLA_Q_RANK)), _full(wuq.shape),
                  _full((1, MLA_KV_RANK)), _full(wukv.shape), _full((1, LANES))],
        out_specs=[_rows(tp, dq), _rows(tp, dq), _rows(tp, dv), _rows(tp, dq), _rows(tp, dv),
                   _rows(tp, dmla), _rows(tp, dmla), _rows(tp, dmla)],
        out_shape=[jax.ShapeDtypeStruct((n, dq), BF16), jax.ShapeDtypeStruct((n, dq), BF16),
                   jax.ShapeDtypeStruct((n, dv), BF16), jax.ShapeDtypeStruct((n, dq), F32),
                   jax.ShapeDtypeStruct((n, dv), BF16), jax.ShapeDtypeStruct((n, dmla), BF16),
                   jax.ShapeDtypeStruct((n, dmla), BF16), jax.ShapeDtypeStruct((n, dmla), BF16)],
        compiler_params=_params("parallel"),
        name="in_proj",
    )(x2, pos2, row1(attn_norm_w), w1, gate_up, row1(gla_gate_bias), row1(mla_q_norm_w),
      wuq.astype(BF16), row1(mla_kv_norm_w), wukv.astype(BF16), inv_pat)
    gq, gk, gv, gla, gog, q, k, v = outs

    tg = GLA_TILE
    seq3 = lambda a: a.reshape(batch, seq, a.shape[-1])
    gspec = lambda w: pl.BlockSpec((1, tg, w), lambda b, i: (b, i, 0))
    y_gla = pl.pallas_call(
        _gla_kernel,
        grid=(batch, seq // tg),
        in_specs=[gspec(dq), gspec(dq), gspec(dv), gspec(dq), gspec(dv), _full((1, GLA_DV))],
        out_specs=gspec(dv),
        out_shape=jax.ShapeDtypeStruct((batch, seq, dv), BF16),
        scratch_shapes=[pltpu.VMEM((GLA_HEADS, GLA_DV, GLA_DK), F32)],
        compiler_params=_params("parallel", "arbitrary"),
        name="gla",
    )(seq3(gq), seq3(gk), seq3(gv), seq3(gla), seq3(gog), row1(gla_norm_w))

    ta = ATT_TILE
    hps = ATT_HEADS_PER_STEP
    y_mla = pl.pallas_call(
        functools.partial(_mla_kernel, key_shift=ATT_KEY_SHIFT),
        grid=(batch, MLA_HEADS // hps, seq // ta),
        in_specs=[pl.BlockSpec((1, ta, hps * HEAD_PAD), lambda b, hp, i: (b, i, hp)),
                  pl.BlockSpec((1, seq, hps * HEAD_PAD), lambda b, hp, i: (b, 0, hp)),
                  pl.BlockSpec((1, seq, hps * HEAD_PAD), lambda b, hp, i: (b, 0, hp))],
        out_specs=pl.BlockSpec((1, ta, hps * MLA_DV), lambda b, hp, i: (b, i, hp)),
        out_shape=jax.ShapeDtypeStruct((batch, seq, dmv), BF16),
        scratch_shapes=[pltpu.VMEM((hps, ta, ta << ATT_KEY_SHIFT), F32)] * 2,
        compiler_params=_params("parallel", "parallel", "arbitrary"),
        name="mla",
    )(seq3(q), seq3(k), seq3(v))

    tr = ROUTER_TILE
    h, hn, route, route_t, counts = pl.pallas_call(
        _router_kernel,
        grid=(n // tr,),
        in_specs=[_rows(tr, d), _rows(tr, dv), _rows(tr, dmv), _full(w_out.shape), _full((1, d)),
                  _full(w_router.shape), _full((1, LANES))],
        out_specs=[_rows(tr, d), _rows(tr, d // 2), _rows(tr, LANES),
                   pl.BlockSpec((ROUTE_FIELDS, tr), lambda i: (0, i)), _full((1, LANES))],
        out_shape=[jax.ShapeDtypeStruct((n, d), F32), jax.ShapeDtypeStruct((n, d // 2), jnp.uint32),
                   jax.ShapeDtypeStruct((n, LANES), F32),
                   jax.ShapeDtypeStruct((ROUTE_FIELDS, n), F32),
                   jax.ShapeDtypeStruct((1, LANES), F32)],
        scratch_shapes=[pltpu.VMEM((1, LANES), F32)],
        compiler_params=_params("arbitrary"),
        name="out_proj_router",
    )(x2, y_gla.reshape(n, dv), y_mla.reshape(n, dmv), w_out.astype(BF16), row1(ffn_norm_w),
      w_router, b_router)

    blk = EXPERT_BLOCK
    nblk = (n * TOP_K) // blk + N_EXPERTS
    cnt = counts[0, :N_EXPERTS].astype(jnp.int32)
    padded = (cnt + blk - 1) // blk * blk
    pend = jnp.cumsum(padded)
    pstart = pend - padded
    eid = route_t[0:TOP_K].astype(jnp.int32)
    rank = route_t[4:4 + TOP_K].astype(jnp.int32)
    pstart_of = jnp.sum(jnp.where(eid[..., None] == jnp.arange(N_EXPERTS, dtype=jnp.int32),
                                  pstart, 0), axis=-1)
    dest = pstart_of + rank
    blk_start = jnp.arange(nblk, dtype=jnp.int32) * blk
    blk_expert = jnp.minimum(
        jnp.sum((pend[None, :] <= blk_start[:, None]).astype(jnp.int32), axis=1), N_EXPERTS - 1)
    n_used = (pend[-1] // blk).astype(jnp.int32).reshape(1)
    blk_valid = jnp.clip(cnt[blk_expert] - (blk_start - pstart[blk_expert]), 0, blk)
    blk_valid = jnp.where(jnp.arange(nblk) < n_used[0], blk_valid, 0).astype(jnp.int32)
    dest_slots = [dest[kk] for kk in range(TOP_K)]

    xbuf = _sc_scatter_rows(hn, dest_slots, nblk * blk)

    def used(j, nu):
        return jnp.maximum(jnp.minimum(j, nu[0] - 1), 0)

    ybuf = pl.pallas_call(
        _expert_kernel,
        grid_spec=pltpu.PrefetchScalarGridSpec(
            num_scalar_prefetch=3,
            grid=(nblk,),
            in_specs=[
                pl.BlockSpec((blk, d // 2), lambda j, be, nu, bv: (used(j, nu), 0)),
                pl.BlockSpec((1, d, D_EXPERT), lambda j, be, nu, bv: (be[used(j, nu)], 0, 0)),
                pl.BlockSpec((1, d, D_EXPERT), lambda j, be, nu, bv: (be[used(j, nu)], 0, 0)),
                pl.BlockSpec((1, D_EXPERT, d), lambda j, be, nu, bv: (be[used(j, nu)], 0, 0)),
            ],
            out_specs=pl.BlockSpec((blk, d // 2), lambda j, be, nu, bv: (j, 0)),
            scratch_shapes=[pltpu.VMEM((d, 2 * D_EXPERT), BF16), pltpu.VMEM((D_EXPERT, d), BF16)],
        ),
        out_shape=jax.ShapeDtypeStruct((nblk * blk, d // 2), jnp.uint32),
        compiler_params=_params("arbitrary"),
        name="experts",
    )(blk_expert, n_used, blk_valid, xbuf, expert_w_gate, expert_w_up, expert_w_down)

    gathered = _sc_gather_rows(ybuf, dest.reshape(TOP_K * n))
    nt = n // tp
    return pl.pallas_call(
        _combine_kernel,
        grid=(nt,),
        in_specs=[_rows(tp, d), _rows(tp, LANES), _full((1, d)), _rows(tp, d // 2),
                  pl.BlockSpec((tp, d // 2), lambda i: (i + nt, 0))],
        out_specs=_rows(tp, d),
        out_shape=jax.ShapeDtypeStruct((n, d), F32),
        compiler_params=_params("parallel"),
        name="combine",
    )(h, route, row1(out_norm_w), gathered, gathered)


def kernel(x, positions, attn_norm_w, w_in, gla_gate_up, gla_gate_bias, gla_norm_w, mla_q_norm_w,
           mla_w_uq, mla_kv_norm_w, mla_w_ukv, w_out, ffn_norm_w, router_group_w, router_group_b,
           router_expert_w, router_expert_b, expert_w_gate, expert_w_up, expert_w_down,
           final_norm_w):
    batch, seq, d = x.shape
    depth = w_in.shape[0]
    assert depth == 1, "the final norm is fused into the last layer's combine step"
    out = _layer(x.reshape(batch * seq, d), positions.reshape(batch * seq, 1),
                 attn_norm_w[0], w_in[0], gla_gate_up[0], gla_gate_bias[0], gla_norm_w[0],
                 mla_q_norm_w[0], mla_w_uq[0], mla_kv_norm_w[0], mla_w_ukv[0], w_out[0],
                 ffn_norm_w[0], router_group_w[0], router_group_b[0], router_expert_w[0],
                 router_expert_b[0], expert_w_gate[0], expert_w_up[0], expert_w_down[0],
                 final_norm_w, batch, seq)
    return out.reshape(batch, seq, d)
```

```python
import functools

import jax
import jax.numpy as jnp
from jax import lax
from jax.experimental import pallas as pl
from jax.experimental.pallas import tpu as pltpu
from jax.experimental.pallas import tpu_sc as plsc

EPS = 1e-6
GLA_HEADS = 4
GLA_DK = 64
GLA_DV = 128
GLA_GATE_RANK = 16
GLA_GATE_NORM = 16.0
GLA_CHUNK = 64
MLA_HEADS = 8
MLA_NOPE = 64
MLA_ROPE = 32
MLA_DV = 64
MLA_Q_RANK = 384
MLA_KV_RANK = 256
ROPE_THETA = 10000.0
N_GROUPS = 4
EXPERTS_PER_GROUP = 8
N_EXPERTS = N_GROUPS * EXPERTS_PER_GROUP
TOP_K = 2
D_EXPERT = 256

LANES = 128
HEAD_PAD = 128
ROPE_HALF = MLA_ROPE // 2
ROPE_LO = MLA_NOPE
ROPE_HI = MLA_NOPE + ROPE_HALF
ROPE_END = MLA_NOPE + MLA_ROPE
GATE_LO = ROPE_END

PROJ_TILE = 512
GLA_TILE = 512
GLA_SUBTILE = 256
ATT_TILE = 512
ATT_HEADS_PER_STEP = 4
EXPERT_BLOCK = 512
ROUTER_TILE = 1024
ROUTE_ROWS = 512
ROUTE_FIELDS = 8
VMEM_LIMIT = 56 * 1024 * 1024
SC_CORES = 2
SC_SUBCORES = 16
SC_WORKERS = SC_CORES * SC_SUBCORES
SC_CHUNK = 64

F32 = jnp.float32
BF16 = jnp.bfloat16
NEG_INF = float("-inf")
LOG2_E = 1.4426950408889634


def _dot(a, b, precision=None):
    return jnp.dot(a, b, preferred_element_type=F32, precision=precision)


def _dot_nt(a, b):
    return lax.dot_general(a, b, (((1,), (1,)), ((), ())), preferred_element_type=F32)


def _dot_tn(a, b):
    return lax.dot_general(a, b, (((0,), (0,)), ((), ())), preferred_element_type=F32)


def _rms(x, w):
    return x * lax.rsqrt(jnp.mean(x * x, axis=-1, keepdims=True) + EPS) * w


def _pack_bf16_pair(a, b):
    ua = lax.bitcast_convert_type(a.astype(BF16).astype(F32), jnp.uint32)
    ub = lax.bitcast_convert_type(b.astype(BF16).astype(F32), jnp.uint32)
    return (ua >> 16) | (ub & jnp.uint32(0xFFFF0000))


def _unpack_bf16_pair(u):
    a = lax.bitcast_convert_type(u << 16, F32)
    b = lax.bitcast_convert_type(u & jnp.uint32(0xFFFF0000), F32)
    return a, b


def _proj_kernel(x_ref, pos_ref, nw_ref, w1_ref, gu_ref, gb_ref, qnw_ref, wuq_ref, kvnw_ref,
                 wukv_ref, inv_ref,
                 gq_ref, gk_ref, gv_ref, gla_ref, gog_ref, q_ref, k_ref, v_ref, *, q_scale):
    x = x_ref[...]
    xn = _rms(x, nw_ref[...]).astype(BF16)
    proj = _dot(xn, w1_ref[...])
    dq = GLA_HEADS * GLA_DK
    dv = GLA_HEADS * GLA_DV
    o = 0
    gq_ref[...] = (proj[:, o:o + dq] * (GLA_DK ** -0.5)).astype(BF16); o += dq
    gk_ref[...] = proj[:, o:o + dq].astype(BF16); o += dq
    gv_ref[...] = proj[:, o:o + dv].astype(BF16); o += dv
    gog_ref[...] = proj[:, o:o + dv].astype(BF16); o += dv
    cq = proj[:, o:o + MLA_Q_RANK]; o += MLA_Q_RANK
    ckv = proj[:, o:o + MLA_KV_RANK]; o += MLA_KV_RANK
    misc = proj[:, o:o + LANES]

    z = _dot(misc.astype(BF16), gu_ref[...]) + gb_ref[...]
    log_sig = jnp.minimum(z, 0.0) - jnp.log1p(jnp.exp(-jnp.abs(z)))
    gla_ref[...] = log_sig * (1.0 / GLA_GATE_NORM)

    lane = lax.broadcasted_iota(jnp.int32, (x.shape[0], LANES), 1)
    ang = pos_ref[...].astype(F32) * inv_ref[...]
    cosv = jnp.cos(ang)
    sinv = jnp.sin(ang)
    in_lo = (lane >= ROPE_LO) & (lane < ROPE_HI)
    in_hi = (lane >= ROPE_HI) & (lane < ROPE_END)
    c_rope = jnp.where(in_lo | in_hi, cosv, 0.0)
    s_up = jnp.where(in_hi, sinv, 0.0)
    s_dn = jnp.where(in_lo, -sinv, 0.0)
    c_q = jnp.where(lane < MLA_NOPE, 1.0, c_rope)

    def rope(t, c):
        return (t * c + pltpu.roll(t, ROPE_HALF, 1) * s_up
                + pltpu.roll(t, LANES - ROPE_HALF, 1) * s_dn)

    k_rope = rope(misc, c_rope)

    q = _dot(_rms(cq, qnw_ref[...]).astype(BF16), wuq_ref[...])
    kv = _dot(_rms(ckv, kvnw_ref[...]).astype(BF16), wukv_ref[...])
    ones_lane = jnp.where(lane == MLA_DV, 1.0, 0.0)
    for h in range(MLA_HEADS):
        sl = slice(h * HEAD_PAD, (h + 1) * HEAD_PAD)
        vsl = slice((MLA_HEADS + h) * HEAD_PAD, (MLA_HEADS + h + 1) * HEAD_PAD)
        q_ref[:, sl] = (rope(q[:, sl], c_q) * q_scale).astype(BF16)
        k_ref[:, sl] = (kv[:, sl] + k_rope).astype(BF16)
        v_ref[:, sl] = (kv[:, vsl] + ones_lane).astype(BF16)


def _gla_kernel(q_ref, k_ref, v_ref, la_ref, og_ref, nw_ref, o_ref, st_ref):
    t = GLA_SUBTILE
    nchunk = t // GLA_CHUNK

    @pl.when(pl.program_id(1) == 0)
    def _():
        st_ref[...] = jnp.zeros_like(st_ref)

    row = lax.broadcasted_iota(jnp.int32, (t, t), 0)
    col = lax.broadcasted_iota(jnp.int32, (t, t), 1)
    chunk_bits = GLA_CHUNK.bit_length() - 1
    tri = ((row >> chunk_bits) == (col >> chunk_bits)) & (col <= row)
    tri_b = tri.astype(BF16)
    nw = nw_ref[...]
    states = [st_ref[h] for h in range(GLA_HEADS)]

    for sub in range(q_ref.shape[1] // t):
        tr = slice(sub * t, (sub + 1) * t)
        la = la_ref[0, tr]
        la_hi = la.astype(BF16)
        la_lo = (la - la_hi.astype(F32)).astype(BF16)
        parts = _dot(tri_b, jnp.concatenate([la_hi, la_lo], axis=1))
        b = parts[:, :la.shape[1]] + parts[:, la.shape[1]:]
        b_last = jnp.concatenate(
            [jnp.broadcast_to(b[(c + 1) * GLA_CHUNK - 1:(c + 1) * GLA_CHUNK],
                              (GLA_CHUNK, b.shape[1])) for c in range(nchunk)], axis=0)
        q_e = (q_ref[0, tr].astype(F32) * jnp.exp(b)).astype(BF16)
        kf = k_ref[0, tr].astype(F32)
        k_e = (kf * jnp.exp(-b)).astype(BF16)
        k_d = (kf * jnp.exp(b_last - b)).astype(BF16)
        decay = jnp.exp(b_last)

        for h in range(GLA_HEADS):
            ks = slice(h * GLA_DK, (h + 1) * GLA_DK)
            vs = slice(h * GLA_DV, (h + 1) * GLA_DV)
            qh, keh, kdh = q_e[:, ks], k_e[:, ks], k_d[:, ks]
            vh = v_ref[0, tr, vs]
            att = jnp.where(tri, _dot_nt(qh, keh), 0.0)
            o = _dot(att.astype(BF16), vh)
            state = states[h]
            inter = []
            for c in range(nchunk):
                rs = slice(c * GLA_CHUNK, (c + 1) * GLA_CHUNK)
                inter.append(_dot_nt(qh[rs], state.astype(BF16)))
                upd = _dot_tn(vh[rs], kdh[rs])
                state = state * decay[c * GLA_CHUNK:c * GLA_CHUNK + 1, ks] + upd
            states[h] = state
            o = o + jnp.concatenate(inter, axis=0)
            o = _rms(o, nw)
            g = og_ref[0, tr, vs].astype(F32)
            o_ref[0, tr, vs] = (o * (g * jax.nn.sigmoid(g))).astype(o_ref.dtype)

    for h in range(GLA_HEADS):
        st_ref[h] = states[h]


def _mla_kernel(q_ref, k_ref, v_ref, o_ref, sh_ref, s0_ref, s1_ref, *, tq):
    heads = q_ref.shape[2] // HEAD_PAD
    nq = q_ref.shape[1] // tq
    qi = pl.program_id(2)
    lane = lax.broadcasted_iota(jnp.int32, (tq, LANES), 1)
    causal = (lax.broadcasted_iota(jnp.int32, (tq, tq), 1)
              <= lax.broadcasted_iota(jnp.int32, (tq, tq), 0))

    def rows(i):
        return pl.ds(pl.multiple_of(i * tq, tq), tq)

    def produce(qblk, j, dst):
        for hh in range(heads):
            hs = slice(hh * HEAD_PAD, (hh + 1) * HEAD_PAD)
            dst[hh] = _dot_nt(q_ref[0, rows(qblk), hs], k_ref[0, rows(j), hs])

    def consume(j, src, carry, diagonal):
        new = []
        for hh in range(heads):
            m, acc = carry[hh]
            hs = slice(hh * HEAD_PAD, (hh + 1) * HEAD_PAD)
            s = src[hh]
            if diagonal:
                s = jnp.where(causal, s, NEG_INF)
            m_new = jnp.maximum(m, jnp.max(s, axis=-1, keepdims=True))
            p = jnp.exp2(s - m_new).astype(BF16)
            acc = jnp.exp2(m - m_new) * acc + _dot(p, v_ref[0, rows(j), hs])
            new.append((m_new, acc))
        return tuple(new)

    def finish(carry):
        for hp in range(heads // 2):
            o0, o1 = (acc / acc[:, MLA_DV:MLA_DV + 1] for _, acc in carry[2 * hp:2 * hp + 2])
            o_ref[0, :, hp * LANES:(hp + 1) * LANES] = jnp.where(
                lane < MLA_DV, o0, pltpu.roll(o1, MLA_DV, 1)).astype(o_ref.dtype)

    init = tuple((jnp.full((tq, 1), NEG_INF, F32), jnp.zeros((tq, LANES), F32))
                 for _ in range(heads))
    nxt = jnp.minimum(qi + 1, nq - 1)

    @pl.when(qi == 0)
    def _():
        produce(0, 0, sh_ref)
        carry = consume(0, sh_ref, init, True)
        produce(nxt, 0, sh_ref)
        finish(carry)

    @pl.when(qi > 0)
    def _():
        produce(qi, 1, s1_ref)
        carry = consume(0, sh_ref, init, False)

        def pair(i, carry):
            j = 1 + 2 * i
            produce(qi, j + 1, s0_ref)
            carry = consume(j, s1_ref, carry, False)
            produce(qi, j + 2, s1_ref)
            return consume(j + 1, s0_ref, carry, False)

        carry = lax.fori_loop(0, lax.shift_right_logical(qi - 1, 1), pair, carry)
        even = (qi & 1) == 0

        @pl.when(even)
        def _():
            produce(qi, qi, s0_ref)
            c = consume(qi - 1, s1_ref, carry, False)
            produce(nxt, 0, sh_ref)
            finish(consume(qi, s0_ref, c, True))

        @pl.when(jnp.logical_not(even))
        def _():
            produce(nxt, 0, sh_ref)
            finish(consume(qi, s1_ref, carry, True))


def _router_kernel(x_ref, yg_ref, ym_ref, wo_ref, fnw_ref, wr_ref, br_ref,
                   h_ref, hn_ref, route_ref, route_t_ref, cnt_ref, carry_ref):
    t = x_ref.shape[0]
    half = wo_ref.shape[0] // 2

    @pl.when(pl.program_id(0) == 0)
    def _():
        carry_ref[...] = jnp.zeros_like(carry_ref)

    h = x_ref[...] + _dot(yg_ref[...], wo_ref[:half]) + _dot(ym_ref[...], wo_ref[half:])
    h_ref[...] = h
    hn = _rms(h, fnw_ref[...])
    hp = hn.shape[1] // 2
    hn_ref[...] = _pack_bf16_pair(hn[:, :hp], hn[:, hp:])

    hn_hi = hn.astype(BF16)
    hn_lo = (hn - hn_hi.astype(F32)).astype(BF16)
    parts = _dot(hn_hi, wr_ref[...]) + _dot(hn_lo, wr_ref[...])
    logits_all = parts[:, :LANES] + parts[:, LANES:] + br_ref[...]

    tg = ROUTE_ROWS
    lane = lax.broadcasted_iota(jnp.int32, (tg, LANES), 1)
    lane_f = lane.astype(F32)
    rr = lax.broadcasted_iota(jnp.int32, (tg, tg), 0)
    cc = lax.broadcasted_iota(jnp.int32, (tg, tg), 1)
    earlier = (cc < rr).astype(BF16)

    def first_argmax(vals, vmax):
        idx = jnp.min(jnp.where(vals == vmax, lane_f, float(LANES)), axis=-1, keepdims=True)
        return idx.astype(jnp.int32)

    carry = carry_ref[...]
    for g in range(t // tg):
        rows = slice(g * tg, (g + 1) * tg)
        logits = logits_all[rows]
        gl = jnp.where((lane >= N_EXPERTS) & (lane < N_EXPERTS + N_GROUPS), logits, NEG_INF)
        gmax = jnp.max(gl, axis=-1, keepdims=True)
        gsel = first_argmax(gl, gmax) - N_EXPERTS
        p_g = 1.0 / jnp.sum(jnp.exp(gl - gmax), axis=-1, keepdims=True)
        lo = gsel * EXPERTS_PER_GROUP
        el = jnp.where((lane >= lo) & (lane < lo + EXPERTS_PER_GROUP), logits, NEG_INF)
        m1 = jnp.max(el, axis=-1, keepdims=True)
        i1 = first_argmax(el, m1)
        el2 = jnp.where(lane == i1, NEG_INF, el)
        m2 = jnp.max(el2, axis=-1, keepdims=True)
        i2 = first_argmax(el2, m2)
        e2 = jnp.exp(m2 - m1)
        g1 = p_g / (1.0 + e2)
        g2 = p_g * e2 / (1.0 + e2)

        is1 = lane == i1
        is2 = lane == i2
        onehot = (is1 | is2).astype(BF16)
        before = _dot(earlier, onehot) + carry
        r1 = jnp.sum(jnp.where(is1, before, 0.0), axis=-1, keepdims=True)
        r2 = jnp.sum(jnp.where(is2, before, 0.0), axis=-1, keepdims=True)
        carry = carry + jnp.sum(onehot.astype(F32), axis=0, keepdims=True)

        route = jnp.where(lane == 0, i1.astype(F32), 0.0)
        route = jnp.where(lane == 1, i2.astype(F32), route)
        route = jnp.where(lane == 2, g1, route)
        route = jnp.where(lane == 3, g2, route)
        route = jnp.where(lane == 4, r1, route)
        route = jnp.where(lane == 5, r2, route)
        route_ref[rows] = route
        route_t_ref[:, rows] = route.T[:ROUTE_FIELDS]
    carry_ref[...] = carry
    cnt_ref[...] = carry


def _dest_kernel(pstart_ref, route_t_ref, dest_ref):
    eid = route_t_ref[0:TOP_K, :].astype(jnp.int32)
    rank = route_t_ref[4:4 + TOP_K, :].astype(jnp.int32)
    start = jnp.zeros_like(eid)
    for e in range(N_EXPERTS):
        start = jnp.where(eid == e, pstart_ref[e], start)
    dest_ref[...] = start + rank


def _sc_mesh():
    return plsc.VectorSubcoreMesh(core_axis_name="c", subcore_axis_name="s",
                                  num_cores=SC_CORES, num_subcores=SC_SUBCORES)


def _sc_worker_base(per_worker):
    return (lax.axis_index("s") * SC_CORES + lax.axis_index("c")) * per_worker


def _sc_scatter_rows(rows, idx_list, nrows):
    m, w = rows.shape
    nidx = len(idx_list)
    per_worker = m // SC_WORKERS
    nchunk = per_worker // SC_CHUNK

    @functools.partial(
        pl.kernel, mesh=_sc_mesh(), out_type=jax.ShapeDtypeStruct((nrows, w), rows.dtype),
        scratch_types=[pltpu.VMEM((SC_CHUNK,), jnp.int32)] * nidx
        + [pltpu.VMEM((SC_CHUNK, w), rows.dtype), pltpu.SemaphoreType.DMA],
        name="sc_scatter_rows")
    def scatter(rows_hbm, *refs):
        idx_hbm, out_hbm = refs[:nidx], refs[nidx]
        idx_v, rows_v, sem = refs[nidx + 1:2 * nidx + 1], refs[2 * nidx + 1], refs[2 * nidx + 2]
        base = _sc_worker_base(per_worker)

        @pl.loop(0, nchunk)
        def _(ci):
            off = pl.multiple_of(base + ci * SC_CHUNK, SC_CHUNK)
            pltpu.sync_copy(rows_hbm.at[pl.ds(off, SC_CHUNK)], rows_v)
            for kk in range(nidx):
                pltpu.sync_copy(idx_hbm[kk].at[pl.ds(off, SC_CHUNK)], idx_v[kk])
            copies = [pltpu.async_copy(rows_v, out_hbm.at[idx_v[kk]], sem) for kk in range(nidx)]
            for c in copies:
                c.wait()

    return scatter(rows, *idx_list)


def _expert_kernel(be_ref, nu_ref, bv_ref, x_ref, wg_ref, wu_ref, wd_ref, y_ref, wgu_s, wd_s):
    j = pl.program_id(0)
    de = wg_ref.shape[2]
    last = jnp.maximum(nu_ref[0] - 1, 0)
    cur = be_ref[jnp.minimum(j, last)]
    prev = be_ref[jnp.minimum(jnp.maximum(j - 1, 0), last)]

    @pl.when((j == 0) | (cur != prev))
    def _():
        wgu_s[:, :de] = wg_ref[0].astype(BF16)
        wgu_s[:, de:] = wu_ref[0].astype(BF16)
        wd_s[...] = wd_ref[0].astype(BF16)

    @pl.when(j < nu_ref[0])
    def _():
        row = lax.broadcasted_iota(jnp.int32, x_ref.shape, 0)
        xu = jnp.where(row < bv_ref[j], x_ref[...], jnp.uint32(0))
        a, b = _unpack_bf16_pair(xu)
        x = jnp.concatenate([a, b], axis=1).astype(BF16)
        h12 = _dot(x, wgu_s[...])
        h1, h2 = h12[:, :de], h12[:, de:]
        hdn = (h1 * jax.nn.sigmoid(h1) * h2).astype(BF16)
        y = _dot(hdn, wd_s[...])
        hp = y.shape[1] // 2
        y_ref[...] = _pack_bf16_pair(y[:, :hp], y[:, hp:])

    @pl.when(pl.program_id(0) >= nu_ref[0])
    def _():
        y_ref[...] = jnp.zeros_like(y_ref)


def _sc_gather_rows(table, idx):
    m = idx.shape[0]
    w = table.shape[1]
    per_worker = m // SC_WORKERS
    nchunk = per_worker // SC_CHUNK

    @functools.partial(
        pl.kernel, mesh=_sc_mesh(), out_type=jax.ShapeDtypeStruct((m, w), table.dtype),
        scratch_types=[pltpu.VMEM((SC_CHUNK,), jnp.int32), pltpu.VMEM((SC_CHUNK, w), table.dtype),
                       pltpu.SemaphoreType.DMA],
        name="sc_gather_rows")
    def gather(table_hbm, idx_hbm, out_hbm, idx_v, rows_v, sem):
        base = _sc_worker_base(per_worker)

        @pl.loop(0, nchunk)
        def _(ci):
            off = pl.multiple_of(base + ci * SC_CHUNK, SC_CHUNK)
            pltpu.sync_copy(idx_hbm.at[pl.ds(off, SC_CHUNK)], idx_v)
            pltpu.async_copy(table_hbm.at[idx_v], rows_v, sem).wait()
            pltpu.sync_copy(rows_v, out_hbm.at[pl.ds(off, SC_CHUNK)])

    return gather(table, idx)


def _combine_kernel(h_ref, route_ref, fw_ref, y0_ref, y1_ref, o_ref):
    route = route_ref[...]
    g1 = route[:, 2:3]
    g2 = route[:, 3:4]
    a0, b0 = _unpack_bf16_pair(y0_ref[...])
    a1, b1 = _unpack_bf16_pair(y1_ref[...])
    moe = jnp.concatenate([a0 * g1 + a1 * g2, b0 * g1 + b1 * g2], axis=1)
    o_ref[...] = _rms(h_ref[...] + moe, fw_ref[...])


def _params(*sem):
    return pltpu.CompilerParams(dimension_semantics=sem, vmem_limit_bytes=VMEM_LIMIT)


def _full(shape):
    return pl.BlockSpec(shape, lambda *_: (0,) * len(shape))


def _rows(tile, width):
    return pl.BlockSpec((tile, width), lambda i: (i, 0))


def _layer(x2, pos2, attn_norm_w, w_in, gla_gate_up, gla_gate_bias, gla_norm_w, mla_q_norm_w,
           mla_w_uq, mla_kv_norm_w, mla_w_ukv, w_out, ffn_norm_w, router_group_w, router_group_b,
           router_expert_w, router_expert_b, expert_w_gate, expert_w_up, expert_w_down,
           out_norm_w, batch, seq):
    n, d = x2.shape
    dq = GLA_HEADS * GLA_DK
    dv = GLA_HEADS * GLA_DV
    dmla = MLA_HEADS * HEAD_PAD
    dmv = MLA_HEADS * MLA_DV

    c_gq, c_gk, c_gv, c_lr, c_og, c_cq, c_ckv, c_kr = jnp.split(
        w_in, [dq, 2 * dq, 2 * dq + dv, 2 * dq + dv + GLA_GATE_RANK,
               2 * dq + 2 * dv + GLA_GATE_RANK,
               2 * dq + 2 * dv + GLA_GATE_RANK + MLA_Q_RANK,
               2 * dq + 2 * dv + GLA_GATE_RANK + MLA_Q_RANK + MLA_KV_RANK], axis=1)
    zeros = lambda r, c: jnp.zeros((r, c), w_in.dtype)
    misc = jnp.concatenate([zeros(d, MLA_NOPE), c_kr, c_lr,
                            zeros(d, LANES - ROPE_END - GLA_GATE_RANK)], axis=1)
    w1 = jnp.concatenate([c_gq, c_gk, c_gv, c_og, c_cq, c_ckv, misc], axis=1).astype(BF16)
    gate_up = jnp.concatenate([zeros(GATE_LO, dq), gla_gate_up,
                               zeros(LANES - GATE_LO - GLA_GATE_RANK, dq)], axis=0).astype(BF16)
    wuq = mla_w_uq.reshape(MLA_Q_RANK, MLA_HEADS, MLA_NOPE + MLA_ROPE)
    wuq = jnp.pad(wuq, ((0, 0), (0, 0), (0, HEAD_PAD - ROPE_END))).reshape(MLA_Q_RANK, dmla)
    wukv = mla_w_ukv.reshape(MLA_KV_RANK, MLA_HEADS, MLA_NOPE + MLA_DV)
    wuk = jnp.pad(wukv[:, :, :MLA_NOPE], ((0, 0), (0, 0), (0, HEAD_PAD - MLA_NOPE)))
    wuv = jnp.pad(wukv[:, :, MLA_NOPE:], ((0, 0), (0, 0), (0, HEAD_PAD - MLA_DV)))
    wukv = jnp.concatenate([wuk.reshape(MLA_KV_RANK, dmla), wuv.reshape(MLA_KV_RANK, dmla)],
                           axis=1)
    inv = ROPE_THETA ** (-jnp.arange(ROPE_HALF, dtype=F32) / ROPE_HALF)
    inv_pat = jnp.concatenate([jnp.zeros((MLA_NOPE,), F32), inv, inv,
                               jnp.zeros((LANES - ROPE_END,), F32)]).reshape(1, LANES)
    w_router = jnp.concatenate(
        [router_expert_w, router_group_w, zeros(d, LANES - N_EXPERTS - N_GROUPS)], axis=1)
    w_router_hi = w_router.astype(BF16)
    w_router = jnp.concatenate(
        [w_router_hi, (w_router - w_router_hi.astype(F32)).astype(BF16)], axis=1)
    b_router = jnp.concatenate(
        [router_expert_b, router_group_b, jnp.zeros((LANES - N_EXPERTS - N_GROUPS,), F32)]
    ).reshape(1, LANES)
    row1 = lambda v: v.reshape(1, -1)

    tp = PROJ_TILE
    outs = pl.pallas_call(
        functools.partial(_proj_kernel, q_scale=(MLA_NOPE + MLA_ROPE) ** -0.5 * LOG2_E),
        grid=(n // tp,),
        in_specs=[_rows(tp, d), _rows(tp, 1), _full((1, d)), _full(w1.shape), _full(gate_up.shape),
                  _full((1, dq)), _full((1, MLA_Q_RANK)), _full(wuq.shape),
                  _full((1, MLA_KV_RANK)), _full(wukv.shape), _full((1, LANES))],
        out_specs=[_rows(tp, dq), _rows(tp, dq), _rows(tp, dv), _rows(tp, dq), _rows(tp, dv),
                   _rows(tp, dmla), _rows(tp, dmla), _rows(tp, dmla)],
        out_shape=[jax.ShapeDtypeStruct((n, dq), BF16), jax.ShapeDtypeStruct((n, dq), BF16),
                   jax.ShapeDtypeStruct((n, dv), BF16), jax.ShapeDtypeStruct((n, dq), F32),
                   jax.ShapeDtypeStruct((n, dv), BF16), jax.ShapeDtypeStruct((n, dmla), BF16),
                   jax.ShapeDtypeStruct((n, dmla), BF16), jax.ShapeDtypeStruct((n, dmla), BF16)],
        compiler_params=_params("parallel"),
        name="in_proj",
    )(x2, pos2, row1(attn_norm_w), w1, gate_up, row1(gla_gate_bias), row1(mla_q_norm_w),
      wuq.astype(BF16), row1(mla_kv_norm_w), wukv.astype(BF16), inv_pat)
    gq, gk, gv, gla, gog, q, k, v = outs

    tg = GLA_TILE
    seq3 = lambda a: a.reshape(batch, seq, a.shape[-1])
    gspec = lambda w: pl.BlockSpec((1, tg, w), lambda b, i: (b, i, 0))
    y_gla = pl.pallas_call(
        _gla_kernel,
        grid=(batch, seq // tg),
        in_specs=[gspec(dq), gspec(dq), gspec(dv), gspec(dq), gspec(dv), _full((1, GLA_DV))],
        out_specs=gspec(dv),
        out_shape=jax.ShapeDtypeStruct((batch, seq, dv), BF16),
        scratch_shapes=[pltpu.VMEM((GLA_HEADS, GLA_DV, GLA_DK), F32)],
        compiler_params=_params("parallel", "arbitrary"),
        name="gla",
    )(seq3(gq), seq3(gk), seq3(gv), seq3(gla), seq3(gog), row1(gla_norm_w))

    ta = ATT_TILE
    hps = ATT_HEADS_PER_STEP
    y_mla = pl.pallas_call(
        functools.partial(_mla_kernel, tq=ta),
        grid=(batch, MLA_HEADS // hps, seq // ta),
        in_specs=[pl.BlockSpec((1, seq, hps * HEAD_PAD), lambda b, hp, i: (b, 0, hp))] * 3,
        out_specs=pl.BlockSpec((1, ta, hps * MLA_DV), lambda b, hp, i: (b, i, hp)),
        out_shape=jax.ShapeDtypeStruct((batch, seq, dmv), BF16),
        scratch_shapes=[pltpu.VMEM((hps, ta, ta), F32)] * 3,
        compiler_params=_params("parallel", "parallel", "arbitrary"),
        name="mla",
    )(seq3(q), seq3(k), seq3(v))

    tr = ROUTER_TILE
    h, hn, route, route_t, counts = pl.pallas_call(
        _router_kernel,
        grid=(n // tr,),
        in_specs=[_rows(tr, d), _rows(tr, dv), _rows(tr, dmv), _full(w_out.shape), _full((1, d)),
                  _full(w_router.shape), _full((1, LANES))],
        out_specs=[_rows(tr, d), _rows(tr, d // 2), _rows(tr, LANES),
                   pl.BlockSpec((ROUTE_FIELDS, tr), lambda i: (0, i)), _full((1, LANES))],
        out_shape=[jax.ShapeDtypeStruct((n, d), F32), jax.ShapeDtypeStruct((n, d // 2), jnp.uint32),
                   jax.ShapeDtypeStruct((n, LANES), F32),
                   jax.ShapeDtypeStruct((ROUTE_FIELDS, n), F32),
                   jax.ShapeDtypeStruct((1, LANES), F32)],
        scratch_shapes=[pltpu.VMEM((1, LANES), F32)],
        compiler_params=_params("arbitrary"),
        name="out_proj_router",
    )(x2, y_gla.reshape(n, dv), y_mla.reshape(n, dmv), w_out.astype(BF16), row1(ffn_norm_w),
      w_router, b_router)

    blk = EXPERT_BLOCK
    nblk = (n * TOP_K) // blk + N_EXPERTS
    cnt = counts[0, :N_EXPERTS].astype(jnp.int32)
    padded = (cnt + blk - 1) // blk * blk
    pend = jnp.cumsum(padded)
    pstart = pend - padded
    dest = pl.pallas_call(
        _dest_kernel,
        grid_spec=pltpu.PrefetchScalarGridSpec(
            num_scalar_prefetch=1, grid=(1,),
            in_specs=[pl.BlockSpec((ROUTE_FIELDS, n), lambda i, ps: (0, 0))],
            out_specs=pl.BlockSpec((TOP_K, n), lambda i, ps: (0, 0))),
        out_shape=jax.ShapeDtypeStruct((TOP_K, n), jnp.int32),
        compiler_params=_params("arbitrary"),
        name="dest_rows",
    )(pstart, route_t)
    blk_start = jnp.arange(nblk, dtype=jnp.int32) * blk
    blk_expert = jnp.minimum(
        jnp.sum((pend[None, :] <= blk_start[:, None]).astype(jnp.int32), axis=1), N_EXPERTS - 1)
    n_used = (pend[-1] // blk).astype(jnp.int32).reshape(1)
    blk_valid = jnp.clip(cnt[blk_expert] - (blk_start - pstart[blk_expert]), 0, blk)
    blk_valid = jnp.where(jnp.arange(nblk) < n_used[0], blk_valid, 0).astype(jnp.int32)
    dest_slots = [dest[kk] for kk in range(TOP_K)]

    xbuf = _sc_scatter_rows(hn, dest_slots, nblk * blk)

    def used(j, nu):
        return jnp.maximum(jnp.minimum(j, nu[0] - 1), 0)

    ybuf = pl.pallas_call(
        _expert_kernel,
        grid_spec=pltpu.PrefetchScalarGridSpec(
            num_scalar_prefetch=3,
            grid=(nblk,),
            in_specs=[
                pl.BlockSpec((blk, d // 2), lambda j, be, nu, bv: (used(j, nu), 0)),
                pl.BlockSpec((1, d, D_EXPERT), lambda j, be, nu, bv: (be[used(j, nu)], 0, 0)),
                pl.BlockSpec((1, d, D_EXPERT), lambda j, be, nu, bv: (be[used(j, nu)], 0, 0)),
                pl.BlockSpec((1, D_EXPERT, d), lambda j, be, nu, bv: (be[used(j, nu)], 0, 0)),
            ],
            out_specs=pl.BlockSpec((blk, d // 2), lambda j, be, nu, bv: (j, 0)),
            scratch_shapes=[pltpu.VMEM((d, 2 * D_EXPERT), BF16), pltpu.VMEM((D_EXPERT, d), BF16)],
        ),
        out_shape=jax.ShapeDtypeStruct((nblk * blk, d // 2), jnp.uint32),
        compiler_params=_params("arbitrary"),
        name="experts",
    )(blk_expert, n_used, blk_valid, xbuf, expert_w_gate, expert_w_up, expert_w_down)

    gathered = _sc_gather_rows(ybuf, dest.reshape(TOP_K * n))
    nt = n // tp
    return pl.pallas_call(
        _combine_kernel,
        grid=(nt,),
        in_specs=[_rows(tp, d), _rows(tp, LANES), _full((1, d)), _rows(tp, d // 2),
                  pl.BlockSpec((tp, d // 2), lambda i: (i + nt, 0))],
        out_specs=_rows(tp, d),
        out_shape=jax.ShapeDtypeStruct((n, d), F32),
        compiler_params=_params("parallel"),
        name="combine",
    )(h, route, row1(out_norm_w), gathered, gathered)


def kernel(x, positions, attn_norm_w, w_in, gla_gate_up, gla_gate_bias, gla_norm_w, mla_q_norm_w,
           mla_w_uq, mla_kv_norm_w, mla_w_ukv, w_out, ffn_norm_w, router_group_w, router_group_b,
           router_expert_w, router_expert_b, expert_w_gate, expert_w_up, expert_w_down,
           final_norm_w):
    batch, seq, d = x.shape
    depth = w_in.shape[0]
    assert depth == 1, "the final norm is fused into the last layer's combine step"
    out = _layer(x.reshape(batch * seq, d), positions.reshape(batch * seq, 1),
                 attn_norm_w[0], w_in[0], gla_gate_up[0], gla_gate_bias[0], gla_norm_w[0],
                 mla_q_norm_w[0], mla_w_uq[0], mla_kv_norm_w[0], mla_w_ukv[0], w_out[0],
                 ffn_norm_w[0], router_group_w[0], router_group_b[0], router_expert_w[0],
                 router_expert_b[0], expert_w_gate[0], expert_w_up[0], expert_w_down[0],
                 final_norm_w, batch, seq)
    return out.reshape(batch, seq, d)
```

```python
import functools

import jax
import jax.numpy as jnp
from jax import lax
from jax.experimental import pallas as pl
from jax.experimental.pallas import tpu as pltpu
from jax.experimental.pallas import tpu_sc as plsc

EPS = 1e-6
GLA_HEADS = 4
GLA_DK = 64
GLA_DV = 128
GLA_GATE_RANK = 16
GLA_GATE_NORM = 16.0
GLA_CHUNK = 64
MLA_HEADS = 8
MLA_NOPE = 64
MLA_ROPE = 32
MLA_DV = 64
MLA_Q_RANK = 384
MLA_KV_RANK = 256
ROPE_THETA = 10000.0
N_GROUPS = 4
EXPERTS_PER_GROUP = 8
N_EXPERTS = N_GROUPS * EXPERTS_PER_GROUP
TOP_K = 2
D_EXPERT = 256

LANES = 128
HEAD_PAD = 128
ROPE_HALF = MLA_ROPE // 2
ROPE_LO = MLA_NOPE
ROPE_HI = MLA_NOPE + ROPE_HALF
ROPE_END = MLA_NOPE + MLA_ROPE
GATE_LO = ROPE_END

PROJ_TILE = 512
GLA_TILE = 512
GLA_SUBTILE = 256
ATT_TILE = 512
ATT_HEADS_PER_STEP = 4
EXPERT_BLOCK = 512
ROUTER_TILE = 1024
ROUTE_ROWS = 512
ROUTE_FIELDS = 8
VMEM_LIMIT = 56 * 1024 * 1024
SC_CORES = 2
SC_SUBCORES = 16
SC_WORKERS = SC_CORES * SC_SUBCORES
SC_CHUNK = 128

F32 = jnp.float32
BF16 = jnp.bfloat16
NEG_INF = float("-inf")
LOG2_E = 1.4426950408889634


def _dot(a, b, precision=None):
    return jnp.dot(a, b, preferred_element_type=F32, precision=precision)


def _dot_nt(a, b):
    return lax.dot_general(a, b, (((1,), (1,)), ((), ())), preferred_element_type=F32)


def _dot_tn(a, b):
    return lax.dot_general(a, b, (((0,), (0,)), ((), ())), preferred_element_type=F32)


def _rms(x, w):
    return x * lax.rsqrt(jnp.mean(x * x, axis=-1, keepdims=True) + EPS) * w


def _pack_bf16_pair(a, b):
    ua = lax.bitcast_convert_type(a.astype(BF16).astype(F32), jnp.uint32)
    ub = lax.bitcast_convert_type(b.astype(BF16).astype(F32), jnp.uint32)
    return (ua >> 16) | (ub & jnp.uint32(0xFFFF0000))


def _unpack_bf16_pair(u):
    a = lax.bitcast_convert_type(u << 16, F32)
    b = lax.bitcast_convert_type(u & jnp.uint32(0xFFFF0000), F32)
    return a, b


def _proj_kernel(x_ref, pos_ref, nw_ref, w1_ref, gu_ref, gb_ref, qnw_ref, wuq_ref, kvnw_ref,
                 wukv_ref, inv_ref,
                 gq_ref, gk_ref, gv_ref, gla_ref, gog_ref, q_ref, k_ref, v_ref, *, q_scale):
    x = x_ref[...]
    xn = _rms(x, nw_ref[...]).astype(BF16)
    proj = _dot(xn, w1_ref[...])
    dq = GLA_HEADS * GLA_DK
    dv = GLA_HEADS * GLA_DV
    o = 0
    gq_ref[...] = (proj[:, o:o + dq] * (GLA_DK ** -0.5)).astype(BF16); o += dq
    gk_ref[...] = proj[:, o:o + dq].astype(BF16); o += dq
    gv_ref[...] = proj[:, o:o + dv].astype(BF16); o += dv
    gog_ref[...] = proj[:, o:o + dv].astype(BF16); o += dv
    cq = proj[:, o:o + MLA_Q_RANK]; o += MLA_Q_RANK
    ckv = proj[:, o:o + MLA_KV_RANK]; o += MLA_KV_RANK
    misc = proj[:, o:o + LANES]

    z = _dot(misc.astype(BF16), gu_ref[...]) + gb_ref[...]
    log_sig = jnp.minimum(z, 0.0) - jnp.log1p(jnp.exp(-jnp.abs(z)))
    gla_ref[...] = log_sig * (1.0 / GLA_GATE_NORM)

    lane = lax.broadcasted_iota(jnp.int32, (x.shape[0], LANES), 1)
    ang = pos_ref[...].astype(F32) * inv_ref[...]
    cosv = jnp.cos(ang)
    sinv = jnp.sin(ang)
    in_lo = (lane >= ROPE_LO) & (lane < ROPE_HI)
    in_hi = (lane >= ROPE_HI) & (lane < ROPE_END)
    c_rope = jnp.where(in_lo | in_hi, cosv, 0.0)
    s_up = jnp.where(in_hi, sinv, 0.0)
    s_dn = jnp.where(in_lo, -sinv, 0.0)
    c_q = jnp.where(lane < MLA_NOPE, 1.0, c_rope)

    def rope(t, c):
        return (t * c + pltpu.roll(t, ROPE_HALF, 1) * s_up
                + pltpu.roll(t, LANES - ROPE_HALF, 1) * s_dn)

    k_rope = rope(misc, c_rope)

    q = _dot(_rms(cq, qnw_ref[...]).astype(BF16), wuq_ref[...])
    kv = _dot(_rms(ckv, kvnw_ref[...]).astype(BF16), wukv_ref[...])
    ones_lane = jnp.where(lane == MLA_DV, 1.0, 0.0)
    for h in range(MLA_HEADS):
        sl = slice(h * HEAD_PAD, (h + 1) * HEAD_PAD)
        vsl = slice((MLA_HEADS + h) * HEAD_PAD, (MLA_HEADS + h + 1) * HEAD_PAD)
        q_ref[:, sl] = (rope(q[:, sl], c_q) * q_scale).astype(BF16)
        k_ref[:, sl] = (kv[:, sl] + k_rope).astype(BF16)
        v_ref[:, sl] = (kv[:, vsl] + ones_lane).astype(BF16)


def _gla_kernel(q_ref, k_ref, v_ref, la_ref, og_ref, nw_ref, o_ref, st_ref):
    t = GLA_SUBTILE
    nchunk = t // GLA_CHUNK

    @pl.when(pl.program_id(1) == 0)
    def _():
        st_ref[...] = jnp.zeros_like(st_ref)

    row = lax.broadcasted_iota(jnp.int32, (t, t), 0)
    col = lax.broadcasted_iota(jnp.int32, (t, t), 1)
    chunk_bits = GLA_CHUNK.bit_length() - 1
    tri = ((row >> chunk_bits) == (col >> chunk_bits)) & (col <= row)
    tri_b = tri.astype(BF16)
    nw = nw_ref[...]
    states = [st_ref[h] for h in range(GLA_HEADS)]

    for sub in range(q_ref.shape[1] // t):
        tr = slice(sub * t, (sub + 1) * t)
        la = la_ref[0, tr]
        la_hi = la.astype(BF16)
        la_lo = (la - la_hi.astype(F32)).astype(BF16)
        parts = _dot(tri_b, jnp.concatenate([la_hi, la_lo], axis=1))
        b = parts[:, :la.shape[1]] + parts[:, la.shape[1]:]
        b_last = jnp.concatenate(
            [jnp.broadcast_to(b[(c + 1) * GLA_CHUNK - 1:(c + 1) * GLA_CHUNK],
                              (GLA_CHUNK, b.shape[1])) for c in range(nchunk)], axis=0)
        q_e = (q_ref[0, tr].astype(F32) * jnp.exp(b)).astype(BF16)
        kf = k_ref[0, tr].astype(F32)
        k_e = (kf * jnp.exp(-b)).astype(BF16)
        k_d = (kf * jnp.exp(b_last - b)).astype(BF16)
        decay = jnp.exp(b_last)

        for h in range(GLA_HEADS):
            ks = slice(h * GLA_DK, (h + 1) * GLA_DK)
            vs = slice(h * GLA_DV, (h + 1) * GLA_DV)
            qh, keh, kdh = q_e[:, ks], k_e[:, ks], k_d[:, ks]
            vh = v_ref[0, tr, vs]
            att = jnp.where(tri, _dot_nt(qh, keh), 0.0)
            o = _dot(att.astype(BF16), vh)
            state = states[h]
            inter = []
            for c in range(nchunk):
                rs = slice(c * GLA_CHUNK, (c + 1) * GLA_CHUNK)
                inter.append(_dot_nt(qh[rs], state.astype(BF16)))
                upd = _dot_tn(vh[rs], kdh[rs])
                state = state * decay[c * GLA_CHUNK:c * GLA_CHUNK + 1, ks] + upd
            states[h] = state
            o = o + jnp.concatenate(inter, axis=0)
            o = _rms(o, nw)
            g = og_ref[0, tr, vs].astype(F32)
            o_ref[0, tr, vs] = (o * (g * jax.nn.sigmoid(g))).astype(o_ref.dtype)

    for h in range(GLA_HEADS):
        st_ref[h] = states[h]


def _mla_kernel(q_ref, k_ref, v_ref, o_ref, sh_ref, s0_ref, s1_ref, *, tq):
    heads = q_ref.shape[2] // HEAD_PAD
    nq = q_ref.shape[1] // tq
    qi = pl.program_id(2)
    lane = lax.broadcasted_iota(jnp.int32, (tq, LANES), 1)
    causal = (lax.broadcasted_iota(jnp.int32, (tq, tq), 1)
              <= lax.broadcasted_iota(jnp.int32, (tq, tq), 0))

    def rows(i):
        return pl.ds(pl.multiple_of(i * tq, tq), tq)

    def produce(qblk, j, dst):
        for hh in range(heads):
            hs = slice(hh * HEAD_PAD, (hh + 1) * HEAD_PAD)
            dst[hh] = _dot_nt(q_ref[0, rows(qblk), hs], k_ref[0, rows(j), hs])

    def consume(j, src, carry, diagonal):
        new = []
        for hh in range(heads):
            m, acc = carry[hh]
            hs = slice(hh * HEAD_PAD, (hh + 1) * HEAD_PAD)
            s = src[hh]
            if diagonal:
                s = jnp.where(causal, s, NEG_INF)
            m_new = jnp.maximum(m, jnp.max(s, axis=-1, keepdims=True))
            p = jnp.exp2(s - m_new).astype(BF16)
            acc = jnp.exp2(m - m_new) * acc + _dot(p, v_ref[0, rows(j), hs])
            new.append((m_new, acc))
        return tuple(new)

    def finish(carry):
        for hp in range(heads // 2):
            o0, o1 = (acc / acc[:, MLA_DV:MLA_DV + 1] for _, acc in carry[2 * hp:2 * hp + 2])
            o_ref[0, :, hp * LANES:(hp + 1) * LANES] = jnp.where(
                lane < MLA_DV, o0, pltpu.roll(o1, MLA_DV, 1)).astype(o_ref.dtype)

    init = tuple((jnp.full((tq, 1), NEG_INF, F32), jnp.zeros((tq, LANES), F32))
                 for _ in range(heads))
    nxt = jnp.minimum(qi + 1, nq - 1)

    @pl.when(qi == 0)
    def _():
        produce(0, 0, sh_ref)
        carry = consume(0, sh_ref, init, True)
        produce(nxt, 0, sh_ref)
        finish(carry)

    @pl.when(qi > 0)
    def _():
        produce(qi, 1, s1_ref)
        carry = consume(0, sh_ref, init, False)

        def pair(i, carry):
            j = 1 + 2 * i
            produce(qi, j + 1, s0_ref)
            carry = consume(j, s1_ref, carry, False)
            produce(qi, j + 2, s1_ref)
            return consume(j + 1, s0_ref, carry, False)

        carry = lax.fori_loop(0, lax.shift_right_logical(qi - 1, 1), pair, carry)
        even = (qi & 1) == 0

        @pl.when(even)
        def _():
            produce(qi, qi, s0_ref)
            c = consume(qi - 1, s1_ref, carry, False)
            produce(nxt, 0, sh_ref)
            finish(consume(qi, s0_ref, c, True))

        @pl.when(jnp.logical_not(even))
        def _():
            produce(nxt, 0, sh_ref)
            finish(consume(qi, s1_ref, carry, True))


def _router_kernel(x_ref, yg_ref, ym_ref, wo_ref, fnw_ref, wr_ref, br_ref,
                   h_ref, hn_ref, route_ref, route_t_ref, cnt_ref, carry_ref, lg_ref):
    t = x_ref.shape[0]
    half = wo_ref.shape[0] // 2
    step = pl.program_id(0)

    @pl.when(step == 0)
    def _():
        carry_ref[...] = jnp.zeros_like(carry_ref)
        lg_ref[...] = jnp.zeros_like(lg_ref)

    logits_prev = lg_ref[...]
    h = x_ref[...] + _dot(yg_ref[...], wo_ref[:half]) + _dot(ym_ref[...], wo_ref[half:])
    h_ref[...] = h
    hn = _rms(h, fnw_ref[...])
    hp = hn.shape[1] // 2
    hn_ref[...] = _pack_bf16_pair(hn[:, :hp], hn[:, hp:])

    hn_hi = hn.astype(BF16)
    hn_lo = (hn - hn_hi.astype(F32)).astype(BF16)
    parts = _dot(hn_hi, wr_ref[...]) + _dot(hn_lo, wr_ref[...])
    lg_ref[...] = parts[:, :LANES] + parts[:, LANES:] + br_ref[...]

    live = (step > 0).astype(F32)
    tg = ROUTE_ROWS
    lane = lax.broadcasted_iota(jnp.int32, (tg, LANES), 1)
    lane_f = lane.astype(F32)
    rr = lax.broadcasted_iota(jnp.int32, (tg, tg), 0)
    cc = lax.broadcasted_iota(jnp.int32, (tg, tg), 1)
    earlier = (cc < rr).astype(BF16)

    def first_argmax(vals, vmax):
        idx = jnp.min(jnp.where(vals == vmax, lane_f, float(LANES)), axis=-1, keepdims=True)
        return idx.astype(jnp.int32)

    carry = carry_ref[...]
    for g in range(t // tg):
        rows = slice(g * tg, (g + 1) * tg)
        logits = logits_prev[rows]
        gl = jnp.where((lane >= N_EXPERTS) & (lane < N_EXPERTS + N_GROUPS), logits, NEG_INF)
        gmax = jnp.max(gl, axis=-1, keepdims=True)
        gsel = first_argmax(gl, gmax) - N_EXPERTS
        p_g = 1.0 / jnp.sum(jnp.exp(gl - gmax), axis=-1, keepdims=True)
        lo = gsel * EXPERTS_PER_GROUP
        el = jnp.where((lane >= lo) & (lane < lo + EXPERTS_PER_GROUP), logits, NEG_INF)
        m1 = jnp.max(el, axis=-1, keepdims=True)
        i1 = first_argmax(el, m1)
        el2 = jnp.where(lane == i1, NEG_INF, el)
        m2 = jnp.max(el2, axis=-1, keepdims=True)
        i2 = first_argmax(el2, m2)
        e2 = jnp.exp(m2 - m1)
        g1 = p_g / (1.0 + e2)
        g2 = p_g * e2 / (1.0 + e2)

        is1 = lane == i1
        is2 = lane == i2
        onehot = (is1 | is2).astype(BF16)
        before = _dot(earlier, onehot) + carry
        r1 = jnp.sum(jnp.where(is1, before, 0.0), axis=-1, keepdims=True)
        r2 = jnp.sum(jnp.where(is2, before, 0.0), axis=-1, keepdims=True)
        carry = carry + live * jnp.sum(onehot.astype(F32), axis=0, keepdims=True)

        route = jnp.where(lane == 0, i1.astype(F32), 0.0)
        route = jnp.where(lane == 1, i2.astype(F32), route)
        route = jnp.where(lane == 2, g1, route)
        route = jnp.where(lane == 3, g2, route)
        route = jnp.where(lane == 4, r1, route)
        route = jnp.where(lane == 5, r2, route)
        route_ref[rows] = route
        route_t_ref[:, rows] = route.T[:ROUTE_FIELDS]
    carry_ref[...] = carry
    cnt_ref[...] = carry


def _dest_kernel(pstart_ref, route_t_ref, dest_ref):
    eid = route_t_ref[0:TOP_K, :].astype(jnp.int32)
    rank = route_t_ref[4:4 + TOP_K, :].astype(jnp.int32)
    start = jnp.zeros_like(eid)
    for e in range(N_EXPERTS):
        start = jnp.where(eid == e, pstart_ref[e], start)
    dest_ref[...] = start + rank


def _sc_mesh():
    return plsc.VectorSubcoreMesh(core_axis_name="c", subcore_axis_name="s",
                                  num_cores=SC_CORES, num_subcores=SC_SUBCORES)


def _sc_worker_base(per_worker):
    return (lax.axis_index("s") * SC_CORES + lax.axis_index("c")) * per_worker


def _sc_scatter_rows(rows, idx_list, nrows):
    m, w = rows.shape
    nidx = len(idx_list)
    per_worker = m // SC_WORKERS
    nchunk = per_worker // SC_CHUNK

    @functools.partial(
        pl.kernel, mesh=_sc_mesh(), out_type=jax.ShapeDtypeStruct((nrows, w), rows.dtype),
        scratch_types=[pltpu.VMEM((SC_CHUNK,), jnp.int32)] * nidx
        + [pltpu.VMEM((SC_CHUNK, w), rows.dtype), pltpu.SemaphoreType.DMA],
        name="sc_scatter_rows")
    def scatter(rows_hbm, *refs):
        idx_hbm, out_hbm = refs[:nidx], refs[nidx]
        idx_v, rows_v, sem = refs[nidx + 1:2 * nidx + 1], refs[2 * nidx + 1], refs[2 * nidx + 2]
        base = _sc_worker_base(per_worker)

        @pl.loop(0, nchunk)
        def _(ci):
            off = pl.multiple_of(base + ci * SC_CHUNK, SC_CHUNK)
            pltpu.sync_copy(rows_hbm.at[pl.ds(off, SC_CHUNK)], rows_v)
            for kk in range(nidx):
                pltpu.sync_copy(idx_hbm[kk].at[pl.ds(off, SC_CHUNK)], idx_v[kk])
            copies = [pltpu.async_copy(rows_v, out_hbm.at[idx_v[kk]], sem) for kk in range(nidx)]
            for c in copies:
                c.wait()

    return scatter(rows, *idx_list)


def _expert_kernel(be_ref, nu_ref, bv_ref, x_ref, wg_ref, wu_ref, wd_ref, y_ref, wgu_s, wd_s):
    j = pl.program_id(0)
    de = wg_ref.shape[2]
    last = jnp.maximum(nu_ref[0] - 1, 0)
    cur = be_ref[jnp.minimum(j, last)]
    prev = be_ref[jnp.minimum(jnp.maximum(j - 1, 0), last)]

    @pl.when((j == 0) | (cur != prev))
    def _():
        wgu_s[:, :de] = wg_ref[0].astype(BF16)
        wgu_s[:, de:] = wu_ref[0].astype(BF16)
        wd_s[...] = wd_ref[0].astype(BF16)

    @pl.when(j < nu_ref[0])
    def _():
        row = lax.broadcasted_iota(jnp.int32, x_ref.shape, 0)
        xu = jnp.where(row < bv_ref[j], x_ref[...], jnp.uint32(0))
        a, b = _unpack_bf16_pair(xu)
        x = jnp.concatenate([a, b], axis=1).astype(BF16)
        h12 = _dot(x, wgu_s[...])
        h1, h2 = h12[:, :de], h12[:, de:]
        hdn = (h1 * jax.nn.sigmoid(h1) * h2).astype(BF16)
        y = _dot(hdn, wd_s[...])
        hp = y.shape[1] // 2
        y_ref[...] = _pack_bf16_pair(y[:, :hp], y[:, hp:])

    @pl.when(pl.program_id(0) >= nu_ref[0])
    def _():
        y_ref[...] = jnp.zeros_like(y_ref)


def _sc_gather_rows(table, idx):
    m = idx.shape[0]
    w = table.shape[1]
    per_worker = m // SC_WORKERS
    nchunk = per_worker // SC_CHUNK

    @functools.partial(
        pl.kernel, mesh=_sc_mesh(), out_type=jax.ShapeDtypeStruct((m, w), table.dtype),
        scratch_types=[pltpu.VMEM((SC_CHUNK,), jnp.int32), pltpu.VMEM((SC_CHUNK, w), table.dtype),
                       pltpu.SemaphoreType.DMA],
        name="sc_gather_rows")
    def gather(table_hbm, idx_hbm, out_hbm, idx_v, rows_v, sem):
        base = _sc_worker_base(per_worker)

        @pl.loop(0, nchunk)
        def _(ci):
            off = pl.multiple_of(base + ci * SC_CHUNK, SC_CHUNK)
            pltpu.sync_copy(idx_hbm.at[pl.ds(off, SC_CHUNK)], idx_v)
            pltpu.async_copy(table_hbm.at[idx_v], rows_v, sem).wait()
            pltpu.sync_copy(rows_v, out_hbm.at[pl.ds(off, SC_CHUNK)])

    return gather(table, idx)


def _combine_kernel(h_ref, route_ref, fw_ref, y0_ref, y1_ref, o_ref):
    route = route_ref[...]
    g1 = route[:, 2:3]
    g2 = route[:, 3:4]
    a0, b0 = _unpack_bf16_pair(y0_ref[...])
    a1, b1 = _unpack_bf16_pair(y1_ref[...])
    moe = jnp.concatenate([a0 * g1 + a1 * g2, b0 * g1 + b1 * g2], axis=1)
    o_ref[...] = _rms(h_ref[...] + moe, fw_ref[...])


def _params(*sem):
    return pltpu.CompilerParams(dimension_semantics=sem, vmem_limit_bytes=VMEM_LIMIT)


def _full(shape):
    return pl.BlockSpec(shape, lambda *_: (0,) * len(shape))


def _rows(tile, width):
    return pl.BlockSpec((tile, width), lambda i: (i, 0))


def _layer(x2, pos2, attn_norm_w, w_in, gla_gate_up, gla_gate_bias, gla_norm_w, mla_q_norm_w,
           mla_w_uq, mla_kv_norm_w, mla_w_ukv, w_out, ffn_norm_w, router_group_w, router_group_b,
           router_expert_w, router_expert_b, expert_w_gate, expert_w_up, expert_w_down,
           out_norm_w, batch, seq):
    n, d = x2.shape
    dq = GLA_HEADS * GLA_DK
    dv = GLA_HEADS * GLA_DV
    dmla = MLA_HEADS * HEAD_PAD
    dmv = MLA_HEADS * MLA_DV

    c_gq, c_gk, c_gv, c_lr, c_og, c_cq, c_ckv, c_kr = jnp.split(
        w_in, [dq, 2 * dq, 2 * dq + dv, 2 * dq + dv + GLA_GATE_RANK,
               2 * dq + 2 * dv + GLA_GATE_RANK,
               2 * dq + 2 * dv + GLA_GATE_RANK + MLA_Q_RANK,
               2 * dq + 2 * dv + GLA_GATE_RANK + MLA_Q_RANK + MLA_KV_RANK], axis=1)
    zeros = lambda r, c: jnp.zeros((r, c), w_in.dtype)
    misc = jnp.concatenate([zeros(d, MLA_NOPE), c_kr, c_lr,
                            zeros(d, LANES - ROPE_END - GLA_GATE_RANK)], axis=1)
    w1 = jnp.concatenate([c_gq, c_gk, c_gv, c_og, c_cq, c_ckv, misc], axis=1).astype(BF16)
    gate_up = jnp.concatenate([zeros(GATE_LO, dq), gla_gate_up,
                               zeros(LANES - GATE_LO - GLA_GATE_RANK, dq)], axis=0).astype(BF16)
    wuq = mla_w_uq.reshape(MLA_Q_RANK, MLA_HEADS, MLA_NOPE + MLA_ROPE)
    wuq = jnp.pad(wuq, ((0, 0), (0, 0), (0, HEAD_PAD - ROPE_END))).reshape(MLA_Q_RANK, dmla)
    wukv = mla_w_ukv.reshape(MLA_KV_RANK, MLA_HEADS, MLA_NOPE + MLA_DV)
    wuk = jnp.pad(wukv[:, :, :MLA_NOPE], ((0, 0), (0, 0), (0, HEAD_PAD - MLA_NOPE)))
    wuv = jnp.pad(wukv[:, :, MLA_NOPE:], ((0, 0), (0, 0), (0, HEAD_PAD - MLA_DV)))
    wukv = jnp.concatenate([wuk.reshape(MLA_KV_RANK, dmla), wuv.reshape(MLA_KV_RANK, dmla)],
                           axis=1)
    inv = ROPE_THETA ** (-jnp.arange(ROPE_HALF, dtype=F32) / ROPE_HALF)
    inv_pat = jnp.concatenate([jnp.zeros((MLA_NOPE,), F32), inv, inv,
                               jnp.zeros((LANES - ROPE_END,), F32)]).reshape(1, LANES)
    w_router = jnp.concatenate(
        [router_expert_w, router_group_w, zeros(d, LANES - N_EXPERTS - N_GROUPS)], axis=1)
    w_router_hi = w_router.astype(BF16)
    w_router = jnp.concatenate(
        [w_router_hi, (w_router - w_router_hi.astype(F32)).astype(BF16)], axis=1)
    b_router = jnp.concatenate(
        [router_expert_b, router_group_b, jnp.zeros((LANES - N_EXPERTS - N_GROUPS,), F32)]
    ).reshape(1, LANES)
    row1 = lambda v: v.reshape(1, -1)

    tp = PROJ_TILE
    outs = pl.pallas_call(
        functools.partial(_proj_kernel, q_scale=(MLA_NOPE + MLA_ROPE) ** -0.5 * LOG2_E),
        grid=(n // tp,),
        in_specs=[_rows(tp, d), _rows(tp, 1), _full((1, d)), _full(w1.shape), _full(gate_up.shape),
                  _full((1, dq)), _full((1, MLA_Q_RANK)), _full(wuq.shape),
                  _full((1, MLA_KV_RANK)), _full(wukv.shape), _full((1, LANES))],
        out_specs=[_rows(tp, dq), _rows(tp, dq), _rows(tp, dv), _rows(tp, dq), _rows(tp, dv),
                   _rows(tp, dmla), _rows(tp, dmla), _rows(tp, dmla)],
        out_shape=[jax.ShapeDtypeStruct((n, dq), BF16), jax.ShapeDtypeStruct((n, dq), BF16),
                   jax.ShapeDtypeStruct((n, dv), BF16), jax.ShapeDtypeStruct((n, dq), F32),
                   jax.ShapeDtypeStruct((n, dv), BF16), jax.ShapeDtypeStruct((n, dmla), BF16),
                   jax.ShapeDtypeStruct((n, dmla), BF16), jax.ShapeDtypeStruct((n, dmla), BF16)],
        compiler_params=_params("parallel"),
        name="in_proj",
    )(x2, pos2, row1(attn_norm_w), w1, gate_up, row1(gla_gate_bias), row1(mla_q_norm_w),
      wuq.astype(BF16), row1(mla_kv_norm_w), wukv.astype(BF16), inv_pat)
    gq, gk, gv, gla, gog, q, k, v = outs

    tg = GLA_TILE
    seq3 = lambda a: a.reshape(batch, seq, a.shape[-1])
    gspec = lambda w: pl.BlockSpec((1, tg, w), lambda b, i: (b, i, 0))
    y_gla = pl.pallas_call(
        _gla_kernel,
        grid=(batch, seq // tg),
        in_specs=[gspec(dq), gspec(dq), gspec(dv), gspec(dq), gspec(dv), _full((1, GLA_DV))],
        out_specs=gspec(dv),
        out_shape=jax.ShapeDtypeStruct((batch, seq, dv), BF16),
        scratch_shapes=[pltpu.VMEM((GLA_HEADS, GLA_DV, GLA_DK), F32)],
        compiler_params=_params("parallel", "arbitrary"),
        name="gla",
    )(seq3(gq), seq3(gk), seq3(gv), seq3(gla), seq3(gog), row1(gla_norm_w))

    ta = ATT_TILE
    hps = ATT_HEADS_PER_STEP
    y_mla = pl.pallas_call(
        functools.partial(_mla_kernel, tq=ta),
        grid=(batch, MLA_HEADS // hps, seq // ta),
        in_specs=[pl.BlockSpec((1, seq, hps * HEAD_PAD), lambda b, hp, i: (b, 0, hp))] * 3,
        out_specs=pl.BlockSpec((1, ta, hps * MLA_DV), lambda b, hp, i: (b, i, hp)),
        out_shape=jax.ShapeDtypeStruct((batch, seq, dmv), BF16),
        scratch_shapes=[pltpu.VMEM((hps, ta, ta), F32)] * 3,
        compiler_params=_params("parallel", "parallel", "arbitrary"),
        name="mla",
    )(seq3(q), seq3(k), seq3(v))

    tr = ROUTER_TILE
    ntr = n // tr
    cur = lambda t, w: pl.BlockSpec((t, w), lambda i: (jnp.minimum(i, ntr - 1), 0))
    prev = lambda t, w: pl.BlockSpec((t, w), lambda i: (jnp.maximum(i - 1, 0), 0))
    h, hn, route, route_t, counts = pl.pallas_call(
        _router_kernel,
        grid=(ntr + 1,),
        in_specs=[cur(tr, d), cur(tr, dv), cur(tr, dmv), _full(w_out.shape), _full((1, d)),
                  _full(w_router.shape), _full((1, LANES))],
        out_specs=[cur(tr, d), cur(tr, d // 2), prev(tr, LANES),
                   pl.BlockSpec((ROUTE_FIELDS, tr), lambda i: (0, jnp.maximum(i - 1, 0))),
                   _full((1, LANES))],
        out_shape=[jax.ShapeDtypeStruct((n, d), F32), jax.ShapeDtypeStruct((n, d // 2), jnp.uint32),
                   jax.ShapeDtypeStruct((n, LANES), F32),
                   jax.ShapeDtypeStruct((ROUTE_FIELDS, n), F32),
                   jax.ShapeDtypeStruct((1, LANES), F32)],
        scratch_shapes=[pltpu.VMEM((1, LANES), F32), pltpu.VMEM((tr, LANES), F32)],
        compiler_params=_params("arbitrary"),
        name="out_proj_router",
    )(x2, y_gla.reshape(n, dv), y_mla.reshape(n, dmv), w_out.astype(BF16), row1(ffn_norm_w),
      w_router, b_router)

    blk = EXPERT_BLOCK
    nblk = (n * TOP_K) // blk + N_EXPERTS
    cnt = counts[0, :N_EXPERTS].astype(jnp.int32)
    padded = (cnt + blk - 1) // blk * blk
    pend = jnp.cumsum(padded)
    pstart = pend - padded
    dest = pl.pallas_call(
        _dest_kernel,
        grid_spec=pltpu.PrefetchScalarGridSpec(
            num_scalar_prefetch=1, grid=(1,),
            in_specs=[pl.BlockSpec((ROUTE_FIELDS, n), lambda i, ps: (0, 0))],
            out_specs=pl.BlockSpec((TOP_K, n), lambda i, ps: (0, 0))),
        out_shape=jax.ShapeDtypeStruct((TOP_K, n), jnp.int32),
        compiler_params=_params("arbitrary"),
        name="dest_rows",
    )(pstart, route_t)
    blk_start = jnp.arange(nblk, dtype=jnp.int32) * blk
    blk_expert = jnp.minimum(
        jnp.sum((pend[None, :] <= blk_start[:, None]).astype(jnp.int32), axis=1), N_EXPERTS - 1)
    n_used = (pend[-1] // blk).astype(jnp.int32).reshape(1)
    blk_valid = jnp.clip(cnt[blk_expert] - (blk_start - pstart[blk_expert]), 0, blk)
    blk_valid = jnp.where(jnp.arange(nblk) < n_used[0], blk_valid, 0).astype(jnp.int32)
    dest_slots = [dest[kk] for kk in range(TOP_K)]

    xbuf = _sc_scatter_rows(hn, dest_slots, nblk * blk)

    def used(j, nu):
        return jnp.maximum(jnp.minimum(j, nu[0] - 1), 0)

    ybuf = pl.pallas_call(
        _expert_kernel,
        grid_spec=pltpu.PrefetchScalarGridSpec(
            num_scalar_prefetch=3,
            grid=(nblk,),
            in_specs=[
                pl.BlockSpec((blk, d // 2), lambda j, be, nu, bv: (used(j, nu), 0)),
                pl.BlockSpec((1, d, D_EXPERT), lambda j, be, nu, bv: (be[used(j, nu)], 0, 0)),
                pl.BlockSpec((1, d, D_EXPERT), lambda j, be, nu, bv: (be[used(j, nu)], 0, 0)),
                pl.BlockSpec((1, D_EXPERT, d), lambda j, be, nu, bv: (be[used(j, nu)], 0, 0)),
            ],
            out_specs=pl.BlockSpec((blk, d // 2), lambda j, be, nu, bv: (j, 0)),
            scratch_shapes=[pltpu.VMEM((d, 2 * D_EXPERT), BF16), pltpu.VMEM((D_EXPERT, d), BF16)],
        ),
        out_shape=jax.ShapeDtypeStruct((nblk * blk, d // 2), jnp.uint32),
        compiler_params=_params("arbitrary"),
        name="experts",
    )(blk_expert, n_used, blk_valid, xbuf, expert_w_gate, expert_w_up, expert_w_down)

    gathered = _sc_gather_rows(ybuf, dest.reshape(TOP_K * n))
    nt = n // tp
    return pl.pallas_call(
        _combine_kernel,
        grid=(nt,),
        in_specs=[_rows(tp, d), _rows(tp, LANES), _full((1, d)), _rows(tp, d // 2),
                  pl.BlockSpec((tp, d // 2), lambda i: (i + nt, 0))],
        out_specs=_rows(tp, d),
        out_shape=jax.ShapeDtypeStruct((n, d), F32),
        compiler_params=_params("parallel"),
        name="combine",
    )(h, route, row1(out_norm_w), gathered, gathered)


def kernel(x, positions, attn_norm_w, w_in, gla_gate_up, gla_gate_bias, gla_norm_w, mla_q_norm_w,
           mla_w_uq, mla_kv_norm_w, mla_w_ukv, w_out, ffn_norm_w, router_group_w, router_group_b,
           router_expert_w, router_expert_b, expert_w_gate, expert_w_up, expert_w_down,
           final_norm_w):
    batch, seq, d = x.shape
    depth = w_in.shape[0]
    assert depth == 1, "the final norm is fused into the last layer's combine step"
    out = _layer(x.reshape(batch * seq, d), positions.reshape(batch * seq, 1),
                 attn_norm_w[0], w_in[0], gla_gate_up[0], gla_gate_bias[0], gla_norm_w[0],
                 mla_q_norm_w[0], mla_w_uq[0], mla_kv_norm_w[0], mla_w_ukv[0], w_out[0],
                 ffn_norm_w[0], router_group_w[0], router_group_b[0], router_expert_w[0],
                 router_expert_b[0], expert_w_gate[0], expert_w_up[0], expert_w_down[0],
                 final_norm_w, batch, seq)
    return out.reshape(batch, seq, d)
```

```python
import functools

import jax
import jax.numpy as jnp
from jax import lax
from jax.experimental import pallas as pl
from jax.experimental.pallas import tpu as pltpu
from jax.experimental.pallas import tpu_sc as plsc

EPS = 1e-6
GLA_HEADS = 4
GLA_DK = 64
GLA_DV = 128
GLA_GATE_RANK = 16
GLA_GATE_NORM = 16.0
GLA_CHUNK = 64
MLA_HEADS = 8
MLA_NOPE = 64
MLA_ROPE = 32
MLA_DV = 64
MLA_Q_RANK = 384
MLA_KV_RANK = 256
ROPE_THETA = 10000.0
N_GROUPS = 4
EXPERTS_PER_GROUP = 8
N_EXPERTS = N_GROUPS * EXPERTS_PER_GROUP
TOP_K = 2
D_EXPERT = 256

LANES = 128
HEAD_PAD = 128
ROPE_HALF = MLA_ROPE // 2
ROPE_LO = MLA_NOPE
ROPE_HI = MLA_NOPE + ROPE_HALF
ROPE_END = MLA_NOPE + MLA_ROPE
GATE_LO = ROPE_END

PROJ_TILE = 512
COMBINE_TILE = 1024
GLA_TILE = 1024
GLA_SUBTILE = 256
ATT_TILE = 512
ATT_HEADS_PER_STEP = 4
EXPERT_BLOCK = 512
ROUTER_TILE = 1024
ROUTE_ROWS = 512
ROUTE_FIELDS = 8
VMEM_LIMIT = 56 * 1024 * 1024
SC_CORES = 2
SC_SUBCORES = 16
SC_WORKERS = SC_CORES * SC_SUBCORES
SC_CHUNK = 128

F32 = jnp.float32
BF16 = jnp.bfloat16
NEG_INF = float("-inf")
LOG2_E = 1.4426950408889634


def _dot(a, b, precision=None):
    return jnp.dot(a, b, preferred_element_type=F32, precision=precision)


def _dot_nt(a, b):
    return lax.dot_general(a, b, (((1,), (1,)), ((), ())), preferred_element_type=F32)


def _dot_tn(a, b):
    return lax.dot_general(a, b, (((0,), (0,)), ((), ())), preferred_element_type=F32)


def _rms(x, w):
    return x * lax.rsqrt(jnp.mean(x * x, axis=-1, keepdims=True) + EPS) * w


def _pack_bf16_pair(a, b):
    ua = lax.bitcast_convert_type(a.astype(BF16).astype(F32), jnp.uint32)
    ub = lax.bitcast_convert_type(b.astype(BF16).astype(F32), jnp.uint32)
    return (ua >> 16) | (ub & jnp.uint32(0xFFFF0000))


def _unpack_bf16_pair(u):
    a = lax.bitcast_convert_type(u << 16, F32)
    b = lax.bitcast_convert_type(u & jnp.uint32(0xFFFF0000), F32)
    return a, b


def _proj_kernel(x_ref, pos_ref, nw_ref, w1_ref, gu_ref, gb_ref, qnw_ref, wuq_ref, kvnw_ref,
                 wukv_ref, inv_ref,
                 gq_ref, gk_ref, gv_ref, gla_ref, gog_ref, q_ref, k_ref, v_ref, *, q_scale):
    x = x_ref[...]
    xn = _rms(x, nw_ref[...]).astype(BF16)
    dq = GLA_HEADS * GLA_DK
    dv = GLA_HEADS * GLA_DV
    o = 0

    def proj(width):
        nonlocal o
        res = _dot(xn, w1_ref[:, o:o + width])
        o += width
        return res

    gq_ref[...] = (proj(dq) * (GLA_DK ** -0.5)).astype(BF16)
    gk_ref[...] = proj(dq).astype(BF16)
    gv_ref[...] = proj(dv).astype(BF16)
    gog_ref[...] = proj(dv).astype(BF16)
    cq = proj(MLA_Q_RANK)
    ckv = proj(MLA_KV_RANK)
    misc = proj(LANES)

    z = _dot(misc.astype(BF16), gu_ref[...]) + gb_ref[...]
    log_sig = jnp.minimum(z, 0.0) - jnp.log1p(jnp.exp(-jnp.abs(z)))
    gla_ref[...] = log_sig * (1.0 / GLA_GATE_NORM)

    lane = lax.broadcasted_iota(jnp.int32, (x.shape[0], LANES), 1)
    ang = pos_ref[...].astype(F32) * inv_ref[...]
    cosv = jnp.cos(ang)
    sinv = jnp.sin(ang)
    in_lo = (lane >= ROPE_LO) & (lane < ROPE_HI)
    in_hi = (lane >= ROPE_HI) & (lane < ROPE_END)
    c_rope = jnp.where(in_lo | in_hi, cosv, 0.0)
    s_up = jnp.where(in_hi, sinv, 0.0)
    s_dn = jnp.where(in_lo, -sinv, 0.0)
    c_q = jnp.where(lane < MLA_NOPE, 1.0, c_rope)

    def rope(t, c):
        return (t * c + pltpu.roll(t, ROPE_HALF, 1) * s_up
                + pltpu.roll(t, LANES - ROPE_HALF, 1) * s_dn)

    k_rope = rope(misc, c_rope)

    q = _dot(_rms(cq, qnw_ref[...]).astype(BF16), wuq_ref[...])
    kv = _dot(_rms(ckv, kvnw_ref[...]).astype(BF16), wukv_ref[...])
    ones_lane = jnp.where(lane == MLA_DV, 1.0, 0.0)
    for h in range(MLA_HEADS):
        sl = slice(h * HEAD_PAD, (h + 1) * HEAD_PAD)
        vsl = slice((MLA_HEADS + h) * HEAD_PAD, (MLA_HEADS + h + 1) * HEAD_PAD)
        q_ref[:, sl] = (rope(q[:, sl], c_q) * q_scale).astype(BF16)
        k_ref[:, sl] = (kv[:, sl] + k_rope).astype(BF16)
        v_ref[:, sl] = (kv[:, vsl] + ones_lane).astype(BF16)


def _gla_kernel(q_ref, k_ref, v_ref, la_ref, og_ref, nw_ref, o_ref, st_ref):
    t = GLA_SUBTILE
    nchunk = t // GLA_CHUNK

    @pl.when(pl.program_id(1) == 0)
    def _():
        st_ref[...] = jnp.zeros_like(st_ref)

    row = lax.broadcasted_iota(jnp.int32, (t, t), 0)
    col = lax.broadcasted_iota(jnp.int32, (t, t), 1)
    chunk_bits = GLA_CHUNK.bit_length() - 1
    tri = ((row >> chunk_bits) == (col >> chunk_bits)) & (col <= row)
    tri_b = tri.astype(BF16)
    nw = nw_ref[...]
    states = [st_ref[h] for h in range(GLA_HEADS)]

    for sub in range(q_ref.shape[1] // t):
        tr = slice(sub * t, (sub + 1) * t)
        la = la_ref[0, tr]
        la_hi = la.astype(BF16)
        la_lo = (la - la_hi.astype(F32)).astype(BF16)
        parts = _dot(tri_b, jnp.concatenate([la_hi, la_lo], axis=1))
        b = parts[:, :la.shape[1]] + parts[:, la.shape[1]:]
        b_last = jnp.concatenate(
            [jnp.broadcast_to(b[(c + 1) * GLA_CHUNK - 1:(c + 1) * GLA_CHUNK],
                              (GLA_CHUNK, b.shape[1])) for c in range(nchunk)], axis=0)
        q_e = (q_ref[0, tr].astype(F32) * jnp.exp(b)).astype(BF16)
        kf = k_ref[0, tr].astype(F32)
        k_e = (kf * jnp.exp(-b)).astype(BF16)
        k_d = (kf * jnp.exp(b_last - b)).astype(BF16)
        decay = jnp.exp(b_last)

        for h in range(GLA_HEADS):
            ks = slice(h * GLA_DK, (h + 1) * GLA_DK)
            vs = slice(h * GLA_DV, (h + 1) * GLA_DV)
            qh, keh, kdh = q_e[:, ks], k_e[:, ks], k_d[:, ks]
            vh = v_ref[0, tr, vs]
            att = jnp.where(tri, _dot_nt(qh, keh), 0.0)
            o = _dot(att.astype(BF16), vh)
            state = states[h]
            inter = []
            for c in range(nchunk):
                rs = slice(c * GLA_CHUNK, (c + 1) * GLA_CHUNK)
                inter.append(_dot_nt(qh[rs], state.astype(BF16)))
                upd = _dot_tn(vh[rs], kdh[rs])
                state = state * decay[c * GLA_CHUNK:c * GLA_CHUNK + 1, ks] + upd
            states[h] = state
            o = o + jnp.concatenate(inter, axis=0)
            o = _rms(o, nw)
            g = og_ref[0, tr, vs].astype(F32)
            o_ref[0, tr, vs] = (o * (g * jax.nn.sigmoid(g))).astype(o_ref.dtype)

    for h in range(GLA_HEADS):
        st_ref[h] = states[h]


def _mla_kernel(q_ref, k_ref, v_ref, o_ref, sh_ref, s0_ref, s1_ref, *, tq):
    heads = q_ref.shape[2] // HEAD_PAD
    nq = q_ref.shape[1] // tq
    qi = pl.program_id(2)
    lane = lax.broadcasted_iota(jnp.int32, (tq, LANES), 1)
    causal = (lax.broadcasted_iota(jnp.int32, (tq, tq), 1)
              <= lax.broadcasted_iota(jnp.int32, (tq, tq), 0))

    def rows(i):
        return pl.ds(pl.multiple_of(i * tq, tq), tq)

    def produce(qblk, j, dst):
        for hh in range(heads):
            hs = slice(hh * HEAD_PAD, (hh + 1) * HEAD_PAD)
            dst[hh] = _dot_nt(q_ref[0, rows(qblk), hs], k_ref[0, rows(j), hs])

    def consume(j, src, carry, diagonal):
        new = []
        for hh in range(heads):
            m, acc = carry[hh]
            hs = slice(hh * HEAD_PAD, (hh + 1) * HEAD_PAD)
            s = src[hh]
            if diagonal:
                s = jnp.where(causal, s, NEG_INF)
            m_new = jnp.maximum(m, jnp.max(s, axis=-1, keepdims=True))
            p = jnp.exp2(s - m_new).astype(BF16)
            acc = jnp.exp2(m - m_new) * acc + _dot(p, v_ref[0, rows(j), hs])
            new.append((m_new, acc))
        return tuple(new)

    def finish(carry):
        for hp in range(heads // 2):
            o0, o1 = (acc / acc[:, MLA_DV:MLA_DV + 1] for _, acc in carry[2 * hp:2 * hp + 2])
            o_ref[0, :, hp * LANES:(hp + 1) * LANES] = jnp.where(
                lane < MLA_DV, o0, pltpu.roll(o1, MLA_DV, 1)).astype(o_ref.dtype)

    init = tuple((jnp.full((tq, 1), NEG_INF, F32), jnp.zeros((tq, LANES), F32))
                 for _ in range(heads))
    nxt = jnp.minimum(qi + 1, nq - 1)

    @pl.when(qi == 0)
    def _():
        produce(0, 0, sh_ref)
        carry = consume(0, sh_ref, init, True)
        produce(nxt, 0, sh_ref)
        finish(carry)

    @pl.when(qi > 0)
    def _():
        produce(qi, 1, s1_ref)
        carry = consume(0, sh_ref, init, False)

        def pair(i, carry):
            j = 1 + 2 * i
            produce(qi, j + 1, s0_ref)
            carry = consume(j, s1_ref, carry, False)
            produce(qi, j + 2, s1_ref)
            return consume(j + 1, s0_ref, carry, False)

        carry = lax.fori_loop(0, lax.shift_right_logical(qi - 1, 1), pair, carry)
        even = (qi & 1) == 0

        @pl.when(even)
        def _():
            produce(qi, qi, s0_ref)
            c = consume(qi - 1, s1_ref, carry, False)
            produce(nxt, 0, sh_ref)
            finish(consume(qi, s0_ref, c, True))

        @pl.when(jnp.logical_not(even))
        def _():
            produce(nxt, 0, sh_ref)
            finish(consume(qi, s1_ref, carry, True))


def _router_kernel(x_ref, yg_ref, ym_ref, wo_ref, fnw_ref, wr_ref, br_ref,
                   h_ref, hn_ref, route_ref, route_t_ref, cnt_ref, carry_ref, lg_ref):
    t = x_ref.shape[0]
    half = wo_ref.shape[0] // 2
    step = pl.program_id(0)

    @pl.when(step == 0)
    def _():
        carry_ref[...] = jnp.zeros_like(carry_ref)
        lg_ref[...] = jnp.zeros_like(lg_ref)

    logits_prev = lg_ref[...]
    h = x_ref[...] + _dot(yg_ref[...], wo_ref[:half]) + _dot(ym_ref[...], wo_ref[half:])
    h_ref[...] = h
    hn = _rms(h, fnw_ref[...])
    hp = hn.shape[1] // 2
    hn_ref[...] = _pack_bf16_pair(hn[:, :hp], hn[:, hp:])

    hn_hi = hn.astype(BF16)
    hn_lo = (hn - hn_hi.astype(F32)).astype(BF16)
    parts = _dot(hn_hi, wr_ref[...]) + _dot(hn_lo, wr_ref[...])
    lg_ref[...] = parts[:, :LANES] + parts[:, LANES:] + br_ref[...]

    live = (step > 0).astype(F32)
    tg = ROUTE_ROWS
    lane = lax.broadcasted_iota(jnp.int32, (tg, LANES), 1)
    lane_f = lane.astype(F32)
    rr = lax.broadcasted_iota(jnp.int32, (tg, tg), 0)
    cc = lax.broadcasted_iota(jnp.int32, (tg, tg), 1)
    earlier = (cc < rr).astype(BF16)

    def first_argmax(vals, vmax):
        idx = jnp.min(jnp.where(vals == vmax, lane_f, float(LANES)), axis=-1, keepdims=True)
        return idx.astype(jnp.int32)

    carry = carry_ref[...]
    for g in range(t // tg):
        rows = slice(g * tg, (g + 1) * tg)
        logits = logits_prev[rows]
        gl = jnp.where((lane >= N_EXPERTS) & (lane < N_EXPERTS + N_GROUPS), logits, NEG_INF)
        gmax = jnp.max(gl, axis=-1, keepdims=True)
        gsel = first_argmax(gl, gmax) - N_EXPERTS
        p_g = 1.0 / jnp.sum(jnp.exp(gl - gmax), axis=-1, keepdims=True)
        lo = gsel * EXPERTS_PER_GROUP
        el = jnp.where((lane >= lo) & (lane < lo + EXPERTS_PER_GROUP), logits, NEG_INF)
        m1 = jnp.max(el, axis=-1, keepdims=True)
        i1 = first_argmax(el, m1)
        el2 = jnp.where(lane == i1, NEG_INF, el)
        m2 = jnp.max(el2, axis=-1, keepdims=True)
        i2 = first_argmax(el2, m2)
        e2 = jnp.exp(m2 - m1)
        g1 = p_g / (1.0 + e2)
        g2 = p_g * e2 / (1.0 + e2)

        is1 = lane == i1
        is2 = lane == i2
        onehot = (is1 | is2).astype(BF16)
        before = _dot(earlier, onehot) + carry
        r1 = jnp.sum(jnp.where(is1, before, 0.0), axis=-1, keepdims=True)
        r2 = jnp.sum(jnp.where(is2, before, 0.0), axis=-1, keepdims=True)
        carry = carry + live * jnp.sum(onehot.astype(F32), axis=0, keepdims=True)

        route = jnp.where(lane == 0, i1.astype(F32), 0.0)
        route = jnp.where(lane == 1, i2.astype(F32), route)
        route = jnp.where(lane == 2, g1, route)
        route = jnp.where(lane == 3, g2, route)
        route = jnp.where(lane == 4, r1, route)
        route = jnp.where(lane == 5, r2, route)
        route_ref[rows] = route
        route_t_ref[:, rows] = route.T[:ROUTE_FIELDS]
    carry_ref[...] = carry
    cnt_ref[...] = carry


def _dest_kernel(pstart_ref, route_t_ref, dest_ref):
    eid = route_t_ref[0:TOP_K, :].astype(jnp.int32)
    rank = route_t_ref[4:4 + TOP_K, :].astype(jnp.int32)
    start = jnp.zeros_like(eid)
    for e in range(N_EXPERTS):
        start = jnp.where(eid == e, pstart_ref[e], start)
    dest_ref[...] = start + rank


def _sc_mesh():
    return plsc.VectorSubcoreMesh(core_axis_name="c", subcore_axis_name="s",
                                  num_cores=SC_CORES, num_subcores=SC_SUBCORES)


def _sc_worker_base(per_worker):
    return (lax.axis_index("s") * SC_CORES + lax.axis_index("c")) * per_worker


def _sc_scatter_rows(rows, idx_list, nrows):
    m, w = rows.shape
    nidx = len(idx_list)
    assert m % (SC_WORKERS * SC_CHUNK) == 0, m
    per_worker = m // SC_WORKERS
    nchunk = per_worker // SC_CHUNK

    @functools.partial(
        pl.kernel, mesh=_sc_mesh(), out_type=jax.ShapeDtypeStruct((nrows, w), rows.dtype),
        scratch_types=[pltpu.VMEM((SC_CHUNK,), jnp.int32)] * nidx
        + [pltpu.VMEM((SC_CHUNK, w), rows.dtype), pltpu.SemaphoreType.DMA],
        name="sc_scatter_rows")
    def scatter(rows_hbm, *refs):
        idx_hbm, out_hbm = refs[:nidx], refs[nidx]
        idx_v, rows_v, sem = refs[nidx + 1:2 * nidx + 1], refs[2 * nidx + 1], refs[2 * nidx + 2]
        base = _sc_worker_base(per_worker)

        @pl.loop(0, nchunk)
        def _(ci):
            off = pl.multiple_of(base + ci * SC_CHUNK, SC_CHUNK)
            pltpu.sync_copy(rows_hbm.at[pl.ds(off, SC_CHUNK)], rows_v)
            for kk in range(nidx):
                pltpu.sync_copy(idx_hbm[kk].at[pl.ds(off, SC_CHUNK)], idx_v[kk])
            copies = [pltpu.async_copy(rows_v, out_hbm.at[idx_v[kk]], sem) for kk in range(nidx)]
            for c in copies:
                c.wait()

    return scatter(rows, *idx_list)


def _expert_kernel(be_ref, nu_ref, bv_ref, x_ref, wg_ref, wu_ref, wd_ref, y_ref, wgu_s, wd_s):
    j = pl.program_id(0)
    de = wg_ref.shape[2]
    last = jnp.maximum(nu_ref[0] - 1, 0)
    cur = be_ref[jnp.minimum(j, last)]
    prev = be_ref[jnp.minimum(jnp.maximum(j - 1, 0), last)]

    @pl.when((j == 0) | (cur != prev))
    def _():
        wgu_s[:, :de] = wg_ref[0].astype(BF16)
        wgu_s[:, de:] = wu_ref[0].astype(BF16)
        wd_s[...] = wd_ref[0].astype(BF16)

    @pl.when(j < nu_ref[0])
    def _():
        row = lax.broadcasted_iota(jnp.int32, x_ref.shape, 0)
        xu = jnp.where(row < bv_ref[j], x_ref[...], jnp.uint32(0))
        a, b = _unpack_bf16_pair(xu)
        x = jnp.concatenate([a, b], axis=1).astype(BF16)
        h12 = _dot(x, wgu_s[...])
        h1, h2 = h12[:, :de], h12[:, de:]
        hdn = (h1 * jax.nn.sigmoid(h1) * h2).astype(BF16)
        y = _dot(hdn, wd_s[...])
        hp = y.shape[1] // 2
        y_ref[...] = _pack_bf16_pair(y[:, :hp], y[:, hp:])

    @pl.when(pl.program_id(0) >= nu_ref[0])
    def _():
        y_ref[...] = jnp.zeros_like(y_ref)


def _sc_gather_rows(table, idx):
    m = idx.shape[0]
    w = table.shape[1]
    assert m % (SC_WORKERS * SC_CHUNK) == 0, m
    per_worker = m // SC_WORKERS
    nchunk = per_worker // SC_CHUNK

    @functools.partial(
        pl.kernel, mesh=_sc_mesh(), out_type=jax.ShapeDtypeStruct((m, w), table.dtype),
        scratch_types=[pltpu.VMEM((SC_CHUNK,), jnp.int32), pltpu.VMEM((SC_CHUNK, w), table.dtype),
                       pltpu.SemaphoreType.DMA],
        name="sc_gather_rows")
    def gather(table_hbm, idx_hbm, out_hbm, idx_v, rows_v, sem):
        base = _sc_worker_base(per_worker)

        @pl.loop(0, nchunk)
        def _(ci):
            off = pl.multiple_of(base + ci * SC_CHUNK, SC_CHUNK)
            pltpu.sync_copy(idx_hbm.at[pl.ds(off, SC_CHUNK)], idx_v)
            pltpu.async_copy(table_hbm.at[idx_v], rows_v, sem).wait()
            pltpu.sync_copy(rows_v, out_hbm.at[pl.ds(off, SC_CHUNK)])

    return gather(table, idx)


def _combine_kernel(h_ref, route_ref, fw_ref, y0_ref, y1_ref, o_ref):
    route = route_ref[...]
    g1 = route[:, 2:3]
    g2 = route[:, 3:4]
    a0, b0 = _unpack_bf16_pair(y0_ref[...])
    a1, b1 = _unpack_bf16_pair(y1_ref[...])
    moe = jnp.concatenate([a0 * g1 + a1 * g2, b0 * g1 + b1 * g2], axis=1)
    o_ref[...] = _rms(h_ref[...] + moe, fw_ref[...])


def _params(*sem):
    return pltpu.CompilerParams(dimension_semantics=sem, vmem_limit_bytes=VMEM_LIMIT)


def _full(shape):
    return pl.BlockSpec(shape, lambda *_: (0,) * len(shape))


def _rows(tile, width):
    return pl.BlockSpec((tile, width), lambda i: (i, 0))


def _layer(x2, pos2, attn_norm_w, w_in, gla_gate_up, gla_gate_bias, gla_norm_w, mla_q_norm_w,
           mla_w_uq, mla_kv_norm_w, mla_w_ukv, w_out, ffn_norm_w, router_group_w, router_group_b,
           router_expert_w, router_expert_b, expert_w_gate, expert_w_up, expert_w_down,
           out_norm_w, batch, seq):
    n, d = x2.shape
    dq = GLA_HEADS * GLA_DK
    dv = GLA_HEADS * GLA_DV
    dmla = MLA_HEADS * HEAD_PAD
    dmv = MLA_HEADS * MLA_DV

    c_gq, c_gk, c_gv, c_lr, c_og, c_cq, c_ckv, c_kr = jnp.split(
        w_in, [dq, 2 * dq, 2 * dq + dv, 2 * dq + dv + GLA_GATE_RANK,
               2 * dq + 2 * dv + GLA_GATE_RANK,
               2 * dq + 2 * dv + GLA_GATE_RANK + MLA_Q_RANK,
               2 * dq + 2 * dv + GLA_GATE_RANK + MLA_Q_RANK + MLA_KV_RANK], axis=1)
    zeros = lambda r, c: jnp.zeros((r, c), w_in.dtype)
    misc = jnp.concatenate([zeros(d, MLA_NOPE), c_kr, c_lr,
                            zeros(d, LANES - ROPE_END - GLA_GATE_RANK)], axis=1)
    w1 = jnp.concatenate([c_gq, c_gk, c_gv, c_og, c_cq, c_ckv, misc], axis=1).astype(BF16)
    gate_up = jnp.concatenate([zeros(GATE_LO, dq), gla_gate_up,
                               zeros(LANES - GATE_LO - GLA_GATE_RANK, dq)], axis=0).astype(BF16)
    wuq = mla_w_uq.reshape(MLA_Q_RANK, MLA_HEADS, MLA_NOPE + MLA_ROPE)
    wuq = jnp.pad(wuq, ((0, 0), (0, 0), (0, HEAD_PAD - ROPE_END))).reshape(MLA_Q_RANK, dmla)
    wukv = mla_w_ukv.reshape(MLA_KV_RANK, MLA_HEADS, MLA_NOPE + MLA_DV)
    wuk = jnp.pad(wukv[:, :, :MLA_NOPE], ((0, 0), (0, 0), (0, HEAD_PAD - MLA_NOPE)))
    wuv = jnp.pad(wukv[:, :, MLA_NOPE:], ((0, 0), (0, 0), (0, HEAD_PAD - MLA_DV)))
    wukv = jnp.concatenate([wuk.reshape(MLA_KV_RANK, dmla), wuv.reshape(MLA_KV_RANK, dmla)],
                           axis=1)
    inv = ROPE_THETA ** (-jnp.arange(ROPE_HALF, dtype=F32) / ROPE_HALF)
    inv_pat = jnp.concatenate([jnp.zeros((MLA_NOPE,), F32), inv, inv,
                               jnp.zeros((LANES - ROPE_END,), F32)]).reshape(1, LANES)
    w_router = jnp.concatenate(
        [router_expert_w, router_group_w, zeros(d, LANES - N_EXPERTS - N_GROUPS)], axis=1)
    w_router_hi = w_router.astype(BF16)
    w_router = jnp.concatenate(
        [w_router_hi, (w_router - w_router_hi.astype(F32)).astype(BF16)], axis=1)
    b_router = jnp.concatenate(
        [router_expert_b, router_group_b, jnp.zeros((LANES - N_EXPERTS - N_GROUPS,), F32)]
    ).reshape(1, LANES)
    row1 = lambda v: v.reshape(1, -1)

    tp = PROJ_TILE
    outs = pl.pallas_call(
        functools.partial(_proj_kernel, q_scale=(MLA_NOPE + MLA_ROPE) ** -0.5 * LOG2_E),
        grid=(n // tp,),
        in_specs=[_rows(tp, d), _rows(tp, 1), _full((1, d)), _full(w1.shape), _full(gate_up.shape),
                  _full((1, dq)), _full((1, MLA_Q_RANK)), _full(wuq.shape),
                  _full((1, MLA_KV_RANK)), _full(wukv.shape), _full((1, LANES))],
        out_specs=[_rows(tp, dq), _rows(tp, dq), _rows(tp, dv), _rows(tp, dq), _rows(tp, dv),
                   _rows(tp, dmla), _rows(tp, dmla), _rows(tp, dmla)],
        out_shape=[jax.ShapeDtypeStruct((n, dq), BF16), jax.ShapeDtypeStruct((n, dq), BF16),
                   jax.ShapeDtypeStruct((n, dv), BF16), jax.ShapeDtypeStruct((n, dq), F32),
                   jax.ShapeDtypeStruct((n, dv), BF16), jax.ShapeDtypeStruct((n, dmla), BF16),
                   jax.ShapeDtypeStruct((n, dmla), BF16), jax.ShapeDtypeStruct((n, dmla), BF16)],
        compiler_params=_params("parallel"),
        name="in_proj",
    )(x2, pos2, row1(attn_norm_w), w1, gate_up, row1(gla_gate_bias), row1(mla_q_norm_w),
      wuq.astype(BF16), row1(mla_kv_norm_w), wukv.astype(BF16), inv_pat)
    gq, gk, gv, gla, gog, q, k, v = outs

    tg = GLA_TILE
    seq3 = lambda a: a.reshape(batch, seq, a.shape[-1])
    gspec = lambda w: pl.BlockSpec((1, tg, w), lambda b, i: (b, i, 0))
    y_gla = pl.pallas_call(
        _gla_kernel,
        grid=(batch, seq // tg),
        in_specs=[gspec(dq), gspec(dq), gspec(dv), gspec(dq), gspec(dv), _full((1, GLA_DV))],
        out_specs=gspec(dv),
        out_shape=jax.ShapeDtypeStruct((batch, seq, dv), BF16),
        scratch_shapes=[pltpu.VMEM((GLA_HEADS, GLA_DV, GLA_DK), F32)],
        compiler_params=_params("parallel", "arbitrary"),
        name="gla",
    )(seq3(gq), seq3(gk), seq3(gv), seq3(gla), seq3(gog), row1(gla_norm_w))

    ta = ATT_TILE
    hps = ATT_HEADS_PER_STEP
    y_mla = pl.pallas_call(
        functools.partial(_mla_kernel, tq=ta),
        grid=(batch, MLA_HEADS // hps, seq // ta),
        in_specs=[pl.BlockSpec((1, seq, hps * HEAD_PAD), lambda b, hp, i: (b, 0, hp))] * 3,
        out_specs=pl.BlockSpec((1, ta, hps * MLA_DV), lambda b, hp, i: (b, i, hp)),
        out_shape=jax.ShapeDtypeStruct((batch, seq, dmv), BF16),
        scratch_shapes=[pltpu.VMEM((hps, ta, ta), F32)] * 3,
        compiler_params=_params("parallel", "parallel", "arbitrary"),
        name="mla",
    )(seq3(q), seq3(k), seq3(v))

    tr = ROUTER_TILE
    ntr = n // tr
    cur = lambda t, w: pl.BlockSpec((t, w), lambda i: (jnp.minimum(i, ntr - 1), 0))
    prev = lambda t, w: pl.BlockSpec((t, w), lambda i: (jnp.maximum(i - 1, 0), 0))
    h, hn, route, route_t, counts = pl.pallas_call(
        _router_kernel,
        grid=(ntr + 1,),
        in_specs=[cur(tr, d), cur(tr, dv), cur(tr, dmv), _full(w_out.shape), _full((1, d)),
                  _full(w_router.shape), _full((1, LANES))],
        out_specs=[cur(tr, d), cur(tr, d // 2), prev(tr, LANES),
                   pl.BlockSpec((ROUTE_FIELDS, tr), lambda i: (0, jnp.maximum(i - 1, 0))),
                   _full((1, LANES))],
        out_shape=[jax.ShapeDtypeStruct((n, d), F32), jax.ShapeDtypeStruct((n, d // 2), jnp.uint32),
                   jax.ShapeDtypeStruct((n, LANES), F32),
                   jax.ShapeDtypeStruct((ROUTE_FIELDS, n), F32),
                   jax.ShapeDtypeStruct((1, LANES), F32)],
        scratch_shapes=[pltpu.VMEM((1, LANES), F32), pltpu.VMEM((tr, LANES), F32)],
        compiler_params=_params("arbitrary"),
        name="out_proj_router",
    )(x2, y_gla.reshape(n, dv), y_mla.reshape(n, dmv), w_out.astype(BF16), row1(ffn_norm_w),
      w_router, b_router)

    blk = EXPERT_BLOCK
    nblk = (n * TOP_K) // blk + N_EXPERTS
    cnt = counts[0, :N_EXPERTS].astype(jnp.int32)
    padded = (cnt + blk - 1) // blk * blk
    pend = jnp.cumsum(padded)
    pstart = pend - padded
    dest = pl.pallas_call(
        _dest_kernel,
        grid_spec=pltpu.PrefetchScalarGridSpec(
            num_scalar_prefetch=1, grid=(1,),
            in_specs=[pl.BlockSpec((ROUTE_FIELDS, n), lambda i, ps: (0, 0))],
            out_specs=pl.BlockSpec((TOP_K, n), lambda i, ps: (0, 0))),
        out_shape=jax.ShapeDtypeStruct((TOP_K, n), jnp.int32),
        compiler_params=_params("arbitrary"),
        name="dest_rows",
    )(pstart, route_t)
    blk_start = jnp.arange(nblk, dtype=jnp.int32) * blk
    blk_expert = jnp.minimum(
        jnp.sum((pend[None, :] <= blk_start[:, None]).astype(jnp.int32), axis=1), N_EXPERTS - 1)
    n_used = (pend[-1] // blk).astype(jnp.int32).reshape(1)
    blk_valid = jnp.clip(cnt[blk_expert] - (blk_start - pstart[blk_expert]), 0, blk)
    blk_valid = jnp.where(jnp.arange(nblk) < n_used[0], blk_valid, 0).astype(jnp.int32)
    dest_slots = [dest[kk] for kk in range(TOP_K)]

    xbuf = _sc_scatter_rows(hn, dest_slots, nblk * blk)

    def used(j, nu):
        return jnp.maximum(jnp.minimum(j, nu[0] - 1), 0)

    ybuf = pl.pallas_call(
        _expert_kernel,
        grid_spec=pltpu.PrefetchScalarGridSpec(
            num_scalar_prefetch=3,
            grid=(nblk,),
            in_specs=[
                pl.BlockSpec((blk, d // 2), lambda j, be, nu, bv: (used(j, nu), 0)),
                pl.BlockSpec((1, d, D_EXPERT), lambda j, be, nu, bv: (be[used(j, nu)], 0, 0)),
                pl.BlockSpec((1, d, D_EXPERT), lambda j, be, nu, bv: (be[used(j, nu)], 0, 0)),
                pl.BlockSpec((1, D_EXPERT, d), lambda j, be, nu, bv: (be[used(j, nu)], 0, 0)),
            ],
            out_specs=pl.BlockSpec((blk, d // 2), lambda j, be, nu, bv: (j, 0)),
            scratch_shapes=[pltpu.VMEM((d, 2 * D_EXPERT), BF16), pltpu.VMEM((D_EXPERT, d), BF16)],
        ),
        out_shape=jax.ShapeDtypeStruct((nblk * blk, d // 2), jnp.uint32),
        compiler_params=_params("arbitrary"),
        name="experts",
    )(blk_expert, n_used, blk_valid, xbuf, expert_w_gate, expert_w_up, expert_w_down)

    gathered = _sc_gather_rows(ybuf, dest.reshape(TOP_K * n))
    tc = COMBINE_TILE
    nt = n // tc
    return pl.pallas_call(
        _combine_kernel,
        grid=(nt,),
        in_specs=[_rows(tc, d), _rows(tc, LANES), _full((1, d)), _rows(tc, d // 2),
                  pl.BlockSpec((tc, d // 2), lambda i: (i + nt, 0))],
        out_specs=_rows(tc, d),
        out_shape=jax.ShapeDtypeStruct((n, d), F32),
        compiler_params=_params("parallel"),
        name="combine",
    )(h, route, row1(out_norm_w), gathered, gathered)


def kernel(x, positions, attn_norm_w, w_in, gla_gate_up, gla_gate_bias, gla_norm_w, mla_q_norm_w,
           mla_w_uq, mla_kv_norm_w, mla_w_ukv, w_out, ffn_norm_w, router_group_w, router_group_b,
           router_expert_w, router_expert_b, expert_w_gate, expert_w_up, expert_w_down,
           final_norm_w):
    batch, seq, d = x.shape
    depth = w_in.shape[0]
    assert depth == 1, "the final norm is fused into the last layer's combine step"
    n = batch * seq
    assert seq % GLA_TILE == 0 and seq % ATT_TILE == 0, (seq, GLA_TILE, ATT_TILE)
    assert n % ROUTER_TILE == 0 and n % COMBINE_TILE == 0 and n % PROJ_TILE == 0, n
    out = _layer(x.reshape(batch * seq, d), positions.reshape(batch * seq, 1),
                 attn_norm_w[0], w_in[0], gla_gate_up[0], gla_gate_bias[0], gla_norm_w[0],
                 mla_q_norm_w[0], mla_w_uq[0], mla_kv_norm_w[0], mla_w_ukv[0], w_out[0],
                 ffn_norm_w[0], router_group_w[0], router_group_b[0], router_expert_w[0],
                 router_expert_b[0], expert_w_gate[0], expert_w_up[0], expert_w_down[0],
                 final_norm_w, batch, seq)
    return out.reshape(batch, seq, d)
```

```python
import functools

import jax
import jax.numpy as jnp
from jax import lax
from jax.experimental import pallas as pl
from jax.experimental.pallas import tpu as pltpu
from jax.experimental.pallas import tpu_sc as plsc

EPS = 1e-6
GLA_HEADS = 4
GLA_DK = 64
GLA_DV = 128
GLA_GATE_RANK = 16
GLA_GATE_NORM = 16.0
GLA_CHUNK = 64
MLA_HEADS = 8
MLA_NOPE = 64
MLA_ROPE = 32
MLA_DV = 64
MLA_Q_RANK = 384
MLA_KV_RANK = 256
ROPE_THETA = 10000.0
N_GROUPS = 4
EXPERTS_PER_GROUP = 8
N_EXPERTS = N_GROUPS * EXPERTS_PER_GROUP
TOP_K = 2
D_EXPERT = 256

LANES = 128
HEAD_PAD = 128
ROPE_HALF = MLA_ROPE // 2
ROPE_LO = MLA_NOPE
ROPE_HI = MLA_NOPE + ROPE_HALF
ROPE_END = MLA_NOPE + MLA_ROPE
GATE_LO = ROPE_END

PROJ_TILE = 512
COMBINE_TILE = 1024
COMBINE_PARTS = 4
GLA_TILE = 1024
GLA_SUBTILE = 256
ATT_TILE = 512
ATT_HEADS_PER_STEP = 4
EXPERT_BLOCK = 512
ROUTER_TILE = 1024
ROUTE_ROWS = 512
ROUTE_FIELDS = 8
VMEM_LIMIT = 56 * 1024 * 1024
SC_CORES = 2
SC_SUBCORES = 16
SC_WORKERS = SC_CORES * SC_SUBCORES
SC_CHUNK = 128

F32 = jnp.float32
BF16 = jnp.bfloat16
NEG_INF = float("-inf")
LOG2_E = 1.4426950408889634


def _dot(a, b, precision=None):
    return jnp.dot(a, b, preferred_element_type=F32, precision=precision)


def _dot_nt(a, b):
    return lax.dot_general(a, b, (((1,), (1,)), ((), ())), preferred_element_type=F32)


def _dot_tn(a, b):
    return lax.dot_general(a, b, (((0,), (0,)), ((), ())), preferred_element_type=F32)


def _rms(x, w):
    return x * lax.rsqrt(jnp.mean(x * x, axis=-1, keepdims=True) + EPS) * w


def _pack_bf16_pair(a, b):
    ua = lax.bitcast_convert_type(a.astype(BF16).astype(F32), jnp.uint32)
    ub = lax.bitcast_convert_type(b.astype(BF16).astype(F32), jnp.uint32)
    return (ua >> 16) | (ub & jnp.uint32(0xFFFF0000))


def _unpack_bf16_pair(u):
    a = lax.bitcast_convert_type(u << 16, F32)
    b = lax.bitcast_convert_type(u & jnp.uint32(0xFFFF0000), F32)
    return a, b


def _proj_kernel(x_ref, pos_ref, nw_ref, w1_ref, gu_ref, gb_ref, qnw_ref, wuq_ref, kvnw_ref,
                 wukv_ref, inv_ref,
                 gq_ref, gk_ref, gv_ref, gla_ref, gog_ref, q_ref, k_ref, v_ref, *, q_scale):
    x = x_ref[...]
    xn = _rms(x, nw_ref[...]).astype(BF16)
    dq = GLA_HEADS * GLA_DK
    dv = GLA_HEADS * GLA_DV
    o = 0

    def proj(width):
        nonlocal o
        res = _dot(xn, w1_ref[:, o:o + width])
        o += width
        return res

    gq_ref[...] = (proj(dq) * (GLA_DK ** -0.5)).astype(BF16)
    gk_ref[...] = proj(dq).astype(BF16)
    gv_ref[...] = proj(dv).astype(BF16)
    gog_ref[...] = proj(dv).astype(BF16)
    cq = proj(MLA_Q_RANK)
    ckv = proj(MLA_KV_RANK)
    misc = proj(LANES)

    z = _dot(misc.astype(BF16), gu_ref[...]) + gb_ref[...]
    log_sig = jnp.minimum(z, 0.0) - jnp.log1p(jnp.exp(-jnp.abs(z)))
    gla_ref[...] = log_sig * (1.0 / GLA_GATE_NORM)

    lane = lax.broadcasted_iota(jnp.int32, (x.shape[0], LANES), 1)
    ang = pos_ref[...].astype(F32) * inv_ref[...]
    cosv = jnp.cos(ang)
    sinv = jnp.sin(ang)
    in_lo = (lane >= ROPE_LO) & (lane < ROPE_HI)
    in_hi = (lane >= ROPE_HI) & (lane < ROPE_END)
    c_rope = jnp.where(in_lo | in_hi, cosv, 0.0)
    s_up = jnp.where(in_hi, sinv, 0.0)
    s_dn = jnp.where(in_lo, -sinv, 0.0)
    c_q = jnp.where(lane < MLA_NOPE, 1.0, c_rope)

    def rope(t, c):
        return (t * c + pltpu.roll(t, ROPE_HALF, 1) * s_up
                + pltpu.roll(t, LANES - ROPE_HALF, 1) * s_dn)

    k_rope = rope(misc, c_rope)

    q = _dot(_rms(cq, qnw_ref[...]).astype(BF16), wuq_ref[...])
    kv = _dot(_rms(ckv, kvnw_ref[...]).astype(BF16), wukv_ref[...])
    ones_lane = jnp.where(lane == MLA_DV, 1.0, 0.0)
    for h in range(MLA_HEADS):
        sl = slice(h * HEAD_PAD, (h + 1) * HEAD_PAD)
        vsl = slice((MLA_HEADS + h) * HEAD_PAD, (MLA_HEADS + h + 1) * HEAD_PAD)
        q_ref[:, sl] = (rope(q[:, sl], c_q) * q_scale).astype(BF16)
        k_ref[:, sl] = (kv[:, sl] + k_rope).astype(BF16)
        v_ref[:, sl] = (kv[:, vsl] + ones_lane).astype(BF16)


def _gla_kernel(q_ref, k_ref, v_ref, la_ref, og_ref, nw_ref, o_ref, st_ref):
    t = GLA_SUBTILE
    nchunk = t // GLA_CHUNK

    @pl.when(pl.program_id(1) == 0)
    def _():
        st_ref[...] = jnp.zeros_like(st_ref)

    row = lax.broadcasted_iota(jnp.int32, (t, t), 0)
    col = lax.broadcasted_iota(jnp.int32, (t, t), 1)
    chunk_bits = GLA_CHUNK.bit_length() - 1
    tri = ((row >> chunk_bits) == (col >> chunk_bits)) & (col <= row)
    tri_b = tri.astype(BF16)
    nw = nw_ref[...]
    states = [st_ref[h] for h in range(GLA_HEADS)]

    for sub in range(q_ref.shape[1] // t):
        tr = slice(sub * t, (sub + 1) * t)
        la = la_ref[0, tr]
        la_hi = la.astype(BF16)
        la_lo = (la - la_hi.astype(F32)).astype(BF16)
        parts = _dot(tri_b, jnp.concatenate([la_hi, la_lo], axis=1))
        b = parts[:, :la.shape[1]] + parts[:, la.shape[1]:]
        b_last = jnp.concatenate(
            [jnp.broadcast_to(b[(c + 1) * GLA_CHUNK - 1:(c + 1) * GLA_CHUNK],
                              (GLA_CHUNK, b.shape[1])) for c in range(nchunk)], axis=0)
        q_e = (q_ref[0, tr].astype(F32) * jnp.exp(b)).astype(BF16)
        kf = k_ref[0, tr].astype(F32)
        k_e = (kf * jnp.exp(-b)).astype(BF16)
        k_d = (kf * jnp.exp(b_last - b)).astype(BF16)
        decay = jnp.exp(b_last)

        for h in range(GLA_HEADS):
            ks = slice(h * GLA_DK, (h + 1) * GLA_DK)
            vs = slice(h * GLA_DV, (h + 1) * GLA_DV)
            qh, keh, kdh = q_e[:, ks], k_e[:, ks], k_d[:, ks]
            vh = v_ref[0, tr, vs]
            att = jnp.where(tri, _dot_nt(qh, keh), 0.0)
            o = _dot(att.astype(BF16), vh)
            state = states[h]
            inter = []
            for c in range(nchunk):
                rs = slice(c * GLA_CHUNK, (c + 1) * GLA_CHUNK)
                inter.append(_dot_nt(qh[rs], state.astype(BF16)))
                upd = _dot_tn(vh[rs], kdh[rs])
                state = state * decay[c * GLA_CHUNK:c * GLA_CHUNK + 1, ks] + upd
            states[h] = state
            o = o + jnp.concatenate(inter, axis=0)
            o = _rms(o, nw)
            g = og_ref[0, tr, vs].astype(F32)
            o_ref[0, tr, vs] = (o * (g * jax.nn.sigmoid(g))).astype(o_ref.dtype)

    for h in range(GLA_HEADS):
        st_ref[h] = states[h]


def _mla_kernel(q_ref, k_ref, v_ref, o_ref, sh_ref, s0_ref, s1_ref, *, tq):
    heads = q_ref.shape[2] // HEAD_PAD
    nq = q_ref.shape[1] // tq
    qi = pl.program_id(2)
    lane = lax.broadcasted_iota(jnp.int32, (tq, LANES), 1)
    causal = (lax.broadcasted_iota(jnp.int32, (tq, tq), 1)
              <= lax.broadcasted_iota(jnp.int32, (tq, tq), 0))

    def rows(i):
        return pl.ds(pl.multiple_of(i * tq, tq), tq)

    def produce(qblk, j, dst):
        for hh in range(heads):
            hs = slice(hh * HEAD_PAD, (hh + 1) * HEAD_PAD)
            dst[hh] = _dot_nt(q_ref[0, rows(qblk), hs], k_ref[0, rows(j), hs])

    def produce_diagonal(qblk, dst):
        th = tq // 2
        q0 = pl.multiple_of(qblk * tq, tq)
        for hh in range(heads):
            hs = slice(hh * HEAD_PAD, (hh + 1) * HEAD_PAD)
            dst[hh, :th, :th] = _dot_nt(q_ref[0, pl.ds(q0, th), hs], k_ref[0, pl.ds(q0, th), hs])
            dst[hh, th:, :] = _dot_nt(q_ref[0, pl.ds(q0 + th, th), hs], k_ref[0, rows(qblk), hs])

    def update(s, m, acc, v):
        m_new = jnp.maximum(m, jnp.max(s, axis=-1, keepdims=True))
        p = jnp.exp2(s - m_new).astype(BF16)
        return m_new, jnp.exp2(m - m_new) * acc + _dot(p, v)

    def consume(j, src, carry, diagonal):
        new = []
        for hh in range(heads):
            m, acc = carry[hh]
            hs = slice(hh * HEAD_PAD, (hh + 1) * HEAD_PAD)
            if not diagonal:
                new.append(update(src[hh], m, acc, v_ref[0, rows(j), hs]))
                continue
            th = tq // 2
            k0 = pl.multiple_of(j * tq, tq)
            top = update(jnp.where(causal[:th, :th], src[hh, :th, :th], NEG_INF),
                         m[:th], acc[:th], v_ref[0, pl.ds(k0, th), hs])
            bot = update(jnp.where(causal[th:], src[hh, th:, :], NEG_INF),
                         m[th:], acc[th:], v_ref[0, pl.ds(k0, tq), hs])
            new.append(tuple(jnp.concatenate([a, b], axis=0) for a, b in zip(top, bot)))
        return tuple(new)

    def finish(carry):
        for hp in range(heads // 2):
            o0, o1 = (acc / acc[:, MLA_DV:MLA_DV + 1] for _, acc in carry[2 * hp:2 * hp + 2])
            o_ref[0, :, hp * LANES:(hp + 1) * LANES] = jnp.where(
                lane < MLA_DV, o0, pltpu.roll(o1, MLA_DV, 1)).astype(o_ref.dtype)

    init = tuple((jnp.full((tq, 1), NEG_INF, F32), jnp.zeros((tq, LANES), F32))
                 for _ in range(heads))
    nxt = jnp.minimum(qi + 1, nq - 1)

    @pl.when(qi == 0)
    def _():
        produce_diagonal(0, sh_ref)
        carry = consume(0, sh_ref, init, True)
        produce(nxt, 0, sh_ref)
        finish(carry)

    @pl.when(qi > 0)
    def _():
        produce(qi, 1, s1_ref)
        carry = consume(0, sh_ref, init, False)

        def pair(i, carry):
            j = 1 + 2 * i
            produce(qi, j + 1, s0_ref)
            carry = consume(j, s1_ref, carry, False)
            produce(qi, j + 2, s1_ref)
            return consume(j + 1, s0_ref, carry, False)

        carry = lax.fori_loop(0, lax.shift_right_logical(qi - 1, 1), pair, carry)
        even = (qi & 1) == 0

        @pl.when(even)
        def _():
            produce_diagonal(qi, s0_ref)
            c = consume(qi - 1, s1_ref, carry, False)
            produce(nxt, 0, sh_ref)
            finish(consume(qi, s0_ref, c, True))

        @pl.when(jnp.logical_not(even))
        def _():
            produce(nxt, 0, sh_ref)
            finish(consume(qi, s1_ref, carry, True))


def _router_kernel(x_ref, yg_ref, ym_ref, wo_ref, fnw_ref, wr_ref, br_ref,
                   h_ref, hn_ref, route_ref, route_t_ref, cnt_ref, carry_ref, lg_ref):
    t = x_ref.shape[0]
    half = wo_ref.shape[0] // 2
    step = pl.program_id(0)

    @pl.when(step == 0)
    def _():
        carry_ref[...] = jnp.zeros_like(carry_ref)
        lg_ref[...] = jnp.zeros_like(lg_ref)

    logits_prev = lg_ref[...]
    h = x_ref[...] + _dot(yg_ref[...], wo_ref[:half]) + _dot(ym_ref[...], wo_ref[half:])
    h_ref[...] = h
    hn = _rms(h, fnw_ref[...])
    hp = hn.shape[1] // 2
    hn_ref[...] = _pack_bf16_pair(hn[:, :hp], hn[:, hp:])

    hn_hi = hn.astype(BF16)
    hn_lo = (hn - hn_hi.astype(F32)).astype(BF16)
    parts = _dot(hn_hi, wr_ref[...]) + _dot(hn_lo, wr_ref[...])
    lg_ref[...] = parts[:, :LANES] + parts[:, LANES:] + br_ref[...]

    live = (step > 0).astype(F32)
    tg = ROUTE_ROWS
    lane = lax.broadcasted_iota(jnp.int32, (tg, LANES), 1)
    lane_f = lane.astype(F32)
    rr = lax.broadcasted_iota(jnp.int32, (tg, tg), 0)
    cc = lax.broadcasted_iota(jnp.int32, (tg, tg), 1)
    earlier = (cc < rr).astype(BF16)

    def first_argmax(vals, vmax):
        idx = jnp.min(jnp.where(vals == vmax, lane_f, float(LANES)), axis=-1, keepdims=True)
        return idx.astype(jnp.int32)

    carry = carry_ref[...]
    for g in range(t // tg):
        rows = slice(g * tg, (g + 1) * tg)
        logits = logits_prev[rows]
        gl = jnp.where((lane >= N_EXPERTS) & (lane < N_EXPERTS + N_GROUPS), logits, NEG_INF)
        gmax = jnp.max(gl, axis=-1, keepdims=True)
        gsel = first_argmax(gl, gmax) - N_EXPERTS
        p_g = 1.0 / jnp.sum(jnp.exp(gl - gmax), axis=-1, keepdims=True)
        lo = gsel * EXPERTS_PER_GROUP
        el = jnp.where((lane >= lo) & (lane < lo + EXPERTS_PER_GROUP), logits, NEG_INF)
        m1 = jnp.max(el, axis=-1, keepdims=True)
        i1 = first_argmax(el, m1)
        el2 = jnp.where(lane == i1, NEG_INF, el)
        m2 = jnp.max(el2, axis=-1, keepdims=True)
        i2 = first_argmax(el2, m2)
        e2 = jnp.exp(m2 - m1)
        g1 = p_g / (1.0 + e2)
        g2 = p_g * e2 / (1.0 + e2)

        is1 = lane == i1
        is2 = lane == i2
        onehot = (is1 | is2).astype(BF16)
        before = _dot(earlier, onehot) + carry
        r1 = jnp.sum(jnp.where(is1, before, 0.0), axis=-1, keepdims=True)
        r2 = jnp.sum(jnp.where(is2, before, 0.0), axis=-1, keepdims=True)
        carry = carry + live * jnp.sum(onehot.astype(F32), axis=0, keepdims=True)

        route = jnp.where(lane == 0, i1.astype(F32), 0.0)
        route = jnp.where(lane == 1, i2.astype(F32), route)
        route = jnp.where(lane == 2, g1, route)
        route = jnp.where(lane == 3, g2, route)
        route = jnp.where(lane == 4, r1, route)
        route = jnp.where(lane == 5, r2, route)
        route_ref[rows] = route
        route_t_ref[:, rows] = route.T[:ROUTE_FIELDS]
    carry_ref[...] = carry
    cnt_ref[...] = carry


def _dest_kernel(pstart_ref, route_t_ref, dest_ref):
    eid = route_t_ref[0:TOP_K, :].astype(jnp.int32)
    rank = route_t_ref[4:4 + TOP_K, :].astype(jnp.int32)
    start = jnp.zeros_like(eid)
    for e in range(N_EXPERTS):
        start = jnp.where(eid == e, pstart_ref[e], start)
    dest_ref[...] = start + rank


def _sc_mesh():
    return plsc.VectorSubcoreMesh(core_axis_name="c", subcore_axis_name="s",
                                  num_cores=SC_CORES, num_subcores=SC_SUBCORES)


def _sc_worker_base(per_worker):
    return (lax.axis_index("s") * SC_CORES + lax.axis_index("c")) * per_worker


def _sc_scatter_rows(rows, idx_list, nrows):
    m, w = rows.shape
    nidx = len(idx_list)
    assert m % (SC_WORKERS * SC_CHUNK) == 0, m
    per_worker = m // SC_WORKERS
    nchunk = per_worker // SC_CHUNK

    @functools.partial(
        pl.kernel, mesh=_sc_mesh(), out_type=jax.ShapeDtypeStruct((nrows, w), rows.dtype),
        scratch_types=[pltpu.VMEM((SC_CHUNK,), jnp.int32)] * nidx
        + [pltpu.VMEM((SC_CHUNK, w), rows.dtype), pltpu.SemaphoreType.DMA],
        name="sc_scatter_rows")
    def scatter(rows_hbm, *refs):
        idx_hbm, out_hbm = refs[:nidx], refs[nidx]
        idx_v, rows_v, sem = refs[nidx + 1:2 * nidx + 1], refs[2 * nidx + 1], refs[2 * nidx + 2]
        base = _sc_worker_base(per_worker)

        @pl.loop(0, nchunk)
        def _(ci):
            off = pl.multiple_of(base + ci * SC_CHUNK, SC_CHUNK)
            pltpu.sync_copy(rows_hbm.at[pl.ds(off, SC_CHUNK)], rows_v)
            for kk in range(nidx):
                pltpu.sync_copy(idx_hbm[kk].at[pl.ds(off, SC_CHUNK)], idx_v[kk])
            copies = [pltpu.async_copy(rows_v, out_hbm.at[idx_v[kk]], sem) for kk in range(nidx)]
            for c in copies:
                c.wait()

    return scatter(rows, *idx_list)


def _expert_kernel(be_ref, nu_ref, bv_ref, x_ref, wg_ref, wu_ref, wd_ref, y_ref, wgu_s, wd_s):
    j = pl.program_id(0)
    de = wg_ref.shape[2]
    last = jnp.maximum(nu_ref[0] - 1, 0)
    cur = be_ref[jnp.minimum(j, last)]
    prev = be_ref[jnp.minimum(jnp.maximum(j - 1, 0), last)]

    @pl.when((j == 0) | (cur != prev))
    def _():
        wgu_s[:, :de] = wg_ref[0].astype(BF16)
        wgu_s[:, de:] = wu_ref[0].astype(BF16)
        wd_s[...] = wd_ref[0].astype(BF16)

    @pl.when(j < nu_ref[0])
    def _():
        row = lax.broadcasted_iota(jnp.int32, x_ref.shape, 0)
        xu = jnp.where(row < bv_ref[j], x_ref[...], jnp.uint32(0))
        a, b = _unpack_bf16_pair(xu)
        x = jnp.concatenate([a, b], axis=1).astype(BF16)
        h12 = _dot(x, wgu_s[...])
        h1, h2 = h12[:, :de], h12[:, de:]
        hdn = (h1 * jax.nn.sigmoid(h1) * h2).astype(BF16)
        y = _dot(hdn, wd_s[...])
        hp = y.shape[1] // 2
        y_ref[...] = _pack_bf16_pair(y[:, :hp], y[:, hp:])

    @pl.when(pl.program_id(0) >= nu_ref[0])
    def _():
        y_ref[...] = jnp.zeros_like(y_ref)


def _sc_gather_rows(table, idx):
    m = idx.shape[0]
    w = table.shape[1]
    assert m % (SC_WORKERS * SC_CHUNK) == 0, m
    per_worker = m // SC_WORKERS
    nchunk = per_worker // SC_CHUNK

    @functools.partial(
        pl.kernel, mesh=_sc_mesh(), out_type=jax.ShapeDtypeStruct((m, w), table.dtype),
        scratch_types=[pltpu.VMEM((SC_CHUNK,), jnp.int32), pltpu.VMEM((SC_CHUNK, w), table.dtype),
                       pltpu.SemaphoreType.DMA],
        name="sc_gather_rows")
    def gather(table_hbm, idx_hbm, out_hbm, idx_v, rows_v, sem):
        base = _sc_worker_base(per_worker)

        @pl.loop(0, nchunk)
        def _(ci):
            off = pl.multiple_of(base + ci * SC_CHUNK, SC_CHUNK)
            pltpu.sync_copy(idx_hbm.at[pl.ds(off, SC_CHUNK)], idx_v)
            pltpu.async_copy(table_hbm.at[idx_v], rows_v, sem).wait()
            pltpu.sync_copy(rows_v, out_hbm.at[pl.ds(off, SC_CHUNK)])

    return gather(table, idx)


def _combine_kernel(h_ref, route_ref, fw_ref, y0_ref, y1_ref, o_ref):
    route = route_ref[...]
    g1 = route[:, 2:3]
    g2 = route[:, 3:4]
    a0, b0 = _unpack_bf16_pair(y0_ref[...])
    a1, b1 = _unpack_bf16_pair(y1_ref[...])
    moe = jnp.concatenate([a0 * g1 + a1 * g2, b0 * g1 + b1 * g2], axis=1)
    o_ref[...] = _rms(h_ref[...] + moe, fw_ref[...])


def _combine_into_kernel(h_ref, route_ref, fw_ref, y0_ref, y1_ref, carried_ref, o_ref):
    del carried_ref
    _combine_kernel(h_ref, route_ref, fw_ref, y0_ref, y1_ref, o_ref)


def _params(*sem):
    return pltpu.CompilerParams(dimension_semantics=sem, vmem_limit_bytes=VMEM_LIMIT)


def _full(shape):
    return pl.BlockSpec(shape, lambda *_: (0,) * len(shape))


def _rows(tile, width):
    return pl.BlockSpec((tile, width), lambda i: (i, 0))


def _layer(x2, pos2, attn_norm_w, w_in, gla_gate_up, gla_gate_bias, gla_norm_w, mla_q_norm_w,
           mla_w_uq, mla_kv_norm_w, mla_w_ukv, w_out, ffn_norm_w, router_group_w, router_group_b,
           router_expert_w, router_expert_b, expert_w_gate, expert_w_up, expert_w_down,
           out_norm_w, batch, seq):
    n, d = x2.shape
    dq = GLA_HEADS * GLA_DK
    dv = GLA_HEADS * GLA_DV
    dmla = MLA_HEADS * HEAD_PAD
    dmv = MLA_HEADS * MLA_DV

    c_gq, c_gk, c_gv, c_lr, c_og, c_cq, c_ckv, c_kr = jnp.split(
        w_in, [dq, 2 * dq, 2 * dq + dv, 2 * dq + dv + GLA_GATE_RANK,
               2 * dq + 2 * dv + GLA_GATE_RANK,
               2 * dq + 2 * dv + GLA_GATE_RANK + MLA_Q_RANK,
               2 * dq + 2 * dv + GLA_GATE_RANK + MLA_Q_RANK + MLA_KV_RANK], axis=1)
    zeros = lambda r, c: jnp.zeros((r, c), w_in.dtype)
    misc = jnp.concatenate([zeros(d, MLA_NOPE), c_kr, c_lr,
                            zeros(d, LANES - ROPE_END - GLA_GATE_RANK)], axis=1)
    w1 = jnp.concatenate([c_gq, c_gk, c_gv, c_og, c_cq, c_ckv, misc], axis=1).astype(BF16)
    gate_up = jnp.concatenate([zeros(GATE_LO, dq), gla_gate_up,
                               zeros(LANES - GATE_LO - GLA_GATE_RANK, dq)], axis=0).astype(BF16)
    wuq = mla_w_uq.reshape(MLA_Q_RANK, MLA_HEADS, MLA_NOPE + MLA_ROPE)
    wuq = jnp.pad(wuq, ((0, 0), (0, 0), (0, HEAD_PAD - ROPE_END))).reshape(MLA_Q_RANK, dmla)
    wukv = mla_w_ukv.reshape(MLA_KV_RANK, MLA_HEADS, MLA_NOPE + MLA_DV)
    wuk = jnp.pad(wukv[:, :, :MLA_NOPE], ((0, 0), (0, 0), (0, HEAD_PAD - MLA_NOPE)))
    wuv = jnp.pad(wukv[:, :, MLA_NOPE:], ((0, 0), (0, 0), (0, HEAD_PAD - MLA_DV)))
    wukv = jnp.concatenate([wuk.reshape(MLA_KV_RANK, dmla), wuv.reshape(MLA_KV_RANK, dmla)],
                           axis=1)
    inv = ROPE_THETA ** (-jnp.arange(ROPE_HALF, dtype=F32) / ROPE_HALF)
    inv_pat = jnp.concatenate([jnp.zeros((MLA_NOPE,), F32), inv, inv,
                               jnp.zeros((LANES - ROPE_END,), F32)]).reshape(1, LANES)
    w_router = jnp.concatenate(
        [router_expert_w, router_group_w, zeros(d, LANES - N_EXPERTS - N_GROUPS)], axis=1)
    w_router_hi = w_router.astype(BF16)
    w_router = jnp.concatenate(
        [w_router_hi, (w_router - w_router_hi.astype(F32)).astype(BF16)], axis=1)
    b_router = jnp.concatenate(
        [router_expert_b, router_group_b, jnp.zeros((LANES - N_EXPERTS - N_GROUPS,), F32)]
    ).reshape(1, LANES)
    row1 = lambda v: v.reshape(1, -1)

    tp = PROJ_TILE
    outs = pl.pallas_call(
        functools.partial(_proj_kernel, q_scale=(MLA_NOPE + MLA_ROPE) ** -0.5 * LOG2_E),
        grid=(n // tp,),
        in_specs=[_rows(tp, d), _rows(tp, 1), _full((1, d)), _full(w1.shape), _full(gate_up.shape),
                  _full((1, dq)), _full((1, MLA_Q_RANK)), _full(wuq.shape),
                  _full((1, MLA_KV_RANK)), _full(wukv.shape), _full((1, LANES))],
        out_specs=[_rows(tp, dq), _rows(tp, dq), _rows(tp, dv), _rows(tp, dq), _rows(tp, dv),
                   _rows(tp, dmla), _rows(tp, dmla), _rows(tp, dmla)],
        out_shape=[jax.ShapeDtypeStruct((n, dq), BF16), jax.ShapeDtypeStruct((n, dq), BF16),
                   jax.ShapeDtypeStruct((n, dv), BF16), jax.ShapeDtypeStruct((n, dq), F32),
                   jax.ShapeDtypeStruct((n, dv), BF16), jax.ShapeDtypeStruct((n, dmla), BF16),
                   jax.ShapeDtypeStruct((n, dmla), BF16), jax.ShapeDtypeStruct((n, dmla), BF16)],
        compiler_params=_params("parallel"),
        name="in_proj",
    )(x2, pos2, row1(attn_norm_w), w1, gate_up, row1(gla_gate_bias), row1(mla_q_norm_w),
      wuq.astype(BF16), row1(mla_kv_norm_w), wukv.astype(BF16), inv_pat)
    gq, gk, gv, gla, gog, q, k, v = outs

    tg = GLA_TILE
    seq3 = lambda a: a.reshape(batch, seq, a.shape[-1])
    gspec = lambda w: pl.BlockSpec((1, tg, w), lambda b, i: (b, i, 0))
    y_gla = pl.pallas_call(
        _gla_kernel,
        grid=(batch, seq // tg),
        in_specs=[gspec(dq), gspec(dq), gspec(dv), gspec(dq), gspec(dv), _full((1, GLA_DV))],
        out_specs=gspec(dv),
        out_shape=jax.ShapeDtypeStruct((batch, seq, dv), BF16),
        scratch_shapes=[pltpu.VMEM((GLA_HEADS, GLA_DV, GLA_DK), F32)],
        compiler_params=_params("parallel", "arbitrary"),
        name="gla",
    )(seq3(gq), seq3(gk), seq3(gv), seq3(gla), seq3(gog), row1(gla_norm_w))

    ta = ATT_TILE
    hps = ATT_HEADS_PER_STEP
    y_mla = pl.pallas_call(
        functools.partial(_mla_kernel, tq=ta),
        grid=(batch, MLA_HEADS // hps, seq // ta),
        in_specs=[pl.BlockSpec((1, seq, hps * HEAD_PAD), lambda b, hp, i: (b, 0, hp))] * 3,
        out_specs=pl.BlockSpec((1, ta, hps * MLA_DV), lambda b, hp, i: (b, i, hp)),
        out_shape=jax.ShapeDtypeStruct((batch, seq, dmv), BF16),
        scratch_shapes=[pltpu.VMEM((hps, ta, ta), F32)] * 3,
        compiler_params=_params("parallel", "parallel", "arbitrary"),
        name="mla",
    )(seq3(q), seq3(k), seq3(v))

    tr = ROUTER_TILE
    ntr = n // tr
    cur = lambda t, w: pl.BlockSpec((t, w), lambda i: (jnp.minimum(i, ntr - 1), 0))
    prev = lambda t, w: pl.BlockSpec((t, w), lambda i: (jnp.maximum(i - 1, 0), 0))
    h, hn, route, route_t, counts = pl.pallas_call(
        _router_kernel,
        grid=(ntr + 1,),
        in_specs=[cur(tr, d), cur(tr, dv), cur(tr, dmv), _full(w_out.shape), _full((1, d)),
                  _full(w_router.shape), _full((1, LANES))],
        out_specs=[cur(tr, d), cur(tr, d // 2), prev(tr, LANES),
                   pl.BlockSpec((ROUTE_FIELDS, tr), lambda i: (0, jnp.maximum(i - 1, 0))),
                   _full((1, LANES))],
        out_shape=[jax.ShapeDtypeStruct((n, d), F32), jax.ShapeDtypeStruct((n, d // 2), jnp.uint32),
                   jax.ShapeDtypeStruct((n, LANES), F32),
                   jax.ShapeDtypeStruct((ROUTE_FIELDS, n), F32),
                   jax.ShapeDtypeStruct((1, LANES), F32)],
        scratch_shapes=[pltpu.VMEM((1, LANES), F32), pltpu.VMEM((tr, LANES), F32)],
        compiler_params=_params("arbitrary"),
        name="out_proj_router",
    )(x2, y_gla.reshape(n, dv), y_mla.reshape(n, dmv), w_out.astype(BF16), row1(ffn_norm_w),
      w_router, b_router)

    blk = EXPERT_BLOCK
    nblk = (n * TOP_K) // blk + N_EXPERTS
    cnt = counts[0, :N_EXPERTS].astype(jnp.int32)
    padded = (cnt + blk - 1) // blk * blk
    pend = jnp.cumsum(padded)
    pstart = pend - padded
    dest = pl.pallas_call(
        _dest_kernel,
        grid_spec=pltpu.PrefetchScalarGridSpec(
            num_scalar_prefetch=1, grid=(1,),
            in_specs=[pl.BlockSpec((ROUTE_FIELDS, n), lambda i, ps: (0, 0))],
            out_specs=pl.BlockSpec((TOP_K, n), lambda i, ps: (0, 0))),
        out_shape=jax.ShapeDtypeStruct((TOP_K, n), jnp.int32),
        compiler_params=_params("arbitrary"),
        name="dest_rows",
    )(pstart, route_t)
    blk_start = jnp.arange(nblk, dtype=jnp.int32) * blk
    blk_expert = jnp.minimum(
        jnp.sum((pend[None, :] <= blk_start[:, None]).astype(jnp.int32), axis=1), N_EXPERTS - 1)
    n_used = (pend[-1] // blk).astype(jnp.int32).reshape(1)
    blk_valid = jnp.clip(cnt[blk_expert] - (blk_start - pstart[blk_expert]), 0, blk)
    blk_valid = jnp.where(jnp.arange(nblk) < n_used[0], blk_valid, 0).astype(jnp.int32)
    dest_slots = [dest[kk] for kk in range(TOP_K)]

    xbuf = _sc_scatter_rows(hn, dest_slots, nblk * blk)

    def used(j, nu):
        return jnp.maximum(jnp.minimum(j, nu[0] - 1), 0)

    ybuf = pl.pallas_call(
        _expert_kernel,
        grid_spec=pltpu.PrefetchScalarGridSpec(
            num_scalar_prefetch=3,
            grid=(nblk,),
            in_specs=[
                pl.BlockSpec((blk, d // 2), lambda j, be, nu, bv: (used(j, nu), 0)),
                pl.BlockSpec((1, d, D_EXPERT), lambda j, be, nu, bv: (be[used(j, nu)], 0, 0)),
                pl.BlockSpec((1, d, D_EXPERT), lambda j, be, nu, bv: (be[used(j, nu)], 0, 0)),
                pl.BlockSpec((1, D_EXPERT, d), lambda j, be, nu, bv: (be[used(j, nu)], 0, 0)),
            ],
            out_specs=pl.BlockSpec((blk, d // 2), lambda j, be, nu, bv: (j, 0)),
            scratch_shapes=[pltpu.VMEM((d, 2 * D_EXPERT), BF16), pltpu.VMEM((D_EXPERT, d), BF16)],
        ),
        out_shape=jax.ShapeDtypeStruct((nblk * blk, d // 2), jnp.uint32),
        compiler_params=_params("arbitrary"),
        name="experts",
    )(blk_expert, n_used, blk_valid, xbuf, expert_w_gate, expert_w_up, expert_w_down)

    tc = COMBINE_TILE
    parts = COMBINE_PARTS
    npart = n // parts
    steps = npart // tc
    out = None
    for p in range(parts):
        part_dest = dest[:, p * npart:(p + 1) * npart].reshape(TOP_K * npart)
        gathered = _sc_gather_rows(ybuf, part_dest)
        here = lambda w, p=p: pl.BlockSpec((tc, w), lambda i: (i + p * steps, 0))
        in_specs = [here(d), here(LANES), _full((1, d)), _rows(tc, d // 2),
                    pl.BlockSpec((tc, d // 2), lambda i: (i + steps, 0))]
        args = [h, route, row1(out_norm_w), gathered, gathered]
        if out is not None:
            in_specs.append(pl.BlockSpec(memory_space=pl.ANY))
            args.append(out)
        out = pl.pallas_call(
            _combine_kernel if out is None else _combine_into_kernel,
            grid=(steps,),
            in_specs=in_specs,
            out_specs=here(d),
            out_shape=jax.ShapeDtypeStruct((n, d), F32),
            input_output_aliases={} if out is None else {len(args) - 1: 0},
            compiler_params=_params("parallel"),
            name=f"combine_{p}",
        )(*args)
    return out


def kernel(x, positions, attn_norm_w, w_in, gla_gate_up, gla_gate_bias, gla_norm_w, mla_q_norm_w,
           mla_w_uq, mla_kv_norm_w, mla_w_ukv, w_out, ffn_norm_w, router_group_w, router_group_b,
           router_expert_w, router_expert_b, expert_w_gate, expert_w_up, expert_w_down,
           final_norm_w):
    batch, seq, d = x.shape
    depth = w_in.shape[0]
    assert depth == 1, "the final norm is fused into the last layer's combine step"
    n = batch * seq
    assert seq % GLA_TILE == 0 and seq % ATT_TILE == 0, (seq, GLA_TILE, ATT_TILE)
    assert n % ROUTER_TILE == 0 and n % PROJ_TILE == 0, n
    assert n % (COMBINE_TILE * COMBINE_PARTS) == 0, n
    out = _layer(x.reshape(batch * seq, d), positions.reshape(batch * seq, 1),
                 attn_norm_w[0], w_in[0], gla_gate_up[0], gla_gate_bias[0], gla_norm_w[0],
                 mla_q_norm_w[0], mla_w_uq[0], mla_kv_norm_w[0], mla_w_ukv[0], w_out[0],
                 ffn_norm_w[0], router_group_w[0], router_group_b[0], router_expert_w[0],
                 router_expert_b[0], expert_w_gate[0], expert_w_up[0], expert_w_down[0],
                 final_norm_w, batch, seq)
    return out.reshape(batch, seq, d)
```

```python
import functools

import jax
import jax.numpy as jnp
from jax import lax
from jax.experimental import pallas as pl
from jax.experimental.pallas import tpu as pltpu
from jax.experimental.pallas import tpu_sc as plsc

EPS = 1e-6
GLA_HEADS = 4
GLA_DK = 64
GLA_DV = 128
GLA_GATE_RANK = 16
GLA_GATE_NORM = 16.0
GLA_CHUNK = 64
MLA_HEADS = 8
MLA_NOPE = 64
MLA_ROPE = 32
MLA_DV = 64
MLA_Q_RANK = 384
MLA_KV_RANK = 256
ROPE_THETA = 10000.0
N_GROUPS = 4
EXPERTS_PER_GROUP = 8
N_EXPERTS = N_GROUPS * EXPERTS_PER_GROUP
TOP_K = 2
D_EXPERT = 256

LANES = 128
HEAD_PAD = 128
ROPE_HALF = MLA_ROPE // 2
ROPE_LO = MLA_NOPE
ROPE_HI = MLA_NOPE + ROPE_HALF
ROPE_END = MLA_NOPE + MLA_ROPE
GATE_LO = ROPE_END

PROJ_TILE = 512
COMBINE_TILE = 1024
COMBINE_PARTS = 4
GLA_TILE = 1024
GLA_SUBTILE = 256
ATT_TILE = 512
ATT_HEADS_PER_STEP = 4
EXPERT_BLOCK = 512
ROUTER_TILE = 1024
ROUTE_ROWS = 512
ROUTE_FIELDS = 8
VMEM_LIMIT = 56 * 1024 * 1024
SC_CORES = 2
SC_SUBCORES = 16
SC_WORKERS = SC_CORES * SC_SUBCORES
SC_CHUNK = 128

F32 = jnp.float32
BF16 = jnp.bfloat16
NEG_INF = float("-inf")
LOG2_E = 1.4426950408889634


def _dot(a, b, precision=None):
    return jnp.dot(a, b, preferred_element_type=F32, precision=precision)


def _dot_nt(a, b):
    return lax.dot_general(a, b, (((1,), (1,)), ((), ())), preferred_element_type=F32)


def _dot_tn(a, b):
    return lax.dot_general(a, b, (((0,), (0,)), ((), ())), preferred_element_type=F32)


def _rms(x, w):
    return x * lax.rsqrt(jnp.mean(x * x, axis=-1, keepdims=True) + EPS) * w


def _pack_bf16_pair(a, b):
    ua = lax.bitcast_convert_type(a.astype(BF16).astype(F32), jnp.uint32)
    ub = lax.bitcast_convert_type(b.astype(BF16).astype(F32), jnp.uint32)
    return (ua >> 16) | (ub & jnp.uint32(0xFFFF0000))


def _unpack_bf16_pair(u):
    a = lax.bitcast_convert_type(u << 16, F32)
    b = lax.bitcast_convert_type(u & jnp.uint32(0xFFFF0000), F32)
    return a, b


def _proj_kernel(x_ref, pos_ref, nw_ref, w1_ref, gu_ref, gb_ref, qnw_ref, wuq_ref, kvnw_ref,
                 wukv_ref, inv_ref,
                 gq_ref, gk_ref, gv_ref, gla_ref, gog_ref, q_ref, k_ref, v_ref, *, q_scale):
    x = x_ref[...]
    xn = _rms(x, nw_ref[...]).astype(BF16)
    dq = GLA_HEADS * GLA_DK
    dv = GLA_HEADS * GLA_DV
    o = 0

    def proj(width):
        nonlocal o
        res = _dot(xn, w1_ref[:, o:o + width])
        o += width
        return res

    gq_ref[...] = (proj(dq) * (GLA_DK ** -0.5)).astype(BF16)
    gk_ref[...] = proj(dq).astype(BF16)
    gv_ref[...] = proj(dv).astype(BF16)
    gog_ref[...] = proj(dv).astype(BF16)
    cq = proj(MLA_Q_RANK)
    ckv = proj(MLA_KV_RANK)
    misc = proj(LANES)

    z = _dot(misc.astype(BF16), gu_ref[...]) + gb_ref[...]
    log_sig = jnp.minimum(z, 0.0) - jnp.log1p(jnp.exp(-jnp.abs(z)))
    gla_ref[...] = log_sig * (1.0 / GLA_GATE_NORM)

    lane = lax.broadcasted_iota(jnp.int32, (x.shape[0], LANES), 1)
    ang = pos_ref[...].astype(F32) * inv_ref[...]
    cosv = jnp.cos(ang)
    sinv = jnp.sin(ang)
    in_lo = (lane >= ROPE_LO) & (lane < ROPE_HI)
    in_hi = (lane >= ROPE_HI) & (lane < ROPE_END)
    c_rope = jnp.where(in_lo | in_hi, cosv, 0.0)
    s_up = jnp.where(in_hi, sinv, 0.0)
    s_dn = jnp.where(in_lo, -sinv, 0.0)
    c_q = jnp.where(lane < MLA_NOPE, 1.0, c_rope)

    def rope(t, c):
        return (t * c + pltpu.roll(t, ROPE_HALF, 1) * s_up
                + pltpu.roll(t, LANES - ROPE_HALF, 1) * s_dn)

    k_rope = rope(misc, c_rope)

    q = _dot(_rms(cq, qnw_ref[...]).astype(BF16), wuq_ref[...])
    kv = _dot(_rms(ckv, kvnw_ref[...]).astype(BF16), wukv_ref[...])
    ones_lane = jnp.where(lane == MLA_DV, 1.0, 0.0)
    for h in range(MLA_HEADS):
        sl = slice(h * HEAD_PAD, (h + 1) * HEAD_PAD)
        vsl = slice((MLA_HEADS + h) * HEAD_PAD, (MLA_HEADS + h + 1) * HEAD_PAD)
        q_ref[:, sl] = (rope(q[:, sl], c_q) * q_scale).astype(BF16)
        k_ref[:, sl] = (kv[:, sl] + k_rope).astype(BF16)
        v_ref[:, sl] = (kv[:, vsl] + ones_lane).astype(BF16)


def _gla_kernel(q_ref, k_ref, v_ref, la_ref, og_ref, nw_ref, o_ref, st_ref):
    t = GLA_SUBTILE
    nchunk = t // GLA_CHUNK

    @pl.when(pl.program_id(1) == 0)
    def _():
        st_ref[...] = jnp.zeros_like(st_ref)

    row = lax.broadcasted_iota(jnp.int32, (t, t), 0)
    col = lax.broadcasted_iota(jnp.int32, (t, t), 1)
    chunk_bits = GLA_CHUNK.bit_length() - 1
    tri = ((row >> chunk_bits) == (col >> chunk_bits)) & (col <= row)
    tri_b = tri.astype(BF16)
    nw = nw_ref[...]
    states = [st_ref[h] for h in range(GLA_HEADS)]

    for sub in range(q_ref.shape[1] // t):
        tr = slice(sub * t, (sub + 1) * t)
        la = la_ref[0, tr]
        la_hi = la.astype(BF16)
        la_lo = (la - la_hi.astype(F32)).astype(BF16)
        parts = _dot(tri_b, jnp.concatenate([la_hi, la_lo], axis=1))
        b = parts[:, :la.shape[1]] + parts[:, la.shape[1]:]
        b_last = jnp.concatenate(
            [jnp.broadcast_to(b[(c + 1) * GLA_CHUNK - 1:(c + 1) * GLA_CHUNK],
                              (GLA_CHUNK, b.shape[1])) for c in range(nchunk)], axis=0)
        q_e = (q_ref[0, tr].astype(F32) * jnp.exp(b)).astype(BF16)
        kf = k_ref[0, tr].astype(F32)
        k_e = (kf * jnp.exp(-b)).astype(BF16)
        k_d = (kf * jnp.exp(b_last - b)).astype(BF16)
        decay = jnp.exp(b_last)

        for h in range(GLA_HEADS):
            ks = slice(h * GLA_DK, (h + 1) * GLA_DK)
            vs = slice(h * GLA_DV, (h + 1) * GLA_DV)
            qh, keh, kdh = q_e[:, ks], k_e[:, ks], k_d[:, ks]
            vh = v_ref[0, tr, vs]
            att = jnp.where(tri, _dot_nt(qh, keh), 0.0)
            o = _dot(att.astype(BF16), vh)
            state = states[h]
            inter = []
            for c in range(nchunk):
                rs = slice(c * GLA_CHUNK, (c + 1) * GLA_CHUNK)
                inter.append(_dot_nt(qh[rs], state.astype(BF16)))
                upd = _dot_tn(vh[rs], kdh[rs])
                state = state * decay[c * GLA_CHUNK:c * GLA_CHUNK + 1, ks] + upd
            states[h] = state
            o = o + jnp.concatenate(inter, axis=0)
            o = _rms(o, nw)
            g = og_ref[0, tr, vs].astype(F32)
            o_ref[0, tr, vs] = (o * (g * jax.nn.sigmoid(g))).astype(o_ref.dtype)

    for h in range(GLA_HEADS):
        st_ref[h] = states[h]


def _mla_kernel(q_ref, k_ref, v_ref, o_ref, sh_ref, s0_ref, s1_ref, *, tq):
    heads = q_ref.shape[2] // HEAD_PAD
    nq = q_ref.shape[1] // tq
    qi = pl.program_id(2)
    lane = lax.broadcasted_iota(jnp.int32, (tq, LANES), 1)
    causal = (lax.broadcasted_iota(jnp.int32, (tq, tq), 1)
              <= lax.broadcasted_iota(jnp.int32, (tq, tq), 0))

    def rows(i):
        return pl.ds(pl.multiple_of(i * tq, tq), tq)

    def produce(qblk, j, dst):
        for hh in range(heads):
            hs = slice(hh * HEAD_PAD, (hh + 1) * HEAD_PAD)
            dst[hh] = _dot_nt(q_ref[0, rows(qblk), hs], k_ref[0, rows(j), hs])

    def produce_diagonal(qblk, dst):
        th = tq // 2
        q0 = pl.multiple_of(qblk * tq, tq)
        for hh in range(heads):
            hs = slice(hh * HEAD_PAD, (hh + 1) * HEAD_PAD)
            dst[hh, :th, :th] = _dot_nt(q_ref[0, pl.ds(q0, th), hs], k_ref[0, pl.ds(q0, th), hs])
            dst[hh, th:, :] = _dot_nt(q_ref[0, pl.ds(q0 + th, th), hs], k_ref[0, rows(qblk), hs])

    def update(s, m, acc, v):
        m_new = jnp.maximum(m, jnp.max(s, axis=-1, keepdims=True))
        p = jnp.exp2(s - m_new).astype(BF16)
        return m_new, jnp.exp2(m - m_new) * acc + _dot(p, v)

    def consume(j, src, carry, diagonal):
        new = []
        for hh in range(heads):
            m, acc = carry[hh]
            hs = slice(hh * HEAD_PAD, (hh + 1) * HEAD_PAD)
            if not diagonal:
                new.append(update(src[hh], m, acc, v_ref[0, rows(j), hs]))
                continue
            th = tq // 2
            k0 = pl.multiple_of(j * tq, tq)
            top = update(jnp.where(causal[:th, :th], src[hh, :th, :th], NEG_INF),
                         m[:th], acc[:th], v_ref[0, pl.ds(k0, th), hs])
            bot = update(jnp.where(causal[th:], src[hh, th:, :], NEG_INF),
                         m[th:], acc[th:], v_ref[0, pl.ds(k0, tq), hs])
            new.append(tuple(jnp.concatenate([a, b], axis=0) for a, b in zip(top, bot)))
        return tuple(new)

    def finish(carry):
        for hp in range(heads // 2):
            o0, o1 = (acc / acc[:, MLA_DV:MLA_DV + 1] for _, acc in carry[2 * hp:2 * hp + 2])
            o_ref[0, :, hp * LANES:(hp + 1) * LANES] = jnp.where(
                lane < MLA_DV, o0, pltpu.roll(o1, MLA_DV, 1)).astype(o_ref.dtype)

    init = tuple((jnp.full((tq, 1), NEG_INF, F32), jnp.zeros((tq, LANES), F32))
                 for _ in range(heads))
    nxt = jnp.minimum(qi + 1, nq - 1)

    @pl.when(qi == 0)
    def _():
        produce_diagonal(0, sh_ref)
        carry = consume(0, sh_ref, init, True)
        produce(nxt, 0, sh_ref)
        finish(carry)

    @pl.when(qi > 0)
    def _():
        produce(qi, 1, s1_ref)
        carry = consume(0, sh_ref, init, False)

        def pair(i, carry):
            j = 1 + 2 * i
            produce(qi, j + 1, s0_ref)
            carry = consume(j, s1_ref, carry, False)
            produce(qi, j + 2, s1_ref)
            return consume(j + 1, s0_ref, carry, False)

        carry = lax.fori_loop(0, lax.shift_right_logical(qi - 1, 1), pair, carry)
        even = (qi & 1) == 0

        @pl.when(even)
        def _():
            produce_diagonal(qi, s0_ref)
            c = consume(qi - 1, s1_ref, carry, False)
            produce(nxt, 0, sh_ref)
            finish(consume(qi, s0_ref, c, True))

        @pl.when(jnp.logical_not(even))
        def _():
            produce(nxt, 0, sh_ref)
            finish(consume(qi, s1_ref, carry, True))


def _router_kernel(x_ref, yg_ref, ym_ref, wo_ref, fnw_ref, wr_ref, br_ref,
                   h_ref, hn_ref, route_ref, route_t_ref, cnt_ref, carry_ref, lg_ref):
    t = x_ref.shape[0]
    half = wo_ref.shape[0] // 2
    step = pl.program_id(0)

    @pl.when(step == 0)
    def _():
        carry_ref[...] = jnp.zeros_like(carry_ref)
        lg_ref[...] = jnp.zeros_like(lg_ref)

    logits_prev = lg_ref[...]
    h = x_ref[...] + _dot(yg_ref[...], wo_ref[:half]) + _dot(ym_ref[...], wo_ref[half:])
    h_ref[...] = h
    hn = _rms(h, fnw_ref[...])
    hp = hn.shape[1] // 2
    hn_ref[...] = _pack_bf16_pair(hn[:, :hp], hn[:, hp:])

    hn_hi = hn.astype(BF16)
    hn_lo = (hn - hn_hi.astype(F32)).astype(BF16)
    parts = _dot(hn_hi, wr_ref[...]) + _dot(hn_lo, wr_ref[...])
    lg_ref[...] = parts[:, :LANES] + parts[:, LANES:] + br_ref[...]

    live = (step > 0).astype(F32)
    tg = ROUTE_ROWS
    lane = lax.broadcasted_iota(jnp.int32, (tg, LANES), 1)
    lane_f = lane.astype(F32)
    rr = lax.broadcasted_iota(jnp.int32, (tg, tg), 0)
    cc = lax.broadcasted_iota(jnp.int32, (tg, tg), 1)
    earlier = (cc < rr).astype(BF16)

    def first_argmax(vals, vmax):
        idx = jnp.min(jnp.where(vals == vmax, lane_f, float(LANES)), axis=-1, keepdims=True)
        return idx.astype(jnp.int32)

    carry = carry_ref[...]
    for g in range(t // tg):
        rows = slice(g * tg, (g + 1) * tg)
        logits = logits_prev[rows]
        gl = jnp.where((lane >= N_EXPERTS) & (lane < N_EXPERTS + N_GROUPS), logits, NEG_INF)
        gmax = jnp.max(gl, axis=-1, keepdims=True)
        gsel = first_argmax(gl, gmax) - N_EXPERTS
        p_g = 1.0 / jnp.sum(jnp.exp(gl - gmax), axis=-1, keepdims=True)
        lo = gsel * EXPERTS_PER_GROUP
        el = jnp.where((lane >= lo) & (lane < lo + EXPERTS_PER_GROUP), logits, NEG_INF)
        m1 = jnp.max(el, axis=-1, keepdims=True)
        i1 = first_argmax(el, m1)
        el2 = jnp.where(lane == i1, NEG_INF, el)
        m2 = jnp.max(el2, axis=-1, keepdims=True)
        i2 = first_argmax(el2, m2)
        e2 = jnp.exp(m2 - m1)
        g1 = p_g / (1.0 + e2)
        g2 = p_g * e2 / (1.0 + e2)

        is1 = lane == i1
        is2 = lane == i2
        onehot = (is1 | is2).astype(BF16)
        before = _dot(earlier, onehot) + carry
        r1 = jnp.sum(jnp.where(is1, before, 0.0), axis=-1, keepdims=True)
        r2 = jnp.sum(jnp.where(is2, before, 0.0), axis=-1, keepdims=True)
        carry = carry + live * jnp.sum(onehot.astype(F32), axis=0, keepdims=True)

        route = jnp.where(lane == 0, i1.astype(F32), 0.0)
        route = jnp.where(lane == 1, i2.astype(F32), route)
        route = jnp.where(lane == 2, g1, route)
        route = jnp.where(lane == 3, g2, route)
        route = jnp.where(lane == 4, r1, route)
        route = jnp.where(lane == 5, r2, route)
        route_ref[rows] = route
        route_t_ref[:, rows] = route.T[:ROUTE_FIELDS]
    carry_ref[...] = carry
    cnt_ref[...] = carry


def _dest_kernel(pstart_ref, route_t_ref, dest_ref):
    eid = route_t_ref[0:TOP_K, :].astype(jnp.int32)
    rank = route_t_ref[4:4 + TOP_K, :].astype(jnp.int32)
    start = jnp.zeros_like(eid)
    for e in range(N_EXPERTS):
        start = jnp.where(eid == e, pstart_ref[e], start)
    dest_ref[...] = start + rank


def _sc_mesh():
    return plsc.VectorSubcoreMesh(core_axis_name="c", subcore_axis_name="s",
                                  num_cores=SC_CORES, num_subcores=SC_SUBCORES)


def _sc_worker_base(per_worker):
    return (lax.axis_index("s") * SC_CORES + lax.axis_index("c")) * per_worker


def _sc_scatter_rows(rows, idx_list, nrows):
    m, w = rows.shape
    nidx = len(idx_list)
    assert m % (SC_WORKERS * SC_CHUNK) == 0, m
    per_worker = m // SC_WORKERS
    nchunk = per_worker // SC_CHUNK

    @functools.partial(
        pl.kernel, mesh=_sc_mesh(), out_type=jax.ShapeDtypeStruct((nrows, w), rows.dtype),
        scratch_types=[pltpu.VMEM((SC_CHUNK,), jnp.int32)] * nidx
        + [pltpu.VMEM((SC_CHUNK, w), rows.dtype), pltpu.SemaphoreType.DMA],
        name="sc_scatter_rows")
    def scatter(rows_hbm, *refs):
        idx_hbm, out_hbm = refs[:nidx], refs[nidx]
        idx_v, rows_v, sem = refs[nidx + 1:2 * nidx + 1], refs[2 * nidx + 1], refs[2 * nidx + 2]
        base = _sc_worker_base(per_worker)

        @pl.loop(0, nchunk)
        def _(ci):
            off = pl.multiple_of(base + ci * SC_CHUNK, SC_CHUNK)
            pltpu.sync_copy(rows_hbm.at[pl.ds(off, SC_CHUNK)], rows_v)
            for kk in range(nidx):
                pltpu.sync_copy(idx_hbm[kk].at[pl.ds(off, SC_CHUNK)], idx_v[kk])
            copies = [pltpu.async_copy(rows_v, out_hbm.at[idx_v[kk]], sem) for kk in range(nidx)]
            for c in copies:
                c.wait()

    return scatter(rows, *idx_list)


def _expert_kernel(be_ref, nu_ref, bv_ref, x_ref, wgu_ref, wd_ref, y_ref):
    del be_ref
    j = pl.program_id(0)
    de = wgu_ref.shape[2] // 2

    @pl.when(j < nu_ref[0])
    def _():
        row = lax.broadcasted_iota(jnp.int32, x_ref.shape, 0)
        xu = jnp.where(row < bv_ref[j], x_ref[...], jnp.uint32(0))
        a, b = _unpack_bf16_pair(xu)
        x = jnp.concatenate([a, b], axis=1).astype(BF16)
        h12 = _dot(x, wgu_ref[0])
        h1, h2 = h12[:, :de], h12[:, de:]
        hdn = (h1 * jax.nn.sigmoid(h1) * h2).astype(BF16)
        y = _dot(hdn, wd_ref[0])
        hp = y.shape[1] // 2
        y_ref[...] = _pack_bf16_pair(y[:, :hp], y[:, hp:])

    @pl.when(pl.program_id(0) >= nu_ref[0])
    def _():
        y_ref[...] = jnp.zeros_like(y_ref)


def _sc_gather_rows(table, idx):
    m = idx.shape[0]
    w = table.shape[1]
    assert m % (SC_WORKERS * SC_CHUNK) == 0, m
    per_worker = m // SC_WORKERS
    nchunk = per_worker // SC_CHUNK

    @functools.partial(
        pl.kernel, mesh=_sc_mesh(), out_type=jax.ShapeDtypeStruct((m, w), table.dtype),
        scratch_types=[pltpu.VMEM((SC_CHUNK,), jnp.int32), pltpu.VMEM((SC_CHUNK, w), table.dtype),
                       pltpu.SemaphoreType.DMA],
        name="sc_gather_rows")
    def gather(table_hbm, idx_hbm, out_hbm, idx_v, rows_v, sem):
        base = _sc_worker_base(per_worker)

        @pl.loop(0, nchunk)
        def _(ci):
            off = pl.multiple_of(base + ci * SC_CHUNK, SC_CHUNK)
            pltpu.sync_copy(idx_hbm.at[pl.ds(off, SC_CHUNK)], idx_v)
            pltpu.async_copy(table_hbm.at[idx_v], rows_v, sem).wait()
            pltpu.sync_copy(rows_v, out_hbm.at[pl.ds(off, SC_CHUNK)])

    return gather(table, idx)


def _combine_kernel(h_ref, route_ref, fw_ref, y0_ref, y1_ref, o_ref):
    route = route_ref[...]
    g1 = route[:, 2:3]
    g2 = route[:, 3:4]
    a0, b0 = _unpack_bf16_pair(y0_ref[...])
    a1, b1 = _unpack_bf16_pair(y1_ref[...])
    moe = jnp.concatenate([a0 * g1 + a1 * g2, b0 * g1 + b1 * g2], axis=1)
    o_ref[...] = _rms(h_ref[...] + moe, fw_ref[...])


def _combine_into_kernel(h_ref, route_ref, fw_ref, y0_ref, y1_ref, carried_ref, o_ref):
    del carried_ref
    _combine_kernel(h_ref, route_ref, fw_ref, y0_ref, y1_ref, o_ref)


def _params(*sem):
    return pltpu.CompilerParams(dimension_semantics=sem, vmem_limit_bytes=VMEM_LIMIT)


def _full(shape):
    return pl.BlockSpec(shape, lambda *_: (0,) * len(shape))


def _rows(tile, width):
    return pl.BlockSpec((tile, width), lambda i: (i, 0))


def _layer(x2, pos2, attn_norm_w, w_in, gla_gate_up, gla_gate_bias, gla_norm_w, mla_q_norm_w,
           mla_w_uq, mla_kv_norm_w, mla_w_ukv, w_out, ffn_norm_w, router_group_w, router_group_b,
           router_expert_w, router_expert_b, expert_w_gate, expert_w_up, expert_w_down,
           out_norm_w, batch, seq):
    n, d = x2.shape
    dq = GLA_HEADS * GLA_DK
    dv = GLA_HEADS * GLA_DV
    dmla = MLA_HEADS * HEAD_PAD
    dmv = MLA_HEADS * MLA_DV

    c_gq, c_gk, c_gv, c_lr, c_og, c_cq, c_ckv, c_kr = jnp.split(
        w_in, [dq, 2 * dq, 2 * dq + dv, 2 * dq + dv + GLA_GATE_RANK,
               2 * dq + 2 * dv + GLA_GATE_RANK,
               2 * dq + 2 * dv + GLA_GATE_RANK + MLA_Q_RANK,
               2 * dq + 2 * dv + GLA_GATE_RANK + MLA_Q_RANK + MLA_KV_RANK], axis=1)
    zeros = lambda r, c: jnp.zeros((r, c), w_in.dtype)
    misc = jnp.concatenate([zeros(d, MLA_NOPE), c_kr, c_lr,
                            zeros(d, LANES - ROPE_END - GLA_GATE_RANK)], axis=1)
    w1 = jnp.concatenate([c_gq, c_gk, c_gv, c_og, c_cq, c_ckv, misc], axis=1).astype(BF16)
    gate_up = jnp.concatenate([zeros(GATE_LO, dq), gla_gate_up,
                               zeros(LANES - GATE_LO - GLA_GATE_RANK, dq)], axis=0).astype(BF16)
    wuq = mla_w_uq.reshape(MLA_Q_RANK, MLA_HEADS, MLA_NOPE + MLA_ROPE)
    wuq = jnp.pad(wuq, ((0, 0), (0, 0), (0, HEAD_PAD - ROPE_END))).reshape(MLA_Q_RANK, dmla)
    wukv = mla_w_ukv.reshape(MLA_KV_RANK, MLA_HEADS, MLA_NOPE + MLA_DV)
    wuk = jnp.pad(wukv[:, :, :MLA_NOPE], ((0, 0), (0, 0), (0, HEAD_PAD - MLA_NOPE)))
    wuv = jnp.pad(wukv[:, :, MLA_NOPE:], ((0, 0), (0, 0), (0, HEAD_PAD - MLA_DV)))
    wukv = jnp.concatenate([wuk.reshape(MLA_KV_RANK, dmla), wuv.reshape(MLA_KV_RANK, dmla)],
                           axis=1)
    inv = ROPE_THETA ** (-jnp.arange(ROPE_HALF, dtype=F32) / ROPE_HALF)
    inv_pat = jnp.concatenate([jnp.zeros((MLA_NOPE,), F32), inv, inv,
                               jnp.zeros((LANES - ROPE_END,), F32)]).reshape(1, LANES)
    w_router = jnp.concatenate(
        [router_expert_w, router_group_w, zeros(d, LANES - N_EXPERTS - N_GROUPS)], axis=1)
    w_router_hi = w_router.astype(BF16)
    w_router = jnp.concatenate(
        [w_router_hi, (w_router - w_router_hi.astype(F32)).astype(BF16)], axis=1)
    b_router = jnp.concatenate(
        [router_expert_b, router_group_b, jnp.zeros((LANES - N_EXPERTS - N_GROUPS,), F32)]
    ).reshape(1, LANES)
    row1 = lambda v: v.reshape(1, -1)

    tp = PROJ_TILE
    outs = pl.pallas_call(
        functools.partial(_proj_kernel, q_scale=(MLA_NOPE + MLA_ROPE) ** -0.5 * LOG2_E),
        grid=(n // tp,),
        in_specs=[_rows(tp, d), _rows(tp, 1), _full((1, d)), _full(w1.shape), _full(gate_up.shape),
                  _full((1, dq)), _full((1, MLA_Q_RANK)), _full(wuq.shape),
                  _full((1, MLA_KV_RANK)), _full(wukv.shape), _full((1, LANES))],
        out_specs=[_rows(tp, dq), _rows(tp, dq), _rows(tp, dv), _rows(tp, dq), _rows(tp, dv),
                   _rows(tp, dmla), _rows(tp, dmla), _rows(tp, dmla)],
        out_shape=[jax.ShapeDtypeStruct((n, dq), BF16), jax.ShapeDtypeStruct((n, dq), BF16),
                   jax.ShapeDtypeStruct((n, dv), BF16), jax.ShapeDtypeStruct((n, dq), F32),
                   jax.ShapeDtypeStruct((n, dv), BF16), jax.ShapeDtypeStruct((n, dmla), BF16),
                   jax.ShapeDtypeStruct((n, dmla), BF16), jax.ShapeDtypeStruct((n, dmla), BF16)],
        compiler_params=_params("parallel"),
        name="in_proj",
    )(x2, pos2, row1(attn_norm_w), w1, gate_up, row1(gla_gate_bias), row1(mla_q_norm_w),
      wuq.astype(BF16), row1(mla_kv_norm_w), wukv.astype(BF16), inv_pat)
    gq, gk, gv, gla, gog, q, k, v = outs

    tg = GLA_TILE
    seq3 = lambda a: a.reshape(batch, seq, a.shape[-1])
    gspec = lambda w: pl.BlockSpec((1, tg, w), lambda b, i: (b, i, 0))
    y_gla = pl.pallas_call(
        _gla_kernel,
        grid=(batch, seq // tg),
        in_specs=[gspec(dq), gspec(dq), gspec(dv), gspec(dq), gspec(dv), _full((1, GLA_DV))],
        out_specs=gspec(dv),
        out_shape=jax.ShapeDtypeStruct((batch, seq, dv), BF16),
        scratch_shapes=[pltpu.VMEM((GLA_HEADS, GLA_DV, GLA_DK), F32)],
        compiler_params=_params("parallel", "arbitrary"),
        name="gla",
    )(seq3(gq), seq3(gk), seq3(gv), seq3(gla), seq3(gog), row1(gla_norm_w))

    ta = ATT_TILE
    hps = ATT_HEADS_PER_STEP
    y_mla = pl.pallas_call(
        functools.partial(_mla_kernel, tq=ta),
        grid=(batch, MLA_HEADS // hps, seq // ta),
        in_specs=[pl.BlockSpec((1, seq, hps * HEAD_PAD), lambda b, hp, i: (b, 0, hp))] * 3,
        out_specs=pl.BlockSpec((1, ta, hps * MLA_DV), lambda b, hp, i: (b, i, hp)),
        out_shape=jax.ShapeDtypeStruct((batch, seq, dmv), BF16),
        scratch_shapes=[pltpu.VMEM((hps, ta, ta), F32)] * 3,
        compiler_params=_params("parallel", "parallel", "arbitrary"),
        name="mla",
    )(seq3(q), seq3(k), seq3(v))

    tr = ROUTER_TILE
    ntr = n // tr
    cur = lambda t, w: pl.BlockSpec((t, w), lambda i: (jnp.minimum(i, ntr - 1), 0))
    prev = lambda t, w: pl.BlockSpec((t, w), lambda i: (jnp.maximum(i - 1, 0), 0))
    h, hn, route, route_t, counts = pl.pallas_call(
        _router_kernel,
        grid=(ntr + 1,),
        in_specs=[cur(tr, d), cur(tr, dv), cur(tr, dmv), _full(w_out.shape), _full((1, d)),
                  _full(w_router.shape), _full((1, LANES))],
        out_specs=[cur(tr, d), cur(tr, d // 2), prev(tr, LANES),
                   pl.BlockSpec((ROUTE_FIELDS, tr), lambda i: (0, jnp.maximum(i - 1, 0))),
                   _full((1, LANES))],
        out_shape=[jax.ShapeDtypeStruct((n, d), F32), jax.ShapeDtypeStruct((n, d // 2), jnp.uint32),
                   jax.ShapeDtypeStruct((n, LANES), F32),
                   jax.ShapeDtypeStruct((ROUTE_FIELDS, n), F32),
                   jax.ShapeDtypeStruct((1, LANES), F32)],
        scratch_shapes=[pltpu.VMEM((1, LANES), F32), pltpu.VMEM((tr, LANES), F32)],
        compiler_params=_params("arbitrary"),
        name="out_proj_router",
    )(x2, y_gla.reshape(n, dv), y_mla.reshape(n, dmv), w_out.astype(BF16), row1(ffn_norm_w),
      w_router, b_router)

    blk = EXPERT_BLOCK
    nblk = (n * TOP_K) // blk + N_EXPERTS
    cnt = counts[0, :N_EXPERTS].astype(jnp.int32)
    padded = (cnt + blk - 1) // blk * blk
    pend = jnp.cumsum(padded)
    pstart = pend - padded
    dest = pl.pallas_call(
        _dest_kernel,
        grid_spec=pltpu.PrefetchScalarGridSpec(
            num_scalar_prefetch=1, grid=(1,),
            in_specs=[pl.BlockSpec((ROUTE_FIELDS, n), lambda i, ps: (0, 0))],
            out_specs=pl.BlockSpec((TOP_K, n), lambda i, ps: (0, 0))),
        out_shape=jax.ShapeDtypeStruct((TOP_K, n), jnp.int32),
        compiler_params=_params("arbitrary"),
        name="dest_rows",
    )(pstart, route_t)
    blk_start = jnp.arange(nblk, dtype=jnp.int32) * blk
    blk_expert = jnp.minimum(
        jnp.sum((pend[None, :] <= blk_start[:, None]).astype(jnp.int32), axis=1), N_EXPERTS - 1)
    n_used = (pend[-1] // blk).astype(jnp.int32).reshape(1)
    blk_valid = jnp.clip(cnt[blk_expert] - (blk_start - pstart[blk_expert]), 0, blk)
    blk_valid = jnp.where(jnp.arange(nblk) < n_used[0], blk_valid, 0).astype(jnp.int32)
    dest_slots = [dest[kk] for kk in range(TOP_K)]

    xbuf = _sc_scatter_rows(hn, dest_slots, nblk * blk)

    w_gate_up = jnp.concatenate([expert_w_gate, expert_w_up], axis=2).astype(BF16)

    def used(j, nu):
        return jnp.maximum(jnp.minimum(j, nu[0] - 1), 0)

    ybuf = pl.pallas_call(
        _expert_kernel,
        grid_spec=pltpu.PrefetchScalarGridSpec(
            num_scalar_prefetch=3,
            grid=(nblk,),
            in_specs=[
                pl.BlockSpec((blk, d // 2), lambda j, be, nu, bv: (used(j, nu), 0)),
                pl.BlockSpec((1, d, 2 * D_EXPERT),
                             lambda j, be, nu, bv: (be[used(j, nu)], 0, 0)),
                pl.BlockSpec((1, D_EXPERT, d), lambda j, be, nu, bv: (be[used(j, nu)], 0, 0)),
            ],
            out_specs=pl.BlockSpec((blk, d // 2), lambda j, be, nu, bv: (j, 0)),
        ),
        out_shape=jax.ShapeDtypeStruct((nblk * blk, d // 2), jnp.uint32),
        compiler_params=_params("arbitrary"),
        name="experts",
    )(blk_expert, n_used, blk_valid, xbuf, w_gate_up, expert_w_down.astype(BF16))

    tc = COMBINE_TILE
    parts = COMBINE_PARTS
    npart = n // parts
    steps = npart // tc
    out = None
    for p in range(parts):
        part_dest = dest[:, p * npart:(p + 1) * npart].reshape(TOP_K * npart)
        gathered = _sc_gather_rows(ybuf, part_dest)
        here = lambda w, p=p: pl.BlockSpec((tc, w), lambda i: (i + p * steps, 0))
        in_specs = [here(d), here(LANES), _full((1, d)), _rows(tc, d // 2),
                    pl.BlockSpec((tc, d // 2), lambda i: (i + steps, 0))]
        args = [h, route, row1(out_norm_w), gathered, gathered]
        if out is not None:
            in_specs.append(pl.BlockSpec(memory_space=pl.ANY))
            args.append(out)
        out = pl.pallas_call(
            _combine_kernel if out is None else _combine_into_kernel,
            grid=(steps,),
            in_specs=in_specs,
            out_specs=here(d),
            out_shape=jax.ShapeDtypeStruct((n, d), F32),
            input_output_aliases={} if out is None else {len(args) - 1: 0},
            compiler_params=_params("parallel"),
            name=f"combine_{p}",
        )(*args)
    return out


def kernel(x, positions, attn_norm_w, w_in, gla_gate_up, gla_gate_bias, gla_norm_w, mla_q_norm_w,
           mla_w_uq, mla_kv_norm_w, mla_w_ukv, w_out, ffn_norm_w, router_group_w, router_group_b,
           router_expert_w, router_expert_b, expert_w_gate, expert_w_up, expert_w_down,
           final_norm_w):
    batch, seq, d = x.shape
    depth = w_in.shape[0]
    assert depth == 1, "the final norm is fused into the last layer's combine step"
    n = batch * seq
    assert seq % GLA_TILE == 0 and seq % ATT_TILE == 0, (seq, GLA_TILE, ATT_TILE)
    assert n % ROUTER_TILE == 0 and n % PROJ_TILE == 0, n
    assert n % (COMBINE_TILE * COMBINE_PARTS) == 0, n
    out = _layer(x.reshape(batch * seq, d), positions.reshape(batch * seq, 1),
                 attn_norm_w[0], w_in[0], gla_gate_up[0], gla_gate_bias[0], gla_norm_w[0],
                 mla_q_norm_w[0], mla_w_uq[0], mla_kv_norm_w[0], mla_w_ukv[0], w_out[0],
                 ffn_norm_w[0], router_group_w[0], router_group_b[0], router_expert_w[0],
                 router_expert_b[0], expert_w_gate[0], expert_w_up[0], expert_w_down[0],
                 final_norm_w, batch, seq)
    return out.reshape(batch, seq, d)
```

```python
import functools

import jax
import jax.numpy as jnp
from jax import lax
from jax.experimental import pallas as pl
from jax.experimental.pallas import tpu as pltpu
from jax.experimental.pallas import tpu_sc as plsc

EPS = 1e-6
GLA_HEADS = 4
GLA_DK = 64
GLA_DV = 128
GLA_GATE_RANK = 16
GLA_GATE_NORM = 16.0
GLA_CHUNK = 64
MLA_HEADS = 8
MLA_NOPE = 64
MLA_ROPE = 32
MLA_DV = 64
MLA_Q_RANK = 384
MLA_KV_RANK = 256
ROPE_THETA = 10000.0
N_GROUPS = 4
EXPERTS_PER_GROUP = 8
N_EXPERTS = N_GROUPS * EXPERTS_PER_GROUP
TOP_K = 2
D_EXPERT = 256

LANES = 128
HEAD_PAD = 128
ROPE_HALF = MLA_ROPE // 2
ROPE_LO = MLA_NOPE
ROPE_HI = MLA_NOPE + ROPE_HALF
ROPE_END = MLA_NOPE + MLA_ROPE
ROPE_PACK = LANES // ROPE_HALF
GATE_LO = ROPE_END

PROJ_TILE = 512
COMBINE_TILE = 1024
COMBINE_PARTS = 4
GLA_TILE = 1024
GLA_SUBTILE = 256
ATT_TILE = 512
ATT_HEADS_PER_STEP = 4
EXPERT_BLOCK = 512
ROUTER_TILE = 1024
ROUTE_ROWS = 512
ROUTE_FIELDS = 8
VMEM_LIMIT = 56 * 1024 * 1024
SC_CORES = 2
SC_SUBCORES = 16
SC_WORKERS = SC_CORES * SC_SUBCORES
SC_CHUNK = 128

F32 = jnp.float32
BF16 = jnp.bfloat16
NEG_INF = float("-inf")
LOG2_E = 1.4426950408889634


def _dot(a, b, precision=None):
    return jnp.dot(a, b, preferred_element_type=F32, precision=precision)


def _dot_nt(a, b):
    return lax.dot_general(a, b, (((1,), (1,)), ((), ())), preferred_element_type=F32)


def _dot_tn(a, b):
    return lax.dot_general(a, b, (((0,), (0,)), ((), ())), preferred_element_type=F32)


def _rms(x, w):
    return x * lax.rsqrt(jnp.mean(x * x, axis=-1, keepdims=True) + EPS) * w


def _pack_bf16_pair(a, b):
    ua = lax.bitcast_convert_type(a.astype(BF16).astype(F32), jnp.uint32)
    ub = lax.bitcast_convert_type(b.astype(BF16).astype(F32), jnp.uint32)
    return (ua >> 16) | (ub & jnp.uint32(0xFFFF0000))


def _unpack_bf16_pair(u):
    a = lax.bitcast_convert_type(u << 16, F32)
    b = lax.bitcast_convert_type(u & jnp.uint32(0xFFFF0000), F32)
    return a, b


def _proj_kernel(x_ref, pos_ref, nw_ref, w1_ref, gu_ref, gb_ref, qnw_ref, wuq_ref, kvnw_ref,
                 wukv_ref, inv_ref,
                 gq_ref, gk_ref, gv_ref, gla_ref, gog_ref, q_ref, k_ref, v_ref, cos_ref, sin_ref,
                 *, q_scale):
    x = x_ref[...]
    xn = _rms(x, nw_ref[...]).astype(BF16)
    dq = GLA_HEADS * GLA_DK
    dv = GLA_HEADS * GLA_DV
    o = 0

    def proj(width):
        nonlocal o
        res = _dot(xn, w1_ref[:, o:o + width])
        o += width
        return res

    gq_ref[...] = (proj(dq) * (GLA_DK ** -0.5)).astype(BF16)
    gk_ref[...] = proj(dq).astype(BF16)
    gv_ref[...] = proj(dv).astype(BF16)
    gog_ref[...] = proj(dv).astype(BF16)
    cq = proj(MLA_Q_RANK)
    ckv = proj(MLA_KV_RANK)
    misc = proj(LANES)

    z = _dot(misc.astype(BF16), gu_ref[...]) + gb_ref[...]
    log_sig = jnp.minimum(z, 0.0) - jnp.log1p(jnp.exp(-jnp.abs(z)))
    gla_ref[...] = log_sig * (1.0 / GLA_GATE_NORM)

    lane = lax.broadcasted_iota(jnp.int32, (x.shape[0], LANES), 1)
    tpk = pos_ref.shape[0]
    lane_p = lax.broadcasted_iota(jnp.int32, (tpk, LANES), 1)
    pos_p = jnp.zeros((tpk, LANES), F32)
    for g in range(ROPE_PACK):
        pos_p = jnp.where((lane_p >> 4) == g, pos_ref[:, g:g + 1].astype(F32), pos_p)
    ang = pos_p * inv_ref[...]
    cos_p = jnp.cos(ang)
    sin_p = jnp.sin(ang)
    for g in range(ROPE_PACK):
        shift = (ROPE_LO - ROPE_HALF * g) % LANES
        cos_ref[pl.ds(g, tpk, stride=ROPE_PACK), :] = pltpu.roll(cos_p, shift, 1)
        sin_ref[pl.ds(g, tpk, stride=ROPE_PACK), :] = pltpu.roll(sin_p, shift, 1)
    cos_lo = cos_ref[...]
    sin_lo = sin_ref[...]
    in_lo = (lane >= ROPE_LO) & (lane < ROPE_HI)
    in_hi = (lane >= ROPE_HI) & (lane < ROPE_END)
    c_rope = jnp.where(in_lo, cos_lo, jnp.where(in_hi, pltpu.roll(cos_lo, ROPE_HALF, 1), 0.0))
    s_up = jnp.where(in_hi, pltpu.roll(sin_lo, ROPE_HALF, 1), 0.0)
    s_dn = jnp.where(in_lo, -sin_lo, 0.0)
    c_q = jnp.where(lane < MLA_NOPE, 1.0, c_rope)

    def rope(t, c):
        return (t * c + pltpu.roll(t, ROPE_HALF, 1) * s_up
                + pltpu.roll(t, LANES - ROPE_HALF, 1) * s_dn)

    k_rope = rope(misc, c_rope)

    q = _dot(_rms(cq, qnw_ref[...]).astype(BF16), wuq_ref[...])
    kv = _dot(_rms(ckv, kvnw_ref[...]).astype(BF16), wukv_ref[...])
    ones_lane = jnp.where(lane == MLA_DV, 1.0, 0.0)
    for h in range(MLA_HEADS):
        sl = slice(h * HEAD_PAD, (h + 1) * HEAD_PAD)
        vsl = slice((MLA_HEADS + h) * HEAD_PAD, (MLA_HEADS + h + 1) * HEAD_PAD)
        q_ref[:, sl] = (rope(q[:, sl], c_q) * q_scale).astype(BF16)
        k_ref[:, sl] = (kv[:, sl] + k_rope).astype(BF16)
        v_ref[:, sl] = (kv[:, vsl] + ones_lane).astype(BF16)


def _gla_kernel(q_ref, k_ref, v_ref, la_ref, og_ref, nw_ref, o_ref, st_ref):
    t = GLA_SUBTILE
    nchunk = t // GLA_CHUNK

    @pl.when(pl.program_id(1) == 0)
    def _():
        st_ref[...] = jnp.zeros_like(st_ref)

    row = lax.broadcasted_iota(jnp.int32, (t, t), 0)
    col = lax.broadcasted_iota(jnp.int32, (t, t), 1)
    chunk_bits = GLA_CHUNK.bit_length() - 1
    tri = ((row >> chunk_bits) == (col >> chunk_bits)) & (col <= row)
    tri_b = tri.astype(BF16)
    nw = nw_ref[...]
    states = [st_ref[h] for h in range(GLA_HEADS)]

    for sub in range(q_ref.shape[1] // t):
        tr = slice(sub * t, (sub + 1) * t)
        la = la_ref[0, tr]
        la_hi = la.astype(BF16)
        la_lo = (la - la_hi.astype(F32)).astype(BF16)
        parts = _dot(tri_b, jnp.concatenate([la_hi, la_lo], axis=1))
        b = parts[:, :la.shape[1]] + parts[:, la.shape[1]:]
        b_last = jnp.concatenate(
            [jnp.broadcast_to(b[(c + 1) * GLA_CHUNK - 1:(c + 1) * GLA_CHUNK],
                              (GLA_CHUNK, b.shape[1])) for c in range(nchunk)], axis=0)
        q_e = (q_ref[0, tr].astype(F32) * jnp.exp(b)).astype(BF16)
        kf = k_ref[0, tr].astype(F32)
        k_e = (kf * jnp.exp(-b)).astype(BF16)
        k_d = (kf * jnp.exp(b_last - b)).astype(BF16)
        decay = jnp.exp(b_last)

        for h in range(GLA_HEADS):
            ks = slice(h * GLA_DK, (h + 1) * GLA_DK)
            vs = slice(h * GLA_DV, (h + 1) * GLA_DV)
            qh, keh, kdh = q_e[:, ks], k_e[:, ks], k_d[:, ks]
            vh = v_ref[0, tr, vs]
            att = jnp.where(tri, _dot_nt(qh, keh), 0.0)
            o = _dot(att.astype(BF16), vh)
            state = states[h]
            inter = []
            for c in range(nchunk):
                rs = slice(c * GLA_CHUNK, (c + 1) * GLA_CHUNK)
                inter.append(_dot_nt(qh[rs], state.astype(BF16)))
                upd = _dot_tn(vh[rs], kdh[rs])
                state = state * decay[c * GLA_CHUNK:c * GLA_CHUNK + 1, ks] + upd
            states[h] = state
            o = o + jnp.concatenate(inter, axis=0)
            o = _rms(o, nw)
            g = og_ref[0, tr, vs].astype(F32)
            o_ref[0, tr, vs] = (o * (g * jax.nn.sigmoid(g))).astype(o_ref.dtype)

    for h in range(GLA_HEADS):
        st_ref[h] = states[h]


def _mla_kernel(q_ref, k_ref, v_ref, o_ref, sh_ref, s0_ref, s1_ref, *, tq):
    heads = q_ref.shape[2] // HEAD_PAD
    nq = q_ref.shape[1] // tq
    qi = pl.program_id(2)
    lane = lax.broadcasted_iota(jnp.int32, (tq, LANES), 1)
    causal = (lax.broadcasted_iota(jnp.int32, (tq, tq), 1)
              <= lax.broadcasted_iota(jnp.int32, (tq, tq), 0))

    def rows(i):
        return pl.ds(pl.multiple_of(i * tq, tq), tq)

    def produce(qblk, j, dst):
        for hh in range(heads):
            hs = slice(hh * HEAD_PAD, (hh + 1) * HEAD_PAD)
            dst[hh] = _dot_nt(q_ref[0, rows(qblk), hs], k_ref[0, rows(j), hs])

    def produce_diagonal(qblk, dst):
        th = tq // 2
        q0 = pl.multiple_of(qblk * tq, tq)
        for hh in range(heads):
            hs = slice(hh * HEAD_PAD, (hh + 1) * HEAD_PAD)
            dst[hh, :th, :th] = _dot_nt(q_ref[0, pl.ds(q0, th), hs], k_ref[0, pl.ds(q0, th), hs])
            dst[hh, th:, :] = _dot_nt(q_ref[0, pl.ds(q0 + th, th), hs], k_ref[0, rows(qblk), hs])

    def update(s, m, acc, v):
        m_new = jnp.maximum(m, jnp.max(s, axis=-1, keepdims=True))
        p = jnp.exp2(s - m_new).astype(BF16)
        return m_new, jnp.exp2(m - m_new) * acc + _dot(p, v)

    def consume(j, src, carry, diagonal):
        new = []
        for hh in range(heads):
            m, acc = carry[hh]
            hs = slice(hh * HEAD_PAD, (hh + 1) * HEAD_PAD)
            if not diagonal:
                new.append(update(src[hh], m, acc, v_ref[0, rows(j), hs]))
                continue
            th = tq // 2
            k0 = pl.multiple_of(j * tq, tq)
            top = update(jnp.where(causal[:th, :th], src[hh, :th, :th], NEG_INF),
                         m[:th], acc[:th], v_ref[0, pl.ds(k0, th), hs])
            bot = update(jnp.where(causal[th:], src[hh, th:, :], NEG_INF),
                         m[th:], acc[th:], v_ref[0, pl.ds(k0, tq), hs])
            new.append(tuple(jnp.concatenate([a, b], axis=0) for a, b in zip(top, bot)))
        return tuple(new)

    def finish(carry):
        for hp in range(heads // 2):
            o0, o1 = (acc / acc[:, MLA_DV:MLA_DV + 1] for _, acc in carry[2 * hp:2 * hp + 2])
            o_ref[0, :, hp * LANES:(hp + 1) * LANES] = jnp.where(
                lane < MLA_DV, o0, pltpu.roll(o1, MLA_DV, 1)).astype(o_ref.dtype)

    init = tuple((jnp.full((tq, 1), NEG_INF, F32), jnp.zeros((tq, LANES), F32))
                 for _ in range(heads))
    nxt = jnp.minimum(qi + 1, nq - 1)

    @pl.when(qi == 0)
    def _():
        produce_diagonal(0, sh_ref)
        carry = consume(0, sh_ref, init, True)
        produce(nxt, 0, sh_ref)
        finish(carry)

    @pl.when(qi > 0)
    def _():
        produce(qi, 1, s1_ref)
        carry = consume(0, sh_ref, init, False)

        def pair(i, carry):
            j = 1 + 2 * i
            produce(qi, j + 1, s0_ref)
            carry = consume(j, s1_ref, carry, False)
            produce(qi, j + 2, s1_ref)
            return consume(j + 1, s0_ref, carry, False)

        carry = lax.fori_loop(0, lax.shift_right_logical(qi - 1, 1), pair, carry)
        even = (qi & 1) == 0

        @pl.when(even)
        def _():
            produce_diagonal(qi, s0_ref)
            c = consume(qi - 1, s1_ref, carry, False)
            produce(nxt, 0, sh_ref)
            finish(consume(qi, s0_ref, c, True))

        @pl.when(jnp.logical_not(even))
        def _():
            produce(nxt, 0, sh_ref)
            finish(consume(qi, s1_ref, carry, True))


def _router_kernel(x_ref, yg_ref, ym_ref, wo_ref, fnw_ref, wr_ref, br_ref,
                   h_ref, hn_ref, route_ref, route_t_ref, cnt_ref, carry_ref, lg_ref):
    t = x_ref.shape[0]
    half = wo_ref.shape[0] // 2
    step = pl.program_id(0)

    @pl.when(step == 0)
    def _():
        carry_ref[...] = jnp.zeros_like(carry_ref)
        lg_ref[...] = jnp.zeros_like(lg_ref)

    logits_prev = lg_ref[...]
    h = x_ref[...] + _dot(yg_ref[...], wo_ref[:half]) + _dot(ym_ref[...], wo_ref[half:])
    h_ref[...] = h
    hn = _rms(h, fnw_ref[...])
    hp = hn.shape[1] // 2
    hn_ref[...] = _pack_bf16_pair(hn[:, :hp], hn[:, hp:])

    hn_hi = hn.astype(BF16)
    hn_lo = (hn - hn_hi.astype(F32)).astype(BF16)
    parts = _dot(hn_hi, wr_ref[...]) + _dot(hn_lo, wr_ref[...])
    lg_ref[...] = parts[:, :LANES] + parts[:, LANES:] + br_ref[...]

    live = (step > 0).astype(F32)
    tg = ROUTE_ROWS
    lane = lax.broadcasted_iota(jnp.int32, (tg, LANES), 1)
    lane_f = lane.astype(F32)
    rr = lax.broadcasted_iota(jnp.int32, (tg, tg), 0)
    cc = lax.broadcasted_iota(jnp.int32, (tg, tg), 1)
    earlier = (cc < rr).astype(BF16)

    def first_argmax(vals, vmax):
        idx = jnp.min(jnp.where(vals == vmax, lane_f, float(LANES)), axis=-1, keepdims=True)
        return idx.astype(jnp.int32)

    carry = carry_ref[...]
    for g in range(t // tg):
        rows = slice(g * tg, (g + 1) * tg)
        logits = logits_prev[rows]
        gl = jnp.where((lane >= N_EXPERTS) & (lane < N_EXPERTS + N_GROUPS), logits, NEG_INF)
        gmax = jnp.max(gl, axis=-1, keepdims=True)
        gsel = first_argmax(gl, gmax) - N_EXPERTS
        p_g = 1.0 / jnp.sum(jnp.exp(gl - gmax), axis=-1, keepdims=True)
        lo = gsel * EXPERTS_PER_GROUP
        el = jnp.where((lane >= lo) & (lane < lo + EXPERTS_PER_GROUP), logits, NEG_INF)
        m1 = jnp.max(el, axis=-1, keepdims=True)
        i1 = first_argmax(el, m1)
        el2 = jnp.where(lane == i1, NEG_INF, el)
        m2 = jnp.max(el2, axis=-1, keepdims=True)
        i2 = first_argmax(el2, m2)
        e2 = jnp.exp(m2 - m1)
        g1 = p_g / (1.0 + e2)
        g2 = p_g * e2 / (1.0 + e2)

        is1 = lane == i1
        is2 = lane == i2
        onehot = (is1 | is2).astype(BF16)
        before = _dot(earlier, onehot) + carry
        r1 = jnp.sum(jnp.where(is1, before, 0.0), axis=-1, keepdims=True)
        r2 = jnp.sum(jnp.where(is2, before, 0.0), axis=-1, keepdims=True)
        carry = carry + live * jnp.sum(onehot.astype(F32), axis=0, keepdims=True)

        route = jnp.where(lane == 0, i1.astype(F32), 0.0)
        route = jnp.where(lane == 1, i2.astype(F32), route)
        route = jnp.where(lane == 2, g1, route)
        route = jnp.where(lane == 3, g2, route)
        route = jnp.where(lane == 4, r1, route)
        route = jnp.where(lane == 5, r2, route)
        route_ref[rows] = route
        route_t_ref[:, rows] = route.T[:ROUTE_FIELDS]
    carry_ref[...] = carry
    cnt_ref[...] = carry


def _dest_kernel(pstart_ref, route_t_ref, dest_ref):
    eid = route_t_ref[0:TOP_K, :].astype(jnp.int32)
    rank = route_t_ref[4:4 + TOP_K, :].astype(jnp.int32)
    start = jnp.zeros_like(eid)
    for e in range(N_EXPERTS):
        start = jnp.where(eid == e, pstart_ref[e], start)
    dest_ref[...] = start + rank


def _sc_mesh():
    return plsc.VectorSubcoreMesh(core_axis_name="c", subcore_axis_name="s",
                                  num_cores=SC_CORES, num_subcores=SC_SUBCORES)


def _sc_worker_base(per_worker):
    return (lax.axis_index("s") * SC_CORES + lax.axis_index("c")) * per_worker


def _sc_scatter_rows(rows, idx_list, nrows):
    m, w = rows.shape
    nidx = len(idx_list)
    assert m % (SC_WORKERS * SC_CHUNK) == 0, m
    per_worker = m // SC_WORKERS
    nchunk = per_worker // SC_CHUNK

    @functools.partial(
        pl.kernel, mesh=_sc_mesh(), out_type=jax.ShapeDtypeStruct((nrows, w), rows.dtype),
        scratch_types=[pltpu.VMEM((SC_CHUNK,), jnp.int32)] * nidx
        + [pltpu.VMEM((SC_CHUNK, w), rows.dtype), pltpu.SemaphoreType.DMA],
        name="sc_scatter_rows")
    def scatter(rows_hbm, *refs):
        idx_hbm, out_hbm = refs[:nidx], refs[nidx]
        idx_v, rows_v, sem = refs[nidx + 1:2 * nidx + 1], refs[2 * nidx + 1], refs[2 * nidx + 2]
        base = _sc_worker_base(per_worker)

        @pl.loop(0, nchunk)
        def _(ci):
            off = pl.multiple_of(base + ci * SC_CHUNK, SC_CHUNK)
            pltpu.sync_copy(rows_hbm.at[pl.ds(off, SC_CHUNK)], rows_v)
            for kk in range(nidx):
                pltpu.sync_copy(idx_hbm[kk].at[pl.ds(off, SC_CHUNK)], idx_v[kk])
            copies = [pltpu.async_copy(rows_v, out_hbm.at[idx_v[kk]], sem) for kk in range(nidx)]
            for c in copies:
                c.wait()

    return scatter(rows, *idx_list)


def _expert_kernel(be_ref, nu_ref, bv_ref, x_ref, wg_ref, wu_ref, wd_ref, y_ref, wgu_s, wd_s):
    j = pl.program_id(0)
    de = wg_ref.shape[2]
    last = jnp.maximum(nu_ref[0] - 1, 0)
    cur = be_ref[jnp.minimum(j, last)]
    prev = be_ref[jnp.minimum(jnp.maximum(j - 1, 0), last)]

    @pl.when((j == 0) | (cur != prev))
    def _():
        wgu_s[:, :de] = wg_ref[0].astype(BF16)
        wgu_s[:, de:] = wu_ref[0].astype(BF16)
        wd_s[...] = wd_ref[0].astype(BF16)

    @pl.when(j < nu_ref[0])
    def _():
        row = lax.broadcasted_iota(jnp.int32, x_ref.shape, 0)
        xu = jnp.where(row < bv_ref[j], x_ref[...], jnp.uint32(0))
        a, b = _unpack_bf16_pair(xu)
        x = jnp.concatenate([a, b], axis=1).astype(BF16)
        h12 = _dot(x, wgu_s[...])
        h1, h2 = h12[:, :de], h12[:, de:]
        hdn = (h1 * jax.nn.sigmoid(h1) * h2).astype(BF16)
        y = _dot(hdn, wd_s[...])
        hp = y.shape[1] // 2
        y_ref[...] = _pack_bf16_pair(y[:, :hp], y[:, hp:])

    @pl.when(pl.program_id(0) >= nu_ref[0])
    def _():
        y_ref[...] = jnp.zeros_like(y_ref)


def _sc_gather_rows(table, idx):
    m = idx.shape[0]
    w = table.shape[1]
    assert m % (SC_WORKERS * SC_CHUNK) == 0, m
    per_worker = m // SC_WORKERS
    nchunk = per_worker // SC_CHUNK

    @functools.partial(
        pl.kernel, mesh=_sc_mesh(), out_type=jax.ShapeDtypeStruct((m, w), table.dtype),
        scratch_types=[pltpu.VMEM((SC_CHUNK,), jnp.int32), pltpu.VMEM((SC_CHUNK, w), table.dtype),
                       pltpu.SemaphoreType.DMA],
        name="sc_gather_rows")
    def gather(table_hbm, idx_hbm, out_hbm, idx_v, rows_v, sem):
        base = _sc_worker_base(per_worker)

        @pl.loop(0, nchunk)
        def _(ci):
            off = pl.multiple_of(base + ci * SC_CHUNK, SC_CHUNK)
            pltpu.sync_copy(idx_hbm.at[pl.ds(off, SC_CHUNK)], idx_v)
            pltpu.async_copy(table_hbm.at[idx_v], rows_v, sem).wait()
            pltpu.sync_copy(rows_v, out_hbm.at[pl.ds(off, SC_CHUNK)])

    return gather(table, idx)


def _combine_kernel(h_ref, route_ref, fw_ref, y0_ref, y1_ref, o_ref):
    route = route_ref[...]
    g1 = route[:, 2:3]
    g2 = route[:, 3:4]
    a0, b0 = _unpack_bf16_pair(y0_ref[...])
    a1, b1 = _unpack_bf16_pair(y1_ref[...])
    moe = jnp.concatenate([a0 * g1 + a1 * g2, b0 * g1 + b1 * g2], axis=1)
    o_ref[...] = _rms(h_ref[...] + moe, fw_ref[...])


def _combine_into_kernel(h_ref, route_ref, fw_ref, y0_ref, y1_ref, carried_ref, o_ref):
    del carried_ref
    _combine_kernel(h_ref, route_ref, fw_ref, y0_ref, y1_ref, o_ref)


def _params(*sem):
    return pltpu.CompilerParams(dimension_semantics=sem, vmem_limit_bytes=VMEM_LIMIT)


def _full(shape):
    return pl.BlockSpec(shape, lambda *_: (0,) * len(shape))


def _rows(tile, width):
    return pl.BlockSpec((tile, width), lambda i: (i, 0))


def _layer(x2, pos2, attn_norm_w, w_in, gla_gate_up, gla_gate_bias, gla_norm_w, mla_q_norm_w,
           mla_w_uq, mla_kv_norm_w, mla_w_ukv, w_out, ffn_norm_w, router_group_w, router_group_b,
           router_expert_w, router_expert_b, expert_w_gate, expert_w_up, expert_w_down,
           out_norm_w, batch, seq):
    n, d = x2.shape
    dq = GLA_HEADS * GLA_DK
    dv = GLA_HEADS * GLA_DV
    dmla = MLA_HEADS * HEAD_PAD
    dmv = MLA_HEADS * MLA_DV

    c_gq, c_gk, c_gv, c_lr, c_og, c_cq, c_ckv, c_kr = jnp.split(
        w_in, [dq, 2 * dq, 2 * dq + dv, 2 * dq + dv + GLA_GATE_RANK,
               2 * dq + 2 * dv + GLA_GATE_RANK,
               2 * dq + 2 * dv + GLA_GATE_RANK + MLA_Q_RANK,
               2 * dq + 2 * dv + GLA_GATE_RANK + MLA_Q_RANK + MLA_KV_RANK], axis=1)
    zeros = lambda r, c: jnp.zeros((r, c), w_in.dtype)
    misc = jnp.concatenate([zeros(d, MLA_NOPE), c_kr, c_lr,
                            zeros(d, LANES - ROPE_END - GLA_GATE_RANK)], axis=1)
    w1 = jnp.concatenate([c_gq, c_gk, c_gv, c_og, c_cq, c_ckv, misc], axis=1).astype(BF16)
    gate_up = jnp.concatenate([zeros(GATE_LO, dq), gla_gate_up,
                               zeros(LANES - GATE_LO - GLA_GATE_RANK, dq)], axis=0).astype(BF16)
    wuq = mla_w_uq.reshape(MLA_Q_RANK, MLA_HEADS, MLA_NOPE + MLA_ROPE)
    wuq = jnp.pad(wuq, ((0, 0), (0, 0), (0, HEAD_PAD - ROPE_END))).reshape(MLA_Q_RANK, dmla)
    wukv = mla_w_ukv.reshape(MLA_KV_RANK, MLA_HEADS, MLA_NOPE + MLA_DV)
    wuk = jnp.pad(wukv[:, :, :MLA_NOPE], ((0, 0), (0, 0), (0, HEAD_PAD - MLA_NOPE)))
    wuv = jnp.pad(wukv[:, :, MLA_NOPE:], ((0, 0), (0, 0), (0, HEAD_PAD - MLA_DV)))
    wukv = jnp.concatenate([wuk.reshape(MLA_KV_RANK, dmla), wuv.reshape(MLA_KV_RANK, dmla)],
                           axis=1)
    inv = ROPE_THETA ** (-jnp.arange(ROPE_HALF, dtype=F32) / ROPE_HALF)
    inv_pat = jnp.tile(inv, ROPE_PACK).reshape(1, LANES)
    w_router = jnp.concatenate(
        [router_expert_w, router_group_w, zeros(d, LANES - N_EXPERTS - N_GROUPS)], axis=1)
    w_router_hi = w_router.astype(BF16)
    w_router = jnp.concatenate(
        [w_router_hi, (w_router - w_router_hi.astype(F32)).astype(BF16)], axis=1)
    b_router = jnp.concatenate(
        [router_expert_b, router_group_b, jnp.zeros((LANES - N_EXPERTS - N_GROUPS,), F32)]
    ).reshape(1, LANES)
    row1 = lambda v: v.reshape(1, -1)

    tp = PROJ_TILE
    outs = pl.pallas_call(
        functools.partial(_proj_kernel, q_scale=(MLA_NOPE + MLA_ROPE) ** -0.5 * LOG2_E),
        grid=(n // tp,),
        in_specs=[_rows(tp, d), _rows(tp // ROPE_PACK, ROPE_PACK), _full((1, d)), _full(w1.shape),
                  _full(gate_up.shape),
                  _full((1, dq)), _full((1, MLA_Q_RANK)), _full(wuq.shape),
                  _full((1, MLA_KV_RANK)), _full(wukv.shape), _full((1, LANES))],
        out_specs=[_rows(tp, dq), _rows(tp, dq), _rows(tp, dv), _rows(tp, dq), _rows(tp, dv),
                   _rows(tp, dmla), _rows(tp, dmla), _rows(tp, dmla)],
        out_shape=[jax.ShapeDtypeStruct((n, dq), BF16), jax.ShapeDtypeStruct((n, dq), BF16),
                   jax.ShapeDtypeStruct((n, dv), BF16), jax.ShapeDtypeStruct((n, dq), F32),
                   jax.ShapeDtypeStruct((n, dv), BF16), jax.ShapeDtypeStruct((n, dmla), BF16),
                   jax.ShapeDtypeStruct((n, dmla), BF16), jax.ShapeDtypeStruct((n, dmla), BF16)],
        scratch_shapes=[pltpu.VMEM((tp, LANES), F32)] * 2,
        compiler_params=_params("parallel"),
        name="in_proj",
    )(x2, pos2, row1(attn_norm_w), w1, gate_up, row1(gla_gate_bias), row1(mla_q_norm_w),
      wuq.astype(BF16), row1(mla_kv_norm_w), wukv.astype(BF16), inv_pat)
    gq, gk, gv, gla, gog, q, k, v = outs

    tg = GLA_TILE
    seq3 = lambda a: a.reshape(batch, seq, a.shape[-1])
    gspec = lambda w: pl.BlockSpec((1, tg, w), lambda b, i: (b, i, 0))
    y_gla = pl.pallas_call(
        _gla_kernel,
        grid=(batch, seq // tg),
        in_specs=[gspec(dq), gspec(dq), gspec(dv), gspec(dq), gspec(dv), _full((1, GLA_DV))],
        out_specs=gspec(dv),
        out_shape=jax.ShapeDtypeStruct((batch, seq, dv), BF16),
        scratch_shapes=[pltpu.VMEM((GLA_HEADS, GLA_DV, GLA_DK), F32)],
        compiler_params=_params("parallel", "arbitrary"),
        name="gla",
    )(seq3(gq), seq3(gk), seq3(gv), seq3(gla), seq3(gog), row1(gla_norm_w))

    ta = ATT_TILE
    hps = ATT_HEADS_PER_STEP
    y_mla = pl.pallas_call(
        functools.partial(_mla_kernel, tq=ta),
        grid=(batch, MLA_HEADS // hps, seq // ta),
        in_specs=[pl.BlockSpec((1, seq, hps * HEAD_PAD), lambda b, hp, i: (b, 0, hp))] * 3,
        out_specs=pl.BlockSpec((1, ta, hps * MLA_DV), lambda b, hp, i: (b, i, hp)),
        out_shape=jax.ShapeDtypeStruct((batch, seq, dmv), BF16),
        scratch_shapes=[pltpu.VMEM((hps, ta, ta), F32)] * 3,
        compiler_params=_params("parallel", "parallel", "arbitrary"),
        name="mla",
    )(seq3(q), seq3(k), seq3(v))

    tr = ROUTER_TILE
    ntr = n // tr
    cur = lambda t, w: pl.BlockSpec((t, w), lambda i: (jnp.minimum(i, ntr - 1), 0))
    prev = lambda t, w: pl.BlockSpec((t, w), lambda i: (jnp.maximum(i - 1, 0), 0))
    h, hn, route, route_t, counts = pl.pallas_call(
        _router_kernel,
        grid=(ntr + 1,),
        in_specs=[cur(tr, d), cur(tr, dv), cur(tr, dmv), _full(w_out.shape), _full((1, d)),
                  _full(w_router.shape), _full((1, LANES))],
        out_specs=[cur(tr, d), cur(tr, d // 2), prev(tr, LANES),
                   pl.BlockSpec((ROUTE_FIELDS, tr), lambda i: (0, jnp.maximum(i - 1, 0))),
                   _full((1, LANES))],
        out_shape=[jax.ShapeDtypeStruct((n, d), F32), jax.ShapeDtypeStruct((n, d // 2), jnp.uint32),
                   jax.ShapeDtypeStruct((n, LANES), F32),
                   jax.ShapeDtypeStruct((ROUTE_FIELDS, n), F32),
                   jax.ShapeDtypeStruct((1, LANES), F32)],
        scratch_shapes=[pltpu.VMEM((1, LANES), F32), pltpu.VMEM((tr, LANES), F32)],
        compiler_params=_params("arbitrary"),
        name="out_proj_router",
    )(x2, y_gla.reshape(n, dv), y_mla.reshape(n, dmv), w_out.astype(BF16), row1(ffn_norm_w),
      w_router, b_router)

    blk = EXPERT_BLOCK
    nblk = (n * TOP_K) // blk + N_EXPERTS
    cnt = counts[0, :N_EXPERTS].astype(jnp.int32)
    padded = (cnt + blk - 1) // blk * blk
    pend = jnp.cumsum(padded)
    pstart = pend - padded
    dest = pl.pallas_call(
        _dest_kernel,
        grid_spec=pltpu.PrefetchScalarGridSpec(
            num_scalar_prefetch=1, grid=(1,),
            in_specs=[pl.BlockSpec((ROUTE_FIELDS, n), lambda i, ps: (0, 0))],
            out_specs=pl.BlockSpec((TOP_K, n), lambda i, ps: (0, 0))),
        out_shape=jax.ShapeDtypeStruct((TOP_K, n), jnp.int32),
        compiler_params=_params("arbitrary"),
        name="dest_rows",
    )(pstart, route_t)
    blk_start = jnp.arange(nblk, dtype=jnp.int32) * blk
    blk_expert = jnp.minimum(
        jnp.sum((pend[None, :] <= blk_start[:, None]).astype(jnp.int32), axis=1), N_EXPERTS - 1)
    n_used = (pend[-1] // blk).astype(jnp.int32).reshape(1)
    blk_valid = jnp.clip(cnt[blk_expert] - (blk_start - pstart[blk_expert]), 0, blk)
    blk_valid = jnp.where(jnp.arange(nblk) < n_used[0], blk_valid, 0).astype(jnp.int32)
    dest_slots = [dest[kk] for kk in range(TOP_K)]

    xbuf = _sc_scatter_rows(hn, dest_slots, nblk * blk)

    def used(j, nu):
        return jnp.maximum(jnp.minimum(j, nu[0] - 1), 0)

    ybuf = pl.pallas_call(
        _expert_kernel,
        grid_spec=pltpu.PrefetchScalarGridSpec(
            num_scalar_prefetch=3,
            grid=(nblk,),
            in_specs=[
                pl.BlockSpec((blk, d // 2), lambda j, be, nu, bv: (used(j, nu), 0)),
                pl.BlockSpec((1, d, D_EXPERT), lambda j, be, nu, bv: (be[used(j, nu)], 0, 0)),
                pl.BlockSpec((1, d, D_EXPERT), lambda j, be, nu, bv: (be[used(j, nu)], 0, 0)),
                pl.BlockSpec((1, D_EXPERT, d), lambda j, be, nu, bv: (be[used(j, nu)], 0, 0)),
            ],
            out_specs=pl.BlockSpec((blk, d // 2), lambda j, be, nu, bv: (j, 0)),
            scratch_shapes=[pltpu.VMEM((d, 2 * D_EXPERT), BF16), pltpu.VMEM((D_EXPERT, d), BF16)],
        ),
        out_shape=jax.ShapeDtypeStruct((nblk * blk, d // 2), jnp.uint32),
        compiler_params=_params("arbitrary"),
        name="experts",
    )(blk_expert, n_used, blk_valid, xbuf, expert_w_gate, expert_w_up, expert_w_down)

    tc = COMBINE_TILE
    parts = COMBINE_PARTS
    npart = n // parts
    steps = npart // tc
    out = None
    for p in range(parts):
        part_dest = dest[:, p * npart:(p + 1) * npart].reshape(TOP_K * npart)
        gathered = _sc_gather_rows(ybuf, part_dest)
        here = lambda w, p=p: pl.BlockSpec((tc, w), lambda i: (i + p * steps, 0))
        in_specs = [here(d), here(LANES), _full((1, d)), _rows(tc, d // 2),
                    pl.BlockSpec((tc, d // 2), lambda i: (i + steps, 0))]
        args = [h, route, row1(out_norm_w), gathered, gathered]
        if out is not None:
            in_specs.append(pl.BlockSpec(memory_space=pl.ANY))
            args.append(out)
        out = pl.pallas_call(
            _combine_kernel if out is None else _combine_into_kernel,
            grid=(steps,),
            in_specs=in_specs,
            out_specs=here(d),
            out_shape=jax.ShapeDtypeStruct((n, d), F32),
            input_output_aliases={} if out is None else {len(args) - 1: 0},
            compiler_params=_params("parallel"),
            name=f"combine_{p}",
        )(*args)
    return out


def kernel(x, positions, attn_norm_w, w_in, gla_gate_up, gla_gate_bias, gla_norm_w, mla_q_norm_w,
           mla_w_uq, mla_kv_norm_w, mla_w_ukv, w_out, ffn_norm_w, router_group_w, router_group_b,
           router_expert_w, router_expert_b, expert_w_gate, expert_w_up, expert_w_down,
           final_norm_w):
    batch, seq, d = x.shape
    depth = w_in.shape[0]
    assert depth == 1, "the final norm is fused into the last layer's combine step"
    n = batch * seq
    assert seq % GLA_TILE == 0 and seq % ATT_TILE == 0, (seq, GLA_TILE, ATT_TILE)
    assert n % ROUTER_TILE == 0 and n % PROJ_TILE == 0, n
    assert n % (COMBINE_TILE * COMBINE_PARTS) == 0, n
    out = _layer(x.reshape(batch * seq, d), positions.reshape(batch * seq // ROPE_PACK, ROPE_PACK),
                 attn_norm_w[0], w_in[0], gla_gate_up[0], gla_gate_bias[0], gla_norm_w[0],
                 mla_q_norm_w[0], mla_w_uq[0], mla_kv_norm_w[0], mla_w_ukv[0], w_out[0],
                 ffn_norm_w[0], router_group_w[0], router_group_b[0], router_expert_w[0],
                 router_expert_b[0], expert_w_gate[0], expert_w_up[0], expert_w_down[0],
                 final_norm_w, batch, seq)
    return out.reshape(batch, seq, d)
```

```python
import functools

import jax
import jax.numpy as jnp
from jax import lax
from jax.experimental import pallas as pl
from jax.experimental.pallas import tpu as pltpu
from jax.experimental.pallas import tpu_sc as plsc

EPS = 1e-6
GLA_HEADS = 4
GLA_DK = 64
GLA_DV = 128
GLA_GATE_RANK = 16
GLA_GATE_NORM = 16.0
GLA_CHUNK = 64
MLA_HEADS = 8
MLA_NOPE = 64
MLA_ROPE = 32
MLA_DV = 64
MLA_Q_RANK = 384
MLA_KV_RANK = 256
ROPE_THETA = 10000.0
N_GROUPS = 4
EXPERTS_PER_GROUP = 8
N_EXPERTS = N_GROUPS * EXPERTS_PER_GROUP
TOP_K = 2
D_EXPERT = 256

LANES = 128
HEAD_PAD = 128
ROPE_HALF = MLA_ROPE // 2
ROPE_LO = MLA_NOPE
ROPE_HI = MLA_NOPE + ROPE_HALF
ROPE_END = MLA_NOPE + MLA_ROPE
ROPE_PACK = LANES // ROPE_HALF
GATE_LO = ROPE_END

PROJ_TILE = 512
COMBINE_TILE = 1024
COMBINE_PARTS = 4
GLA_TILE = 2048
GLA_SUBTILE = 128
ATT_TILE = 512
ATT_HEADS_PER_STEP = 4
EXPERT_BLOCK = 512
ROUTER_TILE = 1024
ROUTE_ROWS = 512
ROUTE_FIELDS = 8
VMEM_LIMIT = 56 * 1024 * 1024
SC_CORES = 2
SC_SUBCORES = 16
SC_WORKERS = SC_CORES * SC_SUBCORES
SC_CHUNK = 128

F32 = jnp.float32
BF16 = jnp.bfloat16
NEG_INF = float("-inf")
LOG2_E = 1.4426950408889634


def _dot(a, b, precision=None):
    return jnp.dot(a, b, preferred_element_type=F32, precision=precision)


def _dot_nt(a, b):
    return lax.dot_general(a, b, (((1,), (1,)), ((), ())), preferred_element_type=F32)


def _dot_tn(a, b):
    return lax.dot_general(a, b, (((0,), (0,)), ((), ())), preferred_element_type=F32)


def _rms(x, w):
    return x * lax.rsqrt(jnp.mean(x * x, axis=-1, keepdims=True) + EPS) * w


def _pack_bf16_pair(a, b):
    ua = lax.bitcast_convert_type(a.astype(BF16).astype(F32), jnp.uint32)
    ub = lax.bitcast_convert_type(b.astype(BF16).astype(F32), jnp.uint32)
    return (ua >> 16) | (ub & jnp.uint32(0xFFFF0000))


def _unpack_bf16_pair(u):
    a = lax.bitcast_convert_type(u << 16, F32)
    b = lax.bitcast_convert_type(u & jnp.uint32(0xFFFF0000), F32)
    return a, b


def _proj_kernel(x_ref, pos_ref, nw_ref, w1_ref, gu_ref, gb_ref, qnw_ref, wuq_ref, kvnw_ref,
                 wukv_ref, inv_ref,
                 gq_ref, gk_ref, gv_ref, gla_ref, gog_ref, q_ref, k_ref, v_ref, cos_ref, sin_ref,
                 *, q_scale):
    x = x_ref[...]
    xn = _rms(x, nw_ref[...]).astype(BF16)
    dq = GLA_HEADS * GLA_DK
    dv = GLA_HEADS * GLA_DV
    o = 0

    def proj(width):
        nonlocal o
        res = _dot(xn, w1_ref[:, o:o + width])
        o += width
        return res

    gq_ref[...] = (proj(dq) * (GLA_DK ** -0.5)).astype(BF16)
    gk_ref[...] = proj(dq).astype(BF16)
    gv_ref[...] = proj(dv).astype(BF16)
    gog_ref[...] = proj(dv).astype(BF16)
    cq = proj(MLA_Q_RANK)
    ckv = proj(MLA_KV_RANK)
    misc = proj(LANES)

    z = _dot(misc.astype(BF16), gu_ref[...]) + gb_ref[...]
    log_sig = jnp.minimum(z, 0.0) - jnp.log1p(jnp.exp(-jnp.abs(z)))
    gla_ref[...] = log_sig * (1.0 / GLA_GATE_NORM)

    lane = lax.broadcasted_iota(jnp.int32, (x.shape[0], LANES), 1)
    tpk = pos_ref.shape[0]
    lane_p = lax.broadcasted_iota(jnp.int32, (tpk, LANES), 1)
    pos_p = jnp.zeros((tpk, LANES), F32)
    for g in range(ROPE_PACK):
        pos_p = jnp.where((lane_p >> 4) == g, pos_ref[:, g:g + 1].astype(F32), pos_p)
    ang = pos_p * inv_ref[...]
    cos_p = jnp.cos(ang)
    sin_p = jnp.sin(ang)
    for g in range(ROPE_PACK):
        shift = (ROPE_LO - ROPE_HALF * g) % LANES
        cos_ref[pl.ds(g, tpk, stride=ROPE_PACK), :] = pltpu.roll(cos_p, shift, 1)
        sin_ref[pl.ds(g, tpk, stride=ROPE_PACK), :] = pltpu.roll(sin_p, shift, 1)
    cos_lo = cos_ref[...]
    sin_lo = sin_ref[...]
    in_lo = (lane >= ROPE_LO) & (lane < ROPE_HI)
    in_hi = (lane >= ROPE_HI) & (lane < ROPE_END)
    c_rope = jnp.where(in_lo, cos_lo, jnp.where(in_hi, pltpu.roll(cos_lo, ROPE_HALF, 1), 0.0))
    s_up = jnp.where(in_hi, pltpu.roll(sin_lo, ROPE_HALF, 1), 0.0)
    s_dn = jnp.where(in_lo, -sin_lo, 0.0)
    c_q = jnp.where(lane < MLA_NOPE, 1.0, c_rope)

    def rope(t, c):
        return (t * c + pltpu.roll(t, ROPE_HALF, 1) * s_up
                + pltpu.roll(t, LANES - ROPE_HALF, 1) * s_dn)

    k_rope = rope(misc, c_rope)

    q = _dot(_rms(cq, qnw_ref[...]).astype(BF16), wuq_ref[...])
    kv = _dot(_rms(ckv, kvnw_ref[...]).astype(BF16), wukv_ref[...])
    ones_lane = jnp.where(lane == MLA_DV, 1.0, 0.0)
    for h in range(MLA_HEADS):
        sl = slice(h * HEAD_PAD, (h + 1) * HEAD_PAD)
        vsl = slice((MLA_HEADS + h) * HEAD_PAD, (MLA_HEADS + h + 1) * HEAD_PAD)
        q_ref[:, sl] = (rope(q[:, sl], c_q) * q_scale).astype(BF16)
        k_ref[:, sl] = (kv[:, sl] + k_rope).astype(BF16)
        v_ref[:, sl] = (kv[:, vsl] + ones_lane).astype(BF16)


def _gla_kernel(q_ref, k_ref, v_ref, la_ref, og_ref, nw_ref, o_ref, st_ref):
    t = GLA_SUBTILE
    nchunk = t // GLA_CHUNK

    @pl.when(pl.program_id(1) == 0)
    def _():
        st_ref[...] = jnp.zeros_like(st_ref)

    row = lax.broadcasted_iota(jnp.int32, (t, t), 0)
    col = lax.broadcasted_iota(jnp.int32, (t, t), 1)
    chunk_bits = GLA_CHUNK.bit_length() - 1
    tri = ((row >> chunk_bits) == (col >> chunk_bits)) & (col <= row)
    tri_b = tri.astype(BF16)
    nw = nw_ref[...]
    states = [st_ref[h] for h in range(GLA_HEADS)]

    for sub in range(q_ref.shape[1] // t):
        tr = slice(sub * t, (sub + 1) * t)
        la = la_ref[0, tr]
        la_hi = la.astype(BF16)
        la_lo = (la - la_hi.astype(F32)).astype(BF16)
        parts = _dot(tri_b, jnp.concatenate([la_hi, la_lo], axis=1))
        b = parts[:, :la.shape[1]] + parts[:, la.shape[1]:]
        b_last = jnp.concatenate(
            [jnp.broadcast_to(b[(c + 1) * GLA_CHUNK - 1:(c + 1) * GLA_CHUNK],
                              (GLA_CHUNK, b.shape[1])) for c in range(nchunk)], axis=0)
        q_e = (q_ref[0, tr].astype(F32) * jnp.exp(b)).astype(BF16)
        kf = k_ref[0, tr].astype(F32)
        k_e = (kf * jnp.exp(-b)).astype(BF16)
        k_d = (kf * jnp.exp(b_last - b)).astype(BF16)
        decay = jnp.exp(b_last)

        for h in range(GLA_HEADS):
            ks = slice(h * GLA_DK, (h + 1) * GLA_DK)
            vs = slice(h * GLA_DV, (h + 1) * GLA_DV)
            qh, keh, kdh = q_e[:, ks], k_e[:, ks], k_d[:, ks]
            vh = v_ref[0, tr, vs]
            att = jnp.where(tri, _dot_nt(qh, keh), 0.0)
            o = _dot(att.astype(BF16), vh)
            state = states[h]
            inter = []
            for c in range(nchunk):
                rs = slice(c * GLA_CHUNK, (c + 1) * GLA_CHUNK)
                inter.append(_dot_nt(qh[rs], state.astype(BF16)))
                upd = _dot_tn(vh[rs], kdh[rs])
                state = state * decay[c * GLA_CHUNK:c * GLA_CHUNK + 1, ks] + upd
            states[h] = state
            o = o + jnp.concatenate(inter, axis=0)
            o = _rms(o, nw)
            g = og_ref[0, tr, vs].astype(F32)
            o_ref[0, tr, vs] = (o * (g * jax.nn.sigmoid(g))).astype(o_ref.dtype)

    for h in range(GLA_HEADS):
        st_ref[h] = states[h]


def _mla_kernel(q_ref, k_ref, v_ref, o_ref, sh_ref, s0_ref, s1_ref, *, tq):
    heads = q_ref.shape[2] // HEAD_PAD
    nq = q_ref.shape[1] // tq
    qi = pl.program_id(2)
    lane = lax.broadcasted_iota(jnp.int32, (tq, LANES), 1)
    causal = (lax.broadcasted_iota(jnp.int32, (tq, tq), 1)
              <= lax.broadcasted_iota(jnp.int32, (tq, tq), 0))

    def rows(i):
        return pl.ds(pl.multiple_of(i * tq, tq), tq)

    def produce(qblk, j, dst):
        for hh in range(heads):
            hs = slice(hh * HEAD_PAD, (hh + 1) * HEAD_PAD)
            dst[hh] = _dot_nt(q_ref[0, rows(qblk), hs], k_ref[0, rows(j), hs])

    def produce_diagonal(qblk, dst):
        th = tq // 2
        q0 = pl.multiple_of(qblk * tq, tq)
        for hh in range(heads):
            hs = slice(hh * HEAD_PAD, (hh + 1) * HEAD_PAD)
            dst[hh, :th, :th] = _dot_nt(q_ref[0, pl.ds(q0, th), hs], k_ref[0, pl.ds(q0, th), hs])
            dst[hh, th:, :] = _dot_nt(q_ref[0, pl.ds(q0 + th, th), hs], k_ref[0, rows(qblk), hs])

    def update(s, m, acc, v):
        m_new = jnp.maximum(m, jnp.max(s, axis=-1, keepdims=True))
        p = jnp.exp2(s - m_new).astype(BF16)
        return m_new, jnp.exp2(m - m_new) * acc + _dot(p, v)

    def consume(j, src, carry, diagonal):
        new = []
        for hh in range(heads):
            m, acc = carry[hh]
            hs = slice(hh * HEAD_PAD, (hh + 1) * HEAD_PAD)
            if not diagonal:
                new.append(update(src[hh], m, acc, v_ref[0, rows(j), hs]))
                continue
            th = tq // 2
            k0 = pl.multiple_of(j * tq, tq)
            top = update(jnp.where(causal[:th, :th], src[hh, :th, :th], NEG_INF),
                         m[:th], acc[:th], v_ref[0, pl.ds(k0, th), hs])
            bot = update(jnp.where(causal[th:], src[hh, th:, :], NEG_INF),
                         m[th:], acc[th:], v_ref[0, pl.ds(k0, tq), hs])
            new.append(tuple(jnp.concatenate([a, b], axis=0) for a, b in zip(top, bot)))
        return tuple(new)

    def finish(carry):
        for hp in range(heads // 2):
            o0, o1 = (acc / acc[:, MLA_DV:MLA_DV + 1] for _, acc in carry[2 * hp:2 * hp + 2])
            o_ref[0, :, hp * LANES:(hp + 1) * LANES] = jnp.where(
                lane < MLA_DV, o0, pltpu.roll(o1, MLA_DV, 1)).astype(o_ref.dtype)

    init = tuple((jnp.full((tq, 1), NEG_INF, F32), jnp.zeros((tq, LANES), F32))
                 for _ in range(heads))
    nxt = jnp.minimum(qi + 1, nq - 1)

    @pl.when(qi == 0)
    def _():
        produce_diagonal(0, sh_ref)
        carry = consume(0, sh_ref, init, True)
        produce(nxt, 0, sh_ref)
        finish(carry)

    @pl.when(qi > 0)
    def _():
        produce(qi, 1, s1_ref)
        carry = consume(0, sh_ref, init, False)

        def pair(i, carry):
            j = 1 + 2 * i
            produce(qi, j + 1, s0_ref)
            carry = consume(j, s1_ref, carry, False)
            produce(qi, j + 2, s1_ref)
            return consume(j + 1, s0_ref, carry, False)

        carry = lax.fori_loop(0, lax.shift_right_logical(qi - 1, 1), pair, carry)
        even = (qi & 1) == 0

        @pl.when(even)
        def _():
            produce_diagonal(qi, s0_ref)
            c = consume(qi - 1, s1_ref, carry, False)
            produce(nxt, 0, sh_ref)
            finish(consume(qi, s0_ref, c, True))

        @pl.when(jnp.logical_not(even))
        def _():
            produce(nxt, 0, sh_ref)
            finish(consume(qi, s1_ref, carry, True))


def _router_kernel(x_ref, yg_ref, ym_ref, wo_ref, fnw_ref, wr_ref, br_ref,
                   h_ref, hn_ref, route_ref, route_t_ref, cnt_ref, carry_ref, lg_ref):
    t = x_ref.shape[0]
    half = wo_ref.shape[0] // 2
    step = pl.program_id(0)

    @pl.when(step == 0)
    def _():
        carry_ref[...] = jnp.zeros_like(carry_ref)
        lg_ref[...] = jnp.zeros_like(lg_ref)

    logits_prev = lg_ref[...]
    h = x_ref[...] + _dot(yg_ref[...], wo_ref[:half]) + _dot(ym_ref[...], wo_ref[half:])
    h_ref[...] = h
    hn = _rms(h, fnw_ref[...])
    hp = hn.shape[1] // 2
    hn_ref[...] = _pack_bf16_pair(hn[:, :hp], hn[:, hp:])

    hn_hi = hn.astype(BF16)
    hn_lo = (hn - hn_hi.astype(F32)).astype(BF16)
    parts = _dot(hn_hi, wr_ref[...]) + _dot(hn_lo, wr_ref[...])
    lg_ref[...] = parts[:, :LANES] + parts[:, LANES:] + br_ref[...]

    live = (step > 0).astype(F32)
    tg = ROUTE_ROWS
    lane = lax.broadcasted_iota(jnp.int32, (tg, LANES), 1)
    lane_f = lane.astype(F32)
    rr = lax.broadcasted_iota(jnp.int32, (tg, tg), 0)
    cc = lax.broadcasted_iota(jnp.int32, (tg, tg), 1)
    earlier = (cc < rr).astype(BF16)

    def first_argmax(vals, vmax):
        idx = jnp.min(jnp.where(vals == vmax, lane_f, float(LANES)), axis=-1, keepdims=True)
        return idx.astype(jnp.int32)

    carry = carry_ref[...]
    for g in range(t // tg):
        rows = slice(g * tg, (g + 1) * tg)
        logits = logits_prev[rows]
        gl = jnp.where((lane >= N_EXPERTS) & (lane < N_EXPERTS + N_GROUPS), logits, NEG_INF)
        gmax = jnp.max(gl, axis=-1, keepdims=True)
        gsel = first_argmax(gl, gmax) - N_EXPERTS
        p_g = 1.0 / jnp.sum(jnp.exp(gl - gmax), axis=-1, keepdims=True)
        lo = gsel * EXPERTS_PER_GROUP
        el = jnp.where((lane >= lo) & (lane < lo + EXPERTS_PER_GROUP), logits, NEG_INF)
        m1 = jnp.max(el, axis=-1, keepdims=True)
        i1 = first_argmax(el, m1)
        el2 = jnp.where(lane == i1, NEG_INF, el)
        m2 = jnp.max(el2, axis=-1, keepdims=True)
        i2 = first_argmax(el2, m2)
        e2 = jnp.exp(m2 - m1)
        g1 = p_g / (1.0 + e2)
        g2 = p_g * e2 / (1.0 + e2)

        is1 = lane == i1
        is2 = lane == i2
        onehot = (is1 | is2).astype(BF16)
        before = _dot(earlier, onehot) + carry
        r1 = jnp.sum(jnp.where(is1, before, 0.0), axis=-1, keepdims=True)
        r2 = jnp.sum(jnp.where(is2, before, 0.0), axis=-1, keepdims=True)
        carry = carry + live * jnp.sum(onehot.astype(F32), axis=0, keepdims=True)

        route = jnp.where(lane == 0, i1.astype(F32), 0.0)
        route = jnp.where(lane == 1, i2.astype(F32), route)
        route = jnp.where(lane == 2, g1, route)
        route = jnp.where(lane == 3, g2, route)
        route = jnp.where(lane == 4, r1, route)
        route = jnp.where(lane == 5, r2, route)
        route_ref[rows] = route
        route_t_ref[:, rows] = route.T[:ROUTE_FIELDS]
    carry_ref[...] = carry
    cnt_ref[...] = carry


def _dest_kernel(pstart_ref, route_t_ref, dest_ref):
    eid = route_t_ref[0:TOP_K, :].astype(jnp.int32)
    rank = route_t_ref[4:4 + TOP_K, :].astype(jnp.int32)
    start = jnp.zeros_like(eid)
    for e in range(N_EXPERTS):
        start = jnp.where(eid == e, pstart_ref[e], start)
    dest_ref[...] = start + rank


def _sc_mesh():
    return plsc.VectorSubcoreMesh(core_axis_name="c", subcore_axis_name="s",
                                  num_cores=SC_CORES, num_subcores=SC_SUBCORES)


def _sc_worker_base(per_worker):
    return (lax.axis_index("s") * SC_CORES + lax.axis_index("c")) * per_worker


def _sc_scatter_rows(rows, idx_list, nrows):
    m, w = rows.shape
    nidx = len(idx_list)
    assert m % (SC_WORKERS * SC_CHUNK) == 0, m
    per_worker = m // SC_WORKERS
    nchunk = per_worker // SC_CHUNK

    @functools.partial(
        pl.kernel, mesh=_sc_mesh(), out_type=jax.ShapeDtypeStruct((nrows, w), rows.dtype),
        scratch_types=[pltpu.VMEM((SC_CHUNK,), jnp.int32)] * nidx
        + [pltpu.VMEM((SC_CHUNK, w), rows.dtype), pltpu.SemaphoreType.DMA],
        name="sc_scatter_rows")
    def scatter(rows_hbm, *refs):
        idx_hbm, out_hbm = refs[:nidx], refs[nidx]
        idx_v, rows_v, sem = refs[nidx + 1:2 * nidx + 1], refs[2 * nidx + 1], refs[2 * nidx + 2]
        base = _sc_worker_base(per_worker)

        @pl.loop(0, nchunk)
        def _(ci):
            off = pl.multiple_of(base + ci * SC_CHUNK, SC_CHUNK)
            pltpu.sync_copy(rows_hbm.at[pl.ds(off, SC_CHUNK)], rows_v)
            for kk in range(nidx):
                pltpu.sync_copy(idx_hbm[kk].at[pl.ds(off, SC_CHUNK)], idx_v[kk])
            copies = [pltpu.async_copy(rows_v, out_hbm.at[idx_v[kk]], sem) for kk in range(nidx)]
            for c in copies:
                c.wait()

    return scatter(rows, *idx_list)


def _expert_kernel(be_ref, nu_ref, bv_ref, x_ref, wg_ref, wu_ref, wd_ref, y_ref, wgu_s, wd_s):
    j = pl.program_id(0)
    de = wg_ref.shape[2]
    last = jnp.maximum(nu_ref[0] - 1, 0)
    cur = be_ref[jnp.minimum(j, last)]
    prev = be_ref[jnp.minimum(jnp.maximum(j - 1, 0), last)]

    @pl.when((j == 0) | (cur != prev))
    def _():
        wgu_s[:, :de] = wg_ref[0].astype(BF16)
        wgu_s[:, de:] = wu_ref[0].astype(BF16)
        wd_s[...] = wd_ref[0].astype(BF16)

    @pl.when(j < nu_ref[0])
    def _():
        row = lax.broadcasted_iota(jnp.int32, x_ref.shape, 0)
        xu = jnp.where(row < bv_ref[j], x_ref[...], jnp.uint32(0))
        a, b = _unpack_bf16_pair(xu)
        x = jnp.concatenate([a, b], axis=1).astype(BF16)
        h12 = _dot(x, wgu_s[...])
        h1, h2 = h12[:, :de], h12[:, de:]
        hdn = (h1 * jax.nn.sigmoid(h1) * h2).astype(BF16)
        y = _dot(hdn, wd_s[...])
        hp = y.shape[1] // 2
        y_ref[...] = _pack_bf16_pair(y[:, :hp], y[:, hp:])

    @pl.when(pl.program_id(0) >= nu_ref[0])
    def _():
        y_ref[...] = jnp.zeros_like(y_ref)


def _sc_gather_rows(table, idx):
    m = idx.shape[0]
    w = table.shape[1]
    assert m % (SC_WORKERS * SC_CHUNK) == 0, m
    per_worker = m // SC_WORKERS
    nchunk = per_worker // SC_CHUNK

    @functools.partial(
        pl.kernel, mesh=_sc_mesh(), out_type=jax.ShapeDtypeStruct((m, w), table.dtype),
        scratch_types=[pltpu.VMEM((SC_CHUNK,), jnp.int32), pltpu.VMEM((SC_CHUNK, w), table.dtype),
                       pltpu.SemaphoreType.DMA],
        name="sc_gather_rows")
    def gather(table_hbm, idx_hbm, out_hbm, idx_v, rows_v, sem):
        base = _sc_worker_base(per_worker)

        @pl.loop(0, nchunk)
        def _(ci):
            off = pl.multiple_of(base + ci * SC_CHUNK, SC_CHUNK)
            pltpu.sync_copy(idx_hbm.at[pl.ds(off, SC_CHUNK)], idx_v)
            pltpu.async_copy(table_hbm.at[idx_v], rows_v, sem).wait()
            pltpu.sync_copy(rows_v, out_hbm.at[pl.ds(off, SC_CHUNK)])

    return gather(table, idx)


def _combine_kernel(h_ref, route_ref, fw_ref, y0_ref, y1_ref, o_ref):
    route = route_ref[...]
    g1 = route[:, 2:3]
    g2 = route[:, 3:4]
    a0, b0 = _unpack_bf16_pair(y0_ref[...])
    a1, b1 = _unpack_bf16_pair(y1_ref[...])
    moe = jnp.concatenate([a0 * g1 + a1 * g2, b0 * g1 + b1 * g2], axis=1)
    o_ref[...] = _rms(h_ref[...] + moe, fw_ref[...])


def _combine_into_kernel(h_ref, route_ref, fw_ref, y0_ref, y1_ref, carried_ref, o_ref):
    del carried_ref
    _combine_kernel(h_ref, route_ref, fw_ref, y0_ref, y1_ref, o_ref)


def _params(*sem):
    return pltpu.CompilerParams(dimension_semantics=sem, vmem_limit_bytes=VMEM_LIMIT)


def _full(shape):
    return pl.BlockSpec(shape, lambda *_: (0,) * len(shape))


def _rows(tile, width):
    return pl.BlockSpec((tile, width), lambda i: (i, 0))


def _layer(x2, pos2, attn_norm_w, w_in, gla_gate_up, gla_gate_bias, gla_norm_w, mla_q_norm_w,
           mla_w_uq, mla_kv_norm_w, mla_w_ukv, w_out, ffn_norm_w, router_group_w, router_group_b,
           router_expert_w, router_expert_b, expert_w_gate, expert_w_up, expert_w_down,
           out_norm_w, batch, seq):
    n, d = x2.shape
    dq = GLA_HEADS * GLA_DK
    dv = GLA_HEADS * GLA_DV
    dmla = MLA_HEADS * HEAD_PAD
    dmv = MLA_HEADS * MLA_DV

    c_gq, c_gk, c_gv, c_lr, c_og, c_cq, c_ckv, c_kr = jnp.split(
        w_in, [dq, 2 * dq, 2 * dq + dv, 2 * dq + dv + GLA_GATE_RANK,
               2 * dq + 2 * dv + GLA_GATE_RANK,
               2 * dq + 2 * dv + GLA_GATE_RANK + MLA_Q_RANK,
               2 * dq + 2 * dv + GLA_GATE_RANK + MLA_Q_RANK + MLA_KV_RANK], axis=1)
    zeros = lambda r, c: jnp.zeros((r, c), w_in.dtype)
    misc = jnp.concatenate([zeros(d, MLA_NOPE), c_kr, c_lr,
                            zeros(d, LANES - ROPE_END - GLA_GATE_RANK)], axis=1)
    w1 = jnp.concatenate([c_gq, c_gk, c_gv, c_og, c_cq, c_ckv, misc], axis=1).astype(BF16)
    gate_up = jnp.concatenate([zeros(GATE_LO, dq), gla_gate_up,
                               zeros(LANES - GATE_LO - GLA_GATE_RANK, dq)], axis=0).astype(BF16)
    wuq = mla_w_uq.reshape(MLA_Q_RANK, MLA_HEADS, MLA_NOPE + MLA_ROPE)
    wuq = jnp.pad(wuq, ((0, 0), (0, 0), (0, HEAD_PAD - ROPE_END))).reshape(MLA_Q_RANK, dmla)
    wukv = mla_w_ukv.reshape(MLA_KV_RANK, MLA_HEADS, MLA_NOPE + MLA_DV)
    wuk = jnp.pad(wukv[:, :, :MLA_NOPE], ((0, 0), (0, 0), (0, HEAD_PAD - MLA_NOPE)))
    wuv = jnp.pad(wukv[:, :, MLA_NOPE:], ((0, 0), (0, 0), (0, HEAD_PAD - MLA_DV)))
    wukv = jnp.concatenate([wuk.reshape(MLA_KV_RANK, dmla), wuv.reshape(MLA_KV_RANK, dmla)],
                           axis=1)
    inv = ROPE_THETA ** (-jnp.arange(ROPE_HALF, dtype=F32) / ROPE_HALF)
    inv_pat = jnp.tile(inv, ROPE_PACK).reshape(1, LANES)
    w_router = jnp.concatenate(
        [router_expert_w, router_group_w, zeros(d, LANES - N_EXPERTS - N_GROUPS)], axis=1)
    w_router_hi = w_router.astype(BF16)
    w_router = jnp.concatenate(
        [w_router_hi, (w_router - w_router_hi.astype(F32)).astype(BF16)], axis=1)
    b_router = jnp.concatenate(
        [router_expert_b, router_group_b, jnp.zeros((LANES - N_EXPERTS - N_GROUPS,), F32)]
    ).reshape(1, LANES)
    row1 = lambda v: v.reshape(1, -1)

    tp = PROJ_TILE
    outs = pl.pallas_call(
        functools.partial(_proj_kernel, q_scale=(MLA_NOPE + MLA_ROPE) ** -0.5 * LOG2_E),
        grid=(n // tp,),
        in_specs=[_rows(tp, d), _rows(tp // ROPE_PACK, ROPE_PACK), _full((1, d)), _full(w1.shape),
                  _full(gate_up.shape),
                  _full((1, dq)), _full((1, MLA_Q_RANK)), _full(wuq.shape),
                  _full((1, MLA_KV_RANK)), _full(wukv.shape), _full((1, LANES))],
        out_specs=[_rows(tp, dq), _rows(tp, dq), _rows(tp, dv), _rows(tp, dq), _rows(tp, dv),
                   _rows(tp, dmla), _rows(tp, dmla), _rows(tp, dmla)],
        out_shape=[jax.ShapeDtypeStruct((n, dq), BF16), jax.ShapeDtypeStruct((n, dq), BF16),
                   jax.ShapeDtypeStruct((n, dv), BF16), jax.ShapeDtypeStruct((n, dq), F32),
                   jax.ShapeDtypeStruct((n, dv), BF16), jax.ShapeDtypeStruct((n, dmla), BF16),
                   jax.ShapeDtypeStruct((n, dmla), BF16), jax.ShapeDtypeStruct((n, dmla), BF16)],
        scratch_shapes=[pltpu.VMEM((tp, LANES), F32)] * 2,
        compiler_params=_params("parallel"),
        name="in_proj",
    )(x2, pos2, row1(attn_norm_w), w1, gate_up, row1(gla_gate_bias), row1(mla_q_norm_w),
      wuq.astype(BF16), row1(mla_kv_norm_w), wukv.astype(BF16), inv_pat)
    gq, gk, gv, gla, gog, q, k, v = outs

    tg = GLA_TILE
    seq3 = lambda a: a.reshape(batch, seq, a.shape[-1])
    gspec = lambda w: pl.BlockSpec((1, tg, w), lambda b, i: (b, i, 0))
    y_gla = pl.pallas_call(
        _gla_kernel,
        grid=(batch, seq // tg),
        in_specs=[gspec(dq), gspec(dq), gspec(dv), gspec(dq), gspec(dv), _full((1, GLA_DV))],
        out_specs=gspec(dv),
        out_shape=jax.ShapeDtypeStruct((batch, seq, dv), BF16),
        scratch_shapes=[pltpu.VMEM((GLA_HEADS, GLA_DV, GLA_DK), F32)],
        compiler_params=_params("parallel", "arbitrary"),
        name="gla",
    )(seq3(gq), seq3(gk), seq3(gv), seq3(gla), seq3(gog), row1(gla_norm_w))

    ta = ATT_TILE
    hps = ATT_HEADS_PER_STEP
    y_mla = pl.pallas_call(
        functools.partial(_mla_kernel, tq=ta),
        grid=(batch, MLA_HEADS // hps, seq // ta),
        in_specs=[pl.BlockSpec((1, seq, hps * HEAD_PAD), lambda b, hp, i: (b, 0, hp))] * 3,
        out_specs=pl.BlockSpec((1, ta, hps * MLA_DV), lambda b, hp, i: (b, i, hp)),
        out_shape=jax.ShapeDtypeStruct((batch, seq, dmv), BF16),
        scratch_shapes=[pltpu.VMEM((hps, ta, ta), F32)] * 3,
        compiler_params=_params("parallel", "parallel", "arbitrary"),
        name="mla",
    )(seq3(q), seq3(k), seq3(v))

    tr = ROUTER_TILE
    ntr = n // tr
    cur = lambda t, w: pl.BlockSpec((t, w), lambda i: (jnp.minimum(i, ntr - 1), 0))
    prev = lambda t, w: pl.BlockSpec((t, w), lambda i: (jnp.maximum(i - 1, 0), 0))
    h, hn, route, route_t, counts = pl.pallas_call(
        _router_kernel,
        grid=(ntr + 1,),
        in_specs=[cur(tr, d), cur(tr, dv), cur(tr, dmv), _full(w_out.shape), _full((1, d)),
                  _full(w_router.shape), _full((1, LANES))],
        out_specs=[cur(tr, d), cur(tr, d // 2), prev(tr, LANES),
                   pl.BlockSpec((ROUTE_FIELDS, tr), lambda i: (0, jnp.maximum(i - 1, 0))),
                   _full((1, LANES))],
        out_shape=[jax.ShapeDtypeStruct((n, d), F32), jax.ShapeDtypeStruct((n, d // 2), jnp.uint32),
                   jax.ShapeDtypeStruct((n, LANES), F32),
                   jax.ShapeDtypeStruct((ROUTE_FIELDS, n), F32),
                   jax.ShapeDtypeStruct((1, LANES), F32)],
        scratch_shapes=[pltpu.VMEM((1, LANES), F32), pltpu.VMEM((tr, LANES), F32)],
        compiler_params=_params("arbitrary"),
        name="out_proj_router",
    )(x2, y_gla.reshape(n, dv), y_mla.reshape(n, dmv), w_out.astype(BF16), row1(ffn_norm_w),
      w_router, b_router)

    blk = EXPERT_BLOCK
    nblk = (n * TOP_K) // blk + N_EXPERTS
    cnt = counts[0, :N_EXPERTS].astype(jnp.int32)
    padded = (cnt + blk - 1) // blk * blk
    pend = jnp.cumsum(padded)
    pstart = pend - padded
    dest = pl.pallas_call(
        _dest_kernel,
        grid_spec=pltpu.PrefetchScalarGridSpec(
            num_scalar_prefetch=1, grid=(1,),
            in_specs=[pl.BlockSpec((ROUTE_FIELDS, n), lambda i, ps: (0, 0))],
            out_specs=pl.BlockSpec((TOP_K, n), lambda i, ps: (0, 0))),
        out_shape=jax.ShapeDtypeStruct((TOP_K, n), jnp.int32),
        compiler_params=_params("arbitrary"),
        name="dest_rows",
    )(pstart, route_t)
    blk_start = jnp.arange(nblk, dtype=jnp.int32) * blk
    blk_expert = jnp.minimum(
        jnp.sum((pend[None, :] <= blk_start[:, None]).astype(jnp.int32), axis=1), N_EXPERTS - 1)
    n_used = (pend[-1] // blk).astype(jnp.int32).reshape(1)
    blk_valid = jnp.clip(cnt[blk_expert] - (blk_start - pstart[blk_expert]), 0, blk)
    blk_valid = jnp.where(jnp.arange(nblk) < n_used[0], blk_valid, 0).astype(jnp.int32)
    dest_slots = [dest[kk] for kk in range(TOP_K)]

    xbuf = _sc_scatter_rows(hn, dest_slots, nblk * blk)

    def used(j, nu):
        return jnp.maximum(jnp.minimum(j, nu[0] - 1), 0)

    ybuf = pl.pallas_call(
        _expert_kernel,
        grid_spec=pltpu.PrefetchScalarGridSpec(
            num_scalar_prefetch=3,
            grid=(nblk,),
            in_specs=[
                pl.BlockSpec((blk, d // 2), lambda j, be, nu, bv: (used(j, nu), 0)),
                pl.BlockSpec((1, d, D_EXPERT), lambda j, be, nu, bv: (be[used(j, nu)], 0, 0)),
                pl.BlockSpec((1, d, D_EXPERT), lambda j, be, nu, bv: (be[used(j, nu)], 0, 0)),
                pl.BlockSpec((1, D_EXPERT, d), lambda j, be, nu, bv: (be[used(j, nu)], 0, 0)),
            ],
            out_specs=pl.BlockSpec((blk, d // 2), lambda j, be, nu, bv: (j, 0)),
            scratch_shapes=[pltpu.VMEM((d, 2 * D_EXPERT), BF16), pltpu.VMEM((D_EXPERT, d), BF16)],
        ),
        out_shape=jax.ShapeDtypeStruct((nblk * blk, d // 2), jnp.uint32),
        compiler_params=_params("arbitrary"),
        name="experts",
    )(blk_expert, n_used, blk_valid, xbuf, expert_w_gate, expert_w_up, expert_w_down)

    tc = COMBINE_TILE
    parts = COMBINE_PARTS
    npart = n // parts
    steps = npart // tc
    out = None
    for p in range(parts):
        part_dest = dest[:, p * npart:(p + 1) * npart].reshape(TOP_K * npart)
        gathered = _sc_gather_rows(ybuf, part_dest)
        here = lambda w, p=p: pl.BlockSpec((tc, w), lambda i: (i + p * steps, 0))
        in_specs = [here(d), here(LANES), _full((1, d)), _rows(tc, d // 2),
                    pl.BlockSpec((tc, d // 2), lambda i: (i + steps, 0))]
        args = [h, route, row1(out_norm_w), gathered, gathered]
        if out is not None:
            in_specs.append(pl.BlockSpec(memory_space=pl.ANY))
            args.append(out)
        out = pl.pallas_call(
            _combine_kernel if out is None else _combine_into_kernel,
            grid=(steps,),
            in_specs=in_specs,
            out_specs=here(d),
            out_shape=jax.ShapeDtypeStruct((n, d), F32),
            input_output_aliases={} if out is None else {len(args) - 1: 0},
            compiler_params=_params("parallel"),
            name=f"combine_{p}",
        )(*args)
    return out


def kernel(x, positions, attn_norm_w, w_in, gla_gate_up, gla_gate_bias, gla_norm_w, mla_q_norm_w,
           mla_w_uq, mla_kv_norm_w, mla_w_ukv, w_out, ffn_norm_w, router_group_w, router_group_b,
           router_expert_w, router_expert_b, expert_w_gate, expert_w_up, expert_w_down,
           final_norm_w):
    batch, seq, d = x.shape
    depth = w_in.shape[0]
    assert depth == 1, "the final norm is fused into the last layer's combine step"
    n = batch * seq
    assert seq % GLA_TILE == 0 and seq % ATT_TILE == 0, (seq, GLA_TILE, ATT_TILE)
    assert n % ROUTER_TILE == 0 and n % PROJ_TILE == 0, n
    assert n % (COMBINE_TILE * COMBINE_PARTS) == 0, n
    out = _layer(x.reshape(batch * seq, d), positions.reshape(batch * seq // ROPE_PACK, ROPE_PACK),
                 attn_norm_w[0], w_in[0], gla_gate_up[0], gla_gate_bias[0], gla_norm_w[0],
                 mla_q_norm_w[0], mla_w_uq[0], mla_kv_norm_w[0], mla_w_ukv[0], w_out[0],
                 ffn_norm_w[0], router_group_w[0], router_group_b[0], router_expert_w[0],
                 router_expert_b[0], expert_w_gate[0], expert_w_up[0], expert_w_down[0],
                 final_norm_w, batch, seq)
    return out.reshape(batch, seq, d)
```

```python
import functools

import jax
import jax.numpy as jnp
from jax import lax
from jax.experimental import pallas as pl
from jax.experimental.pallas import tpu as pltpu
from jax.experimental.pallas import tpu_sc as plsc

EPS = 1e-6
GLA_HEADS = 4
GLA_DK = 64
GLA_DV = 128
GLA_GATE_RANK = 16
GLA_GATE_NORM = 16.0
GLA_CHUNK = 64
MLA_HEADS = 8
MLA_NOPE = 64
MLA_ROPE = 32
MLA_DV = 64
MLA_Q_RANK = 384
MLA_KV_RANK = 256
ROPE_THETA = 10000.0
N_GROUPS = 4
EXPERTS_PER_GROUP = 8
N_EXPERTS = N_GROUPS * EXPERTS_PER_GROUP
TOP_K = 2
D_EXPERT = 256

LANES = 128
HEAD_PAD = 128
ROPE_HALF = MLA_ROPE // 2
ROPE_LO = MLA_NOPE
ROPE_HI = MLA_NOPE + ROPE_HALF
ROPE_END = MLA_NOPE + MLA_ROPE
ROPE_PACK = LANES // ROPE_HALF
GATE_LO = ROPE_END

PROJ_TILE = 512
COMBINE_TILE = 1024
COMBINE_PARTS = 8
GLA_TILE = 2048
GLA_SUBTILE = 128
ATT_TILE = 512
ATT_HEADS_PER_STEP = 4
EXPERT_BLOCK = 512
ROUTER_TILE = 1024
ROUTE_ROWS = 512
ROUTE_FIELDS = 8
VMEM_LIMIT = 56 * 1024 * 1024
SC_CORES = 2
SC_SUBCORES = 16
SC_WORKERS = SC_CORES * SC_SUBCORES
SC_CHUNK = 128

F32 = jnp.float32
BF16 = jnp.bfloat16
NEG_INF = float("-inf")
LOG2_E = 1.4426950408889634


def _dot(a, b, precision=None):
    return jnp.dot(a, b, preferred_element_type=F32, precision=precision)


def _dot_nt(a, b):
    return lax.dot_general(a, b, (((1,), (1,)), ((), ())), preferred_element_type=F32)


def _dot_tn(a, b):
    return lax.dot_general(a, b, (((0,), (0,)), ((), ())), preferred_element_type=F32)


def _rms(x, w):
    return x * lax.rsqrt(jnp.mean(x * x, axis=-1, keepdims=True) + EPS) * w


def _pack_bf16_pair(a, b):
    ua = lax.bitcast_convert_type(a.astype(BF16).astype(F32), jnp.uint32)
    ub = lax.bitcast_convert_type(b.astype(BF16).astype(F32), jnp.uint32)
    return (ua >> 16) | (ub & jnp.uint32(0xFFFF0000))


def _unpack_bf16_pair(u):
    a = lax.bitcast_convert_type(u << 16, F32)
    b = lax.bitcast_convert_type(u & jnp.uint32(0xFFFF0000), F32)
    return a, b


def _proj_kernel(x_ref, pos_ref, nw_ref, w1_ref, gu_ref, gb_ref, qnw_ref, wuq_ref, kvnw_ref,
                 wukv_ref, inv_ref,
                 gq_ref, gk_ref, gv_ref, gla_ref, gog_ref, q_ref, k_ref, v_ref, cos_ref, sin_ref,
                 *, q_scale):
    x = x_ref[...]
    xn = _rms(x, nw_ref[...]).astype(BF16)
    dq = GLA_HEADS * GLA_DK
    dv = GLA_HEADS * GLA_DV
    o = 0

    def proj(width):
        nonlocal o
        res = _dot(xn, w1_ref[:, o:o + width])
        o += width
        return res

    gq_ref[...] = (proj(dq) * (GLA_DK ** -0.5)).astype(BF16)
    gk_ref[...] = proj(dq).astype(BF16)
    gv_ref[...] = proj(dv).astype(BF16)
    gog_ref[...] = proj(dv).astype(BF16)
    cq = proj(MLA_Q_RANK)
    ckv = proj(MLA_KV_RANK)
    misc = proj(LANES)

    z = _dot(misc.astype(BF16), gu_ref[...]) + gb_ref[...]
    log_sig = jnp.minimum(z, 0.0) - jnp.log1p(jnp.exp(-jnp.abs(z)))
    gla_ref[...] = log_sig * (1.0 / GLA_GATE_NORM)

    lane = lax.broadcasted_iota(jnp.int32, (x.shape[0], LANES), 1)
    tpk = pos_ref.shape[0]
    lane_p = lax.broadcasted_iota(jnp.int32, (tpk, LANES), 1)
    pos_p = jnp.zeros((tpk, LANES), F32)
    for g in range(ROPE_PACK):
        pos_p = jnp.where((lane_p >> 4) == g, pos_ref[:, g:g + 1].astype(F32), pos_p)
    ang = pos_p * inv_ref[...]
    cos_p = jnp.cos(ang)
    sin_p = jnp.sin(ang)
    for g in range(ROPE_PACK):
        shift = (ROPE_LO - ROPE_HALF * g) % LANES
        cos_ref[pl.ds(g, tpk, stride=ROPE_PACK), :] = pltpu.roll(cos_p, shift, 1)
        sin_ref[pl.ds(g, tpk, stride=ROPE_PACK), :] = pltpu.roll(sin_p, shift, 1)
    cos_lo = cos_ref[...]
    sin_lo = sin_ref[...]
    in_lo = (lane >= ROPE_LO) & (lane < ROPE_HI)
    in_hi = (lane >= ROPE_HI) & (lane < ROPE_END)
    c_rope = jnp.where(in_lo, cos_lo, jnp.where(in_hi, pltpu.roll(cos_lo, ROPE_HALF, 1), 0.0))
    s_up = jnp.where(in_hi, pltpu.roll(sin_lo, ROPE_HALF, 1), 0.0)
    s_dn = jnp.where(in_lo, -sin_lo, 0.0)
    c_q = jnp.where(lane < MLA_NOPE, 1.0, c_rope)

    def rope(t, c):
        return (t * c + pltpu.roll(t, ROPE_HALF, 1) * s_up
                + pltpu.roll(t, LANES - ROPE_HALF, 1) * s_dn)

    k_rope = rope(misc, c_rope)

    q = _dot(_rms(cq, qnw_ref[...]).astype(BF16), wuq_ref[...])
    kv = _dot(_rms(ckv, kvnw_ref[...]).astype(BF16), wukv_ref[...])
    ones_lane = jnp.where(lane == MLA_DV, 1.0, 0.0)
    for h in range(MLA_HEADS):
        sl = slice(h * HEAD_PAD, (h + 1) * HEAD_PAD)
        vsl = slice((MLA_HEADS + h) * HEAD_PAD, (MLA_HEADS + h + 1) * HEAD_PAD)
        q_ref[:, sl] = (rope(q[:, sl], c_q) * q_scale).astype(BF16)
        k_ref[:, sl] = (kv[:, sl] + k_rope).astype(BF16)
        v_ref[:, sl] = (kv[:, vsl] + ones_lane).astype(BF16)


def _gla_kernel(q_ref, k_ref, v_ref, la_ref, og_ref, nw_ref, o_ref, st_ref):
    t = GLA_SUBTILE
    nchunk = t // GLA_CHUNK

    @pl.when(pl.program_id(1) == 0)
    def _():
        st_ref[...] = jnp.zeros_like(st_ref)

    row = lax.broadcasted_iota(jnp.int32, (t, t), 0)
    col = lax.broadcasted_iota(jnp.int32, (t, t), 1)
    chunk_bits = GLA_CHUNK.bit_length() - 1
    tri = ((row >> chunk_bits) == (col >> chunk_bits)) & (col <= row)
    tri_b = tri.astype(BF16)
    nw = nw_ref[...]
    states = [st_ref[h] for h in range(GLA_HEADS)]

    for sub in range(q_ref.shape[1] // t):
        tr = slice(sub * t, (sub + 1) * t)
        la = la_ref[0, tr]
        la_hi = la.astype(BF16)
        la_lo = (la - la_hi.astype(F32)).astype(BF16)
        parts = _dot(tri_b, jnp.concatenate([la_hi, la_lo], axis=1))
        b = parts[:, :la.shape[1]] + parts[:, la.shape[1]:]
        b_last = jnp.concatenate(
            [jnp.broadcast_to(b[(c + 1) * GLA_CHUNK - 1:(c + 1) * GLA_CHUNK],
                              (GLA_CHUNK, b.shape[1])) for c in range(nchunk)], axis=0)
        q_e = (q_ref[0, tr].astype(F32) * jnp.exp(b)).astype(BF16)
        kf = k_ref[0, tr].astype(F32)
        k_e = (kf * jnp.exp(-b)).astype(BF16)
        k_d = (kf * jnp.exp(b_last - b)).astype(BF16)
        decay = jnp.exp(b_last)

        for h in range(GLA_HEADS):
            ks = slice(h * GLA_DK, (h + 1) * GLA_DK)
            vs = slice(h * GLA_DV, (h + 1) * GLA_DV)
            qh, keh, kdh = q_e[:, ks], k_e[:, ks], k_d[:, ks]
            vh = v_ref[0, tr, vs]
            att = jnp.where(tri, _dot_nt(qh, keh), 0.0)
            o = _dot(att.astype(BF16), vh)
            state = states[h]
            inter = []
            for c in range(nchunk):
                rs = slice(c * GLA_CHUNK, (c + 1) * GLA_CHUNK)
                inter.append(_dot_nt(qh[rs], state.astype(BF16)))
                upd = _dot_tn(vh[rs], kdh[rs])
                state = state * decay[c * GLA_CHUNK:c * GLA_CHUNK + 1, ks] + upd
            states[h] = state
            o = o + jnp.concatenate(inter, axis=0)
            o = _rms(o, nw)
            g = og_ref[0, tr, vs].astype(F32)
            o_ref[0, tr, vs] = (o * (g * jax.nn.sigmoid(g))).astype(o_ref.dtype)

    for h in range(GLA_HEADS):
        st_ref[h] = states[h]


def _mla_kernel(q_ref, k_ref, v_ref, o_ref, sh_ref, s0_ref, s1_ref, *, tq):
    heads = q_ref.shape[2] // HEAD_PAD
    nq = q_ref.shape[1] // tq
    qi = pl.program_id(2)
    lane = lax.broadcasted_iota(jnp.int32, (tq, LANES), 1)
    causal = (lax.broadcasted_iota(jnp.int32, (tq, tq), 1)
              <= lax.broadcasted_iota(jnp.int32, (tq, tq), 0))

    def rows(i):
        return pl.ds(pl.multiple_of(i * tq, tq), tq)

    def produce(qblk, j, dst):
        for hh in range(heads):
            hs = slice(hh * HEAD_PAD, (hh + 1) * HEAD_PAD)
            dst[hh] = _dot_nt(q_ref[0, rows(qblk), hs], k_ref[0, rows(j), hs])

    def produce_diagonal(qblk, dst):
        th = tq // 2
        q0 = pl.multiple_of(qblk * tq, tq)
        for hh in range(heads):
            hs = slice(hh * HEAD_PAD, (hh + 1) * HEAD_PAD)
            dst[hh, :th, :th] = _dot_nt(q_ref[0, pl.ds(q0, th), hs], k_ref[0, pl.ds(q0, th), hs])
            dst[hh, th:, :] = _dot_nt(q_ref[0, pl.ds(q0 + th, th), hs], k_ref[0, rows(qblk), hs])

    def update(s, m, acc, v):
        m_new = jnp.maximum(m, jnp.max(s, axis=-1, keepdims=True))
        p = jnp.exp2(s - m_new).astype(BF16)
        return m_new, jnp.exp2(m - m_new) * acc + _dot(p, v)

    def consume(j, src, carry, diagonal):
        new = []
        for hh in range(heads):
            m, acc = carry[hh]
            hs = slice(hh * HEAD_PAD, (hh + 1) * HEAD_PAD)
            if not diagonal:
                new.append(update(src[hh], m, acc, v_ref[0, rows(j), hs]))
                continue
            th = tq // 2
            k0 = pl.multiple_of(j * tq, tq)
            top = update(jnp.where(causal[:th, :th], src[hh, :th, :th], NEG_INF),
                         m[:th], acc[:th], v_ref[0, pl.ds(k0, th), hs])
            bot = update(jnp.where(causal[th:], src[hh, th:, :], NEG_INF),
                         m[th:], acc[th:], v_ref[0, pl.ds(k0, tq), hs])
            new.append(tuple(jnp.concatenate([a, b], axis=0) for a, b in zip(top, bot)))
        return tuple(new)

    def finish(carry):
        for hp in range(heads // 2):
            o0, o1 = (acc / acc[:, MLA_DV:MLA_DV + 1] for _, acc in carry[2 * hp:2 * hp + 2])
            o_ref[0, :, hp * LANES:(hp + 1) * LANES] = jnp.where(
                lane < MLA_DV, o0, pltpu.roll(o1, MLA_DV, 1)).astype(o_ref.dtype)

    init = tuple((jnp.full((tq, 1), NEG_INF, F32), jnp.zeros((tq, LANES), F32))
                 for _ in range(heads))
    nxt = jnp.minimum(qi + 1, nq - 1)

    @pl.when(qi == 0)
    def _():
        produce_diagonal(0, sh_ref)
        carry = consume(0, sh_ref, init, True)
        produce(nxt, 0, sh_ref)
        finish(carry)

    @pl.when(qi > 0)
    def _():
        produce(qi, 1, s1_ref)
        carry = consume(0, sh_ref, init, False)

        def pair(i, carry):
            j = 1 + 2 * i
            produce(qi, j + 1, s0_ref)
            carry = consume(j, s1_ref, carry, False)
            produce(qi, j + 2, s1_ref)
            return consume(j + 1, s0_ref, carry, False)

        carry = lax.fori_loop(0, lax.shift_right_logical(qi - 1, 1), pair, carry)
        even = (qi & 1) == 0

        @pl.when(even)
        def _():
            produce_diagonal(qi, s0_ref)
            c = consume(qi - 1, s1_ref, carry, False)
            produce(nxt, 0, sh_ref)
            finish(consume(qi, s0_ref, c, True))

        @pl.when(jnp.logical_not(even))
        def _():
            produce(nxt, 0, sh_ref)
            finish(consume(qi, s1_ref, carry, True))


def _router_kernel(x_ref, yg_ref, ym_ref, wo_ref, fnw_ref, wr_ref, br_ref,
                   h_ref, hn_ref, route_ref, route_t_ref, cnt_ref, carry_ref, lg_ref):
    t = x_ref.shape[0]
    half = wo_ref.shape[0] // 2
    step = pl.program_id(0)

    @pl.when(step == 0)
    def _():
        carry_ref[...] = jnp.zeros_like(carry_ref)
        lg_ref[...] = jnp.zeros_like(lg_ref)

    logits_prev = lg_ref[...]
    h = x_ref[...] + _dot(yg_ref[...], wo_ref[:half]) + _dot(ym_ref[...], wo_ref[half:])
    h_ref[...] = h
    hn = _rms(h, fnw_ref[...])
    hp = hn.shape[1] // 2
    hn_ref[...] = _pack_bf16_pair(hn[:, :hp], hn[:, hp:])

    hn_hi = hn.astype(BF16)
    hn_lo = (hn - hn_hi.astype(F32)).astype(BF16)
    parts = _dot(hn_hi, wr_ref[...]) + _dot(hn_lo, wr_ref[...])
    lg_ref[...] = parts[:, :LANES] + parts[:, LANES:] + br_ref[...]

    live = (step > 0).astype(F32)
    tg = ROUTE_ROWS
    lane = lax.broadcasted_iota(jnp.int32, (tg, LANES), 1)
    lane_f = lane.astype(F32)
    rr = lax.broadcasted_iota(jnp.int32, (tg, tg), 0)
    cc = lax.broadcasted_iota(jnp.int32, (tg, tg), 1)
    earlier = (cc < rr).astype(BF16)

    def first_argmax(vals, vmax):
        idx = jnp.min(jnp.where(vals == vmax, lane_f, float(LANES)), axis=-1, keepdims=True)
        return idx.astype(jnp.int32)

    carry = carry_ref[...]
    for g in range(t // tg):
        rows = slice(g * tg, (g + 1) * tg)
        logits = logits_prev[rows]
        gl = jnp.where((lane >= N_EXPERTS) & (lane < N_EXPERTS + N_GROUPS), logits, NEG_INF)
        gmax = jnp.max(gl, axis=-1, keepdims=True)
        gsel = first_argmax(gl, gmax) - N_EXPERTS
        p_g = 1.0 / jnp.sum(jnp.exp(gl - gmax), axis=-1, keepdims=True)
        lo = gsel * EXPERTS_PER_GROUP
        el = jnp.where((lane >= lo) & (lane < lo + EXPERTS_PER_GROUP), logits, NEG_INF)
        m1 = jnp.max(el, axis=-1, keepdims=True)
        i1 = first_argmax(el, m1)
        el2 = jnp.where(lane == i1, NEG_INF, el)
        m2 = jnp.max(el2, axis=-1, keepdims=True)
        i2 = first_argmax(el2, m2)
        e2 = jnp.exp(m2 - m1)
        g1 = p_g / (1.0 + e2)
        g2 = p_g * e2 / (1.0 + e2)

        is1 = lane == i1
        is2 = lane == i2
        onehot = (is1 | is2).astype(BF16)
        before = _dot(earlier, onehot) + carry
        r1 = jnp.sum(jnp.where(is1, before, 0.0), axis=-1, keepdims=True)
        r2 = jnp.sum(jnp.where(is2, before, 0.0), axis=-1, keepdims=True)
        carry = carry + live * jnp.sum(onehot.astype(F32), axis=0, keepdims=True)

        route = jnp.where(lane == 0, i1.astype(F32), 0.0)
        route = jnp.where(lane == 1, i2.astype(F32), route)
        route = jnp.where(lane == 2, g1, route)
        route = jnp.where(lane == 3, g2, route)
        route = jnp.where(lane == 4, r1, route)
        route = jnp.where(lane == 5, r2, route)
        route_ref[rows] = route
        route_t_ref[:, rows] = route.T[:ROUTE_FIELDS]
    carry_ref[...] = carry
    cnt_ref[...] = carry


def _dest_kernel(pstart_ref, route_t_ref, dest_ref):
    eid = route_t_ref[0:TOP_K, :].astype(jnp.int32)
    rank = route_t_ref[4:4 + TOP_K, :].astype(jnp.int32)
    start = jnp.zeros_like(eid)
    for e in range(N_EXPERTS):
        start = jnp.where(eid == e, pstart_ref[e], start)
    dest_ref[...] = start + rank


def _sc_mesh():
    return plsc.VectorSubcoreMesh(core_axis_name="c", subcore_axis_name="s",
                                  num_cores=SC_CORES, num_subcores=SC_SUBCORES)


def _sc_worker_base(per_worker):
    return (lax.axis_index("s") * SC_CORES + lax.axis_index("c")) * per_worker


def _sc_scatter_rows(rows, idx_list, nrows):
    m, w = rows.shape
    nidx = len(idx_list)
    assert m % (SC_WORKERS * SC_CHUNK) == 0, m
    per_worker = m // SC_WORKERS
    nchunk = per_worker // SC_CHUNK

    @functools.partial(
        pl.kernel, mesh=_sc_mesh(), out_type=jax.ShapeDtypeStruct((nrows, w), rows.dtype),
        scratch_types=[pltpu.VMEM((SC_CHUNK,), jnp.int32)] * nidx
        + [pltpu.VMEM((SC_CHUNK, w), rows.dtype), pltpu.SemaphoreType.DMA],
        name="sc_scatter_rows")
    def scatter(rows_hbm, *refs):
        idx_hbm, out_hbm = refs[:nidx], refs[nidx]
        idx_v, rows_v, sem = refs[nidx + 1:2 * nidx + 1], refs[2 * nidx + 1], refs[2 * nidx + 2]
        base = _sc_worker_base(per_worker)

        @pl.loop(0, nchunk)
        def _(ci):
            off = pl.multiple_of(base + ci * SC_CHUNK, SC_CHUNK)
            pltpu.sync_copy(rows_hbm.at[pl.ds(off, SC_CHUNK)], rows_v)
            for kk in range(nidx):
                pltpu.sync_copy(idx_hbm[kk].at[pl.ds(off, SC_CHUNK)], idx_v[kk])
            copies = [pltpu.async_copy(rows_v, out_hbm.at[idx_v[kk]], sem) for kk in range(nidx)]
            for c in copies:
                c.wait()

    return scatter(rows, *idx_list)


def _expert_kernel(be_ref, nu_ref, bv_ref, x_ref, wg_ref, wu_ref, wd_ref, y_ref, wgu_s, wd_s):
    j = pl.program_id(0)
    de = wg_ref.shape[2]
    last = jnp.maximum(nu_ref[0] - 1, 0)
    cur = be_ref[jnp.minimum(j, last)]
    prev = be_ref[jnp.minimum(jnp.maximum(j - 1, 0), last)]

    @pl.when((j == 0) | (cur != prev))
    def _():
        wgu_s[:, :de] = wg_ref[0].astype(BF16)
        wgu_s[:, de:] = wu_ref[0].astype(BF16)
        wd_s[...] = wd_ref[0].astype(BF16)

    @pl.when(j < nu_ref[0])
    def _():
        row = lax.broadcasted_iota(jnp.int32, x_ref.shape, 0)
        xu = jnp.where(row < bv_ref[j], x_ref[...], jnp.uint32(0))
        a, b = _unpack_bf16_pair(xu)
        x = jnp.concatenate([a, b], axis=1).astype(BF16)
        h12 = _dot(x, wgu_s[...])
        h1, h2 = h12[:, :de], h12[:, de:]
        hdn = (h1 * jax.nn.sigmoid(h1) * h2).astype(BF16)
        y = _dot(hdn, wd_s[...])
        hp = y.shape[1] // 2
        y_ref[...] = _pack_bf16_pair(y[:, :hp], y[:, hp:])

    @pl.when(pl.program_id(0) >= nu_ref[0])
    def _():
        y_ref[...] = jnp.zeros_like(y_ref)


def _sc_gather_rows(table, idx):
    m = idx.shape[0]
    w = table.shape[1]
    assert m % (SC_WORKERS * SC_CHUNK) == 0, m
    per_worker = m // SC_WORKERS
    nchunk = per_worker // SC_CHUNK

    @functools.partial(
        pl.kernel, mesh=_sc_mesh(), out_type=jax.ShapeDtypeStruct((m, w), table.dtype),
        scratch_types=[pltpu.VMEM((SC_CHUNK,), jnp.int32), pltpu.VMEM((SC_CHUNK, w), table.dtype),
                       pltpu.SemaphoreType.DMA],
        name="sc_gather_rows")
    def gather(table_hbm, idx_hbm, out_hbm, idx_v, rows_v, sem):
        base = _sc_worker_base(per_worker)

        @pl.loop(0, nchunk)
        def _(ci):
            off = pl.multiple_of(base + ci * SC_CHUNK, SC_CHUNK)
            pltpu.sync_copy(idx_hbm.at[pl.ds(off, SC_CHUNK)], idx_v)
            pltpu.async_copy(table_hbm.at[idx_v], rows_v, sem).wait()
            pltpu.sync_copy(rows_v, out_hbm.at[pl.ds(off, SC_CHUNK)])

    return gather(table, idx)


def _combine_kernel(h_ref, route_ref, fw_ref, y0_ref, y1_ref, o_ref):
    route = route_ref[...]
    g1 = route[:, 2:3]
    g2 = route[:, 3:4]
    a0, b0 = _unpack_bf16_pair(y0_ref[...])
    a1, b1 = _unpack_bf16_pair(y1_ref[...])
    moe = jnp.concatenate([a0 * g1 + a1 * g2, b0 * g1 + b1 * g2], axis=1)
    o_ref[...] = _rms(h_ref[...] + moe, fw_ref[...])


def _combine_into_kernel(h_ref, route_ref, fw_ref, y0_ref, y1_ref, carried_ref, o_ref):
    del carried_ref
    _combine_kernel(h_ref, route_ref, fw_ref, y0_ref, y1_ref, o_ref)


def _params(*sem):
    return pltpu.CompilerParams(dimension_semantics=sem, vmem_limit_bytes=VMEM_LIMIT)


def _full(shape):
    return pl.BlockSpec(shape, lambda *_: (0,) * len(shape))


def _rows(tile, width):
    return pl.BlockSpec((tile, width), lambda i: (i, 0))


def _layer(x2, pos2, attn_norm_w, w_in, gla_gate_up, gla_gate_bias, gla_norm_w, mla_q_norm_w,
           mla_w_uq, mla_kv_norm_w, mla_w_ukv, w_out, ffn_norm_w, router_group_w, router_group_b,
           router_expert_w, router_expert_b, expert_w_gate, expert_w_up, expert_w_down,
           out_norm_w, batch, seq):
    n, d = x2.shape
    dq = GLA_HEADS * GLA_DK
    dv = GLA_HEADS * GLA_DV
    dmla = MLA_HEADS * HEAD_PAD
    dmv = MLA_HEADS * MLA_DV

    c_gq, c_gk, c_gv, c_lr, c_og, c_cq, c_ckv, c_kr = jnp.split(
        w_in, [dq, 2 * dq, 2 * dq + dv, 2 * dq + dv + GLA_GATE_RANK,
               2 * dq + 2 * dv + GLA_GATE_RANK,
               2 * dq + 2 * dv + GLA_GATE_RANK + MLA_Q_RANK,
               2 * dq + 2 * dv + GLA_GATE_RANK + MLA_Q_RANK + MLA_KV_RANK], axis=1)
    zeros = lambda r, c: jnp.zeros((r, c), w_in.dtype)
    misc = jnp.concatenate([zeros(d, MLA_NOPE), c_kr, c_lr,
                            zeros(d, LANES - ROPE_END - GLA_GATE_RANK)], axis=1)
    w1 = jnp.concatenate([c_gq, c_gk, c_gv, c_og, c_cq, c_ckv, misc], axis=1).astype(BF16)
    gate_up = jnp.concatenate([zeros(GATE_LO, dq), gla_gate_up,
                               zeros(LANES - GATE_LO - GLA_GATE_RANK, dq)], axis=0).astype(BF16)
    wuq = mla_w_uq.reshape(MLA_Q_RANK, MLA_HEADS, MLA_NOPE + MLA_ROPE)
    wuq = jnp.pad(wuq, ((0, 0), (0, 0), (0, HEAD_PAD - ROPE_END))).reshape(MLA_Q_RANK, dmla)
    wukv = mla_w_ukv.reshape(MLA_KV_RANK, MLA_HEADS, MLA_NOPE + MLA_DV)
    wuk = jnp.pad(wukv[:, :, :MLA_NOPE], ((0, 0), (0, 0), (0, HEAD_PAD - MLA_NOPE)))
    wuv = jnp.pad(wukv[:, :, MLA_NOPE:], ((0, 0), (0, 0), (0, HEAD_PAD - MLA_DV)))
    wukv = jnp.concatenate([wuk.reshape(MLA_KV_RANK, dmla), wuv.reshape(MLA_KV_RANK, dmla)],
                           axis=1)
    inv = ROPE_THETA ** (-jnp.arange(ROPE_HALF, dtype=F32) / ROPE_HALF)
    inv_pat = jnp.tile(inv, ROPE_PACK).reshape(1, LANES)
    w_router = jnp.concatenate(
        [router_expert_w, router_group_w, zeros(d, LANES - N_EXPERTS - N_GROUPS)], axis=1)
    w_router_hi = w_router.astype(BF16)
    w_router = jnp.concatenate(
        [w_router_hi, (w_router - w_router_hi.astype(F32)).astype(BF16)], axis=1)
    b_router = jnp.concatenate(
        [router_expert_b, router_group_b, jnp.zeros((LANES - N_EXPERTS - N_GROUPS,), F32)]
    ).reshape(1, LANES)
    row1 = lambda v: v.reshape(1, -1)

    tp = PROJ_TILE
    outs = pl.pallas_call(
        functools.partial(_proj_kernel, q_scale=(MLA_NOPE + MLA_ROPE) ** -0.5 * LOG2_E),
        grid=(n // tp,),
        in_specs=[_rows(tp, d), _rows(tp // ROPE_PACK, ROPE_PACK), _full((1, d)), _full(w1.shape),
                  _full(gate_up.shape),
                  _full((1, dq)), _full((1, MLA_Q_RANK)), _full(wuq.shape),
                  _full((1, MLA_KV_RANK)), _full(wukv.shape), _full((1, LANES))],
        out_specs=[_rows(tp, dq), _rows(tp, dq), _rows(tp, dv), _rows(tp, dq), _rows(tp, dv),
                   _rows(tp, dmla), _rows(tp, dmla), _rows(tp, dmla)],
        out_shape=[jax.ShapeDtypeStruct((n, dq), BF16), jax.ShapeDtypeStruct((n, dq), BF16),
                   jax.ShapeDtypeStruct((n, dv), BF16), jax.ShapeDtypeStruct((n, dq), F32),
                   jax.ShapeDtypeStruct((n, dv), BF16), jax.ShapeDtypeStruct((n, dmla), BF16),
                   jax.ShapeDtypeStruct((n, dmla), BF16), jax.ShapeDtypeStruct((n, dmla), BF16)],
        scratch_shapes=[pltpu.VMEM((tp, LANES), F32)] * 2,
        compiler_params=_params("parallel"),
        name="in_proj",
    )(x2, pos2, row1(attn_norm_w), w1, gate_up, row1(gla_gate_bias), row1(mla_q_norm_w),
      wuq.astype(BF16), row1(mla_kv_norm_w), wukv.astype(BF16), inv_pat)
    gq, gk, gv, gla, gog, q, k, v = outs

    tg = GLA_TILE
    seq3 = lambda a: a.reshape(batch, seq, a.shape[-1])
    gspec = lambda w: pl.BlockSpec((1, tg, w), lambda b, i: (b, i, 0))
    y_gla = pl.pallas_call(
        _gla_kernel,
        grid=(batch, seq // tg),
        in_specs=[gspec(dq), gspec(dq), gspec(dv), gspec(dq), gspec(dv), _full((1, GLA_DV))],
        out_specs=gspec(dv),
        out_shape=jax.ShapeDtypeStruct((batch, seq, dv), BF16),
        scratch_shapes=[pltpu.VMEM((GLA_HEADS, GLA_DV, GLA_DK), F32)],
        compiler_params=_params("parallel", "arbitrary"),
        name="gla",
    )(seq3(gq), seq3(gk), seq3(gv), seq3(gla), seq3(gog), row1(gla_norm_w))

    ta = ATT_TILE
    hps = ATT_HEADS_PER_STEP
    y_mla = pl.pallas_call(
        functools.partial(_mla_kernel, tq=ta),
        grid=(batch, MLA_HEADS // hps, seq // ta),
        in_specs=[pl.BlockSpec((1, seq, hps * HEAD_PAD), lambda b, hp, i: (b, 0, hp))] * 3,
        out_specs=pl.BlockSpec((1, ta, hps * MLA_DV), lambda b, hp, i: (b, i, hp)),
        out_shape=jax.ShapeDtypeStruct((batch, seq, dmv), BF16),
        scratch_shapes=[pltpu.VMEM((hps, ta, ta), F32)] * 3,
        compiler_params=_params("parallel", "parallel", "arbitrary"),
        name="mla",
    )(seq3(q), seq3(k), seq3(v))

    tr = ROUTER_TILE
    ntr = n // tr
    cur = lambda t, w: pl.BlockSpec((t, w), lambda i: (jnp.minimum(i, ntr - 1), 0))
    prev = lambda t, w: pl.BlockSpec((t, w), lambda i: (jnp.maximum(i - 1, 0), 0))
    h, hn, route, route_t, counts = pl.pallas_call(
        _router_kernel,
        grid=(ntr + 1,),
        in_specs=[cur(tr, d), cur(tr, dv), cur(tr, dmv), _full(w_out.shape), _full((1, d)),
                  _full(w_router.shape), _full((1, LANES))],
        out_specs=[cur(tr, d), cur(tr, d // 2), prev(tr, LANES),
                   pl.BlockSpec((ROUTE_FIELDS, tr), lambda i: (0, jnp.maximum(i - 1, 0))),
                   _full((1, LANES))],
        out_shape=[jax.ShapeDtypeStruct((n, d), F32), jax.ShapeDtypeStruct((n, d // 2), jnp.uint32),
                   jax.ShapeDtypeStruct((n, LANES), F32),
                   jax.ShapeDtypeStruct((ROUTE_FIELDS, n), F32),
                   jax.ShapeDtypeStruct((1, LANES), F32)],
        scratch_shapes=[pltpu.VMEM((1, LANES), F32), pltpu.VMEM((tr, LANES), F32)],
        compiler_params=_params("arbitrary"),
        name="out_proj_router",
    )(x2, y_gla.reshape(n, dv), y_mla.reshape(n, dmv), w_out.astype(BF16), row1(ffn_norm_w),
      w_router, b_router)

    blk = EXPERT_BLOCK
    nblk = (n * TOP_K) // blk + N_EXPERTS
    cnt = counts[0, :N_EXPERTS].astype(jnp.int32)
    padded = (cnt + blk - 1) // blk * blk
    pend = jnp.cumsum(padded)
    pstart = pend - padded
    dest = pl.pallas_call(
        _dest_kernel,
        grid_spec=pltpu.PrefetchScalarGridSpec(
            num_scalar_prefetch=1, grid=(1,),
            in_specs=[pl.BlockSpec((ROUTE_FIELDS, n), lambda i, ps: (0, 0))],
            out_specs=pl.BlockSpec((TOP_K, n), lambda i, ps: (0, 0))),
        out_shape=jax.ShapeDtypeStruct((TOP_K, n), jnp.int32),
        compiler_params=_params("arbitrary"),
        name="dest_rows",
    )(pstart, route_t)
    blk_start = jnp.arange(nblk, dtype=jnp.int32) * blk
    blk_expert = jnp.minimum(
        jnp.sum((pend[None, :] <= blk_start[:, None]).astype(jnp.int32), axis=1), N_EXPERTS - 1)
    n_used = (pend[-1] // blk).astype(jnp.int32).reshape(1)
    own = blk_expert[:, None] == jnp.arange(N_EXPERTS, dtype=jnp.int32)
    blk_valid = jnp.clip(jnp.sum(jnp.where(own, cnt - (blk_start[:, None] - pstart), 0), axis=1),
                         0, blk)
    blk_valid = jnp.where(jnp.arange(nblk) < n_used[0], blk_valid, 0).astype(jnp.int32)
    dest_slots = [dest[kk] for kk in range(TOP_K)]

    xbuf = _sc_scatter_rows(hn, dest_slots, nblk * blk)

    def used(j, nu):
        return jnp.maximum(jnp.minimum(j, nu[0] - 1), 0)

    ybuf = pl.pallas_call(
        _expert_kernel,
        grid_spec=pltpu.PrefetchScalarGridSpec(
            num_scalar_prefetch=3,
            grid=(nblk,),
            in_specs=[
                pl.BlockSpec((blk, d // 2), lambda j, be, nu, bv: (used(j, nu), 0)),
                pl.BlockSpec((1, d, D_EXPERT), lambda j, be, nu, bv: (be[used(j, nu)], 0, 0)),
                pl.BlockSpec((1, d, D_EXPERT), lambda j, be, nu, bv: (be[used(j, nu)], 0, 0)),
                pl.BlockSpec((1, D_EXPERT, d), lambda j, be, nu, bv: (be[used(j, nu)], 0, 0)),
            ],
            out_specs=pl.BlockSpec((blk, d // 2), lambda j, be, nu, bv: (j, 0)),
            scratch_shapes=[pltpu.VMEM((d, 2 * D_EXPERT), BF16), pltpu.VMEM((D_EXPERT, d), BF16)],
        ),
        out_shape=jax.ShapeDtypeStruct((nblk * blk, d // 2), jnp.uint32),
        compiler_params=_params("arbitrary"),
        name="experts",
    )(blk_expert, n_used, blk_valid, xbuf, expert_w_gate, expert_w_up, expert_w_down)

    tc = COMBINE_TILE
    parts = COMBINE_PARTS
    npart = n // parts
    steps = npart // tc
    out = None
    for p in range(parts):
        part_dest = dest[:, p * npart:(p + 1) * npart].reshape(TOP_K * npart)
        gathered = _sc_gather_rows(ybuf, part_dest)
        here = lambda w, p=p: pl.BlockSpec((tc, w), lambda i: (i + p * steps, 0))
        in_specs = [here(d), here(LANES), _full((1, d)), _rows(tc, d // 2),
                    pl.BlockSpec((tc, d // 2), lambda i: (i + steps, 0))]
        args = [h, route, row1(out_norm_w), gathered, gathered]
        if out is not None:
            in_specs.append(pl.BlockSpec(memory_space=pl.ANY))
            args.append(out)
        out = pl.pallas_call(
            _combine_kernel if out is None else _combine_into_kernel,
            grid=(steps,),
            in_specs=in_specs,
            out_specs=here(d),
            out_shape=jax.ShapeDtypeStruct((n, d), F32),
            input_output_aliases={} if out is None else {len(args) - 1: 0},
            compiler_params=_params("parallel"),
            name=f"combine_{p}",
        )(*args)
    return out


def kernel(x, positions, attn_norm_w, w_in, gla_gate_up, gla_gate_bias, gla_norm_w, mla_q_norm_w,
           mla_w_uq, mla_kv_norm_w, mla_w_ukv, w_out, ffn_norm_w, router_group_w, router_group_b,
           router_expert_w, router_expert_b, expert_w_gate, expert_w_up, expert_w_down,
           final_norm_w):
    batch, seq, d = x.shape
    depth = w_in.shape[0]
    assert depth == 1, "the final norm is fused into the last layer's combine step"
    n = batch * seq
    assert seq % GLA_TILE == 0 and seq % ATT_TILE == 0, (seq, GLA_TILE, ATT_TILE)
    assert n % ROUTER_TILE == 0 and n % PROJ_TILE == 0, n
    assert n % (COMBINE_TILE * COMBINE_PARTS) == 0, n
    out = _layer(x.reshape(batch * seq, d), positions.reshape(batch * seq // ROPE_PACK, ROPE_PACK),
                 attn_norm_w[0], w_in[0], gla_gate_up[0], gla_gate_bias[0], gla_norm_w[0],
                 mla_q_norm_w[0], mla_w_uq[0], mla_kv_norm_w[0], mla_w_ukv[0], w_out[0],
                 ffn_norm_w[0], router_group_w[0], router_group_b[0], router_expert_w[0],
                 router_expert_b[0], expert_w_gate[0], expert_w_up[0], expert_w_down[0],
                 final_norm_w, batch, seq)
    return out.reshape(batch, seq, d)
```

```python
import functools

import jax
import jax.numpy as jnp
from jax import lax
from jax.experimental import pallas as pl
from jax.experimental.pallas import tpu as pltpu
from jax.experimental.pallas import tpu_sc as plsc

EPS = 1e-6
GLA_HEADS = 4
GLA_DK = 64
GLA_DV = 128
GLA_GATE_RANK = 16
GLA_GATE_NORM = 16.0
GLA_CHUNK = 64
MLA_HEADS = 8
MLA_NOPE = 64
MLA_ROPE = 32
MLA_DV = 64
MLA_Q_RANK = 384
MLA_KV_RANK = 256
ROPE_THETA = 10000.0
N_GROUPS = 4
EXPERTS_PER_GROUP = 8
N_EXPERTS = N_GROUPS * EXPERTS_PER_GROUP
TOP_K = 2
D_EXPERT = 256

LANES = 128
HEAD_PAD = 128
ROPE_HALF = MLA_ROPE // 2
ROPE_LO = MLA_NOPE
ROPE_HI = MLA_NOPE + ROPE_HALF
ROPE_END = MLA_NOPE + MLA_ROPE
ROPE_PACK = LANES // ROPE_HALF
GATE_LO = ROPE_END

PROJ_TILE = 1024
COMBINE_TILE = 1024
COMBINE_PARTS = 4
GLA_TILE = 2048
GLA_SUBTILE = 128
ATT_TILE = 512
ATT_HEADS_PER_STEP = 4
EXPERT_BLOCK = 512
ROUTER_TILE = 1024
ROUTE_ROWS = 512
ROUTE_FIELDS = 8
VMEM_LIMIT = 56 * 1024 * 1024
SC_CORES = 2
SC_SUBCORES = 16
SC_WORKERS = SC_CORES * SC_SUBCORES
SC_CHUNK = 128

F32 = jnp.float32
BF16 = jnp.bfloat16
NEG_INF = float("-inf")
LOG2_E = 1.4426950408889634


def _dot(a, b, precision=None):
    return jnp.dot(a, b, preferred_element_type=F32, precision=precision)


def _dot_nt(a, b):
    return lax.dot_general(a, b, (((1,), (1,)), ((), ())), preferred_element_type=F32)


def _dot_tn(a, b):
    return lax.dot_general(a, b, (((0,), (0,)), ((), ())), preferred_element_type=F32)


def _rms(x, w):
    return x * lax.rsqrt(jnp.mean(x * x, axis=-1, keepdims=True) + EPS) * w


def _pack_bf16_pair(a, b):
    ua = lax.bitcast_convert_type(a.astype(BF16).astype(F32), jnp.uint32)
    ub = lax.bitcast_convert_type(b.astype(BF16).astype(F32), jnp.uint32)
    return (ua >> 16) | (ub & jnp.uint32(0xFFFF0000))


def _unpack_bf16_pair(u):
    a = lax.bitcast_convert_type(u << 16, F32)
    b = lax.bitcast_convert_type(u & jnp.uint32(0xFFFF0000), F32)
    return a, b


def _proj_kernel(x_ref, pos_ref, nw_ref, w1_ref, gu_ref, gb_ref, qnw_ref, wuq_ref, kvnw_ref,
                 wukv_ref, inv_ref,
                 gq_ref, gk_ref, gv_ref, gla_ref, gog_ref, q_ref, k_ref, v_ref, cos_ref, sin_ref,
                 *, q_scale):
    x = x_ref[...]
    xn = _rms(x, nw_ref[...]).astype(BF16)
    dq = GLA_HEADS * GLA_DK
    dv = GLA_HEADS * GLA_DV
    o = 0

    def proj(width):
        nonlocal o
        res = _dot(xn, w1_ref[:, o:o + width])
        o += width
        return res

    gq_ref[...] = (proj(dq) * (GLA_DK ** -0.5)).astype(BF16)
    gk_ref[...] = proj(dq).astype(BF16)
    gv_ref[...] = proj(dv).astype(BF16)
    gog_ref[...] = proj(dv).astype(BF16)
    cq = proj(MLA_Q_RANK)
    ckv = proj(MLA_KV_RANK)
    misc = proj(LANES)

    z = _dot(misc.astype(BF16), gu_ref[...]) + gb_ref[...]
    log_sig = jnp.minimum(z, 0.0) - jnp.log1p(jnp.exp(-jnp.abs(z)))
    gla_ref[...] = log_sig * (1.0 / GLA_GATE_NORM)

    lane = lax.broadcasted_iota(jnp.int32, (x.shape[0], LANES), 1)
    tpk = pos_ref.shape[0]
    lane_p = lax.broadcasted_iota(jnp.int32, (tpk, LANES), 1)
    pos_p = jnp.zeros((tpk, LANES), F32)
    for g in range(ROPE_PACK):
        pos_p = jnp.where((lane_p >> 4) == g, pos_ref[:, g:g + 1].astype(F32), pos_p)
    ang = pos_p * inv_ref[...]
    cos_p = jnp.cos(ang)
    sin_p = jnp.sin(ang)
    for g in range(ROPE_PACK):
        shift = (ROPE_LO - ROPE_HALF * g) % LANES
        cos_ref[pl.ds(g, tpk, stride=ROPE_PACK), :] = pltpu.roll(cos_p, shift, 1)
        sin_ref[pl.ds(g, tpk, stride=ROPE_PACK), :] = pltpu.roll(sin_p, shift, 1)
    cos_lo = cos_ref[...]
    sin_lo = sin_ref[...]
    in_lo = (lane >= ROPE_LO) & (lane < ROPE_HI)
    in_hi = (lane >= ROPE_HI) & (lane < ROPE_END)
    c_rope = jnp.where(in_lo, cos_lo, jnp.where(in_hi, pltpu.roll(cos_lo, ROPE_HALF, 1), 0.0))
    s_up = jnp.where(in_hi, pltpu.roll(sin_lo, ROPE_HALF, 1), 0.0)
    s_dn = jnp.where(in_lo, -sin_lo, 0.0)
    c_q = jnp.where(lane < MLA_NOPE, 1.0, c_rope)

    def rope(t, c):
        return (t * c + pltpu.roll(t, ROPE_HALF, 1) * s_up
                + pltpu.roll(t, LANES - ROPE_HALF, 1) * s_dn)

    k_rope = rope(misc, c_rope)

    q = _dot(_rms(cq, qnw_ref[...]).astype(BF16), wuq_ref[...])
    kv = _dot(_rms(ckv, kvnw_ref[...]).astype(BF16), wukv_ref[...])
    ones_lane = jnp.where(lane == MLA_DV, 1.0, 0.0)
    for h in range(MLA_HEADS):
        sl = slice(h * HEAD_PAD, (h + 1) * HEAD_PAD)
        vsl = slice((MLA_HEADS + h) * HEAD_PAD, (MLA_HEADS + h + 1) * HEAD_PAD)
        q_ref[:, sl] = (rope(q[:, sl], c_q) * q_scale).astype(BF16)
        k_ref[:, sl] = (kv[:, sl] + k_rope).astype(BF16)
        v_ref[:, sl] = (kv[:, vsl] + ones_lane).astype(BF16)


def _gla_kernel(q_ref, k_ref, v_ref, la_ref, og_ref, nw_ref, o_ref, st_ref):
    t = GLA_SUBTILE
    nchunk = t // GLA_CHUNK

    @pl.when(pl.program_id(1) == 0)
    def _():
        st_ref[...] = jnp.zeros_like(st_ref)

    row = lax.broadcasted_iota(jnp.int32, (t, t), 0)
    col = lax.broadcasted_iota(jnp.int32, (t, t), 1)
    chunk_bits = GLA_CHUNK.bit_length() - 1
    tri = ((row >> chunk_bits) == (col >> chunk_bits)) & (col <= row)
    tri_b = tri.astype(BF16)
    nw = nw_ref[...]
    states = [st_ref[h] for h in range(GLA_HEADS)]

    for sub in range(q_ref.shape[1] // t):
        tr = slice(sub * t, (sub + 1) * t)
        la = la_ref[0, tr]
        la_hi = la.astype(BF16)
        la_lo = (la - la_hi.astype(F32)).astype(BF16)
        parts = _dot(tri_b, jnp.concatenate([la_hi, la_lo], axis=1))
        b = parts[:, :la.shape[1]] + parts[:, la.shape[1]:]
        b_last = jnp.concatenate(
            [jnp.broadcast_to(b[(c + 1) * GLA_CHUNK - 1:(c + 1) * GLA_CHUNK],
                              (GLA_CHUNK, b.shape[1])) for c in range(nchunk)], axis=0)
        q_e = (q_ref[0, tr].astype(F32) * jnp.exp(b)).astype(BF16)
        kf = k_ref[0, tr].astype(F32)
        k_e = (kf * jnp.exp(-b)).astype(BF16)
        k_d = (kf * jnp.exp(b_last - b)).astype(BF16)
        decay = jnp.exp(b_last)

        for h in range(GLA_HEADS):
            ks = slice(h * GLA_DK, (h + 1) * GLA_DK)
            vs = slice(h * GLA_DV, (h + 1) * GLA_DV)
            qh, keh, kdh = q_e[:, ks], k_e[:, ks], k_d[:, ks]
            vh = v_ref[0, tr, vs]
            att = jnp.where(tri, _dot_nt(qh, keh), 0.0)
            o = _dot(att.astype(BF16), vh)
            state = states[h]
            inter = []
            for c in range(nchunk):
                rs = slice(c * GLA_CHUNK, (c + 1) * GLA_CHUNK)
                inter.append(_dot_nt(qh[rs], state.astype(BF16)))
                upd = _dot_tn(vh[rs], kdh[rs])
                state = state * decay[c * GLA_CHUNK:c * GLA_CHUNK + 1, ks] + upd
            states[h] = state
            o = o + jnp.concatenate(inter, axis=0)
            o = _rms(o, nw)
            g = og_ref[0, tr, vs].astype(F32)
            o_ref[0, tr, vs] = (o * (g * jax.nn.sigmoid(g))).astype(o_ref.dtype)

    for h in range(GLA_HEADS):
        st_ref[h] = states[h]


def _mla_kernel(q_ref, k_ref, v_ref, o_ref, sh_ref, s0_ref, s1_ref, *, tq):
    heads = q_ref.shape[2] // HEAD_PAD
    nq = q_ref.shape[1] // tq
    qi = pl.program_id(2)
    lane = lax.broadcasted_iota(jnp.int32, (tq, LANES), 1)
    causal = (lax.broadcasted_iota(jnp.int32, (tq, tq), 1)
              <= lax.broadcasted_iota(jnp.int32, (tq, tq), 0))

    def rows(i):
        return pl.ds(pl.multiple_of(i * tq, tq), tq)

    def produce(qblk, j, dst):
        for hh in range(heads):
            hs = slice(hh * HEAD_PAD, (hh + 1) * HEAD_PAD)
            dst[hh] = _dot_nt(q_ref[0, rows(qblk), hs], k_ref[0, rows(j), hs])

    def produce_diagonal(qblk, dst):
        th = tq // 2
        q0 = pl.multiple_of(qblk * tq, tq)
        for hh in range(heads):
            hs = slice(hh * HEAD_PAD, (hh + 1) * HEAD_PAD)
            dst[hh, :th, :th] = _dot_nt(q_ref[0, pl.ds(q0, th), hs], k_ref[0, pl.ds(q0, th), hs])
            dst[hh, th:, :] = _dot_nt(q_ref[0, pl.ds(q0 + th, th), hs], k_ref[0, rows(qblk), hs])

    def update(s, m, acc, v):
        m_new = jnp.maximum(m, jnp.max(s, axis=-1, keepdims=True))
        p = jnp.exp2(s - m_new).astype(BF16)
        return m_new, jnp.exp2(m - m_new) * acc + _dot(p, v)

    def consume(j, src, carry, diagonal):
        new = []
        for hh in range(heads):
            m, acc = carry[hh]
            hs = slice(hh * HEAD_PAD, (hh + 1) * HEAD_PAD)
            if not diagonal:
                new.append(update(src[hh], m, acc, v_ref[0, rows(j), hs]))
                continue
            th = tq // 2
            k0 = pl.multiple_of(j * tq, tq)
            top = update(jnp.where(causal[:th, :th], src[hh, :th, :th], NEG_INF),
                         m[:th], acc[:th], v_ref[0, pl.ds(k0, th), hs])
            bot = update(jnp.where(causal[th:], src[hh, th:, :], NEG_INF),
                         m[th:], acc[th:], v_ref[0, pl.ds(k0, tq), hs])
            new.append(tuple(jnp.concatenate([a, b], axis=0) for a, b in zip(top, bot)))
        return tuple(new)

    def finish(carry):
        for hp in range(heads // 2):
            o0, o1 = (acc / acc[:, MLA_DV:MLA_DV + 1] for _, acc in carry[2 * hp:2 * hp + 2])
            o_ref[0, :, hp * LANES:(hp + 1) * LANES] = jnp.where(
                lane < MLA_DV, o0, pltpu.roll(o1, MLA_DV, 1)).astype(o_ref.dtype)

    init = tuple((jnp.full((tq, 1), NEG_INF, F32), jnp.zeros((tq, LANES), F32))
                 for _ in range(heads))
    nxt = jnp.minimum(qi + 1, nq - 1)

    @pl.when(qi == 0)
    def _():
        produce_diagonal(0, sh_ref)
        carry = consume(0, sh_ref, init, True)
        produce(nxt, 0, sh_ref)
        finish(carry)

    @pl.when(qi > 0)
    def _():
        produce(qi, 1, s1_ref)
        carry = consume(0, sh_ref, init, False)

        def pair(i, carry):
            j = 1 + 2 * i
            produce(qi, j + 1, s0_ref)
            carry = consume(j, s1_ref, carry, False)
            produce(qi, j + 2, s1_ref)
            return consume(j + 1, s0_ref, carry, False)

        carry = lax.fori_loop(0, lax.shift_right_logical(qi - 1, 1), pair, carry)
        even = (qi & 1) == 0

        @pl.when(even)
        def _():
            produce_diagonal(qi, s0_ref)
            c = consume(qi - 1, s1_ref, carry, False)
            produce(nxt, 0, sh_ref)
            finish(consume(qi, s0_ref, c, True))

        @pl.when(jnp.logical_not(even))
        def _():
            produce(nxt, 0, sh_ref)
            finish(consume(qi, s1_ref, carry, True))


def _router_kernel(x_ref, yg_ref, ym_ref, wo_ref, fnw_ref, wr_ref, br_ref,
                   h_ref, hn_ref, route_ref, route_t_ref, cnt_ref, carry_ref, lg_ref):
    t = x_ref.shape[0]
    half = wo_ref.shape[0] // 2
    step = pl.program_id(0)

    @pl.when(step == 0)
    def _():
        carry_ref[...] = jnp.zeros_like(carry_ref)
        lg_ref[...] = jnp.zeros_like(lg_ref)

    logits_prev = lg_ref[...]
    h = x_ref[...] + _dot(yg_ref[...], wo_ref[:half]) + _dot(ym_ref[...], wo_ref[half:])
    h_ref[...] = h
    hn = _rms(h, fnw_ref[...])
    hp = hn.shape[1] // 2
    hn_ref[...] = _pack_bf16_pair(hn[:, :hp], hn[:, hp:])

    hn_hi = hn.astype(BF16)
    hn_lo = (hn - hn_hi.astype(F32)).astype(BF16)
    parts = _dot(hn_hi, wr_ref[...]) + _dot(hn_lo, wr_ref[...])
    lg_ref[...] = parts[:, :LANES] + parts[:, LANES:] + br_ref[...]

    live = (step > 0).astype(F32)
    tg = ROUTE_ROWS
    lane = lax.broadcasted_iota(jnp.int32, (tg, LANES), 1)
    lane_f = lane.astype(F32)
    rr = lax.broadcasted_iota(jnp.int32, (tg, tg), 0)
    cc = lax.broadcasted_iota(jnp.int32, (tg, tg), 1)
    earlier = (cc < rr).astype(BF16)

    def first_argmax(vals, vmax):
        idx = jnp.min(jnp.where(vals == vmax, lane_f, float(LANES)), axis=-1, keepdims=True)
        return idx.astype(jnp.int32)

    carry = carry_ref[...]
    for g in range(t // tg):
        rows = slice(g * tg, (g + 1) * tg)
        logits = logits_prev[rows]
        gl = jnp.where((lane >= N_EXPERTS) & (lane < N_EXPERTS + N_GROUPS), logits, NEG_INF)
        gmax = jnp.max(gl, axis=-1, keepdims=True)
        gsel = first_argmax(gl, gmax) - N_EXPERTS
        p_g = 1.0 / jnp.sum(jnp.exp(gl - gmax), axis=-1, keepdims=True)
        lo = gsel * EXPERTS_PER_GROUP
        el = jnp.where((lane >= lo) & (lane < lo + EXPERTS_PER_GROUP), logits, NEG_INF)
        m1 = jnp.max(el, axis=-1, keepdims=True)
        i1 = first_argmax(el, m1)
        el2 = jnp.where(lane == i1, NEG_INF, el)
        m2 = jnp.max(el2, axis=-1, keepdims=True)
        i2 = first_argmax(el2, m2)
        e2 = jnp.exp(m2 - m1)
        g1 = p_g / (1.0 + e2)
        g2 = p_g * e2 / (1.0 + e2)

        is1 = lane == i1
        is2 = lane == i2
        onehot = (is1 | is2).astype(BF16)
        before = _dot(earlier, onehot) + carry
        r1 = jnp.sum(jnp.where(is1, before, 0.0), axis=-1, keepdims=True)
        r2 = jnp.sum(jnp.where(is2, before, 0.0), axis=-1, keepdims=True)
        carry = carry + live * jnp.sum(onehot.astype(F32), axis=0, keepdims=True)

        route = jnp.where(lane == 0, i1.astype(F32), 0.0)
        route = jnp.where(lane == 1, i2.astype(F32), route)
        route = jnp.where(lane == 2, g1, route)
        route = jnp.where(lane == 3, g2, route)
        route = jnp.where(lane == 4, r1, route)
        route = jnp.where(lane == 5, r2, route)
        route_ref[rows] = route
        route_t_ref[:, rows] = route.T[:ROUTE_FIELDS]
    carry_ref[...] = carry
    cnt_ref[...] = carry


def _dest_kernel(pstart_ref, route_t_ref, dest_ref):
    eid = route_t_ref[0:TOP_K, :].astype(jnp.int32)
    rank = route_t_ref[4:4 + TOP_K, :].astype(jnp.int32)
    start = jnp.zeros_like(eid)
    for e in range(N_EXPERTS):
        start = jnp.where(eid == e, pstart_ref[e], start)
    dest_ref[...] = start + rank


def _sc_mesh():
    return plsc.VectorSubcoreMesh(core_axis_name="c", subcore_axis_name="s",
                                  num_cores=SC_CORES, num_subcores=SC_SUBCORES)


def _sc_worker_base(per_worker):
    return (lax.axis_index("s") * SC_CORES + lax.axis_index("c")) * per_worker


def _sc_scatter_rows(rows, idx_list, nrows):
    m, w = rows.shape
    nidx = len(idx_list)
    assert m % (SC_WORKERS * SC_CHUNK) == 0, m
    per_worker = m // SC_WORKERS
    nchunk = per_worker // SC_CHUNK

    @functools.partial(
        pl.kernel, mesh=_sc_mesh(), out_type=jax.ShapeDtypeStruct((nrows, w), rows.dtype),
        scratch_types=[pltpu.VMEM((SC_CHUNK,), jnp.int32)] * nidx
        + [pltpu.VMEM((SC_CHUNK, w), rows.dtype), pltpu.SemaphoreType.DMA],
        name="sc_scatter_rows")
    def scatter(rows_hbm, *refs):
        idx_hbm, out_hbm = refs[:nidx], refs[nidx]
        idx_v, rows_v, sem = refs[nidx + 1:2 * nidx + 1], refs[2 * nidx + 1], refs[2 * nidx + 2]
        base = _sc_worker_base(per_worker)

        @pl.loop(0, nchunk)
        def _(ci):
            off = pl.multiple_of(base + ci * SC_CHUNK, SC_CHUNK)
            pltpu.sync_copy(rows_hbm.at[pl.ds(off, SC_CHUNK)], rows_v)
            for kk in range(nidx):
                pltpu.sync_copy(idx_hbm[kk].at[pl.ds(off, SC_CHUNK)], idx_v[kk])
            copies = [pltpu.async_copy(rows_v, out_hbm.at[idx_v[kk]], sem) for kk in range(nidx)]
            for c in copies:
                c.wait()

    return scatter(rows, *idx_list)


def _expert_kernel(be_ref, nu_ref, bv_ref, x_ref, wg_ref, wu_ref, wd_ref, y_ref, wgu_s, wd_s):
    j = pl.program_id(0)
    de = wg_ref.shape[2]
    last = jnp.maximum(nu_ref[0] - 1, 0)
    cur = be_ref[jnp.minimum(j, last)]
    prev = be_ref[jnp.minimum(jnp.maximum(j - 1, 0), last)]

    @pl.when((j == 0) | (cur != prev))
    def _():
        wgu_s[:, :de] = wg_ref[0].astype(BF16)
        wgu_s[:, de:] = wu_ref[0].astype(BF16)
        wd_s[...] = wd_ref[0].astype(BF16)

    @pl.when(j < nu_ref[0])
    def _():
        row = lax.broadcasted_iota(jnp.int32, x_ref.shape, 0)
        xu = jnp.where(row < bv_ref[j], x_ref[...], jnp.uint32(0))
        a, b = _unpack_bf16_pair(xu)
        x = jnp.concatenate([a, b], axis=1).astype(BF16)
        h12 = _dot(x, wgu_s[...])
        h1, h2 = h12[:, :de], h12[:, de:]
        hdn = (h1 * jax.nn.sigmoid(h1) * h2).astype(BF16)
        y = _dot(hdn, wd_s[...])
        hp = y.shape[1] // 2
        y_ref[...] = _pack_bf16_pair(y[:, :hp], y[:, hp:])

    @pl.when(pl.program_id(0) >= nu_ref[0])
    def _():
        y_ref[...] = jnp.zeros_like(y_ref)


def _sc_gather_rows(table, idx):
    m = idx.shape[0]
    w = table.shape[1]
    assert m % (SC_WORKERS * SC_CHUNK) == 0, m
    per_worker = m // SC_WORKERS
    nchunk = per_worker // SC_CHUNK

    @functools.partial(
        pl.kernel, mesh=_sc_mesh(), out_type=jax.ShapeDtypeStruct((m, w), table.dtype),
        scratch_types=[pltpu.VMEM((SC_CHUNK,), jnp.int32), pltpu.VMEM((SC_CHUNK, w), table.dtype),
                       pltpu.SemaphoreType.DMA],
        name="sc_gather_rows")
    def gather(table_hbm, idx_hbm, out_hbm, idx_v, rows_v, sem):
        base = _sc_worker_base(per_worker)

        @pl.loop(0, nchunk)
        def _(ci):
            off = pl.multiple_of(base + ci * SC_CHUNK, SC_CHUNK)
            pltpu.sync_copy(idx_hbm.at[pl.ds(off, SC_CHUNK)], idx_v)
            pltpu.async_copy(table_hbm.at[idx_v], rows_v, sem).wait()
            pltpu.sync_copy(rows_v, out_hbm.at[pl.ds(off, SC_CHUNK)])

    return gather(table, idx)


def _combine_kernel(h_ref, route_ref, fw_ref, y0_ref, y1_ref, o_ref):
    route = route_ref[...]
    g1 = route[:, 2:3]
    g2 = route[:, 3:4]
    a0, b0 = _unpack_bf16_pair(y0_ref[...])
    a1, b1 = _unpack_bf16_pair(y1_ref[...])
    moe = jnp.concatenate([a0 * g1 + a1 * g2, b0 * g1 + b1 * g2], axis=1)
    o_ref[...] = _rms(h_ref[...] + moe, fw_ref[...])


def _combine_into_kernel(h_ref, route_ref, fw_ref, y0_ref, y1_ref, carried_ref, o_ref):
    del carried_ref
    _combine_kernel(h_ref, route_ref, fw_ref, y0_ref, y1_ref, o_ref)


def _params(*sem):
    return pltpu.CompilerParams(dimension_semantics=sem, vmem_limit_bytes=VMEM_LIMIT)


def _full(shape):
    return pl.BlockSpec(shape, lambda *_: (0,) * len(shape))


def _rows(tile, width):
    return pl.BlockSpec((tile, width), lambda i: (i, 0))


def _layer(x2, pos2, attn_norm_w, w_in, gla_gate_up, gla_gate_bias, gla_norm_w, mla_q_norm_w,
           mla_w_uq, mla_kv_norm_w, mla_w_ukv, w_out, ffn_norm_w, router_group_w, router_group_b,
           router_expert_w, router_expert_b, expert_w_gate, expert_w_up, expert_w_down,
           out_norm_w, batch, seq):
    n, d = x2.shape
    dq = GLA_HEADS * GLA_DK
    dv = GLA_HEADS * GLA_DV
    dmla = MLA_HEADS * HEAD_PAD
    dmv = MLA_HEADS * MLA_DV

    c_gq, c_gk, c_gv, c_lr, c_og, c_cq, c_ckv, c_kr = jnp.split(
        w_in, [dq, 2 * dq, 2 * dq + dv, 2 * dq + dv + GLA_GATE_RANK,
               2 * dq + 2 * dv + GLA_GATE_RANK,
               2 * dq + 2 * dv + GLA_GATE_RANK + MLA_Q_RANK,
               2 * dq + 2 * dv + GLA_GATE_RANK + MLA_Q_RANK + MLA_KV_RANK], axis=1)
    zeros = lambda r, c: jnp.zeros((r, c), w_in.dtype)
    misc = jnp.concatenate([zeros(d, MLA_NOPE), c_kr, c_lr,
                            zeros(d, LANES - ROPE_END - GLA_GATE_RANK)], axis=1)
    w1 = jnp.concatenate([c_gq, c_gk, c_gv, c_og, c_cq, c_ckv, misc], axis=1).astype(BF16)
    gate_up = jnp.concatenate([zeros(GATE_LO, dq), gla_gate_up,
                               zeros(LANES - GATE_LO - GLA_GATE_RANK, dq)], axis=0).astype(BF16)
    wuq = mla_w_uq.reshape(MLA_Q_RANK, MLA_HEADS, MLA_NOPE + MLA_ROPE)
    wuq = jnp.pad(wuq, ((0, 0), (0, 0), (0, HEAD_PAD - ROPE_END))).reshape(MLA_Q_RANK, dmla)
    wukv = mla_w_ukv.reshape(MLA_KV_RANK, MLA_HEADS, MLA_NOPE + MLA_DV)
    wuk = jnp.pad(wukv[:, :, :MLA_NOPE], ((0, 0), (0, 0), (0, HEAD_PAD - MLA_NOPE)))
    wuv = jnp.pad(wukv[:, :, MLA_NOPE:], ((0, 0), (0, 0), (0, HEAD_PAD - MLA_DV)))
    wukv = jnp.concatenate([wuk.reshape(MLA_KV_RANK, dmla), wuv.reshape(MLA_KV_RANK, dmla)],
                           axis=1)
    inv = ROPE_THETA ** (-jnp.arange(ROPE_HALF, dtype=F32) / ROPE_HALF)
    inv_pat = jnp.tile(inv, ROPE_PACK).reshape(1, LANES)
    w_router = jnp.concatenate(
        [router_expert_w, router_group_w, zeros(d, LANES - N_EXPERTS - N_GROUPS)], axis=1)
    w_router_hi = w_router.astype(BF16)
    w_router = jnp.concatenate(
        [w_router_hi, (w_router - w_router_hi.astype(F32)).astype(BF16)], axis=1)
    b_router = jnp.concatenate(
        [router_expert_b, router_group_b, jnp.zeros((LANES - N_EXPERTS - N_GROUPS,), F32)]
    ).reshape(1, LANES)
    row1 = lambda v: v.reshape(1, -1)

    tp = PROJ_TILE
    outs = pl.pallas_call(
        functools.partial(_proj_kernel, q_scale=(MLA_NOPE + MLA_ROPE) ** -0.5 * LOG2_E),
        grid=(n // tp,),
        in_specs=[_rows(tp, d), _rows(tp // ROPE_PACK, ROPE_PACK), _full((1, d)), _full(w1.shape),
                  _full(gate_up.shape),
                  _full((1, dq)), _full((1, MLA_Q_RANK)), _full(wuq.shape),
                  _full((1, MLA_KV_RANK)), _full(wukv.shape), _full((1, LANES))],
        out_specs=[_rows(tp, dq), _rows(tp, dq), _rows(tp, dv), _rows(tp, dq), _rows(tp, dv),
                   _rows(tp, dmla), _rows(tp, dmla), _rows(tp, dmla)],
        out_shape=[jax.ShapeDtypeStruct((n, dq), BF16), jax.ShapeDtypeStruct((n, dq), BF16),
                   jax.ShapeDtypeStruct((n, dv), BF16), jax.ShapeDtypeStruct((n, dq), F32),
                   jax.ShapeDtypeStruct((n, dv), BF16), jax.ShapeDtypeStruct((n, dmla), BF16),
                   jax.ShapeDtypeStruct((n, dmla), BF16), jax.ShapeDtypeStruct((n, dmla), BF16)],
        scratch_shapes=[pltpu.VMEM((tp, LANES), F32)] * 2,
        compiler_params=_params("parallel"),
        name="in_proj",
    )(x2, pos2, row1(attn_norm_w), w1, gate_up, row1(gla_gate_bias), row1(mla_q_norm_w),
      wuq.astype(BF16), row1(mla_kv_norm_w), wukv.astype(BF16), inv_pat)
    gq, gk, gv, gla, gog, q, k, v = outs

    tg = GLA_TILE
    seq3 = lambda a: a.reshape(batch, seq, a.shape[-1])
    gspec = lambda w: pl.BlockSpec((1, tg, w), lambda b, i: (b, i, 0))
    y_gla = pl.pallas_call(
        _gla_kernel,
        grid=(batch, seq // tg),
        in_specs=[gspec(dq), gspec(dq), gspec(dv), gspec(dq), gspec(dv), _full((1, GLA_DV))],
        out_specs=gspec(dv),
        out_shape=jax.ShapeDtypeStruct((batch, seq, dv), BF16),
        scratch_shapes=[pltpu.VMEM((GLA_HEADS, GLA_DV, GLA_DK), F32)],
        compiler_params=_params("parallel", "arbitrary"),
        name="gla",
    )(seq3(gq), seq3(gk), seq3(gv), seq3(gla), seq3(gog), row1(gla_norm_w))

    ta = ATT_TILE
    hps = ATT_HEADS_PER_STEP
    y_mla = pl.pallas_call(
        functools.partial(_mla_kernel, tq=ta),
        grid=(batch, MLA_HEADS // hps, seq // ta),
        in_specs=[pl.BlockSpec((1, seq, hps * HEAD_PAD), lambda b, hp, i: (b, 0, hp))] * 3,
        out_specs=pl.BlockSpec((1, ta, hps * MLA_DV), lambda b, hp, i: (b, i, hp)),
        out_shape=jax.ShapeDtypeStruct((batch, seq, dmv), BF16),
        scratch_shapes=[pltpu.VMEM((hps, ta, ta), F32)] * 3,
        compiler_params=_params("parallel", "parallel", "arbitrary"),
        name="mla",
    )(seq3(q), seq3(k), seq3(v))

    tr = ROUTER_TILE
    ntr = n // tr
    cur = lambda t, w: pl.BlockSpec((t, w), lambda i: (jnp.minimum(i, ntr - 1), 0))
    prev = lambda t, w: pl.BlockSpec((t, w), lambda i: (jnp.maximum(i - 1, 0), 0))
    h, hn, route, route_t, counts = pl.pallas_call(
        _router_kernel,
        grid=(ntr + 1,),
        in_specs=[cur(tr, d), cur(tr, dv), cur(tr, dmv), _full(w_out.shape), _full((1, d)),
                  _full(w_router.shape), _full((1, LANES))],
        out_specs=[cur(tr, d), cur(tr, d // 2), prev(tr, LANES),
                   pl.BlockSpec((ROUTE_FIELDS, tr), lambda i: (0, jnp.maximum(i - 1, 0))),
                   _full((1, LANES))],
        out_shape=[jax.ShapeDtypeStruct((n, d), F32), jax.ShapeDtypeStruct((n, d // 2), jnp.uint32),
                   jax.ShapeDtypeStruct((n, LANES), F32),
                   jax.ShapeDtypeStruct((ROUTE_FIELDS, n), F32),
                   jax.ShapeDtypeStruct((1, LANES), F32)],
        scratch_shapes=[pltpu.VMEM((1, LANES), F32), pltpu.VMEM((tr, LANES), F32)],
        compiler_params=_params("arbitrary"),
        name="out_proj_router",
    )(x2, y_gla.reshape(n, dv), y_mla.reshape(n, dmv), w_out.astype(BF16), row1(ffn_norm_w),
      w_router, b_router)

    blk = EXPERT_BLOCK
    nblk = (n * TOP_K) // blk + N_EXPERTS
    cnt = counts[0, :N_EXPERTS].astype(jnp.int32)
    padded = (cnt + blk - 1) // blk * blk
    pend = jnp.cumsum(padded)
    pstart = pend - padded
    dest = pl.pallas_call(
        _dest_kernel,
        grid_spec=pltpu.PrefetchScalarGridSpec(
            num_scalar_prefetch=1, grid=(1,),
            in_specs=[pl.BlockSpec((ROUTE_FIELDS, n), lambda i, ps: (0, 0))],
            out_specs=pl.BlockSpec((TOP_K, n), lambda i, ps: (0, 0))),
        out_shape=jax.ShapeDtypeStruct((TOP_K, n), jnp.int32),
        compiler_params=_params("arbitrary"),
        name="dest_rows",
    )(pstart, route_t)
    blk_start = jnp.arange(nblk, dtype=jnp.int32) * blk
    blk_expert = jnp.minimum(
        jnp.sum((pend[None, :] <= blk_start[:, None]).astype(jnp.int32), axis=1), N_EXPERTS - 1)
    n_used = (pend[-1] // blk).astype(jnp.int32).reshape(1)
    blk_valid = jnp.clip(cnt[blk_expert] - (blk_start - pstart[blk_expert]), 0, blk)
    blk_valid = jnp.where(jnp.arange(nblk) < n_used[0], blk_valid, 0).astype(jnp.int32)
    dest_slots = [dest[kk] for kk in range(TOP_K)]

    xbuf = _sc_scatter_rows(hn, dest_slots, nblk * blk)

    def used(j, nu):
        return jnp.maximum(jnp.minimum(j, nu[0] - 1), 0)

    ybuf = pl.pallas_call(
        _expert_kernel,
        grid_spec=pltpu.PrefetchScalarGridSpec(
            num_scalar_prefetch=3,
            grid=(nblk,),
            in_specs=[
                pl.BlockSpec((blk, d // 2), lambda j, be, nu, bv: (used(j, nu), 0)),
                pl.BlockSpec((1, d, D_EXPERT), lambda j, be, nu, bv: (be[used(j, nu)], 0, 0)),
                pl.BlockSpec((1, d, D_EXPERT), lambda j, be, nu, bv: (be[used(j, nu)], 0, 0)),
                pl.BlockSpec((1, D_EXPERT, d), lambda j, be, nu, bv: (be[used(j, nu)], 0, 0)),
            ],
            out_specs=pl.BlockSpec((blk, d // 2), lambda j, be, nu, bv: (j, 0)),
            scratch_shapes=[pltpu.VMEM((d, 2 * D_EXPERT), BF16), pltpu.VMEM((D_EXPERT, d), BF16)],
        ),
        out_shape=jax.ShapeDtypeStruct((nblk * blk, d // 2), jnp.uint32),
        compiler_params=_params("arbitrary"),
        name="experts",
    )(blk_expert, n_used, blk_valid, xbuf, expert_w_gate, expert_w_up, expert_w_down)

    tc = COMBINE_TILE
    parts = COMBINE_PARTS
    npart = n // parts
    steps = npart // tc
    out = None
    for p in range(parts):
        part_dest = dest[:, p * npart:(p + 1) * npart].reshape(TOP_K * npart)
        gathered = _sc_gather_rows(ybuf, part_dest)
        here = lambda w, p=p: pl.BlockSpec((tc, w), lambda i: (i + p * steps, 0))
        in_specs = [here(d), here(LANES), _full((1, d)), _rows(tc, d // 2),
                    pl.BlockSpec((tc, d // 2), lambda i: (i + steps, 0))]
        args = [h, route, row1(out_norm_w), gathered, gathered]
        if out is not None:
            in_specs.append(pl.BlockSpec(memory_space=pl.ANY))
            args.append(out)
        out = pl.pallas_call(
            _combine_kernel if out is None else _combine_into_kernel,
            grid=(steps,),
            in_specs=in_specs,
            out_specs=here(d),
            out_shape=jax.ShapeDtypeStruct((n, d), F32),
            input_output_aliases={} if out is None else {len(args) - 1: 0},
            compiler_params=_params("parallel"),
            name=f"combine_{p}",
        )(*args)
    return out


def kernel(x, positions, attn_norm_w, w_in, gla_gate_up, gla_gate_bias, gla_norm_w, mla_q_norm_w,
           mla_w_uq, mla_kv_norm_w, mla_w_ukv, w_out, ffn_norm_w, router_group_w, router_group_b,
           router_expert_w, router_expert_b, expert_w_gate, expert_w_up, expert_w_down,
           final_norm_w):
    batch, seq, d = x.shape
    depth = w_in.shape[0]
    assert depth == 1, "the final norm is fused into the last layer's combine step"
    n = batch * seq
    assert seq % GLA_TILE == 0 and seq % ATT_TILE == 0, (seq, GLA_TILE, ATT_TILE)
    assert n % ROUTER_TILE == 0 and n % PROJ_TILE == 0, n
    assert n % (COMBINE_TILE * COMBINE_PARTS) == 0, n
    out = _layer(x.reshape(batch * seq, d), positions.reshape(batch * seq // ROPE_PACK, ROPE_PACK),
                 attn_norm_w[0], w_in[0], gla_gate_up[0], gla_gate_bias[0], gla_norm_w[0],
                 mla_q_norm_w[0], mla_w_uq[0], mla_kv_norm_w[0], mla_w_ukv[0], w_out[0],
                 ffn_norm_w[0], router_group_w[0], router_group_b[0], router_expert_w[0],
                 router_expert_b[0], expert_w_gate[0], expert_w_up[0], expert_w_down[0],
                 final_norm_w, batch, seq)
    return out.reshape(batch, seq, d)
```

```python
import functools

import jax
import jax.numpy as jnp
from jax import lax
from jax.experimental import pallas as pl
from jax.experimental.pallas import tpu as pltpu
from jax.experimental.pallas import tpu_sc as plsc

EPS = 1e-6
GLA_HEADS = 4
GLA_DK = 64
GLA_DV = 128
GLA_GATE_RANK = 16
GLA_GATE_NORM = 16.0
GLA_CHUNK = 64
MLA_HEADS = 8
MLA_NOPE = 64
MLA_ROPE = 32
MLA_DV = 64
MLA_Q_RANK = 384
MLA_KV_RANK = 256
ROPE_THETA = 10000.0
N_GROUPS = 4
EXPERTS_PER_GROUP = 8
N_EXPERTS = N_GROUPS * EXPERTS_PER_GROUP
TOP_K = 2
D_EXPERT = 256

LANES = 128
HEAD_PAD = 128
ROPE_HALF = MLA_ROPE // 2
ROPE_LO = MLA_NOPE
ROPE_HI = MLA_NOPE + ROPE_HALF
ROPE_END = MLA_NOPE + MLA_ROPE
ROPE_PACK = LANES // ROPE_HALF
GATE_LO = ROPE_END

PROJ_TILE = 1024
COMBINE_TILE = 1024
COMBINE_PARTS = 4
GLA_TILE = 2048
GLA_SUBTILE = 128
ATT_TILE = 512
ATT_HEADS_PER_STEP = 4
EXPERT_BLOCK = 512
ROUTER_TILE = 1024
ROUTE_ROWS = 512
ROUTE_FIELDS = 8
VMEM_LIMIT = 56 * 1024 * 1024
SC_CORES = 2
SC_SUBCORES = 16
SC_WORKERS = SC_CORES * SC_SUBCORES
SC_CHUNK = 128

F32 = jnp.float32
BF16 = jnp.bfloat16
NEG_INF = float("-inf")
LOG2_E = 1.4426950408889634


def _dot(a, b, precision=None):
    return jnp.dot(a, b, preferred_element_type=F32, precision=precision)


def _dot_nt(a, b):
    return lax.dot_general(a, b, (((1,), (1,)), ((), ())), preferred_element_type=F32)


def _dot_tn(a, b):
    return lax.dot_general(a, b, (((0,), (0,)), ((), ())), preferred_element_type=F32)


def _rms(x, w):
    return x * lax.rsqrt(jnp.mean(x * x, axis=-1, keepdims=True) + EPS) * w


def _pack_bf16_pair(a, b):
    ua = lax.bitcast_convert_type(a.astype(BF16).astype(F32), jnp.uint32)
    ub = lax.bitcast_convert_type(b.astype(BF16).astype(F32), jnp.uint32)
    return (ua >> 16) | (ub & jnp.uint32(0xFFFF0000))


def _unpack_bf16_pair(u):
    a = lax.bitcast_convert_type(u << 16, F32)
    b = lax.bitcast_convert_type(u & jnp.uint32(0xFFFF0000), F32)
    return a, b


def _proj_kernel(x_ref, pos_ref, nw_ref, w1_ref, gu_ref, gb_ref, qnw_ref, wuq_ref, kvnw_ref,
                 wukv_ref, inv_ref,
                 gq_ref, gk_ref, gv_ref, gla_ref, gog_ref, q_ref, k_ref, v_ref, cos_ref, sin_ref,
                 *, q_scale):
    x = x_ref[...]
    xn = _rms(x, nw_ref[...]).astype(BF16)
    dq = GLA_HEADS * GLA_DK
    dv = GLA_HEADS * GLA_DV
    o = 0

    def proj(width):
        nonlocal o
        res = _dot(xn, w1_ref[:, o:o + width])
        o += width
        return res

    gq_ref[...] = (proj(dq) * (GLA_DK ** -0.5)).astype(BF16)
    gk_ref[...] = proj(dq).astype(BF16)
    gv_ref[...] = proj(dv).astype(BF16)
    gog_ref[...] = proj(dv).astype(BF16)
    cq = proj(MLA_Q_RANK)
    ckv = proj(MLA_KV_RANK)
    misc = proj(LANES)

    z = _dot(misc.astype(BF16), gu_ref[...]) + gb_ref[...]
    log_sig = jnp.minimum(z, 0.0) - jnp.log1p(jnp.exp(-jnp.abs(z)))
    gla_ref[...] = log_sig * (1.0 / GLA_GATE_NORM)

    lane = lax.broadcasted_iota(jnp.int32, (x.shape[0], LANES), 1)
    tpk = pos_ref.shape[0]
    lane_p = lax.broadcasted_iota(jnp.int32, (tpk, LANES), 1)
    pos_p = jnp.zeros((tpk, LANES), F32)
    for g in range(ROPE_PACK):
        own = (lane_p >> (ROPE_HALF.bit_length() - 1)) == g
        pos_p = jnp.where(own, pos_ref[:, g:g + 1].astype(F32), pos_p)
    ang = pos_p * inv_ref[...]
    cos_p = jnp.cos(ang)
    sin_p = jnp.sin(ang)
    for g in range(ROPE_PACK):
        shift = (ROPE_LO - ROPE_HALF * g) % LANES
        cos_ref[pl.ds(g, tpk, stride=ROPE_PACK), :] = pltpu.roll(cos_p, shift, 1)
        sin_ref[pl.ds(g, tpk, stride=ROPE_PACK), :] = pltpu.roll(sin_p, shift, 1)
    cos_lo = cos_ref[...]
    sin_lo = sin_ref[...]
    in_lo = (lane >= ROPE_LO) & (lane < ROPE_HI)
    in_hi = (lane >= ROPE_HI) & (lane < ROPE_END)
    c_rope = jnp.where(in_lo, cos_lo, jnp.where(in_hi, pltpu.roll(cos_lo, ROPE_HALF, 1), 0.0))
    s_up = jnp.where(in_hi, pltpu.roll(sin_lo, ROPE_HALF, 1), 0.0)
    s_dn = jnp.where(in_lo, -sin_lo, 0.0)
    k_coef = (c_rope, s_up, s_dn)
    q_coef = tuple(c * q_scale for c in (jnp.where(lane < MLA_NOPE, 1.0, c_rope), s_up, s_dn))

    def rope(t, coef):
        c, up, dn = coef
        return t * c + pltpu.roll(t, ROPE_HALF, 1) * up + pltpu.roll(t, LANES - ROPE_HALF, 1) * dn

    k_rope = rope(misc, k_coef)

    q = _dot(_rms(cq, qnw_ref[...]).astype(BF16), wuq_ref[...])
    kv = _dot(_rms(ckv, kvnw_ref[...]).astype(BF16), wukv_ref[...])
    ones_lane = jnp.where(lane == MLA_DV, 1.0, 0.0)
    for h in range(MLA_HEADS):
        sl = slice(h * HEAD_PAD, (h + 1) * HEAD_PAD)
        vsl = slice((MLA_HEADS + h) * HEAD_PAD, (MLA_HEADS + h + 1) * HEAD_PAD)
        q_ref[:, sl] = rope(q[:, sl], q_coef).astype(BF16)
        k_ref[:, sl] = (kv[:, sl] + k_rope).astype(BF16)
        v_ref[:, sl] = (kv[:, vsl] + ones_lane).astype(BF16)


def _gla_kernel(q_ref, k_ref, v_ref, la_ref, og_ref, nw_ref, o_ref, st_ref):
    t = GLA_SUBTILE
    nchunk = t // GLA_CHUNK

    @pl.when(pl.program_id(1) == 0)
    def _():
        st_ref[...] = jnp.zeros_like(st_ref)

    row = lax.broadcasted_iota(jnp.int32, (t, t), 0)
    col = lax.broadcasted_iota(jnp.int32, (t, t), 1)
    chunk_bits = GLA_CHUNK.bit_length() - 1
    tri = ((row >> chunk_bits) == (col >> chunk_bits)) & (col <= row)
    tri_b = tri.astype(BF16)
    nw = nw_ref[...]
    states = [st_ref[h] for h in range(GLA_HEADS)]

    for sub in range(q_ref.shape[1] // t):
        tr = slice(sub * t, (sub + 1) * t)
        la = la_ref[0, tr]
        la_hi = la.astype(BF16)
        la_lo = (la - la_hi.astype(F32)).astype(BF16)
        parts = _dot(tri_b, jnp.concatenate([la_hi, la_lo], axis=1))
        b = parts[:, :la.shape[1]] + parts[:, la.shape[1]:]
        b_last = jnp.concatenate(
            [jnp.broadcast_to(b[(c + 1) * GLA_CHUNK - 1:(c + 1) * GLA_CHUNK],
                              (GLA_CHUNK, b.shape[1])) for c in range(nchunk)], axis=0)
        q_e = (q_ref[0, tr].astype(F32) * jnp.exp(b)).astype(BF16)
        kf = k_ref[0, tr].astype(F32)
        k_e = (kf * jnp.exp(-b)).astype(BF16)
        k_d = (kf * jnp.exp(b_last - b)).astype(BF16)
        decay = jnp.exp(b_last)

        for h in range(GLA_HEADS):
            ks = slice(h * GLA_DK, (h + 1) * GLA_DK)
            vs = slice(h * GLA_DV, (h + 1) * GLA_DV)
            qh, keh, kdh = q_e[:, ks], k_e[:, ks], k_d[:, ks]
            vh = v_ref[0, tr, vs]
            att = jnp.where(tri, _dot_nt(qh, keh), 0.0)
            o = _dot(att.astype(BF16), vh)
            state = states[h]
            inter = []
            for c in range(nchunk):
                rs = slice(c * GLA_CHUNK, (c + 1) * GLA_CHUNK)
                inter.append(_dot_nt(qh[rs], state.astype(BF16)))
                upd = _dot_tn(vh[rs], kdh[rs])
                state = state * decay[c * GLA_CHUNK:c * GLA_CHUNK + 1, ks] + upd
            states[h] = state
            o = o + jnp.concatenate(inter, axis=0)
            o = _rms(o, nw)
            g = og_ref[0, tr, vs].astype(F32)
            o_ref[0, tr, vs] = (o * (g * jax.nn.sigmoid(g))).astype(o_ref.dtype)

    for h in range(GLA_HEADS):
        st_ref[h] = states[h]


def _mla_kernel(q_ref, k_ref, v_ref, o_ref, sh_ref, s0_ref, s1_ref, *, tq):
    heads = q_ref.shape[2] // HEAD_PAD
    nq = q_ref.shape[1] // tq
    qi = pl.program_id(2)
    lane = lax.broadcasted_iota(jnp.int32, (tq, LANES), 1)
    causal = (lax.broadcasted_iota(jnp.int32, (tq, tq), 1)
              <= lax.broadcasted_iota(jnp.int32, (tq, tq), 0))

    def rows(i):
        return pl.ds(pl.multiple_of(i * tq, tq), tq)

    def produce(qblk, j, dst):
        for hh in range(heads):
            hs = slice(hh * HEAD_PAD, (hh + 1) * HEAD_PAD)
            dst[hh] = _dot_nt(q_ref[0, rows(qblk), hs], k_ref[0, rows(j), hs])

    def produce_diagonal(qblk, dst):
        th = tq // 2
        q0 = pl.multiple_of(qblk * tq, tq)
        for hh in range(heads):
            hs = slice(hh * HEAD_PAD, (hh + 1) * HEAD_PAD)
            dst[hh, :th, :th] = _dot_nt(q_ref[0, pl.ds(q0, th), hs], k_ref[0, pl.ds(q0, th), hs])
            dst[hh, th:, :] = _dot_nt(q_ref[0, pl.ds(q0 + th, th), hs], k_ref[0, rows(qblk), hs])

    def update(s, m, acc, v):
        m_new = jnp.maximum(m, jnp.max(s, axis=-1, keepdims=True))
        p = jnp.exp2(s - m_new).astype(BF16)
        return m_new, jnp.exp2(m - m_new) * acc + _dot(p, v)

    def consume(j, src, carry, diagonal):
        new = []
        for hh in range(heads):
            m, acc = carry[hh]
            hs = slice(hh * HEAD_PAD, (hh + 1) * HEAD_PAD)
            if not diagonal:
                new.append(update(src[hh], m, acc, v_ref[0, rows(j), hs]))
                continue
            th = tq // 2
            k0 = pl.multiple_of(j * tq, tq)
            top = update(jnp.where(causal[:th, :th], src[hh, :th, :th], NEG_INF),
                         m[:th], acc[:th], v_ref[0, pl.ds(k0, th), hs])
            bot = update(jnp.where(causal[th:], src[hh, th:, :], NEG_INF),
                         m[th:], acc[th:], v_ref[0, pl.ds(k0, tq), hs])
            new.append(tuple(jnp.concatenate([a, b], axis=0) for a, b in zip(top, bot)))
        return tuple(new)

    def finish(carry):
        for hp in range(heads // 2):
            o0, o1 = (acc / acc[:, MLA_DV:MLA_DV + 1] for _, acc in carry[2 * hp:2 * hp + 2])
            o_ref[0, :, hp * LANES:(hp + 1) * LANES] = jnp.where(
                lane < MLA_DV, o0, pltpu.roll(o1, MLA_DV, 1)).astype(o_ref.dtype)

    init = tuple((jnp.full((tq, 1), NEG_INF, F32), jnp.zeros((tq, LANES), F32))
                 for _ in range(heads))
    nxt = jnp.minimum(qi + 1, nq - 1)

    @pl.when(qi == 0)
    def _():
        produce_diagonal(0, sh_ref)
        carry = consume(0, sh_ref, init, True)
        produce(nxt, 0, sh_ref)
        finish(carry)

    @pl.when(qi > 0)
    def _():
        produce(qi, 1, s1_ref)
        carry = consume(0, sh_ref, init, False)

        def pair(i, carry):
            j = 1 + 2 * i
            produce(qi, j + 1, s0_ref)
            carry = consume(j, s1_ref, carry, False)
            produce(qi, j + 2, s1_ref)
            return consume(j + 1, s0_ref, carry, False)

        carry = lax.fori_loop(0, lax.shift_right_logical(qi - 1, 1), pair, carry)
        even = (qi & 1) == 0

        @pl.when(even)
        def _():
            produce_diagonal(qi, s0_ref)
            c = consume(qi - 1, s1_ref, carry, False)
            produce(nxt, 0, sh_ref)
            finish(consume(qi, s0_ref, c, True))

        @pl.when(jnp.logical_not(even))
        def _():
            produce(nxt, 0, sh_ref)
            finish(consume(qi, s1_ref, carry, True))


def _router_kernel(x_ref, yg_ref, ym_ref, wo_ref, fnw_ref, wr_ref, br_ref,
                   h_ref, hn_ref, route_ref, route_t_ref, cnt_ref, carry_ref, lg_ref):
    t = x_ref.shape[0]
    half = wo_ref.shape[0] // 2
    step = pl.program_id(0)

    @pl.when(step == 0)
    def _():
        carry_ref[...] = jnp.zeros_like(carry_ref)
        lg_ref[...] = jnp.zeros_like(lg_ref)

    logits_prev = lg_ref[...]
    h = x_ref[...] + _dot(yg_ref[...], wo_ref[:half]) + _dot(ym_ref[...], wo_ref[half:])
    h_ref[...] = h
    hn = _rms(h, fnw_ref[...])
    hp = hn.shape[1] // 2
    hn_ref[...] = _pack_bf16_pair(hn[:, :hp], hn[:, hp:])

    hn_hi = hn.astype(BF16)
    hn_lo = (hn - hn_hi.astype(F32)).astype(BF16)
    parts = _dot(hn_hi, wr_ref[...]) + _dot(hn_lo, wr_ref[...])
    lg_ref[...] = parts[:, :LANES] + parts[:, LANES:] + br_ref[...]

    live = (step > 0).astype(F32)
    tg = ROUTE_ROWS
    lane = lax.broadcasted_iota(jnp.int32, (tg, LANES), 1)
    lane_f = lane.astype(F32)
    rr = lax.broadcasted_iota(jnp.int32, (tg, tg), 0)
    cc = lax.broadcasted_iota(jnp.int32, (tg, tg), 1)
    earlier = (cc < rr).astype(BF16)

    def first_argmax(vals, vmax):
        idx = jnp.min(jnp.where(vals == vmax, lane_f, float(LANES)), axis=-1, keepdims=True)
        return idx.astype(jnp.int32)

    carry = carry_ref[...]
    for g in range(t // tg):
        rows = slice(g * tg, (g + 1) * tg)
        logits = logits_prev[rows]
        gl = jnp.where((lane >= N_EXPERTS) & (lane < N_EXPERTS + N_GROUPS), logits, NEG_INF)
        gmax = jnp.max(gl, axis=-1, keepdims=True)
        gsel = first_argmax(gl, gmax) - N_EXPERTS
        p_g = 1.0 / jnp.sum(jnp.exp(gl - gmax), axis=-1, keepdims=True)
        lo = gsel * EXPERTS_PER_GROUP
        el = jnp.where((lane >= lo) & (lane < lo + EXPERTS_PER_GROUP), logits, NEG_INF)
        m1 = jnp.max(el, axis=-1, keepdims=True)
        i1 = first_argmax(el, m1)
        el2 = jnp.where(lane == i1, NEG_INF, el)
        m2 = jnp.max(el2, axis=-1, keepdims=True)
        i2 = first_argmax(el2, m2)
        e2 = jnp.exp(m2 - m1)
        g1 = p_g / (1.0 + e2)
        g2 = p_g * e2 / (1.0 + e2)

        is1 = lane == i1
        is2 = lane == i2
        onehot = (is1 | is2).astype(BF16)
        before = _dot(earlier, onehot) + carry
        r1 = jnp.sum(jnp.where(is1, before, 0.0), axis=-1, keepdims=True)
        r2 = jnp.sum(jnp.where(is2, before, 0.0), axis=-1, keepdims=True)
        carry = carry + live * jnp.sum(onehot.astype(F32), axis=0, keepdims=True)

        route = jnp.where(lane == 0, i1.astype(F32), 0.0)
        route = jnp.where(lane == 1, i2.astype(F32), route)
        route = jnp.where(lane == 2, g1, route)
        route = jnp.where(lane == 3, g2, route)
        route = jnp.where(lane == 4, r1, route)
        route = jnp.where(lane == 5, r2, route)
        route_ref[rows] = route
        route_t_ref[:, rows] = route.T[:ROUTE_FIELDS]
    carry_ref[...] = carry
    cnt_ref[...] = carry


def _dest_kernel(pstart_ref, route_t_ref, dest_ref):
    eid = route_t_ref[0:TOP_K, :].astype(jnp.int32)
    rank = route_t_ref[4:4 + TOP_K, :].astype(jnp.int32)
    start = jnp.zeros_like(eid)
    for e in range(N_EXPERTS):
        start = jnp.where(eid == e, pstart_ref[e], start)
    dest_ref[...] = start + rank


def _sc_mesh():
    return plsc.VectorSubcoreMesh(core_axis_name="c", subcore_axis_name="s",
                                  num_cores=SC_CORES, num_subcores=SC_SUBCORES)


def _sc_worker_base(per_worker):
    return (lax.axis_index("s") * SC_CORES + lax.axis_index("c")) * per_worker


def _sc_scatter_rows(rows, idx_list, nrows):
    m, w = rows.shape
    nidx = len(idx_list)
    assert m % (SC_WORKERS * SC_CHUNK) == 0, m
    per_worker = m // SC_WORKERS
    nchunk = per_worker // SC_CHUNK

    @functools.partial(
        pl.kernel, mesh=_sc_mesh(), out_type=jax.ShapeDtypeStruct((nrows, w), rows.dtype),
        scratch_types=[pltpu.VMEM((SC_CHUNK,), jnp.int32)] * nidx
        + [pltpu.VMEM((SC_CHUNK, w), rows.dtype), pltpu.SemaphoreType.DMA],
        name="sc_scatter_rows")
    def scatter(rows_hbm, *refs):
        idx_hbm, out_hbm = refs[:nidx], refs[nidx]
        idx_v, rows_v, sem = refs[nidx + 1:2 * nidx + 1], refs[2 * nidx + 1], refs[2 * nidx + 2]
        base = _sc_worker_base(per_worker)

        @pl.loop(0, nchunk)
        def _(ci):
            off = pl.multiple_of(base + ci * SC_CHUNK, SC_CHUNK)
            pltpu.sync_copy(rows_hbm.at[pl.ds(off, SC_CHUNK)], rows_v)
            for kk in range(nidx):
                pltpu.sync_copy(idx_hbm[kk].at[pl.ds(off, SC_CHUNK)], idx_v[kk])
            copies = [pltpu.async_copy(rows_v, out_hbm.at[idx_v[kk]], sem) for kk in range(nidx)]
            for c in copies:
                c.wait()

    return scatter(rows, *idx_list)


def _expert_kernel(be_ref, nu_ref, bv_ref, x_ref, wg_ref, wu_ref, wd_ref, y_ref, wgu_s, wd_s):
    j = pl.program_id(0)
    de = wg_ref.shape[2]
    last = jnp.maximum(nu_ref[0] - 1, 0)
    cur = be_ref[jnp.minimum(j, last)]
    prev = be_ref[jnp.minimum(jnp.maximum(j - 1, 0), last)]

    @pl.when((j == 0) | (cur != prev))
    def _():
        wgu_s[:, :de] = wg_ref[0].astype(BF16)
        wgu_s[:, de:] = wu_ref[0].astype(BF16)
        wd_s[...] = wd_ref[0].astype(BF16)

    @pl.when(j < nu_ref[0])
    def _():
        row = lax.broadcasted_iota(jnp.int32, x_ref.shape, 0)
        xu = jnp.where(row < bv_ref[j], x_ref[...], jnp.uint32(0))
        a, b = _unpack_bf16_pair(xu)
        x = jnp.concatenate([a, b], axis=1).astype(BF16)
        h12 = _dot(x, wgu_s[...])
        h1, h2 = h12[:, :de], h12[:, de:]
        hdn = (h1 * jax.nn.sigmoid(h1) * h2).astype(BF16)
        y = _dot(hdn, wd_s[...])
        hp = y.shape[1] // 2
        y_ref[...] = _pack_bf16_pair(y[:, :hp], y[:, hp:])

    @pl.when(pl.program_id(0) >= nu_ref[0])
    def _():
        y_ref[...] = jnp.zeros_like(y_ref)


def _sc_gather_rows(table, idx):
    m = idx.shape[0]
    w = table.shape[1]
    assert m % (SC_WORKERS * SC_CHUNK) == 0, m
    per_worker = m // SC_WORKERS
    nchunk = per_worker // SC_CHUNK

    @functools.partial(
        pl.kernel, mesh=_sc_mesh(), out_type=jax.ShapeDtypeStruct((m, w), table.dtype),
        scratch_types=[pltpu.VMEM((SC_CHUNK,), jnp.int32), pltpu.VMEM((SC_CHUNK, w), table.dtype),
                       pltpu.SemaphoreType.DMA],
        name="sc_gather_rows")
    def gather(table_hbm, idx_hbm, out_hbm, idx_v, rows_v, sem):
        base = _sc_worker_base(per_worker)

        @pl.loop(0, nchunk)
        def _(ci):
            off = pl.multiple_of(base + ci * SC_CHUNK, SC_CHUNK)
            pltpu.sync_copy(idx_hbm.at[pl.ds(off, SC_CHUNK)], idx_v)
            pltpu.async_copy(table_hbm.at[idx_v], rows_v, sem).wait()
            pltpu.sync_copy(rows_v, out_hbm.at[pl.ds(off, SC_CHUNK)])

    return gather(table, idx)


def _combine_kernel(h_ref, route_ref, fw_ref, y0_ref, y1_ref, o_ref):
    route = route_ref[...]
    g1 = route[:, 2:3]
    g2 = route[:, 3:4]
    a0, b0 = _unpack_bf16_pair(y0_ref[...])
    a1, b1 = _unpack_bf16_pair(y1_ref[...])
    moe = jnp.concatenate([a0 * g1 + a1 * g2, b0 * g1 + b1 * g2], axis=1)
    o_ref[...] = _rms(h_ref[...] + moe, fw_ref[...])


def _combine_into_kernel(h_ref, route_ref, fw_ref, y0_ref, y1_ref, carried_ref, o_ref):
    del carried_ref
    _combine_kernel(h_ref, route_ref, fw_ref, y0_ref, y1_ref, o_ref)


def _params(*sem):
    return pltpu.CompilerParams(dimension_semantics=sem, vmem_limit_bytes=VMEM_LIMIT)


def _full(shape):
    return pl.BlockSpec(shape, lambda *_: (0,) * len(shape))


def _rows(tile, width):
    return pl.BlockSpec((tile, width), lambda i: (i, 0))


def _layer(x2, pos2, attn_norm_w, w_in, gla_gate_up, gla_gate_bias, gla_norm_w, mla_q_norm_w,
           mla_w_uq, mla_kv_norm_w, mla_w_ukv, w_out, ffn_norm_w, router_group_w, router_group_b,
           router_expert_w, router_expert_b, expert_w_gate, expert_w_up, expert_w_down,
           out_norm_w, batch, seq):
    n, d = x2.shape
    dq = GLA_HEADS * GLA_DK
    dv = GLA_HEADS * GLA_DV
    dmla = MLA_HEADS * HEAD_PAD
    dmv = MLA_HEADS * MLA_DV

    c_gq, c_gk, c_gv, c_lr, c_og, c_cq, c_ckv, c_kr = jnp.split(
        w_in, [dq, 2 * dq, 2 * dq + dv, 2 * dq + dv + GLA_GATE_RANK,
               2 * dq + 2 * dv + GLA_GATE_RANK,
               2 * dq + 2 * dv + GLA_GATE_RANK + MLA_Q_RANK,
               2 * dq + 2 * dv + GLA_GATE_RANK + MLA_Q_RANK + MLA_KV_RANK], axis=1)
    zeros = lambda r, c: jnp.zeros((r, c), w_in.dtype)
    misc = jnp.concatenate([zeros(d, MLA_NOPE), c_kr, c_lr,
                            zeros(d, LANES - ROPE_END - GLA_GATE_RANK)], axis=1)
    w1 = jnp.concatenate([c_gq, c_gk, c_gv, c_og, c_cq, c_ckv, misc], axis=1).astype(BF16)
    gate_up = jnp.concatenate([zeros(GATE_LO, dq), gla_gate_up,
                               zeros(LANES - GATE_LO - GLA_GATE_RANK, dq)], axis=0).astype(BF16)
    wuq = mla_w_uq.reshape(MLA_Q_RANK, MLA_HEADS, MLA_NOPE + MLA_ROPE)
    wuq = jnp.pad(wuq, ((0, 0), (0, 0), (0, HEAD_PAD - ROPE_END))).reshape(MLA_Q_RANK, dmla)
    wukv = mla_w_ukv.reshape(MLA_KV_RANK, MLA_HEADS, MLA_NOPE + MLA_DV)
    wuk = jnp.pad(wukv[:, :, :MLA_NOPE], ((0, 0), (0, 0), (0, HEAD_PAD - MLA_NOPE)))
    wuv = jnp.pad(wukv[:, :, MLA_NOPE:], ((0, 0), (0, 0), (0, HEAD_PAD - MLA_DV)))
    wukv = jnp.concatenate([wuk.reshape(MLA_KV_RANK, dmla), wuv.reshape(MLA_KV_RANK, dmla)],
                           axis=1)
    inv = ROPE_THETA ** (-jnp.arange(ROPE_HALF, dtype=F32) / ROPE_HALF)
    inv_pat = jnp.tile(inv, ROPE_PACK).reshape(1, LANES)
    w_router = jnp.concatenate(
        [router_expert_w, router_group_w, zeros(d, LANES - N_EXPERTS - N_GROUPS)], axis=1)
    w_router_hi = w_router.astype(BF16)
    w_router = jnp.concatenate(
        [w_router_hi, (w_router - w_router_hi.astype(F32)).astype(BF16)], axis=1)
    b_router = jnp.concatenate(
        [router_expert_b, router_group_b, jnp.zeros((LANES - N_EXPERTS - N_GROUPS,), F32)]
    ).reshape(1, LANES)
    row1 = lambda v: v.reshape(1, -1)

    tp = PROJ_TILE
    outs = pl.pallas_call(
        functools.partial(_proj_kernel, q_scale=(MLA_NOPE + MLA_ROPE) ** -0.5 * LOG2_E),
        grid=(n // tp,),
        in_specs=[_rows(tp, d), _rows(tp // ROPE_PACK, ROPE_PACK), _full((1, d)), _full(w1.shape),
                  _full(gate_up.shape),
                  _full((1, dq)), _full((1, MLA_Q_RANK)), _full(wuq.shape),
                  _full((1, MLA_KV_RANK)), _full(wukv.shape), _full((1, LANES))],
        out_specs=[_rows(tp, dq), _rows(tp, dq), _rows(tp, dv), _rows(tp, dq), _rows(tp, dv),
                   _rows(tp, dmla), _rows(tp, dmla), _rows(tp, dmla)],
        out_shape=[jax.ShapeDtypeStruct((n, dq), BF16), jax.ShapeDtypeStruct((n, dq), BF16),
                   jax.ShapeDtypeStruct((n, dv), BF16), jax.ShapeDtypeStruct((n, dq), F32),
                   jax.ShapeDtypeStruct((n, dv), BF16), jax.ShapeDtypeStruct((n, dmla), BF16),
                   jax.ShapeDtypeStruct((n, dmla), BF16), jax.ShapeDtypeStruct((n, dmla), BF16)],
        scratch_shapes=[pltpu.VMEM((tp, LANES), F32)] * 2,
        compiler_params=_params("parallel"),
        name="in_proj",
    )(x2, pos2, row1(attn_norm_w), w1, gate_up, row1(gla_gate_bias), row1(mla_q_norm_w),
      wuq.astype(BF16), row1(mla_kv_norm_w), wukv.astype(BF16), inv_pat)
    gq, gk, gv, gla, gog, q, k, v = outs

    tg = GLA_TILE
    seq3 = lambda a: a.reshape(batch, seq, a.shape[-1])
    gspec = lambda w: pl.BlockSpec((1, tg, w), lambda b, i: (b, i, 0))
    y_gla = pl.pallas_call(
        _gla_kernel,
        grid=(batch, seq // tg),
        in_specs=[gspec(dq), gspec(dq), gspec(dv), gspec(dq), gspec(dv), _full((1, GLA_DV))],
        out_specs=gspec(dv),
        out_shape=jax.ShapeDtypeStruct((batch, seq, dv), BF16),
        scratch_shapes=[pltpu.VMEM((GLA_HEADS, GLA_DV, GLA_DK), F32)],
        compiler_params=_params("parallel", "arbitrary"),
        name="gla",
    )(seq3(gq), seq3(gk), seq3(gv), seq3(gla), seq3(gog), row1(gla_norm_w))

    ta = ATT_TILE
    hps = ATT_HEADS_PER_STEP
    y_mla = pl.pallas_call(
        functools.partial(_mla_kernel, tq=ta),
        grid=(batch, MLA_HEADS // hps, seq // ta),
        in_specs=[pl.BlockSpec((1, seq, hps * HEAD_PAD), lambda b, hp, i: (b, 0, hp))] * 3,
        out_specs=pl.BlockSpec((1, ta, hps * MLA_DV), lambda b, hp, i: (b, i, hp)),
        out_shape=jax.ShapeDtypeStruct((batch, seq, dmv), BF16),
        scratch_shapes=[pltpu.VMEM((hps, ta, ta), F32)] * 3,
        compiler_params=_params("parallel", "parallel", "arbitrary"),
        name="mla",
    )(seq3(q), seq3(k), seq3(v))

    tr = ROUTER_TILE
    ntr = n // tr
    cur = lambda t, w: pl.BlockSpec((t, w), lambda i: (jnp.minimum(i, ntr - 1), 0))
    prev = lambda t, w: pl.BlockSpec((t, w), lambda i: (jnp.maximum(i - 1, 0), 0))
    h, hn, route, route_t, counts = pl.pallas_call(
        _router_kernel,
        grid=(ntr + 1,),
        in_specs=[cur(tr, d), cur(tr, dv), cur(tr, dmv), _full(w_out.shape), _full((1, d)),
                  _full(w_router.shape), _full((1, LANES))],
        out_specs=[cur(tr, d), cur(tr, d // 2), prev(tr, LANES),
                   pl.BlockSpec((ROUTE_FIELDS, tr), lambda i: (0, jnp.maximum(i - 1, 0))),
                   _full((1, LANES))],
        out_shape=[jax.ShapeDtypeStruct((n, d), F32), jax.ShapeDtypeStruct((n, d // 2), jnp.uint32),
                   jax.ShapeDtypeStruct((n, LANES), F32),
                   jax.ShapeDtypeStruct((ROUTE_FIELDS, n), F32),
                   jax.ShapeDtypeStruct((1, LANES), F32)],
        scratch_shapes=[pltpu.VMEM((1, LANES), F32), pltpu.VMEM((tr, LANES), F32)],
        compiler_params=_params("arbitrary"),
        name="out_proj_router",
    )(x2, y_gla.reshape(n, dv), y_mla.reshape(n, dmv), w_out.astype(BF16), row1(ffn_norm_w),
      w_router, b_router)

    blk = EXPERT_BLOCK
    nblk = (n * TOP_K) // blk + N_EXPERTS
    cnt = counts[0, :N_EXPERTS].astype(jnp.int32)
    padded = (cnt + blk - 1) // blk * blk
    pend = jnp.cumsum(padded)
    pstart = pend - padded
    dest = pl.pallas_call(
        _dest_kernel,
        grid_spec=pltpu.PrefetchScalarGridSpec(
            num_scalar_prefetch=1, grid=(1,),
            in_specs=[pl.BlockSpec((ROUTE_FIELDS, n), lambda i, ps: (0, 0))],
            out_specs=pl.BlockSpec((TOP_K, n), lambda i, ps: (0, 0))),
        out_shape=jax.ShapeDtypeStruct((TOP_K, n), jnp.int32),
        compiler_params=_params("arbitrary"),
        name="dest_rows",
    )(pstart, route_t)
    blk_start = jnp.arange(nblk, dtype=jnp.int32) * blk
    blk_expert = jnp.minimum(
        jnp.sum((pend[None, :] <= blk_start[:, None]).astype(jnp.int32), axis=1), N_EXPERTS - 1)
    n_used = (pend[-1] // blk).astype(jnp.int32).reshape(1)
    blk_valid = jnp.clip(cnt[blk_expert] - (blk_start - pstart[blk_expert]), 0, blk)
    blk_valid = jnp.where(jnp.arange(nblk) < n_used[0], blk_valid, 0).astype(jnp.int32)
    dest_slots = [dest[kk] for kk in range(TOP_K)]

    xbuf = _sc_scatter_rows(hn, dest_slots, nblk * blk)

    def used(j, nu):
        return jnp.maximum(jnp.minimum(j, nu[0] - 1), 0)

    ybuf = pl.pallas_call(
        _expert_kernel,
        grid_spec=pltpu.PrefetchScalarGridSpec(
            num_scalar_prefetch=3,
            grid=(nblk,),
            in_specs=[
                pl.BlockSpec((blk, d // 2), lambda j, be, nu, bv: (used(j, nu), 0)),
                pl.BlockSpec((1, d, D_EXPERT), lambda j, be, nu, bv: (be[used(j, nu)], 0, 0)),
                pl.BlockSpec((1, d, D_EXPERT), lambda j, be, nu, bv: (be[used(j, nu)], 0, 0)),
                pl.BlockSpec((1, D_EXPERT, d), lambda j, be, nu, bv: (be[used(j, nu)], 0, 0)),
            ],
            out_specs=pl.BlockSpec((blk, d // 2), lambda j, be, nu, bv: (j, 0)),
            scratch_shapes=[pltpu.VMEM((d, 2 * D_EXPERT), BF16), pltpu.VMEM((D_EXPERT, d), BF16)],
        ),
        out_shape=jax.ShapeDtypeStruct((nblk * blk, d // 2), jnp.uint32),
        compiler_params=_params("arbitrary"),
        name="experts",
    )(blk_expert, n_used, blk_valid, xbuf, expert_w_gate, expert_w_up, expert_w_down)

    tc = COMBINE_TILE
    parts = COMBINE_PARTS
    npart = n // parts
    steps = npart // tc
    out = None
    for p in range(parts):
        part_dest = dest[:, p * npart:(p + 1) * npart].reshape(TOP_K * npart)
        gathered = _sc_gather_rows(ybuf, part_dest)
        here = lambda w, p=p: pl.BlockSpec((tc, w), lambda i: (i + p * steps, 0))
        in_specs = [here(d), here(LANES), _full((1, d)), _rows(tc, d // 2),
                    pl.BlockSpec((tc, d // 2), lambda i: (i + steps, 0))]
        args = [h, route, row1(out_norm_w), gathered, gathered]
        if out is not None:
            in_specs.append(pl.BlockSpec(memory_space=pl.ANY))
            args.append(out)
        out = pl.pallas_call(
            _combine_kernel if out is None else _combine_into_kernel,
            grid=(steps,),
            in_specs=in_specs,
            out_specs=here(d),
            out_shape=jax.ShapeDtypeStruct((n, d), F32),
            input_output_aliases={} if out is None else {len(args) - 1: 0},
            compiler_params=_params("parallel"),
            name=f"combine_{p}",
        )(*args)
    return out


def kernel(x, positions, attn_norm_w, w_in, gla_gate_up, gla_gate_bias, gla_norm_w, mla_q_norm_w,
           mla_w_uq, mla_kv_norm_w, mla_w_ukv, w_out, ffn_norm_w, router_group_w, router_group_b,
           router_expert_w, router_expert_b, expert_w_gate, expert_w_up, expert_w_down,
           final_norm_w):
    batch, seq, d = x.shape
    depth = w_in.shape[0]
    assert depth == 1, "the final norm is fused into the last layer's combine step"
    n = batch * seq
    assert seq % GLA_TILE == 0 and seq % ATT_TILE == 0, (seq, GLA_TILE, ATT_TILE)
    assert n % ROUTER_TILE == 0 and n % PROJ_TILE == 0, n
    assert n % (COMBINE_TILE * COMBINE_PARTS) == 0, n
    out = _layer(x.reshape(batch * seq, d), positions.reshape(batch * seq // ROPE_PACK, ROPE_PACK),
                 attn_norm_w[0], w_in[0], gla_gate_up[0], gla_gate_bias[0], gla_norm_w[0],
                 mla_q_norm_w[0], mla_w_uq[0], mla_kv_norm_w[0], mla_w_ukv[0], w_out[0],
                 ffn_norm_w[0], router_group_w[0], router_group_b[0], router_expert_w[0],
                 router_expert_b[0], expert_w_gate[0], expert_w_up[0], expert_w_down[0],
                 final_norm_w, batch, seq)
    return out.reshape(batch, seq, d)
```
